```python
import jax, jax.numpy as jnp
from jax import lax
import numpy as np

D_MODEL = 2048
BATCH = 4
SEQ = 2048
DEPTH = 1
DEC_BATCH = 128
DEC_SEQ = 4
PAST_LEN = 16384
PAGE_SIZE = 128

D_MIX = D_MODEL
D_RWKV = D_MIX // 2
D_HGRN = D_MIX - D_RWKV
RWKV_HEAD = 64
N_RWKV_HEADS = D_RWKV // RWKV_HEAD
HGRN_HEAD = 128
N_HGRN_HEADS = D_HGRN // HGRN_HEAD
LORA_W = 64
LORA_A = 64
LORA_G = 160
D_SHIFT = 3 * D_RWKV + LORA_W + LORA_A + LORA_G
D_IN = D_SHIFT + 4 * D_HGRN
HGRN_CHUNK = 64
N_EXPERTS = 32
TOP_K = 4
D_EXPERT = D_MODEL
SWIGLU_LIMIT = 7.0
SWIGLU_ALPHA = 1.702
MOE_BLOCK = 128
D_PLE = 256
RMS_EPS = 1e-6
GN_EPS = 64e-5
HGRN_NORM_EPS = 1e-5

kernel_name = 'hymba_rwkv7_hgrn2_moe_ple_step'

F32 = jnp.float32


def rmsnorm(x, g, eps=RMS_EPS):
    xf = x.astype(F32)
    y = xf * lax.rsqrt(jnp.mean(xf * xf, axis=-1, keepdims=True) + eps) * g.astype(F32)
    return y.astype(x.dtype)


def rwkv7_scan(r, w, k, v, a, b, S0):
    def step(S, inp):
        r_t, w_t, k_t, v_t, a_t, b_t = inp
        sa = jnp.einsum('bhvk,bhk->bhv', S, a_t)
        S = S * w_t[:, :, None, :] + sa[..., None] * b_t[:, :, None, :] + v_t[..., None] * k_t[:, :, None, :]
        return S, jnp.einsum('bhvk,bhk->bhv', S, r_t)
    xs = tuple(jnp.swapaxes(t, 0, 1) for t in (r, w, k, v, a, b))
    S, y = lax.scan(step, S0, xs)
    return jnp.swapaxes(y, 0, 1), S


def hgrn2_chunked(q, k, v, logf, S0):
    Bn, L, H, _ = q.shape
    DV = v.shape[-1]
    C = min(HGRN_CHUNK, L)
    n_chunks = -(-L // C)
    pad = n_chunks * C - L

    def blocks(t):
        t = jnp.pad(t, ((0, 0), (0, pad), (0, 0), (0, 0)))
        return t.reshape(Bn, n_chunks, C, H, t.shape[-1]).transpose(1, 0, 3, 2, 4)

    causal = jnp.tril(jnp.ones((C, C), bool))

    def step(S, inp):
        qc, kc, vc, gc = inp
        bcum = jnp.cumsum(gc, axis=2)
        diff = bcum[:, :, :, None, :] - bcum[:, :, None, :, :]
        decay = jnp.exp(jnp.where(causal[:, :, None], diff, -jnp.inf))
        A = jnp.einsum('bhtd,bhsd,bhtsd->bhts', qc, kc, decay)
        o = jnp.einsum('bhts,bhsv->bhtv', A, vc) + jnp.einsum('bhtd,bhdv->bhtv', qc * jnp.exp(bcum), S)
        b_end = bcum[:, :, -1:, :]
        S = jnp.exp(b_end[:, :, 0, :])[..., None] * S + jnp.einsum('bhsd,bhsv->bhdv', kc * jnp.exp(b_end - bcum), vc)
        return S, o

    S, o = lax.scan(step, S0, (blocks(q), blocks(k), blocks(v), blocks(logf)))
    o = o.transpose(1, 0, 3, 2, 4).reshape(Bn, n_chunks * C, H, DV)[:, :L]
    return o, S


def token_mixers(xn, shift_prev, s_rwkv, s_hgrn, lb, lp):
    Bn, L, _ = xn.shape
    proj = xn @ lp['w_in']
    xr = proj[..., :D_SHIFT]
    xh = proj[..., D_SHIFT:]

    prev = jnp.concatenate([shift_prev[:, None].astype(xr.dtype), xr[:, :-1]], axis=1)
    xs = xr + (prev - xr) * lp['mu_shift']
    new_shift = xr[:, -1]
    cuts = [D_RWKV, 2 * D_RWKV, 3 * D_RWKV, 3 * D_RWKV + LORA_W, 3 * D_RWKV + LORA_W + LORA_A]
    r, k, v, wd, ad, gd = jnp.split(xs, cuts, axis=-1)
    wlog = -jax.nn.softplus(-(lp['w0'] + jnp.tanh(wd) @ lp['w_up']).astype(F32)) - 0.5
    decay = jnp.exp(-jnp.exp(wlog))
    a = jax.nn.sigmoid((lp['a0'] + ad @ lp['a_up']).astype(F32))
    g = (jax.nn.sigmoid(gd) @ lp['g_up']).astype(F32)

    def heads_a(t):
        return t.astype(F32).reshape(Bn, L, N_RWKV_HEADS, RWKV_HEAD)

    kk = heads_a(k * lp['k_k'])
    kk = kk / jnp.maximum(jnp.sqrt(jnp.sum(kk * kk, axis=-1, keepdims=True)), 1e-12)
    k = k.astype(F32) * (1.0 + (a - 1.0) * lp['k_a'].astype(F32))
    rh, kh, vh, ah = heads_a(r), heads_a(k), heads_a(v), heads_a(a)
    y, s_rwkv_new = rwkv7_scan(rh, heads_a(decay), kh, vh, -kk, kk * ah, s_rwkv.astype(F32))
    mu = jnp.mean(y, axis=-1, keepdims=True)
    var = jnp.mean(jnp.square(y - mu), axis=-1, keepdims=True)
    y = ((y - mu) * lax.rsqrt(var + GN_EPS)).reshape(Bn, L, D_RWKV)
    y = y * lp['lnx_w'].astype(F32) + lp['lnx_b'].astype(F32)
    bonus = jnp.sum(rh * kh * lp['r_k'].astype(F32), axis=-1, keepdims=True) * vh
    y_rwkv = (y + bonus.reshape(Bn, L, D_RWKV)) * g

    q, fr, i, og = jnp.split(xh, 4, axis=-1)
    f = lb + (1.0 - lb) * jax.nn.sigmoid(fr.astype(F32))
    logf = jnp.log(f)

    def heads_b(t):
        return t.astype(F32).reshape(Bn, L, N_HGRN_HEADS, HGRN_HEAD)

    o, s_hgrn_new = hgrn2_chunked(heads_b(jax.nn.silu(q.astype(F32))), heads_b(1.0 - f), heads_b(i),
                                  heads_b(logf), s_hgrn.astype(F32))
    o = o * lax.rsqrt(jnp.mean(o * o, axis=-1, keepdims=True) + HGRN_NORM_EPS) * lp['hgrn_norm'].astype(F32)
    y_hgrn = o.reshape(Bn, L, D_HGRN) * jax.nn.silu(og.astype(F32))

    mixed = jnp.concatenate([y_rwkv, y_hgrn], axis=-1).astype(xn.dtype)
    return mixed @ lp['w_out'], new_shift, s_rwkv_new, s_hgrn_new


def moe_ffn(x, w_router, b_router, w_gu, b_gu, w_down, b_down):
    Bn, L, D = x.shape
    T = Bn * L
    xt = x.reshape(T, D)
    logits = (xt @ w_router + b_router).astype(F32)
    top_val, top_idx = lax.top_k(logits, TOP_K)
    gates = jax.nn.softmax(top_val, axis=-1)
    n_assign = T * TOP_K
    flat_e = top_idx.reshape(-1).astype(jnp.int32)
    flat_tok = jnp.arange(n_assign, dtype=jnp.int32) // TOP_K
    order = jnp.argsort(flat_e)
    sorted_e = flat_e[order]
    counts = jnp.bincount(flat_e, length=N_EXPERTS)
    starts = jnp.cumsum(counts) - counts
    padded = (counts + MOE_BLOCK - 1) // MOE_BLOCK * MOE_BLOCK
    pends = jnp.cumsum(padded)
    pstarts = pends - padded
    dest = pstarts[sorted_e] + jnp.arange(n_assign, dtype=jnp.int32) - starts[sorted_e]
    n_blocks = -(-n_assign // MOE_BLOCK) + N_EXPERTS
    n_rows = n_blocks * MOE_BLOCK
    row_tok = jnp.zeros((n_rows,), jnp.int32).at[dest].set(flat_tok[order])
    row_gate = jnp.zeros((n_rows,), F32).at[dest].set(gates.reshape(-1)[order])
    block_e = jnp.minimum(jnp.searchsorted(pends, jnp.arange(n_blocks) * MOE_BLOCK, side='right'), N_EXPERTS - 1)
    xb = xt[row_tok].reshape(n_blocks, MOE_BLOCK, D)

    def expert_block(args):
        xblk, e = args
        gu = xblk @ w_gu[e] + b_gu[e]
        gate, up = jnp.split(gu, 2, axis=-1)
        gate = jnp.minimum(gate, SWIGLU_LIMIT)
        up = jnp.clip(up, -SWIGLU_LIMIT, SWIGLU_LIMIT)
        h = (up + 1.0) * gate * jax.nn.sigmoid(SWIGLU_ALPHA * gate)
        return h @ w_down[e] + b_down[e]

    yb = lax.map(expert_block, (xb, block_e))
    y = jnp.zeros((T, D), F32).at[row_tok].add(yb.reshape(n_rows, D).astype(F32) * row_gate[:, None])
    return y.reshape(Bn, L, D).astype(x.dtype)


def setup_inputs(seed: int = 0) -> dict:
    key = jax.random.key(seed)
    ks = iter(jax.random.split(key, 48))

    def nrm(shape, scale):
        return jax.random.normal(next(ks), shape, F32) * scale

    return {
        'x_prompt': nrm((BATCH, SEQ, D_MODEL), 1.0),
        'x_sample': nrm((DEC_BATCH, DEC_SEQ, D_MODEL), 1.0),
        'p_prompt': nrm((DEPTH, BATCH, SEQ, D_PLE), 1.0),
        'p_sample': nrm((DEPTH, DEC_BATCH, DEC_SEQ, D_PLE), 1.0),
        'state_rwkv_shift': nrm((DEPTH, DEC_BATCH, D_SHIFT), 1.0),
        'state_rwkv': nrm((DEPTH, DEC_BATCH, N_RWKV_HEADS, RWKV_HEAD, RWKV_HEAD), 0.3),
        'state_hgrn': nrm((DEPTH, DEC_BATCH, N_HGRN_HEADS, HGRN_HEAD, HGRN_HEAD), 0.3),
        'norm_mix': 1.0 + nrm((DEPTH, D_MODEL), 0.02),
        'w_in': nrm((DEPTH, D_MODEL, D_IN), D_MODEL ** -0.5),
        'mu_shift': jax.random.uniform(next(ks), (DEPTH, D_SHIFT), F32),
        'w0': nrm((DEPTH, D_RWKV), 1.0) - 1.0,
        'w_up': nrm((DEPTH, LORA_W, D_RWKV), 0.5 * LORA_W ** -0.5),
        'a0': nrm((DEPTH, D_RWKV), 0.5),
        'a_up': nrm((DEPTH, LORA_A, D_RWKV), LORA_A ** -0.5),
        'g_up': nrm((DEPTH, LORA_G, D_RWKV), LORA_G ** -0.5),
        'k_k': 0.85 + nrm((DEPTH, D_RWKV), 0.05),
        'k_a': 1.0 + nrm((DEPTH, D_RWKV), 0.05),
        'r_k': nrm((DEPTH, N_RWKV_HEADS, RWKV_HEAD), 0.3),
        'lnx_w': 1.0 + nrm((DEPTH, D_RWKV), 0.02),
        'lnx_b': nrm((DEPTH, D_RWKV), 0.02),
        'hgrn_lb': nrm((DEPTH + 1, D_HGRN), 0.5),
        'hgrn_norm': 1.0 + nrm((DEPTH, HGRN_HEAD), 0.02),
        'w_out': nrm((DEPTH, D_MIX, D_MODEL), D_MIX ** -0.5),
        'norm_ffn': 1.0 + nrm((DEPTH, D_MODEL), 0.02),
        'w_router': nrm((DEPTH, D_MODEL, N_EXPERTS), D_MODEL ** -0.5),
        'b_router': nrm((DEPTH, N_EXPERTS), 0.01),
        'w_gu': nrm((DEPTH, N_EXPERTS, D_MODEL, 2 * D_EXPERT), D_MODEL ** -0.5),
        'b_gu': nrm((DEPTH, N_EXPERTS, 2 * D_EXPERT), 0.01),
        'w_down': nrm((DEPTH, N_EXPERTS, D_EXPERT, D_MODEL), D_EXPERT ** -0.5),
        'b_down': nrm((DEPTH, N_EXPERTS, D_MODEL), 0.01),
        'norm_ple': 1.0 + nrm((DEPTH, D_MODEL), 0.02),
        'w_ple_gate': nrm((DEPTH, D_MODEL, D_MODEL), D_MODEL ** -0.5),
        'w_ple_proj': nrm((DEPTH, D_PLE, D_MODEL), D_PLE ** -0.5),
        'norm_final': 1.0 + nrm((D_MODEL,), 0.02),
    }


def reference(x_prompt, x_sample, p_prompt, p_sample, state_rwkv_shift, state_rwkv, state_hgrn,
              norm_mix, w_in, mu_shift, w0, w_up, a0, a_up, g_up, k_k, k_a, r_k, lnx_w, lnx_b,
              hgrn_lb, hgrn_norm, w_out, norm_ffn, w_router, b_router, w_gu, b_gu, w_down, b_down,
              norm_ple, w_ple_gate, w_ple_proj, norm_final):
    lower_bounds = jnp.cumsum(jax.nn.softmax(hgrn_lb.astype(F32), axis=0), axis=0)

    def trunk(h, p, shift0, srwkv0, shgrn0):
        shifts, srwkvs, shgrns = [], [], []
        for li in range(DEPTH):
            lp = {'w_in': w_in[li], 'mu_shift': mu_shift[li], 'w0': w0[li], 'w_up': w_up[li], 'a0': a0[li],
                  'a_up': a_up[li], 'g_up': g_up[li], 'k_k': k_k[li], 'k_a': k_a[li], 'r_k': r_k[li],
                  'lnx_w': lnx_w[li], 'lnx_b': lnx_b[li], 'hgrn_norm': hgrn_norm[li], 'w_out': w_out[li]}
            mix, sh, sr, sg = token_mixers(rmsnorm(h, norm_mix[li]), shift0[li], srwkv0[li], shgrn0[li],
                                           lower_bounds[li], lp)
            h = h + mix
            h = h + moe_ffn(rmsnorm(h, norm_ffn[li]), w_router[li], b_router[li], w_gu[li], b_gu[li],
                            w_down[li], b_down[li])
            gate = jax.nn.sigmoid((rmsnorm(h, norm_ple[li]) @ w_ple_gate[li]).astype(F32))
            h = h + (gate * (p[li] @ w_ple_proj[li]).astype(F32)).astype(h.dtype)
            shifts.append(sh.astype(state_rwkv_shift.dtype))
            srwkvs.append(sr.astype(state_rwkv.dtype))
            shgrns.append(sg.astype(state_hgrn.dtype))
        return rmsnorm(h, norm_final), jnp.stack(shifts), jnp.stack(srwkvs), jnp.stack(shgrns)

    zero_shift = jnp.zeros((DEPTH, x_prompt.shape[0], D_SHIFT), x_prompt.dtype)
    zero_rwkv = jnp.zeros((DEPTH, x_prompt.shape[0], N_RWKV_HEADS, RWKV_HEAD, RWKV_HEAD), F32)
    zero_hgrn = jnp.zeros((DEPTH, x_prompt.shape[0], N_HGRN_HEADS, HGRN_HEAD, HGRN_HEAD), F32)
    y_prompt, shift_p, rwkv_p, hgrn_p = trunk(x_prompt, p_prompt, zero_shift, zero_rwkv, zero_hgrn)
    y_sample, shift_s, rwkv_s, hgrn_s = trunk(x_sample, p_sample, state_rwkv_shift, state_rwkv, state_hgrn)
    return (y_prompt, y_sample, shift_p, rwkv_p, hgrn_p, shift_s, rwkv_s, hgrn_s)
```

```python
import functools

import jax
import jax.numpy as jnp
from jax import lax
from jax.experimental import pallas as pl
from jax.experimental.pallas import tpu as pltpu

F32 = jnp.float32
BF16 = jnp.bfloat16
I32 = jnp.int32

D_MODEL = 2048
D_RWKV = 1024
D_HGRN = 1024
RWKV_HEAD = 64
N_RWKV_HEADS = 16
HGRN_HEAD = 128
N_HGRN_HEADS = 8
LORA_W = 64
LORA_A = 64
LORA_G = 160
D_SHIFT = 3 * D_RWKV + LORA_W + LORA_A + LORA_G
N_EXPERTS = 32
TOP_K = 4
D_EXPERT = 2048
SWIGLU_LIMIT = 7.0
SWIGLU_ALPHA = 1.702
D_PLE = 256
RMS_EPS = 1e-6
GN_EPS = 64e-5
HGRN_NORM_EPS = 1e-5

LANE = 128
SUBLANE = 8
N_UNITS = 8

LW_PAD = LANE
LA_PAD = LANE
LG_PAD = 2 * LANE
OFF_WD = 3 * D_RWKV
OFF_AD = OFF_WD + LW_PAD
OFF_GD = OFF_AD + LA_PAD
D_SHIFT_PAD = OFF_GD + LG_PAD
D_IN_PAD = D_SHIFT_PAD + 4 * D_HGRN

MOE_TM = 256
MOE_RT = 6
MOE_TF = 256
VMEM_LIMIT = 56 * 1024 * 1024


def _cparams(sem, vmem=VMEM_LIMIT):
    return pltpu.CompilerParams(dimension_semantics=sem, vmem_limit_bytes=vmem)


def _rmsnorm(x, g):
    return x * lax.rsqrt(jnp.mean(x * x, axis=-1, keepdims=True) + RMS_EPS) * g


def _dot(a, b):
    return jnp.dot(a.astype(BF16), b.astype(BF16), preferred_element_type=F32)


def _dot_nt(a, b):
    return lax.dot_general(a.astype(BF16), b.astype(BF16), (((1,), (1,)), ((), ())),
                           preferred_element_type=F32)


def _dot_tn(a, b):
    return lax.dot_general(a.astype(BF16), b.astype(BF16), (((0,), (0,)), ((), ())),
                           preferred_element_type=F32)


def _split3(x):
    h = x.astype(BF16)
    r = x - h.astype(F32)
    m = r.astype(BF16)
    l = (r - m.astype(F32)).astype(BF16)
    return h, m, l


def _dot_exact_rhs(a, b_bf16):
    h, m, l = _split3(a)
    d = functools.partial(jnp.dot, preferred_element_type=F32)
    return d(h, b_bf16) + d(m, b_bf16) + d(l, b_bf16)


def _dot_exact_lhs(a_bf16, b):
    h, m, l = _split3(b)
    d = functools.partial(jnp.dot, preferred_element_type=F32)
    return d(a_bf16, h) + d(a_bf16, m) + d(a_bf16, l)


def _iota2(shape, dim):
    return lax.broadcasted_iota(I32, shape, dim)


def _cumsum_time(x):
    c = x.shape[0]
    tri = (_iota2((c, c), 0) >= _iota2((c, c), 1)).astype(BF16)
    return _dot_exact_lhs(tri, x)


def _same_head_mask():
    return (_iota2((LANE, LANE), 0) >= RWKV_HEAD) == (_iota2((LANE, LANE), 1) >= RWKV_HEAD)


def _sigmoid(x):
    return 1.0 / (1.0 + jnp.exp(-x))


def _inproj_kernel(x_ref, g_ref, w_ref, o_ref, xn_ref):
    @pl.when(pl.program_id(1) == 0)
    def _():
        xn_ref[...] = _rmsnorm(x_ref[...], g_ref[...]).astype(BF16)

    o_ref[...] = jnp.dot(xn_ref[...], w_ref[...], preferred_element_type=F32)


def _inproj(x, g, w, tm, tn):
    t, d = x.shape
    n = w.shape[1]
    return pl.pallas_call(
        _inproj_kernel,
        out_shape=jax.ShapeDtypeStruct((t, n), F32),
        grid=(t // tm, n // tn),
        in_specs=[pl.BlockSpec((tm, d), lambda i, j: (i, 0)),
                  pl.BlockSpec((1, d), lambda i, j: (0, 0)),
                  pl.BlockSpec((d, tn), lambda i, j: (0, j))],
        out_specs=pl.BlockSpec((tm, tn), lambda i, j: (i, j)),
        scratch_shapes=[pltpu.VMEM((tm, d), BF16)],
        compiler_params=_cparams(("parallel", "arbitrary")),
        name="inproj",
    )(x, g, w)


def _rwkv_prep_kernel(x_ref, p8_ref, sh_ref, mu_ref, w0_ref, wup_ref, a0_ref, aup_ref, gup_ref,
                      kk_ref, ka_ref,
                      r_o, k_o, v_o, lw_o, a_o, b_o, g_o, *, bb, tt, l_valid, l_total):
    ti = pl.program_id(1)
    w = D_SHIFT_PAD
    x3 = x_ref[...].reshape(bb, tt, w)
    rolled = pltpu.roll(x3, 1, axis=1)
    prev_tail = p8_ref[...].reshape(bb, SUBLANE, w)[:, SUBLANE - 1:SUBLANE, :]
    first = jnp.where(ti == 0, sh_ref[...], prev_tail)
    t_in = _iota2((bb, tt, w), 1)
    prev = jnp.where(t_in == 0, first, rolled)
    xs = (x3 + (prev - x3) * mu_ref[...]).reshape(bb * tt, w)

    r = xs[:, 0:D_RWKV]
    k = xs[:, D_RWKV:2 * D_RWKV]
    v = xs[:, 2 * D_RWKV:3 * D_RWKV]
    wd = xs[:, OFF_WD:OFF_WD + LW_PAD]
    ad = xs[:, OFF_AD:OFF_AD + LA_PAD]
    gd = xs[:, OFF_GD:OFF_GD + LG_PAD]

    z = -(w0_ref[...] + _dot(jnp.tanh(wd), wup_ref[...]))
    softplus = jnp.maximum(z, 0.0) + jnp.log(1.0 + jnp.exp(-jnp.abs(z)))
    lw = -jnp.exp(-softplus - 0.5)
    asig = _sigmoid(a0_ref[...] + _dot(ad, aup_ref[...]))
    g = _dot(_sigmoid(gd), gup_ref[...])

    kk = k * kk_ref[...]
    same_head = _same_head_mask().astype(BF16)
    sq = kk * kk
    ssq = jnp.concatenate(
        [_dot_exact_rhs(sq[:, u * LANE:(u + 1) * LANE], same_head) for u in range(N_UNITS)], axis=1)
    kkn = kk / jnp.maximum(jnp.sqrt(ssq), 1e-12)
    k2 = k * (1.0 + (asig - 1.0) * ka_ref[...])
    a_vec = -kkn
    b_vec = kkn * asig

    if l_valid < l_total:
        t_glob = (ti * tt + _iota2((bb, tt, D_RWKV), 1)).reshape(bb * tt, D_RWKV)
        ok = t_glob < l_valid
        zero = jnp.zeros_like(k2)
        lw, k2, v, a_vec, b_vec = (jnp.where(ok, t, zero) for t in (lw, k2, v, a_vec, b_vec))

    r_o[...] = r
    k_o[...] = k2
    v_o[...] = v
    lw_o[...] = lw
    a_o[...] = a_vec
    b_o[...] = b_vec
    g_o[...] = g


def _rwkv_prep(proj, shift_pad, pp, bn, l_total, l_valid, bb, tt):
    nt = l_total // tt
    rows = bb * tt
    w = D_SHIFT_PAD
    row_spec = pl.BlockSpec((rows, w), lambda bi, ti: (bi * nt + ti, 0))
    p8_spec = pl.BlockSpec((bb * SUBLANE, w),
                           lambda bi, ti: (jnp.maximum((bi * nt + ti) * (tt // SUBLANE) - 1, 0), 0))
    vec = lambda n: pl.BlockSpec((1, n), lambda bi, ti: (0, 0))
    mat = lambda a, b: pl.BlockSpec((a, b), lambda bi, ti: (0, 0))
    out_spec = pl.BlockSpec((rows, D_RWKV), lambda bi, ti: (bi * nt + ti, 0))
    out_sds = jax.ShapeDtypeStruct((bn * l_total, D_RWKV), F32)
    kern = functools.partial(_rwkv_prep_kernel, bb=bb, tt=tt, l_valid=l_valid, l_total=l_total)
    return pl.pallas_call(
        kern,
        out_shape=[out_sds] * 7,
        grid=(bn // bb, nt),
        in_specs=[row_spec, p8_spec,
                  pl.BlockSpec((bb, 1, w), lambda bi, ti: (bi, 0, 0)),
                  vec(w), vec(D_RWKV), mat(LW_PAD, D_RWKV), vec(D_RWKV), mat(LA_PAD, D_RWKV),
                  mat(LG_PAD, D_RWKV), vec(D_RWKV), vec(D_RWKV)],
        out_specs=[out_spec] * 7,
        compiler_params=_cparams(("parallel", "arbitrary")),
        name="rwkv_prep",
    )(proj, proj, shift_pad, pp["mu"], pp["w0"], pp["w_up"], pp["a0"], pp["a_up"], pp["g_up"],
      pp["k_k"], pp["k_a"])


def _rwkv_scan_kernel(*refs, bb, c, has_s0):
    if has_s0:
        (r_ref, k_ref, v_ref, lw_ref, a_ref, b_ref, g_ref, rk_ref, lnw_ref, lnb_ref, s0_ref,
         y_ref, so_ref, s_scr) = refs
    else:
        (r_ref, k_ref, v_ref, lw_ref, a_ref, b_ref, g_ref, rk_ref, lnw_ref, lnb_ref,
         y_ref, so_ref, s_scr) = refs
        s0_ref = None
    ci = pl.program_id(2)
    n_chunks = pl.num_programs(2)

    lane = _iota2((1, LANE), 1)
    m0 = (lane < RWKV_HEAD).astype(F32)
    m1 = 1.0 - m0
    bd_mask = _same_head_mask().astype(F32)
    row_lo = _iota2((LANE, LANE), 0) < RWKV_HEAD

    c2 = 2 * c
    ri = _iota2((c2, c2), 0)
    cj = _iota2((c2, c2), 1)
    same_blk = (ri >= c) == (cj >= c)
    mask_s = jnp.where(same_blk, (ri > cj).astype(F32), 0.0)
    mask_i = jnp.where(same_blk, (ri >= cj).astype(F32), 0.0)
    eye = (ri == cj).astype(F32)
    n_sq = max((c - 1).bit_length() - 1, 0)

    head_avg = bd_mask.astype(BF16)

    @pl.when(ci == 0)
    def _():
        for j in range(bb):
            if has_s0:
                s = s0_ref[j, 0]
                s_scr[j] = jnp.concatenate([s, s], axis=1) * bd_mask
            else:
                s_scr[j] = jnp.zeros((LANE, LANE), F32)

    for j in range(bb):
        r = r_ref[j]
        k = k_ref[j]
        v = v_ref[j]
        lw = lw_ref[j]
        a = a_ref[j]
        b = b_ref[j]
        cum = _cumsum_time(lw)
        clast = cum[c - 1:c, :]
        p_t = jnp.exp(cum)
        p_inv = jnp.exp(-cum)
        p_prev = jnp.exp(cum - lw)
        p_last = jnp.exp(clast - cum)
        rt = r * p_t
        at = a * p_prev
        bt = b * p_inv
        kt = k * p_inv
        a2 = jnp.concatenate([at * m0, at * m1], axis=0)
        r2 = jnp.concatenate([rt * m0, rt * m1], axis=0)
        b2 = jnp.concatenate([bt, bt], axis=0)
        k2 = jnp.concatenate([kt, kt], axis=0)
        v2 = jnp.concatenate([v * m0, v * m1], axis=0)
        lab = mask_s * _dot_nt(a2, b2)
        lak = mask_s * _dot_nt(a2, k2)
        rb = mask_i * _dot_nt(r2, b2)
        rkm = mask_i * _dot_nt(r2, k2)
        tinv = eye + lab
        xp = lab
        for _ in range(n_sq):
            xp = _dot(xp, xp)
            tinv = tinv + _dot(tinv, xp)
        s0 = s_scr[j]
        u2 = _dot(tinv, _dot_nt(a2, s0) + _dot(lak, v2))
        y2 = _dot_nt(r2, s0) + _dot(rb, u2) + _dot(rkm, v2)
        y = y2[:c] + y2[c:]
        u = u2[:c] + u2[c:]
        s_new = s0 * jnp.exp(clast) + bd_mask * (_dot_tn(u, b * p_last) + _dot_tn(v, k * p_last))
        s_scr[j] = s_new

        mu = _dot_exact_rhs(y, head_avg) * (1.0 / RWKV_HEAD)
        dlt = y - mu
        var = _dot_exact_rhs(dlt * dlt, head_avg) * (1.0 / RWKV_HEAD)
        yn = dlt * lax.rsqrt(var + GN_EPS) * lnw_ref[...] + lnb_ref[...]
        bonus = _dot_exact_rhs(r * k * rk_ref[...], head_avg) * v
        y_ref[j] = (yn + bonus) * g_ref[j]

        @pl.when(ci == n_chunks - 1)
        def _():
            folded = jnp.where(row_lo, s_new, pltpu.roll(s_new, RWKV_HEAD, axis=1))
            so_ref[j, 0] = folded[:, :RWKV_HEAD]


def _rwkv_scan(streams, rk, lnw, lnb, s0, bn, l_total, bb, c):
    nc = l_total // c
    has_s0 = s0 is not None
    blk = pl.BlockSpec((bb, c, LANE), lambda bi, u, ci: (bi, ci, u))
    vec = pl.BlockSpec((1, LANE), lambda bi, u, ci: (0, u))
    st = pl.BlockSpec((bb, 1, LANE, RWKV_HEAD), lambda bi, u, ci: (bi, u, 0, 0))
    in_specs = [blk] * 7 + [vec] * 3 + ([st] if has_s0 else [])
    args = [s.reshape(bn, l_total, D_RWKV) for s in streams] + [rk, lnw, lnb] + ([s0] if has_s0 else [])
    kern = functools.partial(_rwkv_scan_kernel, bb=bb, c=c, has_s0=has_s0)
    return pl.pallas_call(
        kern,
        out_shape=[jax.ShapeDtypeStruct((bn, l_total, D_RWKV), F32),
                   jax.ShapeDtypeStruct((bn, N_UNITS, LANE, RWKV_HEAD), F32)],
        grid=(bn // bb, N_UNITS, nc),
        in_specs=in_specs,
        out_specs=[blk, st],
        scratch_shapes=[pltpu.VMEM((bb, LANE, LANE), F32)],
        compiler_params=_cparams(("parallel", "parallel", "arbitrary")),
        name="rwkv_scan",
    )(*args)


def _hgrn_kernel(*refs, bb, c, has_s0, l_valid, l_total):
    if has_s0:
        q_ref, f_ref, i_ref, og_ref, lb_ref, nw_ref, s0_ref, y_ref, so_ref, s_scr = refs
    else:
        q_ref, f_ref, i_ref, og_ref, lb_ref, nw_ref, y_ref, so_ref, s_scr = refs
        s0_ref = None
    ci = pl.program_id(2)
    n_chunks = pl.num_programs(2)
    tri = (_iota2((c, c), 0) >= _iota2((c, c), 1)).astype(F32)
    mid = max(c // 2 - 1, 0)

    @pl.when(ci == 0)
    def _():
        for j in range(bb):
            if has_s0:
                s_scr[j] = s0_ref[j, 0].T
            else:
                s_scr[j] = jnp.zeros((LANE, LANE), F32)

    lb = lb_ref[...]
    for j in range(bb):
        qr = q_ref[j]
        q = qr * _sigmoid(qr)
        f = lb + (1.0 - lb) * _sigmoid(f_ref[j])
        logf = jnp.log(f)
        kf = 1.0 - f
        v = i_ref[j]
        if l_valid < l_total:
            ok = (ci * c + _iota2((c, LANE), 0)) < l_valid
            logf = jnp.where(ok, logf, 0.0)
            kf = jnp.where(ok, kf, 0.0)
        cum = _cumsum_time(logf)
        cref = cum[mid:mid + 1, :]
        clast = cum[c - 1:c, :]
        amat = tri * _dot_nt(q * jnp.exp(cum - cref), kf * jnp.exp(cref - cum))
        st = s_scr[j]
        o = _dot_nt(q * jnp.exp(cum), st) + _dot(amat, v)
        s_new = st * jnp.exp(clast) + _dot_tn(v, kf * jnp.exp(clast - cum))
        s_scr[j] = s_new
        o = o * lax.rsqrt(jnp.mean(o * o, axis=-1, keepdims=True) + HGRN_NORM_EPS) * nw_ref[...]
        og = og_ref[j]
        y_ref[j] = o * (og * _sigmoid(og))

        @pl.when(ci == n_chunks - 1)
        def _():
            so_ref[j, 0] = s_new.T


def _hgrn_scan(proj, lb, nw, s0, bn, l_total, l_valid, bb, c):
    nc = l_total // c
    has_s0 = s0 is not None
    base = D_SHIFT_PAD // LANE
    per = D_HGRN // LANE
    proj3 = proj.reshape(bn, l_total, D_IN_PAD)

    def col(part):
        return pl.BlockSpec((bb, c, LANE), lambda bi, u, ci: (bi, ci, base + part * per + u))

    st = pl.BlockSpec((bb, 1, LANE, LANE), lambda bi, u, ci: (bi, u, 0, 0))
    in_specs = [col(0), col(1), col(2), col(3),
                pl.BlockSpec((1, LANE), lambda bi, u, ci: (0, u)),
                pl.BlockSpec((1, LANE), lambda bi, u, ci: (0, 0))] + ([st] if has_s0 else [])
    args = [proj3, proj3, proj3, proj3, lb, nw] + ([s0] if has_s0 else [])
    kern = functools.partial(_hgrn_kernel, bb=bb, c=c, has_s0=has_s0, l_valid=l_valid, l_total=l_total)
    return pl.pallas_call(
        kern,
        out_shape=[jax.ShapeDtypeStruct((bn, l_total, D_HGRN), F32),
                   jax.ShapeDtypeStruct((bn, N_UNITS, LANE, LANE), F32)],
        grid=(bn // bb, N_UNITS, nc),
        in_specs=in_specs,
        out_specs=[pl.BlockSpec((bb, c, LANE), lambda bi, u, ci: (bi, ci, u)), st],
        scratch_shapes=[pltpu.VMEM((bb, LANE, LANE), F32)],
        compiler_params=_cparams(("parallel", "parallel", "arbitrary")),
        name="hgrn_scan",
    )(*args)


def _outproj_kernel(x_ref, yr_ref, yh_ref, wo_ref, nf_ref, wr_ref, br_ref,
                    h_ref, xn_ref, idx_ref, gate_ref):
    h = (x_ref[...] + _dot(yr_ref[...], wo_ref[0:D_RWKV, :]) + _dot(yh_ref[...], wo_ref[D_RWKV:, :]))
    h_ref[...] = h
    xn = _rmsnorm(h, nf_ref[...])
    xn_ref[...] = xn
    wr = wr_ref[...]
    xh = xn.astype(BF16)
    wh = wr.astype(BF16)
    logits = (_dot(xh, wh) + _dot(xn - xh.astype(F32), wh) + _dot(xh, wr - wh.astype(F32))
              + br_ref[...])
    tm = logits.shape[0]
    lane = _iota2((tm, LANE), 1).astype(F32)
    neg = jnp.float32(-jnp.inf)
    work = jnp.where(lane < N_EXPERTS, logits, neg)
    idx_out = jnp.zeros((tm, LANE), I32)
    val_out = jnp.zeros((tm, LANE), F32)
    top0 = None
    for kk in range(TOP_K):
        m = jnp.max(work, axis=-1, keepdims=True)
        sel = jnp.min(jnp.where(work == m, lane, float(LANE)), axis=-1, keepdims=True)
        if kk == 0:
            top0 = m
        idx_out = jnp.where(lane == kk, sel.astype(I32), idx_out)
        val_out = jnp.where(lane == kk, jnp.exp(m - top0), val_out)
        work = jnp.where(lane == sel, neg, work)
    idx_ref[...] = idx_out
    gate_ref[...] = val_out / jnp.sum(val_out, axis=-1, keepdims=True)


def _outproj_router(x, yr, yh, wo, nf, wr, br, tm):
    t = x.shape[0]
    row = lambda n: pl.BlockSpec((tm, n), lambda i: (i, 0))
    full = lambda a, b: pl.BlockSpec((a, b), lambda i: (0, 0))
    return pl.pallas_call(
        _outproj_kernel,
        out_shape=[jax.ShapeDtypeStruct((t, D_MODEL), F32), jax.ShapeDtypeStruct((t, D_MODEL), F32),
                   jax.ShapeDtypeStruct((t, LANE), I32), jax.ShapeDtypeStruct((t, LANE), F32)],
        grid=(t // tm,),
        in_specs=[row(D_MODEL), row(D_RWKV), row(D_HGRN), full(D_MODEL, D_MODEL), full(1, D_MODEL),
                  full(D_MODEL, LANE), full(1, LANE)],
        out_specs=[row(D_MODEL), row(D_MODEL), row(LANE), row(LANE)],
        compiler_params=_cparams(("parallel",)),
        name="outproj_router",
    )(x, yr, yh, wo, nf, wr, br)


def _row_copy(src_hbm, row, dst_vmem, r, sem):
    return pltpu.make_async_copy(src_hbm.at[pl.ds(row, 1)], dst_vmem.at[pl.ds(r, 1)], sem)


def _gather_kernel(rowtok_ref, nused_ref, xp_hbm, xs_hbm, o_ref, sem, *, n_prompt):
    t = pl.program_id(0)

    @pl.when(t < nused_ref[0])
    def _():
        def start(r, carry):
            tok = rowtok_ref[t * MOE_TM + r]

            @pl.when(tok < n_prompt)
            def _():
                _row_copy(xp_hbm, tok, o_ref, r, sem).start()

            @pl.when(tok >= n_prompt)
            def _():
                _row_copy(xs_hbm, tok - n_prompt, o_ref, r, sem).start()

            return carry

        lax.fori_loop(0, MOE_TM, start, 0)

        def wait(r, carry):
            _row_copy(xp_hbm, 0, o_ref, r, sem).wait()
            return carry

        lax.fori_loop(0, MOE_TM, wait, 0)

    @pl.when(t >= nused_ref[0])
    def _():
        o_ref[...] = jnp.zeros_like(o_ref)


def _moe_gather(row_tok, n_used, xn_p, xn_s, n_rows):
    n_tiles = n_rows // MOE_TM
    kern = functools.partial(_gather_kernel, n_prompt=xn_p.shape[0])
    return pl.pallas_call(
        kern,
        out_shape=jax.ShapeDtypeStruct((n_rows, D_MODEL), F32),
        grid_spec=pltpu.PrefetchScalarGridSpec(
            num_scalar_prefetch=2,
            grid=(n_tiles,),
            in_specs=[pl.BlockSpec(memory_space=pl.ANY), pl.BlockSpec(memory_space=pl.ANY)],
            out_specs=pl.BlockSpec((MOE_TM, D_MODEL), lambda t, rt, nu: (t, 0)),
            scratch_shapes=[pltpu.SemaphoreType.DMA(())]),
        compiler_params=_cparams(("arbitrary",)),
        name="moe_gather",
    )(row_tok, n_used, xn_p, xn_s)


def _moe_kernel(ie_ref, it0_ref, int_ref, xs_hbm, wg_ref, wu_ref, wd_ref, bg_ref, bu_ref, bd_ref,
                ys_hbm, xbuf, acc, stage, wgb, wub, wdb, sem):
    i = pl.program_id(0)
    f = pl.program_id(1)
    nf = pl.num_programs(1)
    nt = int_ref[i]
    row0 = it0_ref[i] * MOE_TM

    @pl.when(nt > 0)
    def _():
        @pl.when(f == 0)
        def _():
            def load(t, carry):
                cp = pltpu.make_async_copy(xs_hbm.at[pl.ds(row0 + t * MOE_TM, MOE_TM)], stage, sem)
                cp.start()
                cp.wait()
                xbuf[t] = stage[...].astype(BF16)
                acc[t] = jnp.broadcast_to(bd_ref[0], (MOE_TM, D_MODEL))
                return carry

            lax.fori_loop(0, nt, load, 0)

        wgb[...] = wg_ref[0].astype(BF16)
        wub[...] = wu_ref[0].astype(BF16)
        wdb[...] = wd_ref[0].astype(BF16)
        bg = bg_ref[0]
        bu = bu_ref[0]

        def tile(t, carry):
            x = xbuf[t]
            gate = jnp.dot(x, wgb[...], preferred_element_type=F32) + bg
            up = jnp.dot(x, wub[...], preferred_element_type=F32) + bu
            gate = jnp.minimum(gate, SWIGLU_LIMIT)
            up = jnp.clip(up, -SWIGLU_LIMIT, SWIGLU_LIMIT)
            hid = (up + 1.0) * gate * _sigmoid(SWIGLU_ALPHA * gate)
            acc[t] += jnp.dot(hid.astype(BF16), wdb[...], preferred_element_type=F32)
            return carry

        lax.fori_loop(0, nt, tile, 0)

        @pl.when(f == nf - 1)
        def _():
            def store(t, carry):
                cp = pltpu.make_async_copy(acc.at[t], ys_hbm.at[pl.ds(row0 + t * MOE_TM, MOE_TM)], sem)
                cp.start()
                cp.wait()
                return carry

            lax.fori_loop(0, nt, store, 0)


def _moe_experts(item_e, item_t0, item_nt, xs, w_gu, b_gu, w_down, b_down, n_items):
    n_rows = xs.shape[0]
    nf = D_EXPERT // MOE_TF

    def fcol(i, f, ie, it0, int_):
        return jnp.where(int_[i] > 0, f, nf - 1)

    in_specs = [
        pl.BlockSpec(memory_space=pl.ANY),
        pl.BlockSpec((1, D_MODEL, MOE_TF), lambda i, f, ie, it0, int_: (ie[i], 0, fcol(i, f, ie, it0, int_))),
        pl.BlockSpec((1, D_MODEL, MOE_TF),
                     lambda i, f, ie, it0, int_: (ie[i], 0, nf + fcol(i, f, ie, it0, int_))),
        pl.BlockSpec((1, MOE_TF, D_MODEL), lambda i, f, ie, it0, int_: (ie[i], fcol(i, f, ie, it0, int_), 0)),
        pl.BlockSpec((1, 1, MOE_TF), lambda i, f, ie, it0, int_: (ie[i], 0, fcol(i, f, ie, it0, int_))),
        pl.BlockSpec((1, 1, MOE_TF), lambda i, f, ie, it0, int_: (ie[i], 0, nf + fcol(i, f, ie, it0, int_))),
        pl.BlockSpec((1, 1, D_MODEL), lambda i, f, ie, it0, int_: (ie[i], 0, 0)),
    ]
    return pl.pallas_call(
        _moe_kernel,
        out_shape=jax.ShapeDtypeStruct((n_rows, D_MODEL), F32),
        grid_spec=pltpu.PrefetchScalarGridSpec(
            num_scalar_prefetch=3,
            grid=(n_items, nf),
            in_specs=in_specs,
            out_specs=pl.BlockSpec(memory_space=pl.ANY),
            scratch_shapes=[pltpu.VMEM((MOE_RT, MOE_TM, D_MODEL), BF16),
                            pltpu.VMEM((MOE_RT, MOE_TM, D_MODEL), F32),
                            pltpu.VMEM((MOE_TM, D_MODEL), F32),
                            pltpu.VMEM((D_MODEL, MOE_TF), BF16),
                            pltpu.VMEM((D_MODEL, MOE_TF), BF16),
                            pltpu.VMEM((MOE_TF, D_MODEL), BF16),
                            pltpu.SemaphoreType.DMA(())]),
        input_output_aliases={3: 0},
        compiler_params=_cparams(("arbitrary", "arbitrary")),
        name="moe_experts",
    )(item_e, item_t0, item_nt, xs, w_gu, w_gu, w_down, b_gu, b_gu, b_down)


COMBINE_TT = 64


def _combine_kernel(pos_ref, h_ref, gate_ref, ys_hbm, o_ref, buf, sem):
    t = pl.program_id(0)
    n = COMBINE_TT * TOP_K

    def start(q, carry):
        row = pos_ref[t * n + q]
        pltpu.make_async_copy(ys_hbm.at[pl.ds(row, 1)], buf.at[q % TOP_K, pl.ds(q // TOP_K, 1)], sem).start()
        return carry

    lax.fori_loop(0, n, start, 0)

    def wait(q, carry):
        pltpu.make_async_copy(ys_hbm.at[pl.ds(0, 1)], buf.at[q % TOP_K, pl.ds(q // TOP_K, 1)], sem).wait()
        return carry

    lax.fori_loop(0, n, wait, 0)
    gates = gate_ref[...]
    out = h_ref[...]
    for kk in range(TOP_K):
        out = out + gates[:, kk:kk + 1] * buf[kk]
    o_ref[...] = out


def _moe_combine(pos, h, gates, ys):
    t = h.shape[0]
    return pl.pallas_call(
        _combine_kernel,
        out_shape=jax.ShapeDtypeStruct((t, D_MODEL), F32),
        grid_spec=pltpu.PrefetchScalarGridSpec(
            num_scalar_prefetch=1,
            grid=(t // COMBINE_TT,),
            in_specs=[pl.BlockSpec((COMBINE_TT, D_MODEL), lambda i, p: (i, 0)),
                      pl.BlockSpec((COMBINE_TT, LANE), lambda i, p: (i, 0)),
                      pl.BlockSpec(memory_space=pl.ANY)],
            out_specs=pl.BlockSpec((COMBINE_TT, D_MODEL), lambda i, p: (i, 0)),
            scratch_shapes=[pltpu.VMEM((TOP_K, COMBINE_TT, D_MODEL), F32),
                            pltpu.SemaphoreType.DMA(())]),
        compiler_params=_cparams(("arbitrary",)),
        name="moe_combine",
    )(pos, h, gates, ys)


def _ple_kernel(h_ref, p_ref, np_ref, wg_ref, wp_ref, nfin_ref, o_ref):
    h = h_ref[...]
    gate = _sigmoid(_dot(_rmsnorm(h, np_ref[...]), wg_ref[...]))
    h = h + gate * _dot(p_ref[...], wp_ref[...])
    o_ref[...] = _rmsnorm(h, nfin_ref[...])


def _ple_final(h, p, n_ple, wg, wp, n_fin, tm):
    t = h.shape[0]
    row = lambda n: pl.BlockSpec((tm, n), lambda i: (i, 0))
    full = lambda a, b: pl.BlockSpec((a, b), lambda i: (0, 0))
    return pl.pallas_call(
        _ple_kernel,
        out_shape=jax.ShapeDtypeStruct((t, D_MODEL), F32),
        grid=(t // tm,),
        in_specs=[row(D_MODEL), row(D_PLE), full(1, D_MODEL), full(D_MODEL, D_MODEL),
                  full(D_PLE, D_MODEL), full(1, D_MODEL)],
        out_specs=row(D_MODEL),
        compiler_params=_cparams(("parallel",)),
        name="ple_final",
    )(h, p, n_ple, wg, wp, n_fin)


def _pad_shift_cols(a):
    def z(n):
        return jnp.zeros(a.shape[:-1] + (n,), a.dtype)
    c0 = 3 * D_RWKV
    c1 = c0 + LORA_W
    c2 = c1 + LORA_A
    return jnp.concatenate([a[..., :c0], a[..., c0:c1], z(LW_PAD - LORA_W), a[..., c1:c2], z(LA_PAD - LORA_A),
                            a[..., c2:], z(LG_PAD - LORA_G)], axis=-1)


def _unpad_shift_cols(a):
    return jnp.concatenate([a[..., :OFF_WD], a[..., OFF_WD:OFF_WD + LORA_W], a[..., OFF_AD:OFF_AD + LORA_A],
                            a[..., OFF_GD:OFF_GD + LORA_G]], axis=-1)


def _pad_rows(a, n):
    return jnp.concatenate([a, jnp.zeros((n - a.shape[0],) + a.shape[1:], a.dtype)], axis=0)


def _routing(idx, src_rows, n_rows_cap, n_items_cap):
    t = idx.shape[0]
    na = t * TOP_K
    flat_e = idx.reshape(na)
    onehot = (flat_e[:, None] == jnp.arange(N_EXPERTS, dtype=I32)[None, :]).astype(I32)
    cs = jnp.cumsum(onehot, axis=0)
    rank = jnp.take_along_axis(cs, flat_e[:, None], axis=1)[:, 0] - 1
    counts = cs[-1]
    ptiles = (counts + MOE_TM - 1) // MOE_TM
    pend = jnp.cumsum(ptiles)
    pstart = pend - ptiles
    pos = pstart[flat_e] * MOE_TM + rank
    row_tok = jnp.zeros((n_rows_cap,), I32).at[pos].set(jnp.repeat(src_rows, TOP_K))
    n_used = pend[-1:].astype(I32)
    items_per_e = (ptiles + MOE_RT - 1) // MOE_RT
    iend = jnp.cumsum(items_per_e)
    istart = iend - items_per_e
    ii = jnp.arange(n_items_cap, dtype=I32)
    e_of = jnp.minimum(jnp.searchsorted(iend, ii, side="right"), N_EXPERTS - 1).astype(I32)
    jj = ii - istart[e_of]
    used = ii < iend[-1]
    item_nt = jnp.where(used, jnp.clip(ptiles[e_of] - jj * MOE_RT, 0, MOE_RT), 0).astype(I32)
    item_t0 = jnp.where(used, pstart[e_of] + jj * MOE_RT, 0).astype(I32)
    last_e = e_of[jnp.maximum(iend[-1] - 1, 0)]
    item_e = jnp.where(used, e_of, last_e).astype(I32)
    return pos.astype(I32), row_tok, n_used, item_e, item_t0, item_nt


def kernel(x_prompt, x_sample, p_prompt, p_sample, state_rwkv_shift, state_rwkv, state_hgrn, norm_mix, w_in,
           mu_shift, w0, w_up, a0, a_up, g_up, k_k, k_a, r_k, lnx_w, lnx_b, hgrn_lb, hgrn_norm, w_out,
           norm_ffn, w_router, b_router, w_gu, b_gu, w_down, b_down, norm_ple, w_ple_gate, w_ple_proj,
           norm_final):
    depth = w_in.shape[0]
    assert depth == 1
    li = 0
    bp, lp = x_prompt.shape[0], x_prompt.shape[1]
    bs, ls = x_sample.shape[0], x_sample.shape[1]
    ls_pad = SUBLANE
    tp = bp * lp

    w_in_p = jnp.concatenate([_pad_shift_cols(w_in[li][:, :D_SHIFT]), w_in[li][:, D_SHIFT:]], axis=1).astype(BF16)
    row = lambda a: a.reshape(1, -1).astype(F32)
    pp = {
        "mu": row(_pad_shift_cols(mu_shift[li])),
        "w0": row(w0[li]), "a0": row(a0[li]), "k_k": row(k_k[li]), "k_a": row(k_a[li]),
        "w_up": _pad_rows(w_up[li], LW_PAD).astype(BF16),
        "a_up": _pad_rows(a_up[li], LA_PAD).astype(BF16),
        "g_up": _pad_rows(g_up[li], LG_PAD).astype(BF16),
    }
    rk = row(r_k[li])
    lnw = row(lnx_w[li])
    lnb = row(lnx_b[li])
    lower = jax.nn.softmax(hgrn_lb.astype(F32), axis=0)
    lb = row(jnp.cumsum(lower, axis=0)[li])
    nw = row(hgrn_norm[li])
    wo = w_out[li].astype(BF16)
    nf = row(norm_ffn[li])
    wr = jnp.concatenate([w_router[li], jnp.zeros((D_MODEL, LANE - N_EXPERTS), F32)], axis=1)
    br = jnp.concatenate([b_router[li], jnp.zeros((LANE - N_EXPERTS,), F32)]).reshape(1, LANE)
    n_ple = row(norm_ple[li])
    wpg = w_ple_gate[li].astype(BF16)
    wpp = w_ple_proj[li].astype(BF16)
    n_fin = row(norm_final)
    g_mix = row(norm_mix[li])

    def mixer(x2d, shift_prev, s_rwkv, s_hgrn, bn, l_total, l_valid, tm_in, bb_prep, tt, bb_scan, c, tm_out):
        proj = _inproj(x2d, g_mix, w_in_p, tm_in, 768)
        streams = _rwkv_prep(proj, shift_prev, pp, bn, l_total, l_valid, bb_prep, tt)
        s0r = None if s_rwkv is None else s_rwkv.reshape(bn, N_UNITS, LANE, RWKV_HEAD)
        yr, s_rwkv_new = _rwkv_scan(streams, rk, lnw, lnb, s0r, bn, l_total, bb_scan, c)
        yh, s_hgrn_new = _hgrn_scan(proj, lb, nw, s_hgrn, bn, l_total, l_valid, bb_scan, c)
        h1, xn2, idx, gates = _outproj_router(x2d, yr.reshape(bn * l_total, D_RWKV),
                                              yh.reshape(bn * l_total, D_HGRN), wo, nf, wr, br, tm_out)
        new_shift = _unpad_shift_cols(proj.reshape(bn, l_total, D_IN_PAD)[:, l_valid - 1, :D_SHIFT_PAD])
        return (h1, xn2, idx, gates, new_shift,
                s_rwkv_new.reshape(bn, N_RWKV_HEADS, RWKV_HEAD, RWKV_HEAD), s_hgrn_new)

    xp2 = x_prompt.reshape(tp, D_MODEL)
    zero_shift = jnp.zeros((bp, 1, D_SHIFT_PAD), F32)
    h1p, xn2p, idxp, gatesp, shift_p, rwkv_p, hgrn_p = mixer(
        xp2, zero_shift, None, None, bp, lp, lp, 512, 1, 256, bp, 64, 256)

    xs_pad = jnp.concatenate([x_sample, jnp.zeros((bs, ls_pad - ls, D_MODEL), F32)], axis=1)
    xs2 = xs_pad.reshape(bs * ls_pad, D_MODEL)
    shift_s0 = _pad_shift_cols(state_rwkv_shift[li]).reshape(bs, 1, D_SHIFT_PAD)
    h1s, xn2s, idxs, gatess, shift_s, rwkv_s, hgrn_s = mixer(
        xs2, shift_s0, state_rwkv[li], state_hgrn[li], bs, ls_pad, ls, 512, 16, ls_pad, 16, ls_pad, 256)

    def compact(a):
        return a.reshape(bs, ls_pad, a.shape[-1])[:, :ls].reshape(bs * ls, a.shape[-1])

    h1s, xn2s, idxs, gatess = compact(h1s), compact(xn2s), compact(idxs), compact(gatess)
    ts = bs * ls

    t_all = tp + ts
    idx_all = jnp.concatenate([idxp[:, :TOP_K], idxs[:, :TOP_K]], axis=0)
    n_tiles_cap = -(-(t_all * TOP_K) // MOE_TM) + N_EXPERTS
    n_rows_cap = n_tiles_cap * MOE_TM
    n_items_cap = N_EXPERTS + n_tiles_cap // MOE_RT
    pos, row_tok, n_used, item_e, item_t0, item_nt = _routing(
        idx_all, jnp.arange(t_all, dtype=I32), n_rows_cap, n_items_cap)
    xsorted = _moe_gather(row_tok, n_used, xn2p, xn2s, n_rows_cap)
    ysorted = _moe_experts(item_e, item_t0, item_nt, xsorted, w_gu[li], b_gu[li].reshape(N_EXPERTS, 1, -1),
                           w_down[li], b_down[li].reshape(N_EXPERTS, 1, -1), n_items_cap)
    h2p = _moe_combine(pos[:tp * TOP_K], h1p, gatesp, ysorted)
    h2s = _moe_combine(pos[tp * TOP_K:], h1s, gatess, ysorted)

    y_p = _ple_final(h2p, p_prompt[li].reshape(tp, D_PLE), n_ple, wpg, wpp, n_fin, 256)
    y_s = _ple_final(h2s, p_sample[li].reshape(ts, D_PLE), n_ple, wpg, wpp, n_fin, 256)

    return (y_p.reshape(bp, lp, D_MODEL), y_s.reshape(bs, ls, D_MODEL),
            shift_p[None], rwkv_p[None], hgrn_p[None],
            shift_s[None], rwkv_s[None], hgrn_s[None])
```

```python
import functools

import jax
import jax.numpy as jnp
from jax import lax
from jax.experimental import pallas as pl
from jax.experimental.pallas import tpu as pltpu

F32 = jnp.float32
BF16 = jnp.bfloat16
I32 = jnp.int32

D_MODEL = 2048
D_RWKV = 1024
D_HGRN = 1024
RWKV_HEAD = 64
N_RWKV_HEADS = 16
HGRN_HEAD = 128
N_HGRN_HEADS = 8
LORA_W = 64
LORA_A = 64
LORA_G = 160
D_SHIFT = 3 * D_RWKV + LORA_W + LORA_A + LORA_G
N_EXPERTS = 32
TOP_K = 4
D_EXPERT = 2048
SWIGLU_LIMIT = 7.0
SWIGLU_ALPHA = 1.702
D_PLE = 256
RMS_EPS = 1e-6
GN_EPS = 64e-5
HGRN_NORM_EPS = 1e-5

LANE = 128
SUBLANE = 8
N_UNITS = 8

LW_PAD = LANE
LA_PAD = LANE
LG_PAD = 2 * LANE
OFF_WD = 3 * D_RWKV
OFF_AD = OFF_WD + LW_PAD
OFF_GD = OFF_AD + LA_PAD
D_SHIFT_PAD = OFF_GD + LG_PAD
D_IN_PAD = D_SHIFT_PAD + 4 * D_HGRN

MOE_TM = 256
MOE_RT = 6
MOE_TF = 256
VMEM_LIMIT = 56 * 1024 * 1024


def _cparams(sem, vmem=VMEM_LIMIT):
    return pltpu.CompilerParams(dimension_semantics=sem, vmem_limit_bytes=vmem)


def _rmsnorm(x, g):
    return x * lax.rsqrt(jnp.mean(x * x, axis=-1, keepdims=True) + RMS_EPS) * g


def _dot(a, b):
    return jnp.dot(a.astype(BF16), b.astype(BF16), preferred_element_type=F32)


def _dot_nt(a, b):
    return lax.dot_general(a.astype(BF16), b.astype(BF16), (((1,), (1,)), ((), ())),
                           preferred_element_type=F32)


def _dot_tn(a, b):
    return lax.dot_general(a.astype(BF16), b.astype(BF16), (((0,), (0,)), ((), ())),
                           preferred_element_type=F32)


def _split3(x):
    h = x.astype(BF16)
    r = x - h.astype(F32)
    m = r.astype(BF16)
    l = (r - m.astype(F32)).astype(BF16)
    return h, m, l


def _dot_exact_rhs(a, b_bf16):
    h, m, l = _split3(a)
    d = functools.partial(jnp.dot, preferred_element_type=F32)
    return d(h, b_bf16) + d(m, b_bf16) + d(l, b_bf16)


def _dot_exact_lhs(a_bf16, b):
    h, m, l = _split3(b)
    d = functools.partial(jnp.dot, preferred_element_type=F32)
    return d(a_bf16, h) + d(a_bf16, m) + d(a_bf16, l)


def _iota2(shape, dim):
    return lax.broadcasted_iota(I32, shape, dim)


def _cumsum_time(x):
    c = x.shape[0]
    tri = (_iota2((c, c), 0) >= _iota2((c, c), 1)).astype(BF16)
    return _dot_exact_lhs(tri, x)


def _same_head_mask():
    return (_iota2((LANE, LANE), 0) >= RWKV_HEAD) == (_iota2((LANE, LANE), 1) >= RWKV_HEAD)


def _sigmoid(x):
    return 1.0 / (1.0 + jnp.exp(-x))


def _inproj_kernel(x_ref, g_ref, w_ref, o_ref, xn_ref):
    @pl.when(pl.program_id(1) == 0)
    def _():
        xn_ref[...] = _rmsnorm(x_ref[...], g_ref[...]).astype(BF16)

    o_ref[...] = jnp.dot(xn_ref[...], w_ref[...], preferred_element_type=F32)


def _inproj(x, g, w, tm, tn):
    t, d = x.shape
    n = w.shape[1]
    return pl.pallas_call(
        _inproj_kernel,
        out_shape=jax.ShapeDtypeStruct((t, n), F32),
        grid=(t // tm, n // tn),
        in_specs=[pl.BlockSpec((tm, d), lambda i, j: (i, 0)),
                  pl.BlockSpec((1, d), lambda i, j: (0, 0)),
                  pl.BlockSpec((d, tn), lambda i, j: (0, j))],
        out_specs=pl.BlockSpec((tm, tn), lambda i, j: (i, j)),
        scratch_shapes=[pltpu.VMEM((tm, d), BF16)],
        compiler_params=_cparams(("parallel", "arbitrary")),
        name="inproj",
    )(x, g, w)


def _rwkv_prep_kernel(x_ref, p8_ref, sh_ref, mu_ref, w0_ref, wup_ref, a0_ref, aup_ref, gup_ref,
                      kk_ref, ka_ref,
                      r_o, k_o, v_o, lw_o, a_o, b_o, g_o, *, bb, tt, l_valid, l_total):
    ti = pl.program_id(1)
    w = D_SHIFT_PAD
    x3 = x_ref[...].reshape(bb, tt, w)
    rolled = pltpu.roll(x3, 1, axis=1)
    prev_tail = p8_ref[...].reshape(bb, SUBLANE, w)[:, SUBLANE - 1:SUBLANE, :]
    first = jnp.where(ti == 0, sh_ref[...], prev_tail)
    t_in = _iota2((bb, tt, w), 1)
    prev = jnp.where(t_in == 0, first, rolled)
    xs = (x3 + (prev - x3) * mu_ref[...]).reshape(bb * tt, w)

    r = xs[:, 0:D_RWKV]
    k = xs[:, D_RWKV:2 * D_RWKV]
    v = xs[:, 2 * D_RWKV:3 * D_RWKV]
    wd = xs[:, OFF_WD:OFF_WD + LW_PAD]
    ad = xs[:, OFF_AD:OFF_AD + LA_PAD]
    gd = xs[:, OFF_GD:OFF_GD + LG_PAD]

    z = -(w0_ref[...] + _dot(jnp.tanh(wd), wup_ref[...]))
    softplus = jnp.maximum(z, 0.0) + jnp.log(1.0 + jnp.exp(-jnp.abs(z)))
    lw = -jnp.exp(-softplus - 0.5)
    asig = _sigmoid(a0_ref[...] + _dot(ad, aup_ref[...]))
    g = _dot(_sigmoid(gd), gup_ref[...])

    kk = k * kk_ref[...]
    same_head = _same_head_mask().astype(BF16)
    sq = kk * kk
    ssq = jnp.concatenate(
        [_dot_exact_rhs(sq[:, u * LANE:(u + 1) * LANE], same_head) for u in range(N_UNITS)], axis=1)
    kkn = kk / jnp.maximum(jnp.sqrt(ssq), 1e-12)
    k2 = k * (1.0 + (asig - 1.0) * ka_ref[...])
    a_vec = -kkn
    b_vec = kkn * asig

    if l_valid < l_total:
        t_glob = (ti * tt + _iota2((bb, tt, D_RWKV), 1)).reshape(bb * tt, D_RWKV)
        ok = t_glob < l_valid
        zero = jnp.zeros_like(k2)
        lw, k2, v, a_vec, b_vec = (jnp.where(ok, t, zero) for t in (lw, k2, v, a_vec, b_vec))

    r_o[...] = r
    k_o[...] = k2
    v_o[...] = v
    lw_o[...] = lw
    a_o[...] = a_vec
    b_o[...] = b_vec
    g_o[...] = g


def _rwkv_prep(proj, shift_pad, pp, bn, l_total, l_valid, bb, tt):
    nt = l_total // tt
    rows = bb * tt
    w = D_SHIFT_PAD
    row_spec = pl.BlockSpec((rows, w), lambda bi, ti: (bi * nt + ti, 0))
    p8_spec = pl.BlockSpec((bb * SUBLANE, w),
                           lambda bi, ti: (jnp.maximum((bi * nt + ti) * (tt // SUBLANE) - 1, 0), 0))
    vec = lambda n: pl.BlockSpec((1, n), lambda bi, ti: (0, 0))
    mat = lambda a, b: pl.BlockSpec((a, b), lambda bi, ti: (0, 0))
    out_spec = pl.BlockSpec((rows, D_RWKV), lambda bi, ti: (bi * nt + ti, 0))
    out_sds = jax.ShapeDtypeStruct((bn * l_total, D_RWKV), F32)
    kern = functools.partial(_rwkv_prep_kernel, bb=bb, tt=tt, l_valid=l_valid, l_total=l_total)
    return pl.pallas_call(
        kern,
        out_shape=[out_sds] * 7,
        grid=(bn // bb, nt),
        in_specs=[row_spec, p8_spec,
                  pl.BlockSpec((bb, 1, w), lambda bi, ti: (bi, 0, 0)),
                  vec(w), vec(D_RWKV), mat(LW_PAD, D_RWKV), vec(D_RWKV), mat(LA_PAD, D_RWKV),
                  mat(LG_PAD, D_RWKV), vec(D_RWKV), vec(D_RWKV)],
        out_specs=[out_spec] * 7,
        compiler_params=_cparams(("parallel", "arbitrary")),
        name="rwkv_prep",
    )(proj, proj, shift_pad, pp["mu"], pp["w0"], pp["w_up"], pp["a0"], pp["a_up"], pp["g_up"],
      pp["k_k"], pp["k_a"])


def _rwkv_scan_kernel(*refs, bb, ub, c, has_s0):
    if has_s0:
        (r_ref, k_ref, v_ref, lw_ref, a_ref, b_ref, g_ref, rk_ref, lnw_ref, lnb_ref, s0_ref,
         y_ref, so_ref, s_scr) = refs
    else:
        (r_ref, k_ref, v_ref, lw_ref, a_ref, b_ref, g_ref, rk_ref, lnw_ref, lnb_ref,
         y_ref, so_ref, s_scr) = refs
        s0_ref = None
    ci = pl.program_id(2)
    n_chunks = pl.num_programs(2)

    lane = _iota2((1, LANE), 1)
    m0 = (lane < RWKV_HEAD).astype(F32)
    m1 = 1.0 - m0
    bd_mask = _same_head_mask().astype(F32)
    row_lo = _iota2((LANE, LANE), 0) < RWKV_HEAD

    c2 = 2 * c
    ri = _iota2((c2, c2), 0)
    cj = _iota2((c2, c2), 1)
    same_blk = (ri >= c) == (cj >= c)
    mask_s = jnp.where(same_blk, (ri > cj).astype(F32), 0.0)
    mask_i = jnp.where(same_blk, (ri >= cj).astype(F32), 0.0)
    eye = (ri == cj).astype(F32)
    n_sq = max((c - 1).bit_length() - 1, 0)

    head_avg = bd_mask.astype(BF16)

    chains = [(j, w) for j in range(bb) for w in range(ub)]
    seqs = range(len(chains))

    @pl.when(ci == 0)
    def _():
        for n, (j, w) in enumerate(chains):
            if has_s0:
                s = s0_ref[j, w]
                s_scr[n] = jnp.concatenate([s, s], axis=1) * bd_mask
            else:
                s_scr[n] = jnp.zeros((LANE, LANE), F32)

    stack2 = lambda lo, hi: jnp.concatenate([lo, hi], axis=0)
    unit = lambda ref, j, w: ref[j, :, w * LANE:(w + 1) * LANE]
    vec = lambda ref, w: ref[:, w * LANE:(w + 1) * LANE]
    r = [unit(r_ref, j, w) for j, w in chains]
    k = [unit(k_ref, j, w) for j, w in chains]
    v = [unit(v_ref, j, w) for j, w in chains]
    lw = [unit(lw_ref, j, w) for j, w in chains]
    a = [unit(a_ref, j, w) for j, w in chains]
    b = [unit(b_ref, j, w) for j, w in chains]
    cum = [_cumsum_time(x) for x in lw]
    clast = [x[c - 1:c, :] for x in cum]
    p_inv = [jnp.exp(-x) for x in cum]
    a2 = [stack2(a[j] * jnp.exp(cum[j] - lw[j]) * m0, a[j] * jnp.exp(cum[j] - lw[j]) * m1) for j in seqs]
    r2 = [stack2(r[j] * jnp.exp(cum[j]) * m0, r[j] * jnp.exp(cum[j]) * m1) for j in seqs]
    b2 = [stack2(b[j] * p_inv[j], b[j] * p_inv[j]) for j in seqs]
    k2 = [stack2(k[j] * p_inv[j], k[j] * p_inv[j]) for j in seqs]
    v2 = [stack2(v[j] * m0, v[j] * m1) for j in seqs]
    lab = [mask_s * _dot_nt(a2[j], b2[j]) for j in seqs]
    lak = [mask_s * _dot_nt(a2[j], k2[j]) for j in seqs]
    rb = [mask_i * _dot_nt(r2[j], b2[j]) for j in seqs]
    rkm = [mask_i * _dot_nt(r2[j], k2[j]) for j in seqs]
    tinv = [eye + x for x in lab]
    xp = lab
    for _ in range(n_sq):
        xp = [_dot(x, x) for x in xp]
        tinv = [tinv[j] + _dot(tinv[j], xp[j]) for j in seqs]
    lakv = [_dot(lak[j], v2[j]) for j in seqs]
    rkv = [_dot(rkm[j], v2[j]) for j in seqs]
    s0 = [s_scr[n] for n in seqs]
    ar_s0 = [_dot_nt(stack2(a2[n], r2[n]), s0[n]) for n in seqs]
    u2 = [_dot(tinv[n], ar_s0[n][:c2] + lakv[n]) for n in seqs]
    y2 = [ar_s0[n][c2:] + _dot(rb[n], u2[n]) + rkv[n] for n in seqs]
    y = [x[:c] + x[c:] for x in y2]
    u = [x[:c] + x[c:] for x in u2]
    p_last = [jnp.exp(clast[n] - cum[n]) for n in seqs]
    for n in seqs:
        s_scr[n] = s0[n] * jnp.exp(clast[n]) + bd_mask * _dot_tn(
            stack2(u[n], v[n]), stack2(b[n] * p_last[n], k[n] * p_last[n]))

    mu = [_dot(x, head_avg) * (1.0 / RWKV_HEAD) for x in y]
    dlt = [y[n] - mu[n] for n in seqs]
    var = [_dot(x * x, head_avg) * (1.0 / RWKV_HEAD) for x in dlt]
    bonus = [_dot(r[n] * k[n] * vec(rk_ref, w), head_avg) * v[n] for n, (j, w) in enumerate(chains)]
    for n, (j, w) in enumerate(chains):
        yn = dlt[n] * lax.rsqrt(var[n] + GN_EPS) * vec(lnw_ref, w) + vec(lnb_ref, w)
        y_ref[j, :, w * LANE:(w + 1) * LANE] = (yn + bonus[n]) * unit(g_ref, j, w)

    @pl.when(ci == n_chunks - 1)
    def _():
        for n, (j, w) in enumerate(chains):
            s_fin = s_scr[n]
            folded = jnp.where(row_lo, s_fin, pltpu.roll(s_fin, RWKV_HEAD, axis=1))
            so_ref[j, w] = folded[:, :RWKV_HEAD]


def _rwkv_scan(streams, rk, lnw, lnb, s0, bn, l_total, bb, ub, c):
    nc = l_total // c
    has_s0 = s0 is not None
    blk = pl.BlockSpec((bb, c, ub * LANE), lambda bi, u, ci: (bi, ci, u))
    vec = pl.BlockSpec((1, ub * LANE), lambda bi, u, ci: (0, u))
    st = pl.BlockSpec((bb, ub, LANE, RWKV_HEAD), lambda bi, u, ci: (bi, u, 0, 0))
    in_specs = [blk] * 7 + [vec] * 3 + ([st] if has_s0 else [])
    args = [s.reshape(bn, l_total, D_RWKV) for s in streams] + [rk, lnw, lnb] + ([s0] if has_s0 else [])
    kern = functools.partial(_rwkv_scan_kernel, bb=bb, ub=ub, c=c, has_s0=has_s0)
    return pl.pallas_call(
        kern,
        out_shape=[jax.ShapeDtypeStruct((bn, l_total, D_RWKV), F32),
                   jax.ShapeDtypeStruct((bn, N_UNITS, LANE, RWKV_HEAD), F32)],
        grid=(bn // bb, N_UNITS // ub, nc),
        in_specs=in_specs,
        out_specs=[blk, st],
        scratch_shapes=[pltpu.VMEM((bb * ub, LANE, LANE), F32)],
        compiler_params=_cparams(("parallel", "parallel", "arbitrary")),
        name="rwkv_scan",
    )(*args)


def _hgrn_kernel(*refs, bb, ub, c, has_s0, l_valid, l_total):
    if has_s0:
        q_ref, f_ref, i_ref, og_ref, lb_ref, nw_ref, s0_ref, y_ref, so_ref, s_scr = refs
    else:
        q_ref, f_ref, i_ref, og_ref, lb_ref, nw_ref, y_ref, so_ref, s_scr = refs
        s0_ref = None
    ci = pl.program_id(2)
    n_chunks = pl.num_programs(2)
    tri = (_iota2((c, c), 0) >= _iota2((c, c), 1)).astype(F32)
    mid = max(c // 2 - 1, 0)
    chains = [(j, w) for j in range(bb) for w in range(ub)]
    seqs = range(len(chains))
    unit = lambda ref, j, w: ref[j, :, w * LANE:(w + 1) * LANE]

    @pl.when(ci == 0)
    def _():
        for n, (j, w) in enumerate(chains):
            if has_s0:
                s_scr[n] = s0_ref[j, w].T
            else:
                s_scr[n] = jnp.zeros((LANE, LANE), F32)

    q = [unit(q_ref, j, w) * _sigmoid(unit(q_ref, j, w)) for j, w in chains]
    f = [lb_ref[:, w * LANE:(w + 1) * LANE] + (1.0 - lb_ref[:, w * LANE:(w + 1) * LANE])
         * _sigmoid(unit(f_ref, j, w)) for j, w in chains]
    logf = [jnp.log(x) for x in f]
    kf = [1.0 - x for x in f]
    v = [unit(i_ref, j, w) for j, w in chains]
    if l_valid < l_total:
        ok = (ci * c + _iota2((c, LANE), 0)) < l_valid
        logf = [jnp.where(ok, x, 0.0) for x in logf]
        kf = [jnp.where(ok, x, 0.0) for x in kf]
    cum = [_cumsum_time(x) for x in logf]
    cref = [x[mid:mid + 1, :] for x in cum]
    clast = [x[c - 1:c, :] for x in cum]
    amat = [tri * _dot_nt(q[j] * jnp.exp(cum[j] - cref[j]), kf[j] * jnp.exp(cref[j] - cum[j])) for j in seqs]
    st = [s_scr[j] for j in seqs]
    o = [_dot_nt(q[j] * jnp.exp(cum[j]), st[j]) + _dot(amat[j], v[j]) for j in seqs]
    for j in seqs:
        s_scr[j] = st[j] * jnp.exp(clast[j]) + _dot_tn(v[j], kf[j] * jnp.exp(clast[j] - cum[j]))
    for n, (j, w) in enumerate(chains):
        on = o[n] * lax.rsqrt(jnp.mean(o[n] * o[n], axis=-1, keepdims=True) + HGRN_NORM_EPS) * nw_ref[...]
        og = unit(og_ref, j, w)
        y_ref[j, :, w * LANE:(w + 1) * LANE] = on * (og * _sigmoid(og))

    @pl.when(ci == n_chunks - 1)
    def _():
        for n, (j, w) in enumerate(chains):
            so_ref[j, w] = s_scr[n].T


def _hgrn_scan(proj, lb, nw, s0, bn, l_total, l_valid, bb, ub, c):
    nc = l_total // c
    has_s0 = s0 is not None
    base = D_SHIFT_PAD // (ub * LANE)
    per = D_HGRN // (ub * LANE)
    proj3 = proj.reshape(bn, l_total, D_IN_PAD)

    def col(part):
        return pl.BlockSpec((bb, c, ub * LANE), lambda bi, u, ci: (bi, ci, base + part * per + u))

    st = pl.BlockSpec((bb, ub, LANE, LANE), lambda bi, u, ci: (bi, u, 0, 0))
    in_specs = [col(0), col(1), col(2), col(3),
                pl.BlockSpec((1, ub * LANE), lambda bi, u, ci: (0, u)),
                pl.BlockSpec((1, LANE), lambda bi, u, ci: (0, 0))] + ([st] if has_s0 else [])
    args = [proj3, proj3, proj3, proj3, lb, nw] + ([s0] if has_s0 else [])
    kern = functools.partial(_hgrn_kernel, bb=bb, ub=ub, c=c, has_s0=has_s0, l_valid=l_valid,
                             l_total=l_total)
    return pl.pallas_call(
        kern,
        out_shape=[jax.ShapeDtypeStruct((bn, l_total, D_HGRN), F32),
                   jax.ShapeDtypeStruct((bn, N_UNITS, LANE, LANE), F32)],
        grid=(bn // bb, N_UNITS // ub, nc),
        in_specs=in_specs,
        out_specs=[pl.BlockSpec((bb, c, ub * LANE), lambda bi, u, ci: (bi, ci, u)), st],
        scratch_shapes=[pltpu.VMEM((bb * ub, LANE, LANE), F32)],
        compiler_params=_cparams(("parallel", "parallel", "arbitrary")),
        name="hgrn_scan",
    )(*args)


def _outproj_kernel(x_ref, yr_ref, yh_ref, wo_ref, nf_ref, wr_ref, br_ref,
                    h_ref, xn_ref, idx_ref, gate_ref):
    h = (x_ref[...] + _dot(yr_ref[...], wo_ref[0:D_RWKV, :]) + _dot(yh_ref[...], wo_ref[D_RWKV:, :]))
    h_ref[...] = h
    xn = _rmsnorm(h, nf_ref[...])
    xn_ref[...] = xn
    wr = wr_ref[...]
    xh = xn.astype(BF16)
    wh = wr.astype(BF16)
    logits = (_dot(xh, wh) + _dot(xn - xh.astype(F32), wh) + _dot(xh, wr - wh.astype(F32))
              + br_ref[...])
    tm = logits.shape[0]
    lane = _iota2((tm, LANE), 1).astype(F32)
    neg = jnp.float32(-jnp.inf)
    work = jnp.where(lane < N_EXPERTS, logits, neg)
    idx_out = jnp.zeros((tm, LANE), I32)
    val_out = jnp.zeros((tm, LANE), F32)
    top0 = None
    for kk in range(TOP_K):
        m = jnp.max(work, axis=-1, keepdims=True)
        sel = jnp.min(jnp.where(work == m, lane, float(LANE)), axis=-1, keepdims=True)
        if kk == 0:
            top0 = m
        idx_out = jnp.where(lane == kk, sel.astype(I32), idx_out)
        val_out = jnp.where(lane == kk, jnp.exp(m - top0), val_out)
        work = jnp.where(lane == sel, neg, work)
    idx_ref[...] = idx_out
    gate_ref[...] = val_out / jnp.sum(val_out, axis=-1, keepdims=True)


def _outproj_router(x, yr, yh, wo, nf, wr, br, tm):
    t = x.shape[0]
    row = lambda n: pl.BlockSpec((tm, n), lambda i: (i, 0))
    full = lambda a, b: pl.BlockSpec((a, b), lambda i: (0, 0))
    return pl.pallas_call(
        _outproj_kernel,
        out_shape=[jax.ShapeDtypeStruct((t, D_MODEL), F32), jax.ShapeDtypeStruct((t, D_MODEL), F32),
                   jax.ShapeDtypeStruct((t, LANE), I32), jax.ShapeDtypeStruct((t, LANE), F32)],
        grid=(t // tm,),
        in_specs=[row(D_MODEL), row(D_RWKV), row(D_HGRN), full(D_MODEL, D_MODEL), full(1, D_MODEL),
                  full(D_MODEL, LANE), full(1, LANE)],
        out_specs=[row(D_MODEL), row(D_MODEL), row(LANE), row(LANE)],
        compiler_params=_cparams(("parallel",)),
        name="outproj_router",
    )(x, yr, yh, wo, nf, wr, br)


def _row_copy(src_hbm, row, dst_vmem, r, sem):
    return pltpu.make_async_copy(src_hbm.at[pl.ds(row, 1)], dst_vmem.at[pl.ds(r, 1)], sem)


def _gather_kernel(rowtok_ref, nused_ref, xp_hbm, xs_hbm, o_ref, sem, *, n_prompt):
    t = pl.program_id(0)

    @pl.when(t < nused_ref[0])
    def _():
        def start(r, carry):
            tok = rowtok_ref[t * MOE_TM + r]

            @pl.when(tok < n_prompt)
            def _():
                _row_copy(xp_hbm, tok, o_ref, r, sem).start()

            @pl.when(tok >= n_prompt)
            def _():
                _row_copy(xs_hbm, tok - n_prompt, o_ref, r, sem).start()

            return carry

        lax.fori_loop(0, MOE_TM, start, 0)

        def wait(r, carry):
            _row_copy(xp_hbm, 0, o_ref, r, sem).wait()
            return carry

        lax.fori_loop(0, MOE_TM, wait, 0)

    @pl.when(t >= nused_ref[0])
    def _():
        o_ref[...] = jnp.zeros_like(o_ref)


def _moe_gather(row_tok, n_used, xn_p, xn_s, n_rows):
    n_tiles = n_rows // MOE_TM
    kern = functools.partial(_gather_kernel, n_prompt=xn_p.shape[0])
    return pl.pallas_call(
        kern,
        out_shape=jax.ShapeDtypeStruct((n_rows, D_MODEL), F32),
        grid_spec=pltpu.PrefetchScalarGridSpec(
            num_scalar_prefetch=2,
            grid=(n_tiles,),
            in_specs=[pl.BlockSpec(memory_space=pl.ANY), pl.BlockSpec(memory_space=pl.ANY)],
            out_specs=pl.BlockSpec((MOE_TM, D_MODEL), lambda t, rt, nu: (t, 0)),
            scratch_shapes=[pltpu.SemaphoreType.DMA(())]),
        compiler_params=_cparams(("arbitrary",)),
        name="moe_gather",
    )(row_tok, n_used, xn_p, xn_s)


def _moe_kernel(ie_ref, it0_ref, int_ref, xs_hbm, wg_ref, wu_ref, wd_ref, bg_ref, bu_ref, bd_ref,
                ys_hbm, xbuf, acc, stage, wgb, wub, wdb, sem):
    i = pl.program_id(0)
    f = pl.program_id(1)
    nf = pl.num_programs(1)
    nt = int_ref[i]
    row0 = it0_ref[i] * MOE_TM

    @pl.when(nt > 0)
    def _():
        @pl.when(f == 0)
        def _():
            def load(t, carry):
                cp = pltpu.make_async_copy(xs_hbm.at[pl.ds(row0 + t * MOE_TM, MOE_TM)], stage, sem)
                cp.start()
                cp.wait()
                xbuf[t] = stage[...].astype(BF16)
                acc[t] = jnp.broadcast_to(bd_ref[0], (MOE_TM, D_MODEL))
                return carry

            lax.fori_loop(0, nt, load, 0)

        wgb[...] = wg_ref[0].astype(BF16)
        wub[...] = wu_ref[0].astype(BF16)
        wdb[...] = wd_ref[0].astype(BF16)
        bg = bg_ref[0]
        bu = bu_ref[0]

        def tile(t, carry):
            x = xbuf[t]
            gate = jnp.dot(x, wgb[...], preferred_element_type=F32) + bg
            up = jnp.dot(x, wub[...], preferred_element_type=F32) + bu
            gate = jnp.minimum(gate, SWIGLU_LIMIT)
            up = jnp.clip(up, -SWIGLU_LIMIT, SWIGLU_LIMIT)
            hid = (up + 1.0) * gate * _sigmoid(SWIGLU_ALPHA * gate)
            acc[t] += jnp.dot(hid.astype(BF16), wdb[...], preferred_element_type=F32)
            return carry

        lax.fori_loop(0, nt, tile, 0)

        @pl.when(f == nf - 1)
        def _():
            def store(t, carry):
                cp = pltpu.make_async_copy(acc.at[t], ys_hbm.at[pl.ds(row0 + t * MOE_TM, MOE_TM)], sem)
                cp.start()
                cp.wait()
                return carry

            lax.fori_loop(0, nt, store, 0)


def _moe_experts(item_e, item_t0, item_nt, xs, w_gu, b_gu, w_down, b_down, n_items):
    n_rows = xs.shape[0]
    nf = D_EXPERT // MOE_TF

    def fcol(i, f, ie, it0, int_):
        return jnp.where(int_[i] > 0, f, nf - 1)

    in_specs = [
        pl.BlockSpec(memory_space=pl.ANY),
        pl.BlockSpec((1, D_MODEL, MOE_TF), lambda i, f, ie, it0, int_: (ie[i], 0, fcol(i, f, ie, it0, int_))),
        pl.BlockSpec((1, D_MODEL, MOE_TF),
                     lambda i, f, ie, it0, int_: (ie[i], 0, nf + fcol(i, f, ie, it0, int_))),
        pl.BlockSpec((1, MOE_TF, D_MODEL), lambda i, f, ie, it0, int_: (ie[i], fcol(i, f, ie, it0, int_), 0)),
        pl.BlockSpec((1, 1, MOE_TF), lambda i, f, ie, it0, int_: (ie[i], 0, fcol(i, f, ie, it0, int_))),
        pl.BlockSpec((1, 1, MOE_TF), lambda i, f, ie, it0, int_: (ie[i], 0, nf + fcol(i, f, ie, it0, int_))),
        pl.BlockSpec((1, 1, D_MODEL), lambda i, f, ie, it0, int_: (ie[i], 0, 0)),
    ]
    return pl.pallas_call(
        _moe_kernel,
        out_shape=jax.ShapeDtypeStruct((n_rows, D_MODEL), F32),
        grid_spec=pltpu.PrefetchScalarGridSpec(
            num_scalar_prefetch=3,
            grid=(n_items, nf),
            in_specs=in_specs,
            out_specs=pl.BlockSpec(memory_space=pl.ANY),
            scratch_shapes=[pltpu.VMEM((MOE_RT, MOE_TM, D_MODEL), BF16),
                            pltpu.VMEM((MOE_RT, MOE_TM, D_MODEL), F32),
                            pltpu.VMEM((MOE_TM, D_MODEL), F32),
                            pltpu.VMEM((D_MODEL, MOE_TF), BF16),
                            pltpu.VMEM((D_MODEL, MOE_TF), BF16),
                            pltpu.VMEM((MOE_TF, D_MODEL), BF16),
                            pltpu.SemaphoreType.DMA(())]),
        input_output_aliases={3: 0},
        compiler_params=_cparams(("arbitrary", "arbitrary")),
        name="moe_experts",
    )(item_e, item_t0, item_nt, xs, w_gu, w_gu, w_down, b_gu, b_gu, b_down)


COMBINE_TT = 64


def _combine_kernel(pos_ref, h_ref, gate_ref, ys_hbm, o_ref, buf, sem):
    t = pl.program_id(0)
    n = COMBINE_TT * TOP_K

    def start(q, carry):
        row = pos_ref[t * n + q]
        pltpu.make_async_copy(ys_hbm.at[pl.ds(row, 1)], buf.at[q % TOP_K, pl.ds(q // TOP_K, 1)], sem).start()
        return carry

    lax.fori_loop(0, n, start, 0)

    def wait(q, carry):
        pltpu.make_async_copy(ys_hbm.at[pl.ds(0, 1)], buf.at[q % TOP_K, pl.ds(q // TOP_K, 1)], sem).wait()
        return carry

    lax.fori_loop(0, n, wait, 0)
    gates = gate_ref[...]
    out = h_ref[...]
    for kk in range(TOP_K):
        out = out + gates[:, kk:kk + 1] * buf[kk]
    o_ref[...] = out


def _moe_combine(pos, h, gates, ys):
    t = h.shape[0]
    return pl.pallas_call(
        _combine_kernel,
        out_shape=jax.ShapeDtypeStruct((t, D_MODEL), F32),
        grid_spec=pltpu.PrefetchScalarGridSpec(
            num_scalar_prefetch=1,
            grid=(t // COMBINE_TT,),
            in_specs=[pl.BlockSpec((COMBINE_TT, D_MODEL), lambda i, p: (i, 0)),
                      pl.BlockSpec((COMBINE_TT, LANE), lambda i, p: (i, 0)),
                      pl.BlockSpec(memory_space=pl.ANY)],
            out_specs=pl.BlockSpec((COMBINE_TT, D_MODEL), lambda i, p: (i, 0)),
            scratch_shapes=[pltpu.VMEM((TOP_K, COMBINE_TT, D_MODEL), F32),
                            pltpu.SemaphoreType.DMA(())]),
        compiler_params=_cparams(("arbitrary",)),
        name="moe_combine",
    )(pos, h, gates, ys)


def _ple_kernel(h_ref, p_ref, np_ref, wg_ref, wp_ref, nfin_ref, o_ref):
    h = h_ref[...]
    gate = _sigmoid(_dot(_rmsnorm(h, np_ref[...]), wg_ref[...]))
    h = h + gate * _dot(p_ref[...], wp_ref[...])
    o_ref[...] = _rmsnorm(h, nfin_ref[...])


def _ple_final(h, p, n_ple, wg, wp, n_fin, tm):
    t = h.shape[0]
    row = lambda n: pl.BlockSpec((tm, n), lambda i: (i, 0))
    full = lambda a, b: pl.BlockSpec((a, b), lambda i: (0, 0))
    return pl.pallas_call(
        _ple_kernel,
        out_shape=jax.ShapeDtypeStruct((t, D_MODEL), F32),
        grid=(t // tm,),
        in_specs=[row(D_MODEL), row(D_PLE), full(1, D_MODEL), full(D_MODEL, D_MODEL),
                  full(D_PLE, D_MODEL), full(1, D_MODEL)],
        out_specs=row(D_MODEL),
        compiler_params=_cparams(("parallel",)),
        name="ple_final",
    )(h, p, n_ple, wg, wp, n_fin)


def _pad_shift_cols(a):
    def z(n):
        return jnp.zeros(a.shape[:-1] + (n,), a.dtype)
    c0 = 3 * D_RWKV
    c1 = c0 + LORA_W
    c2 = c1 + LORA_A
    return jnp.concatenate([a[..., :c0], a[..., c0:c1], z(LW_PAD - LORA_W), a[..., c1:c2], z(LA_PAD - LORA_A),
                            a[..., c2:], z(LG_PAD - LORA_G)], axis=-1)


def _unpad_shift_cols(a):
    return jnp.concatenate([a[..., :OFF_WD], a[..., OFF_WD:OFF_WD + LORA_W], a[..., OFF_AD:OFF_AD + LORA_A],
                            a[..., OFF_GD:OFF_GD + LORA_G]], axis=-1)


def _pad_rows(a, n):
    return jnp.concatenate([a, jnp.zeros((n - a.shape[0],) + a.shape[1:], a.dtype)], axis=0)


def _routing(idx, src_rows, n_rows_cap, n_items_cap):
    t = idx.shape[0]
    na = t * TOP_K
    flat_e = idx.reshape(na)
    onehot = (flat_e[:, None] == jnp.arange(N_EXPERTS, dtype=I32)[None, :]).astype(I32)
    cs = jnp.cumsum(onehot, axis=0)
    rank = jnp.take_along_axis(cs, flat_e[:, None], axis=1)[:, 0] - 1
    counts = cs[-1]
    ptiles = (counts + MOE_TM - 1) // MOE_TM
    pend = jnp.cumsum(ptiles)
    pstart = pend - ptiles
    pos = pstart[flat_e] * MOE_TM + rank
    row_tok = jnp.zeros((n_rows_cap,), I32).at[pos].set(jnp.repeat(src_rows, TOP_K))
    n_used = pend[-1:].astype(I32)
    items_per_e = (ptiles + MOE_RT - 1) // MOE_RT
    iend = jnp.cumsum(items_per_e)
    istart = iend - items_per_e
    ii = jnp.arange(n_items_cap, dtype=I32)
    e_of = jnp.minimum(jnp.searchsorted(iend, ii, side="right"), N_EXPERTS - 1).astype(I32)
    jj = ii - istart[e_of]
    used = ii < iend[-1]
    item_nt = jnp.where(used, jnp.clip(ptiles[e_of] - jj * MOE_RT, 0, MOE_RT), 0).astype(I32)
    item_t0 = jnp.where(used, pstart[e_of] + jj * MOE_RT, 0).astype(I32)
    last_e = e_of[jnp.maximum(iend[-1] - 1, 0)]
    item_e = jnp.where(used, e_of, last_e).astype(I32)
    return pos.astype(I32), row_tok, n_used, item_e, item_t0, item_nt


def kernel(x_prompt, x_sample, p_prompt, p_sample, state_rwkv_shift, state_rwkv, state_hgrn, norm_mix, w_in,
           mu_shift, w0, w_up, a0, a_up, g_up, k_k, k_a, r_k, lnx_w, lnx_b, hgrn_lb, hgrn_norm, w_out,
           norm_ffn, w_router, b_router, w_gu, b_gu, w_down, b_down, norm_ple, w_ple_gate, w_ple_proj,
           norm_final):
    depth = w_in.shape[0]
    assert depth == 1
    li = 0
    bp, lp = x_prompt.shape[0], x_prompt.shape[1]
    bs, ls = x_sample.shape[0], x_sample.shape[1]
    ls_pad = SUBLANE
    tp = bp * lp

    w_in_p = jnp.concatenate([_pad_shift_cols(w_in[li][:, :D_SHIFT]), w_in[li][:, D_SHIFT:]], axis=1).astype(BF16)
    row = lambda a: a.reshape(1, -1).astype(F32)
    pp = {
        "mu": row(_pad_shift_cols(mu_shift[li])),
        "w0": row(w0[li]), "a0": row(a0[li]), "k_k": row(k_k[li]), "k_a": row(k_a[li]),
        "w_up": _pad_rows(w_up[li], LW_PAD).astype(BF16),
        "a_up": _pad_rows(a_up[li], LA_PAD).astype(BF16),
        "g_up": _pad_rows(g_up[li], LG_PAD).astype(BF16),
    }
    rk = row(r_k[li])
    lnw = row(lnx_w[li])
    lnb = row(lnx_b[li])
    lower = jax.nn.softmax(hgrn_lb.astype(F32), axis=0)
    lb = row(jnp.cumsum(lower, axis=0)[li])
    nw = row(hgrn_norm[li])
    wo = w_out[li].astype(BF16)
    nf = row(norm_ffn[li])
    wr = jnp.concatenate([w_router[li], jnp.zeros((D_MODEL, LANE - N_EXPERTS), F32)], axis=1)
    br = jnp.concatenate([b_router[li], jnp.zeros((LANE - N_EXPERTS,), F32)]).reshape(1, LANE)
    n_ple = row(norm_ple[li])
    wpg = w_ple_gate[li].astype(BF16)
    wpp = w_ple_proj[li].astype(BF16)
    n_fin = row(norm_final)
    g_mix = row(norm_mix[li])

    def mixer(x2d, shift_prev, s_rwkv, s_hgrn, bn, l_total, l_valid, tm_in, bb_prep, tt, bb_scan, ub_scan, c,
              tm_out):
        proj = _inproj(x2d, g_mix, w_in_p, tm_in, 768)
        streams = _rwkv_prep(proj, shift_prev, pp, bn, l_total, l_valid, bb_prep, tt)
        s0r = None if s_rwkv is None else s_rwkv.reshape(bn, N_UNITS, LANE, RWKV_HEAD)
        yr, s_rwkv_new = _rwkv_scan(streams, rk, lnw, lnb, s0r, bn, l_total, bb_scan, ub_scan, c)
        yh, s_hgrn_new = _hgrn_scan(proj, lb, nw, s_hgrn, bn, l_total, l_valid, bb_scan, ub_scan, c)
        h1, xn2, idx, gates = _outproj_router(x2d, yr.reshape(bn * l_total, D_RWKV),
                                              yh.reshape(bn * l_total, D_HGRN), wo, nf, wr, br, tm_out)
        new_shift = _unpad_shift_cols(proj.reshape(bn, l_total, D_IN_PAD)[:, l_valid - 1, :D_SHIFT_PAD])
        return (h1, xn2, idx, gates, new_shift,
                s_rwkv_new.reshape(bn, N_RWKV_HEADS, RWKV_HEAD, RWKV_HEAD), s_hgrn_new)

    xp2 = x_prompt.reshape(tp, D_MODEL)
    zero_shift = jnp.zeros((bp, 1, D_SHIFT_PAD), F32)
    h1p, xn2p, idxp, gatesp, shift_p, rwkv_p, hgrn_p = mixer(
        xp2, zero_shift, None, None, bp, lp, lp, 512, 1, 256, bp, 2, 64, 256)

    xs_pad = jnp.concatenate([x_sample, jnp.zeros((bs, ls_pad - ls, D_MODEL), F32)], axis=1)
    xs2 = xs_pad.reshape(bs * ls_pad, D_MODEL)
    shift_s0 = _pad_shift_cols(state_rwkv_shift[li]).reshape(bs, 1, D_SHIFT_PAD)
    h1s, xn2s, idxs, gatess, shift_s, rwkv_s, hgrn_s = mixer(
        xs2, shift_s0, state_rwkv[li], state_hgrn[li], bs, ls_pad, ls, 512, 16, ls_pad, 16, 1, ls_pad, 256)

    def compact(a):
        return a.reshape(bs, ls_pad, a.shape[-1])[:, :ls].reshape(bs * ls, a.shape[-1])

    h1s, xn2s, idxs, gatess = compact(h1s), compact(xn2s), compact(idxs), compact(gatess)
    ts = bs * ls

    t_all = tp + ts
    idx_all = jnp.concatenate([idxp[:, :TOP_K], idxs[:, :TOP_K]], axis=0)
    n_tiles_cap = -(-(t_all * TOP_K) // MOE_TM) + N_EXPERTS
    n_rows_cap = n_tiles_cap * MOE_TM
    n_items_cap = N_EXPERTS + n_tiles_cap // MOE_RT
    pos, row_tok, n_used, item_e, item_t0, item_nt = _routing(
        idx_all, jnp.arange(t_all, dtype=I32), n_rows_cap, n_items_cap)
    xsorted = _moe_gather(row_tok, n_used, xn2p, xn2s, n_rows_cap)
    ysorted = _moe_experts(item_e, item_t0, item_nt, xsorted, w_gu[li], b_gu[li].reshape(N_EXPERTS, 1, -1),
                           w_down[li], b_down[li].reshape(N_EXPERTS, 1, -1), n_items_cap)
    h2p = _moe_combine(pos[:tp * TOP_K], h1p, gatesp, ysorted)
    h2s = _moe_combine(pos[tp * TOP_K:], h1s, gatess, ysorted)

    y_p = _ple_final(h2p, p_prompt[li].reshape(tp, D_PLE), n_ple, wpg, wpp, n_fin, 256)
    y_s = _ple_final(h2s, p_sample[li].reshape(ts, D_PLE), n_ple, wpg, wpp, n_fin, 256)

    return (y_p.reshape(bp, lp, D_MODEL), y_s.reshape(bs, ls, D_MODEL),
            shift_p[None], rwkv_p[None], hgrn_p[None],
            shift_s[None], rwkv_s[None], hgrn_s[None])
```

```python
import functools

import jax
import jax.numpy as jnp
from jax import lax
from jax.experimental import pallas as pl
from jax.experimental.pallas import tpu as pltpu

F32 = jnp.float32
BF16 = jnp.bfloat16
I32 = jnp.int32
U32 = jnp.uint32

D_MODEL = 2048
D_RWKV = 1024
D_HGRN = 1024
RWKV_HEAD = 64
N_RWKV_HEADS = 16
HGRN_HEAD = 128
N_HGRN_HEADS = 8
LORA_W = 64
LORA_A = 64
LORA_G = 160
D_SHIFT = 3 * D_RWKV + LORA_W + LORA_A + LORA_G
N_EXPERTS = 32
TOP_K = 4
D_EXPERT = 2048
SWIGLU_LIMIT = 7.0
SWIGLU_ALPHA = 1.702
D_PLE = 256
RMS_EPS = 1e-6
GN_EPS = 64e-5
HGRN_NORM_EPS = 1e-5

LANE = 128
SUBLANE = 8
N_UNITS = 8

LW_PAD = LANE
LA_PAD = LANE
LG_PAD = 2 * LANE
OFF_WD = 3 * D_RWKV
OFF_AD = OFF_WD + LW_PAD
OFF_GD = OFF_AD + LA_PAD
D_SHIFT_PAD = OFF_GD + LG_PAD
D_IN_PAD = D_SHIFT_PAD + 4 * D_HGRN

MOE_TM = 256
MOE_RT = 6
MOE_TF = 256
VMEM_LIMIT = 56 * 1024 * 1024


def _cparams(sem, vmem=VMEM_LIMIT):
    return pltpu.CompilerParams(dimension_semantics=sem, vmem_limit_bytes=vmem)


def _rmsnorm(x, g):
    return x * lax.rsqrt(jnp.mean(x * x, axis=-1, keepdims=True) + RMS_EPS) * g


def _dot(a, b):
    return jnp.dot(a.astype(BF16), b.astype(BF16), preferred_element_type=F32)


def _dot_nt(a, b):
    return lax.dot_general(a.astype(BF16), b.astype(BF16), (((1,), (1,)), ((), ())),
                           preferred_element_type=F32)


def _dot_tn(a, b):
    return lax.dot_general(a.astype(BF16), b.astype(BF16), (((0,), (0,)), ((), ())),
                           preferred_element_type=F32)


def _split3(x):
    h = x.astype(BF16)
    r = x - h.astype(F32)
    m = r.astype(BF16)
    l = (r - m.astype(F32)).astype(BF16)
    return h, m, l


def _dot_exact_rhs(a, b_bf16):
    h, m, l = _split3(a)
    d = functools.partial(jnp.dot, preferred_element_type=F32)
    return d(h, b_bf16) + d(m, b_bf16) + d(l, b_bf16)


def _dot_exact_lhs(a_bf16, b):
    h, m, l = _split3(b)
    d = functools.partial(jnp.dot, preferred_element_type=F32)
    return d(a_bf16, h) + d(a_bf16, m) + d(a_bf16, l)


def _iota2(shape, dim):
    return lax.broadcasted_iota(I32, shape, dim)


def _cumsum_time(x):
    c = x.shape[0]
    tri = (_iota2((c, c), 0) >= _iota2((c, c), 1)).astype(BF16)
    return _dot_exact_lhs(tri, x)


def _same_head_mask():
    return (_iota2((LANE, LANE), 0) >= RWKV_HEAD) == (_iota2((LANE, LANE), 1) >= RWKV_HEAD)


def _sigmoid(x):
    return 1.0 / (1.0 + jnp.exp(-x))


def _inproj_kernel(x_ref, g_ref, w_ref, o_ref, xn_ref):
    @pl.when(pl.program_id(1) == 0)
    def _():
        xn_ref[...] = _rmsnorm(x_ref[...], g_ref[...]).astype(BF16)

    o_ref[...] = jnp.dot(xn_ref[...], w_ref[...], preferred_element_type=F32)


def _inproj(x, g, w, tm, tn):
    t, d = x.shape
    n = w.shape[1]
    return pl.pallas_call(
        _inproj_kernel,
        out_shape=jax.ShapeDtypeStruct((t, n), F32),
        grid=(t // tm, n // tn),
        in_specs=[pl.BlockSpec((tm, d), lambda i, j: (i, 0)),
                  pl.BlockSpec((1, d), lambda i, j: (0, 0)),
                  pl.BlockSpec((d, tn), lambda i, j: (0, j))],
        out_specs=pl.BlockSpec((tm, tn), lambda i, j: (i, j)),
        scratch_shapes=[pltpu.VMEM((tm, d), BF16)],
        compiler_params=_cparams(("parallel", "arbitrary")),
        name="inproj",
    )(x, g, w)


def _rwkv_prep_kernel(x_ref, p8_ref, sh_ref, mu_ref, w0_ref, wup_ref, a0_ref, aup_ref, gup_ref,
                      kk_ref, ka_ref,
                      r_o, k_o, v_o, lw_o, a_o, b_o, g_o, *, bb, tt, l_valid, l_total):
    ti = pl.program_id(1)
    w = D_SHIFT_PAD
    x3 = x_ref[...].reshape(bb, tt, w)
    rolled = pltpu.roll(x3, 1, axis=1)
    prev_tail = p8_ref[...].reshape(bb, SUBLANE, w)[:, SUBLANE - 1:SUBLANE, :]
    first = jnp.where(ti == 0, sh_ref[...], prev_tail)
    t_in = _iota2((bb, tt, w), 1)
    prev = jnp.where(t_in == 0, first, rolled)
    xs = (x3 + (prev - x3) * mu_ref[...]).reshape(bb * tt, w)

    r = xs[:, 0:D_RWKV]
    k = xs[:, D_RWKV:2 * D_RWKV]
    v = xs[:, 2 * D_RWKV:3 * D_RWKV]
    wd = xs[:, OFF_WD:OFF_WD + LW_PAD]
    ad = xs[:, OFF_AD:OFF_AD + LA_PAD]
    gd = xs[:, OFF_GD:OFF_GD + LG_PAD]

    z = -(w0_ref[...] + _dot(jnp.tanh(wd), wup_ref[...]))
    softplus = jnp.maximum(z, 0.0) + jnp.log(1.0 + jnp.exp(-jnp.abs(z)))
    lw = -jnp.exp(-softplus - 0.5)
    asig = _sigmoid(a0_ref[...] + _dot(ad, aup_ref[...]))
    g = _dot(_sigmoid(gd), gup_ref[...])

    kk = k * kk_ref[...]
    same_head = _same_head_mask().astype(BF16)
    sq = kk * kk
    ssq = jnp.concatenate(
        [_dot_exact_rhs(sq[:, u * LANE:(u + 1) * LANE], same_head) for u in range(N_UNITS)], axis=1)
    kkn = kk / jnp.maximum(jnp.sqrt(ssq), 1e-12)
    k2 = k * (1.0 + (asig - 1.0) * ka_ref[...])
    a_vec = -kkn
    b_vec = kkn * asig

    if l_valid < l_total:
        t_glob = (ti * tt + _iota2((bb, tt, D_RWKV), 1)).reshape(bb * tt, D_RWKV)
        ok = t_glob < l_valid
        zero = jnp.zeros_like(k2)
        lw, k2, v, a_vec, b_vec = (jnp.where(ok, t, zero) for t in (lw, k2, v, a_vec, b_vec))

    r_o[...] = r
    k_o[...] = k2
    v_o[...] = v
    lw_o[...] = lw
    a_o[...] = a_vec
    b_o[...] = b_vec
    g_o[...] = g


def _rwkv_prep(proj, shift_pad, pp, bn, l_total, l_valid, bb, tt):
    nt = l_total // tt
    rows = bb * tt
    w = D_SHIFT_PAD
    row_spec = pl.BlockSpec((rows, w), lambda bi, ti: (bi * nt + ti, 0))
    p8_spec = pl.BlockSpec((bb * SUBLANE, w),
                           lambda bi, ti: (jnp.maximum((bi * nt + ti) * (tt // SUBLANE) - 1, 0), 0))
    vec = lambda n: pl.BlockSpec((1, n), lambda bi, ti: (0, 0))
    mat = lambda a, b: pl.BlockSpec((a, b), lambda bi, ti: (0, 0))
    out_spec = pl.BlockSpec((rows, D_RWKV), lambda bi, ti: (bi * nt + ti, 0))
    out_sds = jax.ShapeDtypeStruct((bn * l_total, D_RWKV), F32)
    kern = functools.partial(_rwkv_prep_kernel, bb=bb, tt=tt, l_valid=l_valid, l_total=l_total)
    return pl.pallas_call(
        kern,
        out_shape=[out_sds] * 7,
        grid=(bn // bb, nt),
        in_specs=[row_spec, p8_spec,
                  pl.BlockSpec((bb, 1, w), lambda bi, ti: (bi, 0, 0)),
                  vec(w), vec(D_RWKV), mat(LW_PAD, D_RWKV), vec(D_RWKV), mat(LA_PAD, D_RWKV),
                  mat(LG_PAD, D_RWKV), vec(D_RWKV), vec(D_RWKV)],
        out_specs=[out_spec] * 7,
        compiler_params=_cparams(("parallel", "arbitrary")),
        name="rwkv_prep",
    )(proj, proj, shift_pad, pp["mu"], pp["w0"], pp["w_up"], pp["a0"], pp["a_up"], pp["g_up"],
      pp["k_k"], pp["k_a"])


def _rwkv_scan_kernel(*refs, bb, ub, c, has_s0):
    if has_s0:
        (r_ref, k_ref, v_ref, lw_ref, a_ref, b_ref, g_ref, rk_ref, lnw_ref, lnb_ref, s0_ref,
         y_ref, so_ref, s_scr) = refs
    else:
        (r_ref, k_ref, v_ref, lw_ref, a_ref, b_ref, g_ref, rk_ref, lnw_ref, lnb_ref,
         y_ref, so_ref, s_scr) = refs
        s0_ref = None
    ci = pl.program_id(2)
    n_chunks = pl.num_programs(2)

    lane = _iota2((1, LANE), 1)
    m0 = (lane < RWKV_HEAD).astype(F32)
    m1 = 1.0 - m0
    bd_mask = _same_head_mask().astype(F32)
    row_lo = _iota2((LANE, LANE), 0) < RWKV_HEAD

    c2 = 2 * c
    ri = _iota2((c2, c2), 0)
    cj = _iota2((c2, c2), 1)
    same_blk = (ri >= c) == (cj >= c)
    mask_s = jnp.where(same_blk, (ri > cj).astype(F32), 0.0)
    mask_i = jnp.where(same_blk, (ri >= cj).astype(F32), 0.0)
    eye = (ri == cj).astype(F32)
    n_sq = max((c - 1).bit_length() - 1, 0)

    head_avg = bd_mask.astype(BF16)

    chains = [(j, w) for j in range(bb) for w in range(ub)]
    seqs = range(len(chains))

    @pl.when(ci == 0)
    def _():
        for n, (j, w) in enumerate(chains):
            if has_s0:
                s = s0_ref[j, w]
                s_scr[n] = jnp.concatenate([s, s], axis=1) * bd_mask
            else:
                s_scr[n] = jnp.zeros((LANE, LANE), F32)

    stack2 = lambda lo, hi: jnp.concatenate([lo, hi], axis=0)
    unit = lambda ref, j, w: ref[j, :, w * LANE:(w + 1) * LANE]
    vec = lambda ref, w: ref[:, w * LANE:(w + 1) * LANE]
    r = [unit(r_ref, j, w) for j, w in chains]
    k = [unit(k_ref, j, w) for j, w in chains]
    v = [unit(v_ref, j, w) for j, w in chains]
    lw = [unit(lw_ref, j, w) for j, w in chains]
    a = [unit(a_ref, j, w) for j, w in chains]
    b = [unit(b_ref, j, w) for j, w in chains]
    cum = [_cumsum_time(x) for x in lw]
    clast = [x[c - 1:c, :] for x in cum]
    p_inv = [jnp.exp(-x) for x in cum]
    a2 = [stack2(a[j] * jnp.exp(cum[j] - lw[j]) * m0, a[j] * jnp.exp(cum[j] - lw[j]) * m1) for j in seqs]
    r2 = [stack2(r[j] * jnp.exp(cum[j]) * m0, r[j] * jnp.exp(cum[j]) * m1) for j in seqs]
    b2 = [stack2(b[j] * p_inv[j], b[j] * p_inv[j]) for j in seqs]
    k2 = [stack2(k[j] * p_inv[j], k[j] * p_inv[j]) for j in seqs]
    v2 = [stack2(v[j] * m0, v[j] * m1) for j in seqs]
    lab = [mask_s * _dot_nt(a2[j], b2[j]) for j in seqs]
    lak = [mask_s * _dot_nt(a2[j], k2[j]) for j in seqs]
    rb = [mask_i * _dot_nt(r2[j], b2[j]) for j in seqs]
    rkm = [mask_i * _dot_nt(r2[j], k2[j]) for j in seqs]
    tinv = [eye + x for x in lab]
    xp = lab
    for _ in range(n_sq):
        xp = [_dot(x, x) for x in xp]
        tinv = [tinv[j] + _dot(tinv[j], xp[j]) for j in seqs]
    lakv = [_dot(lak[j], v2[j]) for j in seqs]
    rkv = [_dot(rkm[j], v2[j]) for j in seqs]
    s0 = [s_scr[n] for n in seqs]
    ar_s0 = [_dot_nt(stack2(a2[n], r2[n]), s0[n]) for n in seqs]
    u2 = [_dot(tinv[n], ar_s0[n][:c2] + lakv[n]) for n in seqs]
    y2 = [ar_s0[n][c2:] + _dot(rb[n], u2[n]) + rkv[n] for n in seqs]
    y = [x[:c] + x[c:] for x in y2]
    u = [x[:c] + x[c:] for x in u2]
    p_last = [jnp.exp(clast[n] - cum[n]) for n in seqs]
    for n in seqs:
        s_scr[n] = s0[n] * jnp.exp(clast[n]) + bd_mask * _dot_tn(
            stack2(u[n], v[n]), stack2(b[n] * p_last[n], k[n] * p_last[n]))

    mu = [_dot(x, head_avg) * (1.0 / RWKV_HEAD) for x in y]
    dlt = [y[n] - mu[n] for n in seqs]
    var = [_dot(x * x, head_avg) * (1.0 / RWKV_HEAD) for x in dlt]
    bonus = [_dot(r[n] * k[n] * vec(rk_ref, w), head_avg) * v[n] for n, (j, w) in enumerate(chains)]
    for n, (j, w) in enumerate(chains):
        yn = dlt[n] * lax.rsqrt(var[n] + GN_EPS) * vec(lnw_ref, w) + vec(lnb_ref, w)
        y_ref[j, :, w * LANE:(w + 1) * LANE] = (yn + bonus[n]) * unit(g_ref, j, w)

    @pl.when(ci == n_chunks - 1)
    def _():
        for n, (j, w) in enumerate(chains):
            s_fin = s_scr[n]
            folded = jnp.where(row_lo, s_fin, pltpu.roll(s_fin, RWKV_HEAD, axis=1))
            so_ref[j, w] = folded[:, :RWKV_HEAD]


def _rwkv_scan(streams, rk, lnw, lnb, s0, bn, l_total, bb, ub, c):
    nc = l_total // c
    has_s0 = s0 is not None
    blk = pl.BlockSpec((bb, c, ub * LANE), lambda bi, u, ci: (bi, ci, u))
    vec = pl.BlockSpec((1, ub * LANE), lambda bi, u, ci: (0, u))
    st = pl.BlockSpec((bb, ub, LANE, RWKV_HEAD), lambda bi, u, ci: (bi, u, 0, 0))
    in_specs = [blk] * 7 + [vec] * 3 + ([st] if has_s0 else [])
    args = [s.reshape(bn, l_total, D_RWKV) for s in streams] + [rk, lnw, lnb] + ([s0] if has_s0 else [])
    kern = functools.partial(_rwkv_scan_kernel, bb=bb, ub=ub, c=c, has_s0=has_s0)
    return pl.pallas_call(
        kern,
        out_shape=[jax.ShapeDtypeStruct((bn, l_total, D_RWKV), F32),
                   jax.ShapeDtypeStruct((bn, N_UNITS, LANE, RWKV_HEAD), F32)],
        grid=(bn // bb, N_UNITS // ub, nc),
        in_specs=in_specs,
        out_specs=[blk, st],
        scratch_shapes=[pltpu.VMEM((bb * ub, LANE, LANE), F32)],
        compiler_params=_cparams(("parallel", "parallel", "arbitrary")),
        name="rwkv_scan",
    )(*args)


def _hgrn_kernel(*refs, bb, ub, c, has_s0, l_valid, l_total):
    if has_s0:
        q_ref, f_ref, i_ref, og_ref, lb_ref, nw_ref, s0_ref, y_ref, so_ref, s_scr = refs
    else:
        q_ref, f_ref, i_ref, og_ref, lb_ref, nw_ref, y_ref, so_ref, s_scr = refs
        s0_ref = None
    ci = pl.program_id(2)
    n_chunks = pl.num_programs(2)
    tri = (_iota2((c, c), 0) >= _iota2((c, c), 1)).astype(F32)
    mid = max(c // 2 - 1, 0)
    chains = [(j, w) for j in range(bb) for w in range(ub)]
    seqs = range(len(chains))
    unit = lambda ref, j, w: ref[j, :, w * LANE:(w + 1) * LANE]

    @pl.when(ci == 0)
    def _():
        for n, (j, w) in enumerate(chains):
            if has_s0:
                s_scr[n] = s0_ref[j, w].T
            else:
                s_scr[n] = jnp.zeros((LANE, LANE), F32)

    q = [unit(q_ref, j, w) * _sigmoid(unit(q_ref, j, w)) for j, w in chains]
    f = [lb_ref[:, w * LANE:(w + 1) * LANE] + (1.0 - lb_ref[:, w * LANE:(w + 1) * LANE])
         * _sigmoid(unit(f_ref, j, w)) for j, w in chains]
    logf = [jnp.log(x) for x in f]
    kf = [1.0 - x for x in f]
    v = [unit(i_ref, j, w) for j, w in chains]
    if l_valid < l_total:
        ok = (ci * c + _iota2((c, LANE), 0)) < l_valid
        logf = [jnp.where(ok, x, 0.0) for x in logf]
        kf = [jnp.where(ok, x, 0.0) for x in kf]
    cum = [_cumsum_time(x) for x in logf]
    cref = [x[mid:mid + 1, :] for x in cum]
    clast = [x[c - 1:c, :] for x in cum]
    amat = [tri * _dot_nt(q[j] * jnp.exp(cum[j] - cref[j]), kf[j] * jnp.exp(cref[j] - cum[j])) for j in seqs]
    st = [s_scr[j] for j in seqs]
    o = [_dot_nt(q[j] * jnp.exp(cum[j]), st[j]) + _dot(amat[j], v[j]) for j in seqs]
    for j in seqs:
        s_scr[j] = st[j] * jnp.exp(clast[j]) + _dot_tn(v[j], kf[j] * jnp.exp(clast[j] - cum[j]))
    for n, (j, w) in enumerate(chains):
        on = o[n] * lax.rsqrt(jnp.mean(o[n] * o[n], axis=-1, keepdims=True) + HGRN_NORM_EPS) * nw_ref[...]
        og = unit(og_ref, j, w)
        y_ref[j, :, w * LANE:(w + 1) * LANE] = on * (og * _sigmoid(og))

    @pl.when(ci == n_chunks - 1)
    def _():
        for n, (j, w) in enumerate(chains):
            so_ref[j, w] = s_scr[n].T


def _hgrn_scan(proj, lb, nw, s0, bn, l_total, l_valid, bb, ub, c):
    nc = l_total // c
    has_s0 = s0 is not None
    base = D_SHIFT_PAD // (ub * LANE)
    per = D_HGRN // (ub * LANE)
    proj3 = proj.reshape(bn, l_total, D_IN_PAD)

    def col(part):
        return pl.BlockSpec((bb, c, ub * LANE), lambda bi, u, ci: (bi, ci, base + part * per + u))

    st = pl.BlockSpec((bb, ub, LANE, LANE), lambda bi, u, ci: (bi, u, 0, 0))
    in_specs = [col(0), col(1), col(2), col(3),
                pl.BlockSpec((1, ub * LANE), lambda bi, u, ci: (0, u)),
                pl.BlockSpec((1, LANE), lambda bi, u, ci: (0, 0))] + ([st] if has_s0 else [])
    args = [proj3, proj3, proj3, proj3, lb, nw] + ([s0] if has_s0 else [])
    kern = functools.partial(_hgrn_kernel, bb=bb, ub=ub, c=c, has_s0=has_s0, l_valid=l_valid,
                             l_total=l_total)
    return pl.pallas_call(
        kern,
        out_shape=[jax.ShapeDtypeStruct((bn, l_total, D_HGRN), F32),
                   jax.ShapeDtypeStruct((bn, N_UNITS, LANE, LANE), F32)],
        grid=(bn // bb, N_UNITS // ub, nc),
        in_specs=in_specs,
        out_specs=[pl.BlockSpec((bb, c, ub * LANE), lambda bi, u, ci: (bi, ci, u)), st],
        scratch_shapes=[pltpu.VMEM((bb * ub, LANE, LANE), F32)],
        compiler_params=_cparams(("parallel", "parallel", "arbitrary")),
        name="hgrn_scan",
    )(*args)


def _outproj_kernel(x_ref, yr_ref, yh_ref, wo_ref, nf_ref, wr_ref, br_ref,
                    h_ref, xn_ref, idx_ref, gate_ref):
    h = (x_ref[...] + _dot(yr_ref[...], wo_ref[0:D_RWKV, :]) + _dot(yh_ref[...], wo_ref[D_RWKV:, :]))
    h_ref[...] = h
    xn = _rmsnorm(h, nf_ref[...])
    xh = xn.astype(BF16)
    bits = lax.bitcast_convert_type(xh.astype(F32), U32)
    xn_ref[...] = (bits[:, D_MODEL // 2:] & jnp.uint32(0xFFFF0000)) | (bits[:, :D_MODEL // 2] >> 16)
    wr = wr_ref[...]
    wh = wr.astype(BF16)
    logits = (_dot(xh, wh) + _dot(xn - xh.astype(F32), wh) + _dot(xh, wr - wh.astype(F32))
              + br_ref[...])
    tm = logits.shape[0]
    lane = _iota2((tm, LANE), 1).astype(F32)
    neg = jnp.float32(-jnp.inf)
    work = jnp.where(lane < N_EXPERTS, logits, neg)
    idx_out = jnp.zeros((tm, LANE), I32)
    val_out = jnp.zeros((tm, LANE), F32)
    top0 = None
    for kk in range(TOP_K):
        m = jnp.max(work, axis=-1, keepdims=True)
        sel = jnp.min(jnp.where(work == m, lane, float(LANE)), axis=-1, keepdims=True)
        if kk == 0:
            top0 = m
        idx_out = jnp.where(lane == kk, sel.astype(I32), idx_out)
        val_out = jnp.where(lane == kk, jnp.exp(m - top0), val_out)
        work = jnp.where(lane == sel, neg, work)
    idx_ref[...] = idx_out
    gate_ref[...] = val_out / jnp.sum(val_out, axis=-1, keepdims=True)


def _outproj_router(x, yr, yh, wo, nf, wr, br, tm):
    t = x.shape[0]
    row = lambda n: pl.BlockSpec((tm, n), lambda i: (i, 0))
    full = lambda a, b: pl.BlockSpec((a, b), lambda i: (0, 0))
    return pl.pallas_call(
        _outproj_kernel,
        out_shape=[jax.ShapeDtypeStruct((t, D_MODEL), F32), jax.ShapeDtypeStruct((t, D_MODEL // 2), U32),
                   jax.ShapeDtypeStruct((t, LANE), I32), jax.ShapeDtypeStruct((t, LANE), F32)],
        grid=(t // tm,),
        in_specs=[row(D_MODEL), row(D_RWKV), row(D_HGRN), full(D_MODEL, D_MODEL), full(1, D_MODEL),
                  full(D_MODEL, LANE), full(1, LANE)],
        out_specs=[row(D_MODEL), row(D_MODEL // 2), row(LANE), row(LANE)],
        compiler_params=_cparams(("parallel",)),
        name="outproj_router",
    )(x, yr, yh, wo, nf, wr, br)


MOE_NF = D_EXPERT // MOE_TF
MOE_ISSUE = MOE_TM // MOE_NF
MOE_AHEAD_TILES = MOE_RT
MOE_AHEAD = MOE_AHEAD_TILES * MOE_TM
MOE_RING = 2 * MOE_RT
HALF = D_MODEL // 2


def _ring_row_copy(x_hbm, tok, ring, sems, r):
    slot = (r // MOE_TM) % MOE_RING
    return pltpu.make_async_copy(x_hbm.at[pl.ds(tok, 1)], ring.at[slot, pl.ds(r % MOE_TM, 1)], sems.at[slot])


def _ring_tile_wait(x_hbm, ring, sems, tile):
    slot = tile % MOE_RING
    pltpu.make_async_copy(x_hbm.at[pl.ds(0, MOE_TM)], ring.at[slot], sems.at[slot]).wait()


def _moe_kernel(ie_ref, it0_ref, int_ref, nused_ref, rowtok_ref, x_hbm, wg_ref, wu_ref, wd_ref, bg_ref, bu_ref,
                bd_ref, ys_hbm, ring, xbuf, acc, wgb, wub, wdb, ring_sems, out_sem):
    i = pl.program_id(0)
    f = pl.program_id(1)
    n_items = pl.num_programs(0)
    nt = int_ref[i]
    tile0 = it0_ref[i]
    n_used = nused_ref[0]

    @pl.when((i == 0) & (f == 0))
    def _():
        def head(r, carry):
            _ring_row_copy(x_hbm, rowtok_ref[r], ring, ring_sems, r).start()
            return carry

        lax.fori_loop(0, MOE_AHEAD, head, 0)

    @pl.when(nt > 0)
    def _():
        @pl.when(f == 0)
        def _():
            def load(t, carry):
                _ring_tile_wait(x_hbm, ring, ring_sems, tile0 + t)
                w = ring[(tile0 + t) % MOE_RING]
                lo = lax.bitcast_convert_type(w << 16, F32).astype(BF16)
                hi = lax.bitcast_convert_type(w & jnp.uint32(0xFFFF0000), F32).astype(BF16)
                xbuf[t, :, 0:HALF] = lo
                xbuf[t, :, HALF:D_MODEL] = hi
                acc[t] = jnp.broadcast_to(bd_ref[0], (MOE_TM, D_MODEL))
                return carry

            lax.fori_loop(0, nt, load, 0)

        wgb[...] = wg_ref[0].astype(BF16)
        wub[...] = wu_ref[0].astype(BF16)
        wdb[...] = wd_ref[0].astype(BF16)
        bg = bg_ref[0]
        bu = bu_ref[0]

        def tile(t, carry):
            base = tile0 * MOE_TM + MOE_AHEAD + (f * nt + t) * MOE_ISSUE
            for q in range(MOE_ISSUE):
                _ring_row_copy(x_hbm, rowtok_ref[base + q], ring, ring_sems, base + q).start()
            x = xbuf[t]
            gate = jnp.dot(x, wgb[...], preferred_element_type=F32) + bg
            up = jnp.dot(x, wub[...], preferred_element_type=F32) + bu
            gate = jnp.minimum(gate, SWIGLU_LIMIT)
            up = jnp.clip(up, -SWIGLU_LIMIT, SWIGLU_LIMIT)
            hid = (up + 1.0) * gate * _sigmoid(SWIGLU_ALPHA * gate)
            acc[t] += jnp.dot(hid.astype(BF16), wdb[...], preferred_element_type=F32)
            return carry

        lax.fori_loop(0, nt, tile, 0)

        @pl.when(f == MOE_NF - 1)
        def _():
            def store(t, carry):
                cp = pltpu.make_async_copy(acc.at[t], ys_hbm.at[pl.ds((tile0 + t) * MOE_TM, MOE_TM)], out_sem)
                cp.start()
                cp.wait()
                return carry

            lax.fori_loop(0, nt, store, 0)

    @pl.when((i == n_items - 1) & (f == MOE_NF - 1))
    def _():
        for s in range(MOE_AHEAD_TILES):
            _ring_tile_wait(x_hbm, ring, ring_sems, n_used + s)
        acc[0] = jnp.zeros((MOE_TM, D_MODEL), F32)
        n_tiles = ys_hbm.shape[0] // MOE_TM

        def fill(t, carry):
            cp = pltpu.make_async_copy(acc.at[0], ys_hbm.at[pl.ds(t * MOE_TM, MOE_TM)], out_sem)
            cp.start()
            cp.wait()
            return carry

        lax.fori_loop(n_used, n_tiles, fill, 0)


def _moe_experts(item_e, item_t0, item_nt, n_used, row_tok, x_packed, w_gu, b_gu, w_down, b_down, n_items,
                 n_rows):
    nf = MOE_NF

    def fcol(i, f, int_):
        return jnp.where(int_[i] > 0, f, nf - 1)

    in_specs = [
        pl.BlockSpec(memory_space=pl.ANY),
        pl.BlockSpec((1, D_MODEL, MOE_TF), lambda i, f, ie, it0, int_, nu, rt: (ie[i], 0, fcol(i, f, int_))),
        pl.BlockSpec((1, D_MODEL, MOE_TF), lambda i, f, ie, it0, int_, nu, rt: (ie[i], 0, nf + fcol(i, f, int_))),
        pl.BlockSpec((1, MOE_TF, D_MODEL), lambda i, f, ie, it0, int_, nu, rt: (ie[i], fcol(i, f, int_), 0)),
        pl.BlockSpec((1, 1, MOE_TF), lambda i, f, ie, it0, int_, nu, rt: (ie[i], 0, fcol(i, f, int_))),
        pl.BlockSpec((1, 1, MOE_TF), lambda i, f, ie, it0, int_, nu, rt: (ie[i], 0, nf + fcol(i, f, int_))),
        pl.BlockSpec((1, 1, D_MODEL), lambda i, f, ie, it0, int_, nu, rt: (ie[i], 0, 0)),
    ]
    return pl.pallas_call(
        _moe_kernel,
        out_shape=jax.ShapeDtypeStruct((n_rows, D_MODEL), F32),
        grid_spec=pltpu.PrefetchScalarGridSpec(
            num_scalar_prefetch=5,
            grid=(n_items, nf),
            in_specs=in_specs,
            out_specs=pl.BlockSpec(memory_space=pl.ANY),
            scratch_shapes=[pltpu.VMEM((MOE_RING, MOE_TM, HALF), U32),
                            pltpu.VMEM((MOE_RT, MOE_TM, D_MODEL), BF16),
                            pltpu.VMEM((MOE_RT, MOE_TM, D_MODEL), F32),
                            pltpu.VMEM((D_MODEL, MOE_TF), BF16),
                            pltpu.VMEM((D_MODEL, MOE_TF), BF16),
                            pltpu.VMEM((MOE_TF, D_MODEL), BF16),
                            pltpu.SemaphoreType.DMA((MOE_RING,)),
                            pltpu.SemaphoreType.DMA(())]),
        compiler_params=_cparams(("arbitrary", "arbitrary")),
        name="moe_experts",
    )(item_e, item_t0, item_nt, n_used, row_tok, x_packed, w_gu, w_gu, w_down, b_gu, b_gu, b_down)


COMBINE_TT = 64


def _combine_kernel(pos_ref, h_ref, gate_ref, ys_hbm, o_ref, buf, sem):
    t = pl.program_id(0)
    n = COMBINE_TT * TOP_K

    def start(q, carry):
        row = pos_ref[t * n + q]
        pltpu.make_async_copy(ys_hbm.at[pl.ds(row, 1)], buf.at[q % TOP_K, pl.ds(q // TOP_K, 1)], sem).start()
        return carry

    lax.fori_loop(0, n, start, 0)

    def wait(q, carry):
        pltpu.make_async_copy(ys_hbm.at[pl.ds(0, 1)], buf.at[q % TOP_K, pl.ds(q // TOP_K, 1)], sem).wait()
        return carry

    lax.fori_loop(0, n, wait, 0)
    gates = gate_ref[...]
    out = h_ref[...]
    for kk in range(TOP_K):
        out = out + gates[:, kk:kk + 1] * buf[kk]
    o_ref[...] = out


def _moe_combine(pos, h, gates, ys):
    t = h.shape[0]
    return pl.pallas_call(
        _combine_kernel,
        out_shape=jax.ShapeDtypeStruct((t, D_MODEL), F32),
        grid_spec=pltpu.PrefetchScalarGridSpec(
            num_scalar_prefetch=1,
            grid=(t // COMBINE_TT,),
            in_specs=[pl.BlockSpec((COMBINE_TT, D_MODEL), lambda i, p: (i, 0)),
                      pl.BlockSpec((COMBINE_TT, LANE), lambda i, p: (i, 0)),
                      pl.BlockSpec(memory_space=pl.ANY)],
            out_specs=pl.BlockSpec((COMBINE_TT, D_MODEL), lambda i, p: (i, 0)),
            scratch_shapes=[pltpu.VMEM((TOP_K, COMBINE_TT, D_MODEL), F32),
                            pltpu.SemaphoreType.DMA(())]),
        compiler_params=_cparams(("arbitrary",)),
        name="moe_combine",
    )(pos, h, gates, ys)


def _ple_kernel(h_ref, p_ref, np_ref, wg_ref, wp_ref, nfin_ref, o_ref):
    h = h_ref[...]
    gate = _sigmoid(_dot(_rmsnorm(h, np_ref[...]), wg_ref[...]))
    h = h + gate * _dot(p_ref[...], wp_ref[...])
    o_ref[...] = _rmsnorm(h, nfin_ref[...])


def _ple_final(h, p, n_ple, wg, wp, n_fin, tm):
    t = h.shape[0]
    row = lambda n: pl.BlockSpec((tm, n), lambda i: (i, 0))
    full = lambda a, b: pl.BlockSpec((a, b), lambda i: (0, 0))
    return pl.pallas_call(
        _ple_kernel,
        out_shape=jax.ShapeDtypeStruct((t, D_MODEL), F32),
        grid=(t // tm,),
        in_specs=[row(D_MODEL), row(D_PLE), full(1, D_MODEL), full(D_MODEL, D_MODEL),
                  full(D_PLE, D_MODEL), full(1, D_MODEL)],
        out_specs=row(D_MODEL),
        compiler_params=_cparams(("parallel",)),
        name="ple_final",
    )(h, p, n_ple, wg, wp, n_fin)


def _pad_shift_cols(a):
    def z(n):
        return jnp.zeros(a.shape[:-1] + (n,), a.dtype)
    c0 = 3 * D_RWKV
    c1 = c0 + LORA_W
    c2 = c1 + LORA_A
    return jnp.concatenate([a[..., :c0], a[..., c0:c1], z(LW_PAD - LORA_W), a[..., c1:c2], z(LA_PAD - LORA_A),
                            a[..., c2:], z(LG_PAD - LORA_G)], axis=-1)


def _unpad_shift_cols(a):
    return jnp.concatenate([a[..., :OFF_WD], a[..., OFF_WD:OFF_WD + LORA_W], a[..., OFF_AD:OFF_AD + LORA_A],
                            a[..., OFF_GD:OFF_GD + LORA_G]], axis=-1)


def _pad_rows(a, n):
    return jnp.concatenate([a, jnp.zeros((n - a.shape[0],) + a.shape[1:], a.dtype)], axis=0)


def _routing(idx, src_rows, n_rows_cap, n_items_cap):
    t = idx.shape[0]
    na = t * TOP_K
    flat_e = idx.reshape(na)
    onehot = (flat_e[:, None] == jnp.arange(N_EXPERTS, dtype=I32)[None, :]).astype(I32)
    cs = jnp.cumsum(onehot, axis=0)
    rank = jnp.take_along_axis(cs, flat_e[:, None], axis=1)[:, 0] - 1
    counts = cs[-1]
    ptiles = (counts + MOE_TM - 1) // MOE_TM
    pend = jnp.cumsum(ptiles)
    pstart = pend - ptiles
    pos = pstart[flat_e] * MOE_TM + rank
    row_tok = jnp.zeros((n_rows_cap + MOE_AHEAD,), I32).at[pos].set(jnp.repeat(src_rows, TOP_K))
    n_used = pend[-1:].astype(I32)
    items_per_e = (ptiles + MOE_RT - 1) // MOE_RT
    iend = jnp.cumsum(items_per_e)
    istart = iend - items_per_e
    ii = jnp.arange(n_items_cap, dtype=I32)
    e_of = jnp.minimum(jnp.searchsorted(iend, ii, side="right"), N_EXPERTS - 1).astype(I32)
    jj = ii - istart[e_of]
    used = ii < iend[-1]
    item_nt = jnp.where(used, jnp.clip(ptiles[e_of] - jj * MOE_RT, 0, MOE_RT), 0).astype(I32)
    item_t0 = jnp.where(used, pstart[e_of] + jj * MOE_RT, 0).astype(I32)
    last_e = e_of[jnp.maximum(iend[-1] - 1, 0)]
    item_e = jnp.where(used, e_of, last_e).astype(I32)
    return pos.astype(I32), row_tok, n_used, item_e, item_t0, item_nt


def kernel(x_prompt, x_sample, p_prompt, p_sample, state_rwkv_shift, state_rwkv, state_hgrn, norm_mix, w_in,
           mu_shift, w0, w_up, a0, a_up, g_up, k_k, k_a, r_k, lnx_w, lnx_b, hgrn_lb, hgrn_norm, w_out,
           norm_ffn, w_router, b_router, w_gu, b_gu, w_down, b_down, norm_ple, w_ple_gate, w_ple_proj,
           norm_final):
    depth = w_in.shape[0]
    assert depth == 1
    li = 0
    bp, lp = x_prompt.shape[0], x_prompt.shape[1]
    bs, ls = x_sample.shape[0], x_sample.shape[1]
    ls_pad = SUBLANE
    tp = bp * lp

    w_in_p = jnp.concatenate([_pad_shift_cols(w_in[li][:, :D_SHIFT]), w_in[li][:, D_SHIFT:]], axis=1).astype(BF16)
    row = lambda a: a.reshape(1, -1).astype(F32)
    pp = {
        "mu": row(_pad_shift_cols(mu_shift[li])),
        "w0": row(w0[li]), "a0": row(a0[li]), "k_k": row(k_k[li]), "k_a": row(k_a[li]),
        "w_up": _pad_rows(w_up[li], LW_PAD).astype(BF16),
        "a_up": _pad_rows(a_up[li], LA_PAD).astype(BF16),
        "g_up": _pad_rows(g_up[li], LG_PAD).astype(BF16),
    }
    rk = row(r_k[li])
    lnw = row(lnx_w[li])
    lnb = row(lnx_b[li])
    lower = jax.nn.softmax(hgrn_lb.astype(F32), axis=0)
    lb = row(jnp.cumsum(lower, axis=0)[li])
    nw = row(hgrn_norm[li])
    wo = w_out[li].astype(BF16)
    nf = row(norm_ffn[li])
    wr = jnp.concatenate([w_router[li], jnp.zeros((D_MODEL, LANE - N_EXPERTS), F32)], axis=1)
    br = jnp.concatenate([b_router[li], jnp.zeros((LANE - N_EXPERTS,), F32)]).reshape(1, LANE)
    n_ple = row(norm_ple[li])
    wpg = w_ple_gate[li].astype(BF16)
    wpp = w_ple_proj[li].astype(BF16)
    n_fin = row(norm_final)
    g_mix = row(norm_mix[li])

    def mixer(x2d, shift_prev, s_rwkv, s_hgrn, bn, l_total, l_valid, tm_in, bb_prep, tt, bb_scan, ub_scan, c,
              tm_out):
        proj = _inproj(x2d, g_mix, w_in_p, tm_in, 768)
        streams = _rwkv_prep(proj, shift_prev, pp, bn, l_total, l_valid, bb_prep, tt)
        s0r = None if s_rwkv is None else s_rwkv.reshape(bn, N_UNITS, LANE, RWKV_HEAD)
        yr, s_rwkv_new = _rwkv_scan(streams, rk, lnw, lnb, s0r, bn, l_total, bb_scan, ub_scan, c)
        yh, s_hgrn_new = _hgrn_scan(proj, lb, nw, s_hgrn, bn, l_total, l_valid, bb_scan, ub_scan, c)
        h1, xn2, idx, gates = _outproj_router(x2d, yr.reshape(bn * l_total, D_RWKV),
                                              yh.reshape(bn * l_total, D_HGRN), wo, nf, wr, br, tm_out)
        new_shift = _unpad_shift_cols(proj.reshape(bn, l_total, D_IN_PAD)[:, l_valid - 1, :D_SHIFT_PAD])
        return (h1, xn2, idx, gates, new_shift,
                s_rwkv_new.reshape(bn, N_RWKV_HEADS, RWKV_HEAD, RWKV_HEAD), s_hgrn_new)

    xp2 = x_prompt.reshape(tp, D_MODEL)
    zero_shift = jnp.zeros((bp, 1, D_SHIFT_PAD), F32)
    h1p, xn2p, idxp, gatesp, shift_p, rwkv_p, hgrn_p = mixer(
        xp2, zero_shift, None, None, bp, lp, lp, 512, 1, 256, bp, 2, 64, 256)

    xs_pad = jnp.concatenate([x_sample, jnp.zeros((bs, ls_pad - ls, D_MODEL), F32)], axis=1)
    xs2 = xs_pad.reshape(bs * ls_pad, D_MODEL)
    shift_s0 = _pad_shift_cols(state_rwkv_shift[li]).reshape(bs, 1, D_SHIFT_PAD)
    h1s, xn2s, idxs, gatess, shift_s, rwkv_s, hgrn_s = mixer(
        xs2, shift_s0, state_rwkv[li], state_hgrn[li], bs, ls_pad, ls, 512, 16, ls_pad, 16, 1, ls_pad, 256)

    def compact(a):
        return a.reshape(bs, ls_pad, a.shape[-1])[:, :ls].reshape(bs * ls, a.shape[-1])

    h1s, xn2s, idxs, gatess = compact(h1s), compact(xn2s), compact(idxs), compact(gatess)
    ts = bs * ls

    t_all = tp + ts
    idx_all = jnp.concatenate([idxp[:, :TOP_K], idxs[:, :TOP_K]], axis=0)
    n_tiles_cap = -(-(t_all * TOP_K) // MOE_TM) + N_EXPERTS
    n_rows_cap = n_tiles_cap * MOE_TM
    n_items_cap = N_EXPERTS + n_tiles_cap // MOE_RT
    pos, row_tok, n_used, item_e, item_t0, item_nt = _routing(
        idx_all, jnp.arange(t_all, dtype=I32), n_rows_cap, n_items_cap)
    x_packed = jnp.concatenate([xn2p, xn2s], axis=0)
    ysorted = _moe_experts(item_e, item_t0, item_nt, n_used, row_tok, x_packed, w_gu[li],
                           b_gu[li].reshape(N_EXPERTS, 1, -1), w_down[li], b_down[li].reshape(N_EXPERTS, 1, -1),
                           n_items_cap, n_rows_cap)
    h2p = _moe_combine(pos[:tp * TOP_K], h1p, gatesp, ysorted)
    h2s = _moe_combine(pos[tp * TOP_K:], h1s, gatess, ysorted)

    y_p = _ple_final(h2p, p_prompt[li].reshape(tp, D_PLE), n_ple, wpg, wpp, n_fin, 256)
    y_s = _ple_final(h2s, p_sample[li].reshape(ts, D_PLE), n_ple, wpg, wpp, n_fin, 256)

    return (y_p.reshape(bp, lp, D_MODEL), y_s.reshape(bs, ls, D_MODEL),
            shift_p[None], rwkv_p[None], hgrn_p[None],
            shift_s[None], rwkv_s[None], hgrn_s[None])
```

```python
import functools

import jax
import jax.numpy as jnp
from jax import lax
from jax.experimental import pallas as pl
from jax.experimental.pallas import tpu as pltpu

F32 = jnp.float32
BF16 = jnp.bfloat16
I32 = jnp.int32
U32 = jnp.uint32

D_MODEL = 2048
D_RWKV = 1024
D_HGRN = 1024
RWKV_HEAD = 64
N_RWKV_HEADS = 16
HGRN_HEAD = 128
N_HGRN_HEADS = 8
LORA_W = 64
LORA_A = 64
LORA_G = 160
D_SHIFT = 3 * D_RWKV + LORA_W + LORA_A + LORA_G
N_EXPERTS = 32
TOP_K = 4
D_EXPERT = 2048
SWIGLU_LIMIT = 7.0
SWIGLU_ALPHA = 1.702
D_PLE = 256
RMS_EPS = 1e-6
GN_EPS = 64e-5
HGRN_NORM_EPS = 1e-5

LANE = 128
SUBLANE = 8
N_UNITS = 8

LW_PAD = LANE
LA_PAD = LANE
LG_PAD = 2 * LANE
OFF_WD = 3 * D_RWKV
OFF_AD = OFF_WD + LW_PAD
OFF_GD = OFF_AD + LA_PAD
D_SHIFT_PAD = OFF_GD + LG_PAD
D_IN_PAD = D_SHIFT_PAD + 4 * D_HGRN

MOE_TM = 256
MOE_RT = 5
MOE_TF = 256
VMEM_LIMIT = 56 * 1024 * 1024


def _cparams(sem, vmem=VMEM_LIMIT):
    return pltpu.CompilerParams(dimension_semantics=sem, vmem_limit_bytes=vmem)


def _rmsnorm(x, g):
    return x * lax.rsqrt(jnp.mean(x * x, axis=-1, keepdims=True) + RMS_EPS) * g


def _dot(a, b):
    return jnp.dot(a.astype(BF16), b.astype(BF16), preferred_element_type=F32)


def _dot_nt(a, b):
    return lax.dot_general(a.astype(BF16), b.astype(BF16), (((1,), (1,)), ((), ())),
                           preferred_element_type=F32)


def _dot_tn(a, b):
    return lax.dot_general(a.astype(BF16), b.astype(BF16), (((0,), (0,)), ((), ())),
                           preferred_element_type=F32)


def _split3(x):
    h = x.astype(BF16)
    r = x - h.astype(F32)
    m = r.astype(BF16)
    l = (r - m.astype(F32)).astype(BF16)
    return h, m, l


def _dot_exact_rhs(a, b_bf16):
    h, m, l = _split3(a)
    d = functools.partial(jnp.dot, preferred_element_type=F32)
    return d(h, b_bf16) + d(m, b_bf16) + d(l, b_bf16)


def _dot_exact_lhs(a_bf16, b):
    h, m, l = _split3(b)
    d = functools.partial(jnp.dot, preferred_element_type=F32)
    return d(a_bf16, h) + d(a_bf16, m) + d(a_bf16, l)


def _iota2(shape, dim):
    return lax.broadcasted_iota(I32, shape, dim)


def _cumsum_time(x):
    c = x.shape[0]
    tri = (_iota2((c, c), 0) >= _iota2((c, c), 1)).astype(BF16)
    return _dot_exact_lhs(tri, x)


def _same_head_mask():
    return (_iota2((LANE, LANE), 0) >= RWKV_HEAD) == (_iota2((LANE, LANE), 1) >= RWKV_HEAD)


def _sigmoid(x):
    return 1.0 / (1.0 + jnp.exp(-x))


def _inproj_kernel(x_ref, g_ref, w_ref, o_ref, xn_ref):
    @pl.when(pl.program_id(1) == 0)
    def _():
        xn_ref[...] = _rmsnorm(x_ref[...], g_ref[...]).astype(BF16)

    o_ref[...] = jnp.dot(xn_ref[...], w_ref[...], preferred_element_type=F32)


def _inproj(x, g, w, tm, tn):
    t, d = x.shape
    n = w.shape[1]
    return pl.pallas_call(
        _inproj_kernel,
        out_shape=jax.ShapeDtypeStruct((t, n), F32),
        grid=(t // tm, n // tn),
        in_specs=[pl.BlockSpec((tm, d), lambda i, j: (i, 0)),
                  pl.BlockSpec((1, d), lambda i, j: (0, 0)),
                  pl.BlockSpec((d, tn), lambda i, j: (0, j))],
        out_specs=pl.BlockSpec((tm, tn), lambda i, j: (i, j)),
        scratch_shapes=[pltpu.VMEM((tm, d), BF16)],
        compiler_params=_cparams(("parallel", "arbitrary")),
        name="inproj",
    )(x, g, w)


def _rwkv_prep_kernel(x_ref, p8_ref, sh_ref, mu_ref, w0_ref, wup_ref, a0_ref, aup_ref, gup_ref,
                      kk_ref, ka_ref,
                      r_o, k_o, v_o, lw_o, a_o, b_o, g_o, *, bb, tt, l_valid, l_total):
    ti = pl.program_id(1)
    w = D_SHIFT_PAD
    x3 = x_ref[...].reshape(bb, tt, w)
    rolled = pltpu.roll(x3, 1, axis=1)
    prev_tail = p8_ref[...].reshape(bb, SUBLANE, w)[:, SUBLANE - 1:SUBLANE, :]
    first = jnp.where(ti == 0, sh_ref[...], prev_tail)
    t_in = _iota2((bb, tt, w), 1)
    prev = jnp.where(t_in == 0, first, rolled)
    xs = (x3 + (prev - x3) * mu_ref[...]).reshape(bb * tt, w)

    r = xs[:, 0:D_RWKV]
    k = xs[:, D_RWKV:2 * D_RWKV]
    v = xs[:, 2 * D_RWKV:3 * D_RWKV]
    wd = xs[:, OFF_WD:OFF_WD + LW_PAD]
    ad = xs[:, OFF_AD:OFF_AD + LA_PAD]
    gd = xs[:, OFF_GD:OFF_GD + LG_PAD]

    z = -(w0_ref[...] + _dot(jnp.tanh(wd), wup_ref[...]))
    softplus = jnp.maximum(z, 0.0) + jnp.log(1.0 + jnp.exp(-jnp.abs(z)))
    lw = -jnp.exp(-softplus - 0.5)
    asig = _sigmoid(a0_ref[...] + _dot(ad, aup_ref[...]))
    g = _dot(_sigmoid(gd), gup_ref[...])

    kk = k * kk_ref[...]
    same_head = _same_head_mask().astype(BF16)
    sq = kk * kk
    ssq = jnp.concatenate(
        [_dot_exact_rhs(sq[:, u * LANE:(u + 1) * LANE], same_head) for u in range(N_UNITS)], axis=1)
    kkn = kk / jnp.maximum(jnp.sqrt(ssq), 1e-12)
    k2 = k * (1.0 + (asig - 1.0) * ka_ref[...])
    a_vec = -kkn
    b_vec = kkn * asig

    if l_valid < l_total:
        t_glob = (ti * tt + _iota2((bb, tt, D_RWKV), 1)).reshape(bb * tt, D_RWKV)
        ok = t_glob < l_valid
        zero = jnp.zeros_like(k2)
        lw, k2, v, a_vec, b_vec = (jnp.where(ok, t, zero) for t in (lw, k2, v, a_vec, b_vec))

    r_o[...] = r
    k_o[...] = k2
    v_o[...] = v
    lw_o[...] = lw
    a_o[...] = a_vec
    b_o[...] = b_vec
    g_o[...] = g


def _rwkv_prep(proj, shift_pad, pp, bn, l_total, l_valid, bb, tt):
    nt = l_total // tt
    rows = bb * tt
    w = D_SHIFT_PAD
    row_spec = pl.BlockSpec((rows, w), lambda bi, ti: (bi * nt + ti, 0))
    p8_spec = pl.BlockSpec((bb * SUBLANE, w),
                           lambda bi, ti: (jnp.maximum((bi * nt + ti) * (tt // SUBLANE) - 1, 0), 0))
    vec = lambda n: pl.BlockSpec((1, n), lambda bi, ti: (0, 0))
    mat = lambda a, b: pl.BlockSpec((a, b), lambda bi, ti: (0, 0))
    out_spec = pl.BlockSpec((rows, D_RWKV), lambda bi, ti: (bi * nt + ti, 0))
    out_sds = jax.ShapeDtypeStruct((bn * l_total, D_RWKV), F32)
    kern = functools.partial(_rwkv_prep_kernel, bb=bb, tt=tt, l_valid=l_valid, l_total=l_total)
    return pl.pallas_call(
        kern,
        out_shape=[out_sds] * 7,
        grid=(bn // bb, nt),
        in_specs=[row_spec, p8_spec,
                  pl.BlockSpec((bb, 1, w), lambda bi, ti: (bi, 0, 0)),
                  vec(w), vec(D_RWKV), mat(LW_PAD, D_RWKV), vec(D_RWKV), mat(LA_PAD, D_RWKV),
                  mat(LG_PAD, D_RWKV), vec(D_RWKV), vec(D_RWKV)],
        out_specs=[out_spec] * 7,
        compiler_params=_cparams(("parallel", "arbitrary")),
        name="rwkv_prep",
    )(proj, proj, shift_pad, pp["mu"], pp["w0"], pp["w_up"], pp["a0"], pp["a_up"], pp["g_up"],
      pp["k_k"], pp["k_a"])


def _rwkv_scan_kernel(*refs, bb, ub, c, has_s0):
    if has_s0:
        (r_ref, k_ref, v_ref, lw_ref, a_ref, b_ref, g_ref, rk_ref, lnw_ref, lnb_ref, s0_ref,
         y_ref, so_ref, s_scr) = refs
    else:
        (r_ref, k_ref, v_ref, lw_ref, a_ref, b_ref, g_ref, rk_ref, lnw_ref, lnb_ref,
         y_ref, so_ref, s_scr) = refs
        s0_ref = None
    ci = pl.program_id(2)
    n_chunks = pl.num_programs(2)

    lane = _iota2((1, LANE), 1)
    m0 = (lane < RWKV_HEAD).astype(F32)
    m1 = 1.0 - m0
    bd_mask = _same_head_mask().astype(F32)

    c2 = 2 * c
    ri = _iota2((c2, c2), 0)
    cj = _iota2((c2, c2), 1)
    same_blk = (ri >= c) == (cj >= c)
    mask_s = jnp.where(same_blk, (ri > cj).astype(F32), 0.0)
    mask_i = jnp.where(same_blk, (ri >= cj).astype(F32), 0.0)
    eye = (ri == cj).astype(F32)
    n_sq = max((c - 1).bit_length() - 1, 0)

    head_avg = bd_mask.astype(BF16)

    chains = [(j, w) for j in range(bb) for w in range(ub)]
    seqs = range(len(chains))

    @pl.when(ci == 0)
    def _():
        for n, (j, w) in enumerate(chains):
            if has_s0:
                zero = jnp.zeros((RWKV_HEAD, RWKV_HEAD), F32)
                top = jnp.concatenate([s0_ref[j, 2 * w], zero], axis=1)
                bottom = jnp.concatenate([zero, s0_ref[j, 2 * w + 1]], axis=1)
                s_scr[n] = jnp.concatenate([top, bottom], axis=0)
            else:
                s_scr[n] = jnp.zeros((LANE, LANE), F32)

    stack2 = lambda lo, hi: jnp.concatenate([lo, hi], axis=0)
    unit = lambda ref, j, w: ref[j, :, w * LANE:(w + 1) * LANE]
    vec = lambda ref, w: ref[:, w * LANE:(w + 1) * LANE]
    r = [unit(r_ref, j, w) for j, w in chains]
    k = [unit(k_ref, j, w) for j, w in chains]
    v = [unit(v_ref, j, w) for j, w in chains]
    lw = [unit(lw_ref, j, w) for j, w in chains]
    a = [unit(a_ref, j, w) for j, w in chains]
    b = [unit(b_ref, j, w) for j, w in chains]
    cum = [_cumsum_time(x) for x in lw]
    clast = [x[c - 1:c, :] for x in cum]
    p_inv = [jnp.exp(-x) for x in cum]
    a2 = [stack2(a[j] * jnp.exp(cum[j] - lw[j]) * m0, a[j] * jnp.exp(cum[j] - lw[j]) * m1) for j in seqs]
    r2 = [stack2(r[j] * jnp.exp(cum[j]) * m0, r[j] * jnp.exp(cum[j]) * m1) for j in seqs]
    b2 = [stack2(b[j] * p_inv[j], b[j] * p_inv[j]) for j in seqs]
    k2 = [stack2(k[j] * p_inv[j], k[j] * p_inv[j]) for j in seqs]
    v2 = [stack2(v[j] * m0, v[j] * m1) for j in seqs]
    lab = [mask_s * _dot_nt(a2[j], b2[j]) for j in seqs]
    lak = [mask_s * _dot_nt(a2[j], k2[j]) for j in seqs]
    rb = [mask_i * _dot_nt(r2[j], b2[j]) for j in seqs]
    rkm = [mask_i * _dot_nt(r2[j], k2[j]) for j in seqs]
    tinv = [eye + x for x in lab]
    xp = lab
    for _ in range(n_sq):
        xp = [_dot(x, x) for x in xp]
        tinv = [tinv[j] + _dot(tinv[j], xp[j]) for j in seqs]
    lakv = [_dot(lak[j], v2[j]) for j in seqs]
    rkv = [_dot(rkm[j], v2[j]) for j in seqs]
    s0 = [s_scr[n] for n in seqs]
    ar_s0 = [_dot_nt(stack2(a2[n], r2[n]), s0[n]) for n in seqs]
    u2 = [_dot(tinv[n], ar_s0[n][:c2] + lakv[n]) for n in seqs]
    y2 = [ar_s0[n][c2:] + _dot(rb[n], u2[n]) + rkv[n] for n in seqs]
    y = [x[:c] + x[c:] for x in y2]
    u = [x[:c] + x[c:] for x in u2]
    p_last = [jnp.exp(clast[n] - cum[n]) for n in seqs]
    for n in seqs:
        s_scr[n] = s0[n] * jnp.exp(clast[n]) + bd_mask * _dot_tn(
            stack2(u[n], v[n]), stack2(b[n] * p_last[n], k[n] * p_last[n]))

    mu = [_dot(x, head_avg) * (1.0 / RWKV_HEAD) for x in y]
    dlt = [y[n] - mu[n] for n in seqs]
    var = [_dot(x * x, head_avg) * (1.0 / RWKV_HEAD) for x in dlt]
    bonus = [_dot(r[n] * k[n] * vec(rk_ref, w), head_avg) * v[n] for n, (j, w) in enumerate(chains)]
    for n, (j, w) in enumerate(chains):
        yn = dlt[n] * lax.rsqrt(var[n] + GN_EPS) * vec(lnw_ref, w) + vec(lnb_ref, w)
        y_ref[j, :, w * LANE:(w + 1) * LANE] = (yn + bonus[n]) * unit(g_ref, j, w)

    @pl.when(ci == n_chunks - 1)
    def _():
        for n, (j, w) in enumerate(chains):
            s_fin = s_scr[n]
            so_ref[j, 2 * w] = s_fin[:RWKV_HEAD, :RWKV_HEAD]
            so_ref[j, 2 * w + 1] = pltpu.roll(s_fin, RWKV_HEAD, axis=1)[RWKV_HEAD:, :RWKV_HEAD]


def _rwkv_scan(streams, rk, lnw, lnb, s0, bn, l_total, bb, ub, c):
    nc = l_total // c
    has_s0 = s0 is not None
    blk = pl.BlockSpec((bb, c, ub * LANE), lambda bi, u, ci: (bi, ci, u))
    vec = pl.BlockSpec((1, ub * LANE), lambda bi, u, ci: (0, u))
    st = pl.BlockSpec((bb, 2 * ub, RWKV_HEAD, RWKV_HEAD), lambda bi, u, ci: (bi, u, 0, 0))
    in_specs = [blk] * 7 + [vec] * 3 + ([st] if has_s0 else [])
    args = [s.reshape(bn, l_total, D_RWKV) for s in streams] + [rk, lnw, lnb] + ([s0] if has_s0 else [])
    kern = functools.partial(_rwkv_scan_kernel, bb=bb, ub=ub, c=c, has_s0=has_s0)
    return pl.pallas_call(
        kern,
        out_shape=[jax.ShapeDtypeStruct((bn, l_total, D_RWKV), F32),
                   jax.ShapeDtypeStruct((bn, N_RWKV_HEADS, RWKV_HEAD, RWKV_HEAD), F32)],
        grid=(bn // bb, N_UNITS // ub, nc),
        in_specs=in_specs,
        out_specs=[blk, st],
        scratch_shapes=[pltpu.VMEM((bb * ub, LANE, LANE), F32)],
        compiler_params=_cparams(("parallel", "parallel", "arbitrary")),
        name="rwkv_scan",
    )(*args)


def _hgrn_kernel(*refs, bb, ub, c, has_s0, l_valid, l_total):
    if has_s0:
        q_ref, f_ref, i_ref, og_ref, lb_ref, nw_ref, s0_ref, y_ref, so_ref, s_scr = refs
    else:
        q_ref, f_ref, i_ref, og_ref, lb_ref, nw_ref, y_ref, so_ref, s_scr = refs
        s0_ref = None
    ci = pl.program_id(2)
    n_chunks = pl.num_programs(2)
    tri = (_iota2((c, c), 0) >= _iota2((c, c), 1)).astype(F32)
    mid = max(c // 2 - 1, 0)
    chains = [(j, w) for j in range(bb) for w in range(ub)]
    seqs = range(len(chains))
    unit = lambda ref, j, w: ref[j, :, w * LANE:(w + 1) * LANE]

    @pl.when(ci == 0)
    def _():
        for n, (j, w) in enumerate(chains):
            if has_s0:
                s_scr[n] = s0_ref[j, w].T
            else:
                s_scr[n] = jnp.zeros((LANE, LANE), F32)

    q = [unit(q_ref, j, w) * _sigmoid(unit(q_ref, j, w)) for j, w in chains]
    f = [lb_ref[:, w * LANE:(w + 1) * LANE] + (1.0 - lb_ref[:, w * LANE:(w + 1) * LANE])
         * _sigmoid(unit(f_ref, j, w)) for j, w in chains]
    logf = [jnp.log(x) for x in f]
    kf = [1.0 - x for x in f]
    v = [unit(i_ref, j, w) for j, w in chains]
    if l_valid < l_total:
        ok = (ci * c + _iota2((c, LANE), 0)) < l_valid
        logf = [jnp.where(ok, x, 0.0) for x in logf]
        kf = [jnp.where(ok, x, 0.0) for x in kf]
    cum = [_cumsum_time(x) for x in logf]
    cref = [x[mid:mid + 1, :] for x in cum]
    clast = [x[c - 1:c, :] for x in cum]
    amat = [tri * _dot_nt(q[j] * jnp.exp(cum[j] - cref[j]), kf[j] * jnp.exp(cref[j] - cum[j])) for j in seqs]
    st = [s_scr[j] for j in seqs]
    o = [_dot_nt(q[j] * jnp.exp(cum[j]), st[j]) + _dot(amat[j], v[j]) for j in seqs]
    for j in seqs:
        s_scr[j] = st[j] * jnp.exp(clast[j]) + _dot_tn(v[j], kf[j] * jnp.exp(clast[j] - cum[j]))
    for n, (j, w) in enumerate(chains):
        on = o[n] * lax.rsqrt(jnp.mean(o[n] * o[n], axis=-1, keepdims=True) + HGRN_NORM_EPS) * nw_ref[...]
        og = unit(og_ref, j, w)
        y_ref[j, :, w * LANE:(w + 1) * LANE] = on * (og * _sigmoid(og))

    @pl.when(ci == n_chunks - 1)
    def _():
        for n, (j, w) in enumerate(chains):
            so_ref[j, w] = s_scr[n].T


def _hgrn_scan(proj, lb, nw, s0, bn, l_total, l_valid, bb, ub, c):
    nc = l_total // c
    has_s0 = s0 is not None
    base = D_SHIFT_PAD // (ub * LANE)
    per = D_HGRN // (ub * LANE)
    proj3 = proj.reshape(bn, l_total, D_IN_PAD)

    def col(part):
        return pl.BlockSpec((bb, c, ub * LANE), lambda bi, u, ci: (bi, ci, base + part * per + u))

    st = pl.BlockSpec((bb, ub, LANE, LANE), lambda bi, u, ci: (bi, u, 0, 0))
    in_specs = [col(0), col(1), col(2), col(3),
                pl.BlockSpec((1, ub * LANE), lambda bi, u, ci: (0, u)),
                pl.BlockSpec((1, LANE), lambda bi, u, ci: (0, 0))] + ([st] if has_s0 else [])
    args = [proj3, proj3, proj3, proj3, lb, nw] + ([s0] if has_s0 else [])
    kern = functools.partial(_hgrn_kernel, bb=bb, ub=ub, c=c, has_s0=has_s0, l_valid=l_valid,
                             l_total=l_total)
    return pl.pallas_call(
        kern,
        out_shape=[jax.ShapeDtypeStruct((bn, l_total, D_HGRN), F32),
                   jax.ShapeDtypeStruct((bn, N_UNITS, LANE, LANE), F32)],
        grid=(bn // bb, N_UNITS // ub, nc),
        in_specs=in_specs,
        out_specs=[pl.BlockSpec((bb, c, ub * LANE), lambda bi, u, ci: (bi, ci, u)), st],
        scratch_shapes=[pltpu.VMEM((bb * ub, LANE, LANE), F32)],
        compiler_params=_cparams(("parallel", "parallel", "arbitrary")),
        name="hgrn_scan",
    )(*args)


def _outproj_kernel(x_ref, yr_ref, yh_ref, wo_ref, nf_ref, wr_ref, br_ref,
                    h_ref, xn_ref, idx_ref, gate_ref):
    h = (x_ref[...] + _dot(yr_ref[...], wo_ref[0:D_RWKV, :]) + _dot(yh_ref[...], wo_ref[D_RWKV:, :]))
    h_ref[...] = h
    xn = _rmsnorm(h, nf_ref[...])
    xh = xn.astype(BF16)
    bits = lax.bitcast_convert_type(xh.astype(F32), U32)
    xn_ref[...] = (bits[:, D_MODEL // 2:] & jnp.uint32(0xFFFF0000)) | (bits[:, :D_MODEL // 2] >> 16)
    wr = wr_ref[...]
    wh = wr.astype(BF16)
    logits = (_dot(xh, wh) + _dot(xn - xh.astype(F32), wh) + _dot(xh, wr - wh.astype(F32))
              + br_ref[...])
    tm = logits.shape[0]
    lane = _iota2((tm, LANE), 1).astype(F32)
    neg = jnp.float32(-jnp.inf)
    work = jnp.where(lane < N_EXPERTS, logits, neg)
    idx_out = jnp.zeros((tm, LANE), I32)
    val_out = jnp.zeros((tm, LANE), F32)
    top0 = None
    for kk in range(TOP_K):
        m = jnp.max(work, axis=-1, keepdims=True)
        sel = jnp.min(jnp.where(work == m, lane, float(LANE)), axis=-1, keepdims=True)
        if kk == 0:
            top0 = m
        idx_out = jnp.where(lane == kk, sel.astype(I32), idx_out)
        val_out = jnp.where(lane == kk, jnp.exp(m - top0), val_out)
        work = jnp.where(lane == sel, neg, work)
    idx_ref[...] = idx_out
    gate_ref[...] = val_out / jnp.sum(val_out, axis=-1, keepdims=True)


def _outproj_router(x, yr, yh, wo, nf, wr, br, tm):
    t = x.shape[0]
    row = lambda n: pl.BlockSpec((tm, n), lambda i: (i, 0))
    full = lambda a, b: pl.BlockSpec((a, b), lambda i: (0, 0))
    return pl.pallas_call(
        _outproj_kernel,
        out_shape=[jax.ShapeDtypeStruct((t, D_MODEL), F32), jax.ShapeDtypeStruct((t, D_MODEL // 2), U32),
                   jax.ShapeDtypeStruct((t, LANE), I32), jax.ShapeDtypeStruct((t, LANE), F32)],
        grid=(t // tm,),
        in_specs=[row(D_MODEL), row(D_RWKV), row(D_HGRN), full(D_MODEL, D_MODEL), full(1, D_MODEL),
                  full(D_MODEL, LANE), full(1, LANE)],
        out_specs=[row(D_MODEL), row(D_MODEL // 2), row(LANE), row(LANE)],
        compiler_params=_cparams(("parallel",)),
        name="outproj_router",
    )(x, yr, yh, wo, nf, wr, br)


MOE_NF = D_EXPERT // MOE_TF
MOE_ISSUE = MOE_TM // MOE_NF
MOE_AHEAD_TILES = MOE_RT
MOE_AHEAD = MOE_AHEAD_TILES * MOE_TM
MOE_RING = 2 * MOE_RT
MOE_DUMP = 16 * MOE_TM
HALF = D_MODEL // 2


def _ring_row_copy(x_hbm, tok, ring, sems, u):
    slot = (u // MOE_TM) % MOE_RING
    return pltpu.make_async_copy(x_hbm.at[pl.ds(tok, 1)], ring.at[slot, pl.ds(u % MOE_TM, 1)], sems.at[slot])


def _ring_tile_wait(x_hbm, ring, sems, tile):
    slot = tile % MOE_RING
    pltpu.make_async_copy(x_hbm.at[pl.ds(0, MOE_TM)], ring.at[slot], sems.at[slot]).wait()


def _scatter_row_copy(ybuf, ysc_hbm, sems, dst, u):
    slot = (u // MOE_TM) % MOE_RING
    return pltpu.make_async_copy(ybuf.at[slot, pl.ds(u % MOE_TM, 1)], ysc_hbm.at[pl.ds(dst, 1)], sems.at[slot])


def _scatter_tile_wait(ybuf, ysc_hbm, sems, tile):
    slot = tile % MOE_RING
    pltpu.make_async_copy(ybuf.at[slot], ysc_hbm.at[pl.ds(0, MOE_TM)], sems.at[slot]).wait()


def _moe_kernel(ie_ref, it0_ref, int_ref, nused_ref, rowsrc_ref, rowslot_ref, x_hbm, wg_ref, wu_ref, wd_ref,
                bg_ref, bu_ref, bd_ref, ysc_hbm, ring, xbuf, acc, ybuf, wgb, wub, wdb, ring_sems, scat_sems,
                fill_sem, *, dump0):
    i = pl.program_id(0)
    f = pl.program_id(1)
    n_items = pl.num_programs(0)
    nt = int_ref[i]
    tile0 = it0_ref[i]
    n_used = nused_ref[0]

    @pl.when((i == 0) & (f == 0))
    def _():
        for t in range(MOE_RING):
            ybuf[t] = jnp.zeros((MOE_TM, HALF), U32)
        for s in range(MOE_DUMP // MOE_TM):
            cp = pltpu.make_async_copy(ybuf.at[0], ysc_hbm.at[pl.ds(dump0 + s * MOE_TM, MOE_TM)], fill_sem)
            cp.start()
            cp.wait()

        def head(r, carry):
            _ring_row_copy(x_hbm, rowsrc_ref[r], ring, ring_sems, r).start()
            return carry

        lax.fori_loop(0, MOE_AHEAD, head, 0)

    @pl.when(nt > 0)
    def _():
        @pl.when(f == 0)
        def _():
            def load(t, carry):
                _ring_tile_wait(x_hbm, ring, ring_sems, tile0 + t)
                w = ring[(tile0 + t) % MOE_RING]
                lo = lax.bitcast_convert_type(w << 16, F32).astype(BF16)
                hi = lax.bitcast_convert_type(w & jnp.uint32(0xFFFF0000), F32).astype(BF16)
                xbuf[t, :, 0:HALF] = lo
                xbuf[t, :, HALF:D_MODEL] = hi
                acc[t] = jnp.broadcast_to(bd_ref[0], (MOE_TM, D_MODEL))
                return carry

            lax.fori_loop(0, nt, load, 0)

        wgb[...] = wg_ref[0].astype(BF16)
        wub[...] = wu_ref[0].astype(BF16)
        wdb[...] = wd_ref[0].astype(BF16)
        bg = bg_ref[0]
        bu = bu_ref[0]

        def tile(t, carry):
            step = f * nt + t
            pos = tile0 * MOE_TM + step * MOE_ISSUE
            row = (step * MOE_ISSUE) % MOE_TM
            tile_s = tile0 + step // MOE_NF
            slot_s = tile_s % MOE_RING
            slot_g = (tile_s + MOE_AHEAD_TILES) % MOE_RING
            for q in range(MOE_ISSUE):
                pltpu.make_async_copy(x_hbm.at[pl.ds(rowsrc_ref[pos + MOE_AHEAD + q], 1)],
                                      ring.at[slot_g, pl.ds(row + q, 1)], ring_sems.at[slot_g]).start()
            for q in range(MOE_ISSUE):
                pltpu.make_async_copy(ybuf.at[slot_s, pl.ds(row + q, 1)],
                                      ysc_hbm.at[pl.ds(rowslot_ref[pos + q], 1)], scat_sems.at[slot_s]).start()
            x = xbuf[t]
            gate = jnp.dot(x, wgb[...], preferred_element_type=F32) + bg
            up = jnp.dot(x, wub[...], preferred_element_type=F32) + bu
            gate = jnp.minimum(gate, SWIGLU_LIMIT)
            up = jnp.clip(up, -SWIGLU_LIMIT, SWIGLU_LIMIT)
            hid = (up + 1.0) * gate * _sigmoid(SWIGLU_ALPHA * gate)
            acc[t] += jnp.dot(hid.astype(BF16), wdb[...], preferred_element_type=F32)
            return carry

        lax.fori_loop(0, nt, tile, 0)

        @pl.when(f == MOE_NF - 1)
        def _():
            def pack(t, carry):
                vt = tile0 + t + MOE_AHEAD_TILES

                @pl.when(vt >= MOE_RING)
                def _():
                    _scatter_tile_wait(ybuf, ysc_hbm, scat_sems, vt)

                bits = lax.bitcast_convert_type(acc[t].astype(BF16).astype(F32), U32)
                ybuf[vt % MOE_RING] = (bits[:, HALF:] & jnp.uint32(0xFFFF0000)) | (bits[:, :HALF] >> 16)
                return carry

            lax.fori_loop(0, nt, pack, 0)

    @pl.when((i == n_items - 1) & (f == MOE_NF - 1))
    def _():
        for s in range(MOE_AHEAD_TILES):
            _ring_tile_wait(x_hbm, ring, ring_sems, n_used + s)

        def flush(u, carry):
            _scatter_row_copy(ybuf, ysc_hbm, scat_sems, rowslot_ref[u], u).start()
            return carry

        lax.fori_loop(n_used * MOE_TM, n_used * MOE_TM + MOE_AHEAD, flush, 0)

        def retire(vt, carry):
            _scatter_tile_wait(ybuf, ysc_hbm, scat_sems, vt)
            return carry

        lax.fori_loop(jnp.maximum(n_used - MOE_AHEAD_TILES, 0), n_used + MOE_AHEAD_TILES, retire, 0)


def _moe_experts(item_e, item_t0, item_nt, n_used, row_src, row_slot, x_packed, w_gu, b_gu, w_down, b_down,
                 n_items, n_tok):
    nf = MOE_NF
    dump0 = TOP_K * n_tok

    def fcol(i, f, int_):
        return jnp.where(int_[i] > 0, f, nf - 1)

    in_specs = [
        pl.BlockSpec(memory_space=pl.ANY),
        pl.BlockSpec((1, D_MODEL, MOE_TF), lambda i, f, ie, it0, int_, *_: (ie[i], 0, fcol(i, f, int_))),
        pl.BlockSpec((1, D_MODEL, MOE_TF), lambda i, f, ie, it0, int_, *_: (ie[i], 0, nf + fcol(i, f, int_))),
        pl.BlockSpec((1, MOE_TF, D_MODEL), lambda i, f, ie, it0, int_, *_: (ie[i], fcol(i, f, int_), 0)),
        pl.BlockSpec((1, 1, MOE_TF), lambda i, f, ie, it0, int_, *_: (ie[i], 0, fcol(i, f, int_))),
        pl.BlockSpec((1, 1, MOE_TF), lambda i, f, ie, it0, int_, *_: (ie[i], 0, nf + fcol(i, f, int_))),
        pl.BlockSpec((1, 1, D_MODEL), lambda i, f, ie, it0, int_, *_: (ie[i], 0, 0)),
    ]
    return pl.pallas_call(
        functools.partial(_moe_kernel, dump0=dump0),
        out_shape=jax.ShapeDtypeStruct((dump0 + MOE_DUMP, HALF), U32),
        grid_spec=pltpu.PrefetchScalarGridSpec(
            num_scalar_prefetch=6,
            grid=(n_items, nf),
            in_specs=in_specs,
            out_specs=pl.BlockSpec(memory_space=pl.ANY),
            scratch_shapes=[pltpu.VMEM((MOE_RING, MOE_TM, HALF), U32),
                            pltpu.VMEM((MOE_RT, MOE_TM, D_MODEL), BF16),
                            pltpu.VMEM((MOE_RT, MOE_TM, D_MODEL), F32),
                            pltpu.VMEM((MOE_RING, MOE_TM, HALF), U32),
                            pltpu.VMEM((D_MODEL, MOE_TF), BF16),
                            pltpu.VMEM((D_MODEL, MOE_TF), BF16),
                            pltpu.VMEM((MOE_TF, D_MODEL), BF16),
                            pltpu.SemaphoreType.DMA((MOE_RING,)),
                            pltpu.SemaphoreType.DMA((MOE_RING,)),
                            pltpu.SemaphoreType.DMA(())]),
        compiler_params=_cparams(("arbitrary", "arbitrary")),
        name="moe_experts",
    )(item_e, item_t0, item_nt, n_used, row_src, row_slot, x_packed, w_gu, w_gu, w_down, b_gu, b_gu, b_down)


def _ple_kernel(h_ref, gate_ref, y0_ref, y1_ref, y2_ref, y3_ref, p_ref, np_ref, wg_ref, wp_ref, nfin_ref, o_ref):
    gates = gate_ref[...]
    lo = jnp.zeros((h_ref.shape[0], HALF), F32)
    hi = jnp.zeros((h_ref.shape[0], HALF), F32)
    for kk, y_ref in enumerate((y0_ref, y1_ref, y2_ref, y3_ref)):
        w = y_ref[...]
        g = gates[:, kk:kk + 1]
        lo = lo + g * lax.bitcast_convert_type(w << 16, F32)
        hi = hi + g * lax.bitcast_convert_type(w & jnp.uint32(0xFFFF0000), F32)
    h = h_ref[...] + jnp.concatenate([lo, hi], axis=1)
    gate = _sigmoid(_dot(_rmsnorm(h, np_ref[...]), wg_ref[...]))
    h = h + gate * _dot(p_ref[...], wp_ref[...])
    o_ref[...] = _rmsnorm(h, nfin_ref[...])


def _ple_final(h, gates, ysc, plane, tok0, p, n_ple, wg, wp, n_fin, tm):
    t = h.shape[0]
    row = lambda n: pl.BlockSpec((tm, n), lambda i: (i, 0))
    full = lambda a, b: pl.BlockSpec((a, b), lambda i: (0, 0))
    ysp = lambda kk: pl.BlockSpec((tm, HALF), lambda i: ((kk * plane + tok0) // tm + i, 0))
    return pl.pallas_call(
        _ple_kernel,
        out_shape=jax.ShapeDtypeStruct((t, D_MODEL), F32),
        grid=(t // tm,),
        in_specs=[row(D_MODEL), row(LANE), ysp(0), ysp(1), ysp(2), ysp(3), row(D_PLE), full(1, D_MODEL),
                  full(D_MODEL, D_MODEL), full(D_PLE, D_MODEL), full(1, D_MODEL)],
        out_specs=row(D_MODEL),
        compiler_params=_cparams(("parallel",)),
        name="ple_final",
    )(h, gates, ysc, ysc, ysc, ysc, p, n_ple, wg, wp, n_fin)


def _pad_shift_cols(a):
    def z(n):
        return jnp.zeros(a.shape[:-1] + (n,), a.dtype)
    c0 = 3 * D_RWKV
    c1 = c0 + LORA_W
    c2 = c1 + LORA_A
    return jnp.concatenate([a[..., :c0], a[..., c0:c1], z(LW_PAD - LORA_W), a[..., c1:c2], z(LA_PAD - LORA_A),
                            a[..., c2:], z(LG_PAD - LORA_G)], axis=-1)


def _unpad_shift_cols(a):
    return jnp.concatenate([a[..., :OFF_WD], a[..., OFF_WD:OFF_WD + LORA_W], a[..., OFF_AD:OFF_AD + LORA_A],
                            a[..., OFF_GD:OFF_GD + LORA_G]], axis=-1)


def _pad_rows(a, n):
    return jnp.concatenate([a, jnp.zeros((n - a.shape[0],) + a.shape[1:], a.dtype)], axis=0)


def _routing(idx, n_rows_cap, n_items_cap):
    t = idx.shape[0]
    na = t * TOP_K
    flat_e = idx.reshape(na)
    onehot = (flat_e[:, None] == jnp.arange(N_EXPERTS, dtype=I32)[None, :]).astype(I32)
    cs = jnp.cumsum(onehot, axis=0)
    rank = jnp.take_along_axis(cs, flat_e[:, None], axis=1)[:, 0] - 1
    counts = cs[-1]
    ptiles = (counts + MOE_TM - 1) // MOE_TM
    pend = jnp.cumsum(ptiles)
    pstart = pend - ptiles
    pos = pstart[flat_e] * MOE_TM + rank
    n_stream = n_rows_cap + 2 * MOE_AHEAD
    assign = jnp.full((n_stream,), -1, I32).at[pos].set(jnp.arange(na, dtype=I32))
    is_pad = assign < 0
    row_src = jnp.where(is_pad, t, assign // TOP_K)
    u = jnp.arange(n_stream, dtype=I32)
    shifted = jnp.concatenate([jnp.full((MOE_AHEAD,), -1, I32), assign[:n_stream - MOE_AHEAD]])
    row_slot = jnp.where(shifted < 0, TOP_K * t + u % MOE_DUMP, (shifted % TOP_K) * t + shifted // TOP_K)
    items_per_e = (ptiles + MOE_RT - 1) // MOE_RT
    iend = jnp.cumsum(items_per_e)
    istart = iend - items_per_e
    ii = jnp.arange(n_items_cap, dtype=I32)
    e_of = jnp.minimum(jnp.searchsorted(iend, ii, side="right"), N_EXPERTS - 1).astype(I32)
    jj = ii - istart[e_of]
    used = ii < iend[-1]
    item_nt = jnp.where(used, jnp.clip(ptiles[e_of] - jj * MOE_RT, 0, MOE_RT), 0).astype(I32)
    item_t0 = jnp.where(used, pstart[e_of] + jj * MOE_RT, 0).astype(I32)
    last_e = e_of[jnp.maximum(iend[-1] - 1, 0)]
    item_e = jnp.where(used, e_of, last_e).astype(I32)
    n_used = jnp.stack([pend[-1], iend[-1]]).astype(I32)
    return row_src, row_slot, n_used, item_e, item_t0, item_nt


def kernel(x_prompt, x_sample, p_prompt, p_sample, state_rwkv_shift, state_rwkv, state_hgrn, norm_mix, w_in,
           mu_shift, w0, w_up, a0, a_up, g_up, k_k, k_a, r_k, lnx_w, lnx_b, hgrn_lb, hgrn_norm, w_out,
           norm_ffn, w_router, b_router, w_gu, b_gu, w_down, b_down, norm_ple, w_ple_gate, w_ple_proj,
           norm_final):
    depth = w_in.shape[0]
    assert depth == 1
    li = 0
    bp, lp = x_prompt.shape[0], x_prompt.shape[1]
    bs, ls = x_sample.shape[0], x_sample.shape[1]
    ls_pad = SUBLANE
    tp = bp * lp

    w_in_p = jnp.concatenate([_pad_shift_cols(w_in[li][:, :D_SHIFT]), w_in[li][:, D_SHIFT:]], axis=1).astype(BF16)
    row = lambda a: a.reshape(1, -1).astype(F32)
    pp = {
        "mu": row(_pad_shift_cols(mu_shift[li])),
        "w0": row(w0[li]), "a0": row(a0[li]), "k_k": row(k_k[li]), "k_a": row(k_a[li]),
        "w_up": _pad_rows(w_up[li], LW_PAD).astype(BF16),
        "a_up": _pad_rows(a_up[li], LA_PAD).astype(BF16),
        "g_up": _pad_rows(g_up[li], LG_PAD).astype(BF16),
    }
    rk = row(r_k[li])
    lnw = row(lnx_w[li])
    lnb = row(lnx_b[li])
    lower = jax.nn.softmax(hgrn_lb.astype(F32), axis=0)
    lb = row(jnp.cumsum(lower, axis=0)[li])
    nw = row(hgrn_norm[li])
    wo = w_out[li].astype(BF16)
    nf = row(norm_ffn[li])
    wr = jnp.concatenate([w_router[li], jnp.zeros((D_MODEL, LANE - N_EXPERTS), F32)], axis=1)
    br = jnp.concatenate([b_router[li], jnp.zeros((LANE - N_EXPERTS,), F32)]).reshape(1, LANE)
    n_ple = row(norm_ple[li])
    wpg = w_ple_gate[li].astype(BF16)
    wpp = w_ple_proj[li].astype(BF16)
    n_fin = row(norm_final)
    g_mix = row(norm_mix[li])

    def mixer(x2d, shift_prev, s_rwkv, s_hgrn, bn, l_total, l_valid, tm_in, bb_prep, tt, bb_scan, ub_scan, c,
              tm_out):
        proj = _inproj(x2d, g_mix, w_in_p, tm_in, 768)
        streams = _rwkv_prep(proj, shift_prev, pp, bn, l_total, l_valid, bb_prep, tt)
        yr, s_rwkv_new = _rwkv_scan(streams, rk, lnw, lnb, s_rwkv, bn, l_total, bb_scan, ub_scan, c)
        yh, s_hgrn_new = _hgrn_scan(proj, lb, nw, s_hgrn, bn, l_total, l_valid, bb_scan, ub_scan, c)
        h1, xn2, idx, gates = _outproj_router(x2d, yr.reshape(bn * l_total, D_RWKV),
                                              yh.reshape(bn * l_total, D_HGRN), wo, nf, wr, br, tm_out)
        new_shift = _unpad_shift_cols(proj.reshape(bn, l_total, D_IN_PAD)[:, l_valid - 1, :D_SHIFT_PAD])
        return h1, xn2, idx, gates, new_shift, s_rwkv_new, s_hgrn_new

    xp2 = x_prompt.reshape(tp, D_MODEL)
    zero_shift = jnp.zeros((bp, 1, D_SHIFT_PAD), F32)
    h1p, xn2p, idxp, gatesp, shift_p, rwkv_p, hgrn_p = mixer(
        xp2, zero_shift, None, None, bp, lp, lp, 512, 1, 256, bp, 4, 64, 256)

    xs_pad = jnp.concatenate([x_sample, jnp.zeros((bs, ls_pad - ls, D_MODEL), F32)], axis=1)
    xs2 = xs_pad.reshape(bs * ls_pad, D_MODEL)
    shift_s0 = _pad_shift_cols(state_rwkv_shift[li]).reshape(bs, 1, D_SHIFT_PAD)
    h1s, xn2s, idxs, gatess, shift_s, rwkv_s, hgrn_s = mixer(
        xs2, shift_s0, state_rwkv[li], state_hgrn[li], bs, ls_pad, ls, 512, 16, ls_pad, 16, 1, ls_pad, 256)

    def compact(a):
        return a.reshape(bs, ls_pad, a.shape[-1])[:, :ls].reshape(bs * ls, a.shape[-1])

    h1s, xn2s, idxs, gatess = compact(h1s), compact(xn2s), compact(idxs), compact(gatess)
    ts = bs * ls

    t_all = tp + ts
    idx_all = jnp.concatenate([idxp[:, :TOP_K], idxs[:, :TOP_K]], axis=0)
    n_tiles_cap = -(-(t_all * TOP_K) // MOE_TM) + N_EXPERTS
    n_rows_cap = n_tiles_cap * MOE_TM
    n_items_cap = N_EXPERTS + n_tiles_cap // MOE_RT
    row_src, row_slot, n_used, item_e, item_t0, item_nt = _routing(idx_all, n_rows_cap, n_items_cap)
    x_packed = jnp.concatenate([xn2p, xn2s, jnp.zeros((SUBLANE, HALF), U32)], axis=0)
    ysc = _moe_experts(item_e, item_t0, item_nt, n_used, row_src, row_slot, x_packed, w_gu[li],
                       b_gu[li].reshape(N_EXPERTS, 1, -1), w_down[li], b_down[li].reshape(N_EXPERTS, 1, -1),
                       n_items_cap, t_all)

    tm_fin = 256
    assert tp % tm_fin == 0 and ts % tm_fin == 0
    y_p = _ple_final(h1p, gatesp, ysc, t_all, 0, p_prompt[li].reshape(tp, D_PLE), n_ple, wpg, wpp, n_fin, tm_fin)
    y_s = _ple_final(h1s, gatess, ysc, t_all, tp, p_sample[li].reshape(ts, D_PLE), n_ple, wpg, wpp, n_fin, tm_fin)

    return (y_p.reshape(bp, lp, D_MODEL), y_s.reshape(bs, ls, D_MODEL),
            shift_p[None], rwkv_p[None], hgrn_p[None],
            shift_s[None], rwkv_s[None], hgrn_s[None])
```

```python
import functools

import jax
import jax.numpy as jnp
from jax import lax
from jax.experimental import pallas as pl
from jax.experimental.pallas import tpu as pltpu

F32 = jnp.float32
BF16 = jnp.bfloat16
I32 = jnp.int32
U32 = jnp.uint32

D_MODEL = 2048
D_RWKV = 1024
D_HGRN = 1024
RWKV_HEAD = 64
N_RWKV_HEADS = 16
HGRN_HEAD = 128
N_HGRN_HEADS = 8
LORA_W = 64
LORA_A = 64
LORA_G = 160
D_SHIFT = 3 * D_RWKV + LORA_W + LORA_A + LORA_G
N_EXPERTS = 32
TOP_K = 4
D_EXPERT = 2048
SWIGLU_LIMIT = 7.0
SWIGLU_ALPHA = 1.702
D_PLE = 256
RMS_EPS = 1e-6
GN_EPS = 64e-5
HGRN_NORM_EPS = 1e-5

LANE = 128
SUBLANE = 8
N_UNITS = 8

LW_PAD = LANE
LA_PAD = LANE
LG_PAD = 2 * LANE
OFF_WD = 3 * D_RWKV
OFF_AD = OFF_WD + LW_PAD
OFF_GD = OFF_AD + LA_PAD
D_SHIFT_PAD = OFF_GD + LG_PAD
D_IN_PAD = D_SHIFT_PAD + 4 * D_HGRN

MOE_TM = 256
MOE_RT = 5
MOE_TF = 256
VMEM_LIMIT = 56 * 1024 * 1024


def _cparams(sem, vmem=VMEM_LIMIT):
    return pltpu.CompilerParams(dimension_semantics=sem, vmem_limit_bytes=vmem)


def _rmsnorm(x, g):
    return x * lax.rsqrt(jnp.mean(x * x, axis=-1, keepdims=True) + RMS_EPS) * g


def _dot(a, b):
    return jnp.dot(a.astype(BF16), b.astype(BF16), preferred_element_type=F32)


def _dot_nt(a, b):
    return lax.dot_general(a.astype(BF16), b.astype(BF16), (((1,), (1,)), ((), ())),
                           preferred_element_type=F32)


def _dot_tn(a, b):
    return lax.dot_general(a.astype(BF16), b.astype(BF16), (((0,), (0,)), ((), ())),
                           preferred_element_type=F32)


def _split3(x):
    h = x.astype(BF16)
    r = x - h.astype(F32)
    m = r.astype(BF16)
    l = (r - m.astype(F32)).astype(BF16)
    return h, m, l


def _dot_exact_rhs(a, b_bf16):
    h, m, l = _split3(a)
    d = functools.partial(jnp.dot, preferred_element_type=F32)
    return d(h, b_bf16) + d(m, b_bf16) + d(l, b_bf16)


def _dot_exact_lhs(a_bf16, b):
    h, m, l = _split3(b)
    d = functools.partial(jnp.dot, preferred_element_type=F32)
    return d(a_bf16, h) + d(a_bf16, m) + d(a_bf16, l)


def _iota2(shape, dim):
    return lax.broadcasted_iota(I32, shape, dim)


def _cumsum_time(x):
    c = x.shape[0]
    tri = (_iota2((c, c), 0) >= _iota2((c, c), 1)).astype(BF16)
    return _dot_exact_lhs(tri, x)


def _same_head_mask():
    return (_iota2((LANE, LANE), 0) >= RWKV_HEAD) == (_iota2((LANE, LANE), 1) >= RWKV_HEAD)


def _sigmoid(x):
    return 1.0 / (1.0 + jnp.exp(-x))


def _win_layout_kernel(w_ref, o_ref):
    w = w_ref[...]
    rows = w.shape[0]
    c0 = 3 * D_RWKV
    c1 = c0 + LORA_W
    c2 = c1 + LORA_A
    zeros = lambda n: jnp.zeros((rows, n), F32)
    o_ref[...] = jnp.concatenate(
        [w[:, :c0], w[:, c0:c1], zeros(LW_PAD - LORA_W), w[:, c1:c2], zeros(LA_PAD - LORA_A),
         w[:, c2:D_SHIFT], zeros(LG_PAD - LORA_G), w[:, D_SHIFT:]], axis=1).astype(BF16)


def _win_layout(w, tk):
    d, n = w.shape
    return pl.pallas_call(
        _win_layout_kernel,
        out_shape=jax.ShapeDtypeStruct((d, D_IN_PAD), BF16),
        grid=(d // tk,),
        in_specs=[pl.BlockSpec((tk, n), lambda i: (i, 0))],
        out_specs=pl.BlockSpec((tk, D_IN_PAD), lambda i: (i, 0)),
        compiler_params=_cparams(("parallel",)),
        name="win_layout",
    )(w)


def _inproj_kernel(x_ref, g_ref, w_ref, o_ref, xn_ref):
    @pl.when(pl.program_id(1) == 0)
    def _():
        xn_ref[...] = _rmsnorm(x_ref[...], g_ref[...]).astype(BF16)

    o_ref[...] = jnp.dot(xn_ref[...], w_ref[...], preferred_element_type=F32)


def _inproj(x, g, w, tm, tn):
    t, d = x.shape
    n = w.shape[1]
    return pl.pallas_call(
        _inproj_kernel,
        out_shape=jax.ShapeDtypeStruct((t, n), F32),
        grid=(t // tm, n // tn),
        in_specs=[pl.BlockSpec((tm, d), lambda i, j: (i, 0)),
                  pl.BlockSpec((1, d), lambda i, j: (0, 0)),
                  pl.BlockSpec((d, tn), lambda i, j: (0, j))],
        out_specs=pl.BlockSpec((tm, tn), lambda i, j: (i, j)),
        scratch_shapes=[pltpu.VMEM((tm, d), BF16)],
        compiler_params=_cparams(("parallel", "arbitrary")),
        name="inproj",
    )(x, g, w)


def _rwkv_prep_kernel(x_ref, p8_ref, sh_ref, mu_ref, w0_ref, wup_ref, a0_ref, aup_ref, gup_ref,
                      kk_ref, ka_ref,
                      r_o, k_o, v_o, lw_o, a_o, b_o, g_o, *, bb, tt, l_valid, l_total):
    ti = pl.program_id(1)
    w = D_SHIFT_PAD
    x3 = x_ref[...].reshape(bb, tt, w)
    rolled = pltpu.roll(x3, 1, axis=1)
    prev_tail = p8_ref[...].reshape(bb, SUBLANE, w)[:, SUBLANE - 1:SUBLANE, :]
    first = jnp.where(ti == 0, sh_ref[...], prev_tail)
    t_in = _iota2((bb, tt, w), 1)
    prev = jnp.where(t_in == 0, first, rolled)
    xs = (x3 + (prev - x3) * mu_ref[...]).reshape(bb * tt, w)

    r = xs[:, 0:D_RWKV]
    k = xs[:, D_RWKV:2 * D_RWKV]
    v = xs[:, 2 * D_RWKV:3 * D_RWKV]
    wd = xs[:, OFF_WD:OFF_WD + LW_PAD]
    ad = xs[:, OFF_AD:OFF_AD + LA_PAD]
    gd = xs[:, OFF_GD:OFF_GD + LG_PAD]

    z = -(w0_ref[...] + _dot(jnp.tanh(wd), wup_ref[...]))
    softplus = jnp.maximum(z, 0.0) + jnp.log(1.0 + jnp.exp(-jnp.abs(z)))
    lw = -jnp.exp(-softplus - 0.5)
    asig = _sigmoid(a0_ref[...] + _dot(ad, aup_ref[...]))
    g = _dot(_sigmoid(gd), gup_ref[...])

    kk = k * kk_ref[...]
    same_head = _same_head_mask().astype(BF16)
    sq = kk * kk
    ssq = jnp.concatenate(
        [_dot_exact_rhs(sq[:, u * LANE:(u + 1) * LANE], same_head) for u in range(N_UNITS)], axis=1)
    kkn = kk / jnp.maximum(jnp.sqrt(ssq), 1e-12)
    k2 = k * (1.0 + (asig - 1.0) * ka_ref[...])
    a_vec = -kkn
    b_vec = kkn * asig

    if l_valid < l_total:
        t_glob = (ti * tt + _iota2((bb, tt, D_RWKV), 1)).reshape(bb * tt, D_RWKV)
        ok = t_glob < l_valid
        zero = jnp.zeros_like(k2)
        lw, k2, v, a_vec, b_vec = (jnp.where(ok, t, zero) for t in (lw, k2, v, a_vec, b_vec))

    r_o[...] = r
    k_o[...] = k2
    v_o[...] = v
    lw_o[...] = lw
    a_o[...] = a_vec
    b_o[...] = b_vec
    g_o[...] = g


def _rwkv_prep(proj, shift_pad, pp, bn, l_total, l_valid, bb, tt):
    nt = l_total // tt
    rows = bb * tt
    w = D_SHIFT_PAD
    row_spec = pl.BlockSpec((rows, w), lambda bi, ti: (bi * nt + ti, 0))
    p8_spec = pl.BlockSpec((bb * SUBLANE, w),
                           lambda bi, ti: (jnp.maximum((bi * nt + ti) * (tt // SUBLANE) - 1, 0), 0))
    vec = lambda n: pl.BlockSpec((1, n), lambda bi, ti: (0, 0))
    mat = lambda a, b: pl.BlockSpec((a, b), lambda bi, ti: (0, 0))
    out_spec = pl.BlockSpec((rows, D_RWKV), lambda bi, ti: (bi * nt + ti, 0))
    out_sds = jax.ShapeDtypeStruct((bn * l_total, D_RWKV), F32)
    kern = functools.partial(_rwkv_prep_kernel, bb=bb, tt=tt, l_valid=l_valid, l_total=l_total)
    return pl.pallas_call(
        kern,
        out_shape=[out_sds] * 7,
        grid=(bn // bb, nt),
        in_specs=[row_spec, p8_spec,
                  pl.BlockSpec((bb, 1, w), lambda bi, ti: (bi, 0, 0)),
                  vec(w), vec(D_RWKV), mat(LW_PAD, D_RWKV), vec(D_RWKV), mat(LA_PAD, D_RWKV),
                  mat(LG_PAD, D_RWKV), vec(D_RWKV), vec(D_RWKV)],
        out_specs=[out_spec] * 7,
        compiler_params=_cparams(("parallel", "arbitrary")),
        name="rwkv_prep",
    )(proj, proj, shift_pad, pp["mu"], pp["w0"], pp["w_up"], pp["a0"], pp["a_up"], pp["g_up"],
      pp["k_k"], pp["k_a"])


def _rwkv_scan_kernel(*refs, bb, ub, c, has_s0):
    if has_s0:
        (r_ref, k_ref, v_ref, lw_ref, a_ref, b_ref, g_ref, rk_ref, lnw_ref, lnb_ref, s0_ref,
         y_ref, so_ref, s_scr) = refs
    else:
        (r_ref, k_ref, v_ref, lw_ref, a_ref, b_ref, g_ref, rk_ref, lnw_ref, lnb_ref,
         y_ref, so_ref, s_scr) = refs
        s0_ref = None
    ci = pl.program_id(2)
    n_chunks = pl.num_programs(2)

    lane = _iota2((1, LANE), 1)
    m0 = (lane < RWKV_HEAD).astype(F32)
    m1 = 1.0 - m0
    bd_mask = _same_head_mask().astype(F32)

    c2 = 2 * c
    ri = _iota2((c2, c2), 0)
    cj = _iota2((c2, c2), 1)
    same_blk = (ri >= c) == (cj >= c)
    mask_s = jnp.where(same_blk, (ri > cj).astype(F32), 0.0)
    mask_i = jnp.where(same_blk, (ri >= cj).astype(F32), 0.0)
    eye = (ri == cj).astype(F32)
    n_sq = max((c - 1).bit_length() - 1, 0)

    head_avg = bd_mask.astype(BF16)

    chains = [(j, w) for j in range(bb) for w in range(ub)]
    seqs = range(len(chains))

    @pl.when(ci == 0)
    def _():
        for n, (j, w) in enumerate(chains):
            if has_s0:
                zero = jnp.zeros((RWKV_HEAD, RWKV_HEAD), F32)
                top = jnp.concatenate([s0_ref[j, 2 * w], zero], axis=1)
                bottom = jnp.concatenate([zero, s0_ref[j, 2 * w + 1]], axis=1)
                s_scr[n] = jnp.concatenate([top, bottom], axis=0)
            else:
                s_scr[n] = jnp.zeros((LANE, LANE), F32)

    stack2 = lambda lo, hi: jnp.concatenate([lo, hi], axis=0)
    unit = lambda ref, j, w: ref[j, :, w * LANE:(w + 1) * LANE]
    vec = lambda ref, w: ref[:, w * LANE:(w + 1) * LANE]
    r = [unit(r_ref, j, w) for j, w in chains]
    k = [unit(k_ref, j, w) for j, w in chains]
    v = [unit(v_ref, j, w) for j, w in chains]
    lw = [unit(lw_ref, j, w) for j, w in chains]
    a = [unit(a_ref, j, w) for j, w in chains]
    b = [unit(b_ref, j, w) for j, w in chains]
    cum = [_cumsum_time(x) for x in lw]
    clast = [x[c - 1:c, :] for x in cum]
    p_inv = [jnp.exp(-x) for x in cum]
    a2 = [stack2(a[j] * jnp.exp(cum[j] - lw[j]) * m0, a[j] * jnp.exp(cum[j] - lw[j]) * m1) for j in seqs]
    r2 = [stack2(r[j] * jnp.exp(cum[j]) * m0, r[j] * jnp.exp(cum[j]) * m1) for j in seqs]
    b2 = [stack2(b[j] * p_inv[j], b[j] * p_inv[j]) for j in seqs]
    k2 = [stack2(k[j] * p_inv[j], k[j] * p_inv[j]) for j in seqs]
    v2 = [stack2(v[j] * m0, v[j] * m1) for j in seqs]
    lab = [mask_s * _dot_nt(a2[j], b2[j]) for j in seqs]
    lak = [mask_s * _dot_nt(a2[j], k2[j]) for j in seqs]
    rb = [mask_i * _dot_nt(r2[j], b2[j]) for j in seqs]
    rkm = [mask_i * _dot_nt(r2[j], k2[j]) for j in seqs]
    tinv = [eye + x for x in lab]
    xp = lab
    for _ in range(n_sq):
        xp = [_dot(x, x) for x in xp]
        tinv = [tinv[j] + _dot(tinv[j], xp[j]) for j in seqs]
    lakv = [_dot(lak[j], v2[j]) for j in seqs]
    rkv = [_dot(rkm[j], v2[j]) for j in seqs]
    s0 = [s_scr[n] for n in seqs]
    ar_s0 = [_dot_nt(stack2(a2[n], r2[n]), s0[n]) for n in seqs]
    u2 = [_dot(tinv[n], ar_s0[n][:c2] + lakv[n]) for n in seqs]
    y2 = [ar_s0[n][c2:] + _dot(rb[n], u2[n]) + rkv[n] for n in seqs]
    y = [x[:c] + x[c:] for x in y2]
    u = [x[:c] + x[c:] for x in u2]
    p_last = [jnp.exp(clast[n] - cum[n]) for n in seqs]
    for n in seqs:
        s_scr[n] = s0[n] * jnp.exp(clast[n]) + bd_mask * _dot_tn(
            stack2(u[n], v[n]), stack2(b[n] * p_last[n], k[n] * p_last[n]))

    mu = [_dot(x, head_avg) * (1.0 / RWKV_HEAD) for x in y]
    dlt = [y[n] - mu[n] for n in seqs]
    var = [_dot(x * x, head_avg) * (1.0 / RWKV_HEAD) for x in dlt]
    bonus = [_dot(r[n] * k[n] * vec(rk_ref, w), head_avg) * v[n] for n, (j, w) in enumerate(chains)]
    for n, (j, w) in enumerate(chains):
        yn = dlt[n] * lax.rsqrt(var[n] + GN_EPS) * vec(lnw_ref, w) + vec(lnb_ref, w)
        y_ref[j, :, w * LANE:(w + 1) * LANE] = (yn + bonus[n]) * unit(g_ref, j, w)

    @pl.when(ci == n_chunks - 1)
    def _():
        for n, (j, w) in enumerate(chains):
            s_fin = s_scr[n]
            so_ref[j, 2 * w] = s_fin[:RWKV_HEAD, :RWKV_HEAD]
            so_ref[j, 2 * w + 1] = pltpu.roll(s_fin, RWKV_HEAD, axis=1)[RWKV_HEAD:, :RWKV_HEAD]


def _rwkv_scan(streams, rk, lnw, lnb, s0, bn, l_total, bb, ub, c):
    nc = l_total // c
    has_s0 = s0 is not None
    blk = pl.BlockSpec((bb, c, ub * LANE), lambda bi, u, ci: (bi, ci, u))
    vec = pl.BlockSpec((1, ub * LANE), lambda bi, u, ci: (0, u))
    st = pl.BlockSpec((bb, 2 * ub, RWKV_HEAD, RWKV_HEAD), lambda bi, u, ci: (bi, u, 0, 0))
    in_specs = [blk] * 7 + [vec] * 3 + ([st] if has_s0 else [])
    args = [s.reshape(bn, l_total, D_RWKV) for s in streams] + [rk, lnw, lnb] + ([s0] if has_s0 else [])
    kern = functools.partial(_rwkv_scan_kernel, bb=bb, ub=ub, c=c, has_s0=has_s0)
    return pl.pallas_call(
        kern,
        out_shape=[jax.ShapeDtypeStruct((bn, l_total, D_RWKV), F32),
                   jax.ShapeDtypeStruct((bn, N_RWKV_HEADS, RWKV_HEAD, RWKV_HEAD), F32)],
        grid=(bn // bb, N_UNITS // ub, nc),
        in_specs=in_specs,
        out_specs=[blk, st],
        scratch_shapes=[pltpu.VMEM((bb * ub, LANE, LANE), F32)],
        compiler_params=_cparams(("parallel", "parallel", "arbitrary")),
        name="rwkv_scan",
    )(*args)


def _hgrn_kernel(*refs, bb, ub, c, has_s0, l_valid, l_total):
    if has_s0:
        q_ref, f_ref, i_ref, og_ref, lb_ref, nw_ref, s0_ref, y_ref, so_ref, s_scr = refs
    else:
        q_ref, f_ref, i_ref, og_ref, lb_ref, nw_ref, y_ref, so_ref, s_scr = refs
        s0_ref = None
    ci = pl.program_id(2)
    n_chunks = pl.num_programs(2)
    tri = (_iota2((c, c), 0) >= _iota2((c, c), 1)).astype(F32)
    mid = max(c // 2 - 1, 0)
    chains = [(j, w) for j in range(bb) for w in range(ub)]
    seqs = range(len(chains))
    unit = lambda ref, j, w: ref[j, :, w * LANE:(w + 1) * LANE]

    @pl.when(ci == 0)
    def _():
        for n, (j, w) in enumerate(chains):
            if has_s0:
                s_scr[n] = s0_ref[j, w]
            else:
                s_scr[n] = jnp.zeros((LANE, LANE), F32)

    def key_column(x):
        ones = jnp.ones((c, LANE), BF16)
        d = lambda p: lax.dot_general(p, ones, (((0,), (0,)), ((), ())), preferred_element_type=F32)
        h, m, l = _split3(x)
        return d(h) + d(m) + d(l)

    q = [unit(q_ref, j, w) * _sigmoid(unit(q_ref, j, w)) for j, w in chains]
    f = [lb_ref[:, w * LANE:(w + 1) * LANE] + (1.0 - lb_ref[:, w * LANE:(w + 1) * LANE])
         * _sigmoid(unit(f_ref, j, w)) for j, w in chains]
    logf = [jnp.log(x) for x in f]
    kf = [1.0 - x for x in f]
    v = [unit(i_ref, j, w) for j, w in chains]
    if l_valid < l_total:
        ok = (ci * c + _iota2((c, LANE), 0)) < l_valid
        logf = [jnp.where(ok, x, 0.0) for x in logf]
        kf = [jnp.where(ok, x, 0.0) for x in kf]
    cum = [_cumsum_time(x) for x in logf]
    cref = [x[mid:mid + 1, :] for x in cum]
    clast = [x[c - 1:c, :] for x in cum]
    amat = [tri * _dot_nt(q[j] * jnp.exp(cum[j] - cref[j]), kf[j] * jnp.exp(cref[j] - cum[j])) for j in seqs]
    st = [s_scr[j] for j in seqs]
    o = [_dot(q[j] * jnp.exp(cum[j]), st[j]) + _dot(amat[j], v[j]) for j in seqs]
    decay = [jnp.exp(key_column(x)) for x in logf]
    for j in seqs:
        s_scr[j] = st[j] * decay[j] + _dot_tn(kf[j] * jnp.exp(clast[j] - cum[j]), v[j])
    for n, (j, w) in enumerate(chains):
        on = o[n] * lax.rsqrt(jnp.mean(o[n] * o[n], axis=-1, keepdims=True) + HGRN_NORM_EPS) * nw_ref[...]
        og = unit(og_ref, j, w)
        y_ref[j, :, w * LANE:(w + 1) * LANE] = on * (og * _sigmoid(og))

    @pl.when(ci == n_chunks - 1)
    def _():
        for n, (j, w) in enumerate(chains):
            so_ref[j, w] = s_scr[n]


def _hgrn_scan(proj, lb, nw, s0, bn, l_total, l_valid, bb, ub, c):
    nc = l_total // c
    has_s0 = s0 is not None
    base = D_SHIFT_PAD // (ub * LANE)
    per = D_HGRN // (ub * LANE)
    proj3 = proj.reshape(bn, l_total, D_IN_PAD)

    def col(part):
        return pl.BlockSpec((bb, c, ub * LANE), lambda bi, u, ci: (bi, ci, base + part * per + u))

    st = pl.BlockSpec((bb, ub, LANE, LANE), lambda bi, u, ci: (bi, u, 0, 0))
    in_specs = [col(0), col(1), col(2), col(3),
                pl.BlockSpec((1, ub * LANE), lambda bi, u, ci: (0, u)),
                pl.BlockSpec((1, LANE), lambda bi, u, ci: (0, 0))] + ([st] if has_s0 else [])
    args = [proj3, proj3, proj3, proj3, lb, nw] + ([s0] if has_s0 else [])
    kern = functools.partial(_hgrn_kernel, bb=bb, ub=ub, c=c, has_s0=has_s0, l_valid=l_valid,
                             l_total=l_total)
    return pl.pallas_call(
        kern,
        out_shape=[jax.ShapeDtypeStruct((bn, l_total, D_HGRN), F32),
                   jax.ShapeDtypeStruct((bn, N_UNITS, LANE, LANE), F32)],
        grid=(bn // bb, N_UNITS // ub, nc),
        in_specs=in_specs,
        out_specs=[pl.BlockSpec((bb, c, ub * LANE), lambda bi, u, ci: (bi, ci, u)), st],
        scratch_shapes=[pltpu.VMEM((bb * ub, LANE, LANE), F32)],
        compiler_params=_cparams(("parallel", "parallel", "arbitrary")),
        name="hgrn_scan",
    )(*args)


def _outproj_kernel(x_ref, yr_ref, yh_ref, wo_ref, nf_ref, wr_ref, br_ref,
                    h_ref, xn_ref, idx_ref, gate_ref):
    h = (x_ref[...] + _dot(yr_ref[...], wo_ref[0:D_RWKV, :]) + _dot(yh_ref[...], wo_ref[D_RWKV:, :]))
    h_ref[...] = h
    xn = _rmsnorm(h, nf_ref[...])
    xh = xn.astype(BF16)
    bits = lax.bitcast_convert_type(xh.astype(F32), U32)
    xn_ref[...] = (bits[:, D_MODEL // 2:] & jnp.uint32(0xFFFF0000)) | (bits[:, :D_MODEL // 2] >> 16)
    wr = wr_ref[...]
    wh = wr.astype(BF16)
    logits = (_dot(xh, wh) + _dot(xn - xh.astype(F32), wh) + _dot(xh, wr - wh.astype(F32))
              + br_ref[...])
    tm = logits.shape[0]
    lane = _iota2((tm, LANE), 1).astype(F32)
    neg = jnp.float32(-jnp.inf)
    work = jnp.where(lane < N_EXPERTS, logits, neg)
    idx_out = jnp.zeros((tm, LANE), I32)
    val_out = jnp.zeros((tm, LANE), F32)
    top0 = None
    for kk in range(TOP_K):
        m = jnp.max(work, axis=-1, keepdims=True)
        sel = jnp.min(jnp.where(work == m, lane, float(LANE)), axis=-1, keepdims=True)
        if kk == 0:
            top0 = m
        idx_out = jnp.where(lane == kk, sel.astype(I32), idx_out)
        val_out = jnp.where(lane == kk, jnp.exp(m - top0), val_out)
        work = jnp.where(lane == sel, neg, work)
    idx_ref[...] = idx_out
    gate_ref[...] = val_out / jnp.sum(val_out, axis=-1, keepdims=True)


def _outproj_router(x, yr, yh, wo, nf, wr, br, tm):
    t = x.shape[0]
    row = lambda n: pl.BlockSpec((tm, n), lambda i: (i, 0))
    full = lambda a, b: pl.BlockSpec((a, b), lambda i: (0, 0))
    return pl.pallas_call(
        _outproj_kernel,
        out_shape=[jax.ShapeDtypeStruct((t, D_MODEL), F32), jax.ShapeDtypeStruct((t, D_MODEL // 2), U32),
                   jax.ShapeDtypeStruct((t, LANE), I32), jax.ShapeDtypeStruct((t, LANE), F32)],
        grid=(t // tm,),
        in_specs=[row(D_MODEL), row(D_RWKV), row(D_HGRN), full(D_MODEL, D_MODEL), full(1, D_MODEL),
                  full(D_MODEL, LANE), full(1, LANE)],
        out_specs=[row(D_MODEL), row(D_MODEL // 2), row(LANE), row(LANE)],
        compiler_params=_cparams(("parallel",)),
        name="outproj_router",
    )(x, yr, yh, wo, nf, wr, br)


MOE_NF = D_EXPERT // MOE_TF
MOE_ISSUE = MOE_TM // MOE_NF
MOE_AHEAD_TILES = MOE_RT
MOE_AHEAD = MOE_AHEAD_TILES * MOE_TM
MOE_RING = 2 * MOE_RT
MOE_DUMP = 16 * MOE_TM
HALF = D_MODEL // 2


def _ring_row_copy(x_hbm, tok, ring, sems, u):
    slot = (u // MOE_TM) % MOE_RING
    return pltpu.make_async_copy(x_hbm.at[pl.ds(tok, 1)], ring.at[slot, pl.ds(u % MOE_TM, 1)], sems.at[slot])


def _ring_tile_wait(x_hbm, ring, sems, tile):
    slot = tile % MOE_RING
    pltpu.make_async_copy(x_hbm.at[pl.ds(0, MOE_TM)], ring.at[slot], sems.at[slot]).wait()


def _scatter_row_copy(ybuf, ysc_hbm, sems, dst, u):
    slot = (u // MOE_TM) % MOE_RING
    return pltpu.make_async_copy(ybuf.at[slot, pl.ds(u % MOE_TM, 1)], ysc_hbm.at[pl.ds(dst, 1)], sems.at[slot])


def _scatter_tile_wait(ybuf, ysc_hbm, sems, tile):
    slot = tile % MOE_RING
    pltpu.make_async_copy(ybuf.at[slot], ysc_hbm.at[pl.ds(0, MOE_TM)], sems.at[slot]).wait()


def _moe_kernel(ie_ref, it0_ref, int_ref, nused_ref, rowsrc_ref, rowslot_ref, x_hbm, wg_ref, wu_ref, wd_ref,
                bg_ref, bu_ref, bd_ref, ysc_hbm, ring, xbuf, acc, ybuf, wgb, wub, wdb, ring_sems, scat_sems,
                fill_sem, *, dump0):
    i = pl.program_id(0)
    f = pl.program_id(1)
    n_items = pl.num_programs(0)
    nt = int_ref[i]
    tile0 = it0_ref[i]
    n_used = nused_ref[0]

    @pl.when((i == 0) & (f == 0))
    def _():
        for t in range(MOE_RING):
            ybuf[t] = jnp.zeros((MOE_TM, HALF), U32)
        for s in range(MOE_DUMP // MOE_TM):
            cp = pltpu.make_async_copy(ybuf.at[0], ysc_hbm.at[pl.ds(dump0 + s * MOE_TM, MOE_TM)], fill_sem)
            cp.start()
            cp.wait()

        def head(r, carry):
            _ring_row_copy(x_hbm, rowsrc_ref[r], ring, ring_sems, r).start()
            return carry

        lax.fori_loop(0, MOE_AHEAD, head, 0)

    @pl.when(nt > 0)
    def _():
        @pl.when(f == 0)
        def _():
            def load(t, carry):
                _ring_tile_wait(x_hbm, ring, ring_sems, tile0 + t)
                w = ring[(tile0 + t) % MOE_RING]
                lo = lax.bitcast_convert_type(w << 16, F32).astype(BF16)
                hi = lax.bitcast_convert_type(w & jnp.uint32(0xFFFF0000), F32).astype(BF16)
                xbuf[t, :, 0:HALF] = lo
                xbuf[t, :, HALF:D_MODEL] = hi
                acc[t] = jnp.broadcast_to(bd_ref[0], (MOE_TM, D_MODEL))
                return carry

            lax.fori_loop(0, nt, load, 0)

        wgb[...] = wg_ref[0].astype(BF16)
        wub[...] = wu_ref[0].astype(BF16)
        wdb[...] = wd_ref[0].astype(BF16)
        bg = bg_ref[0]
        bu = bu_ref[0]

        def issue(t):
            step = f * nt + t
            pos = tile0 * MOE_TM + step * MOE_ISSUE
            row = (step * MOE_ISSUE) % MOE_TM
            tile_s = tile0 + step // MOE_NF
            slot_s = tile_s % MOE_RING
            slot_g = (tile_s + MOE_AHEAD_TILES) % MOE_RING
            for q in range(MOE_ISSUE):
                pltpu.make_async_copy(x_hbm.at[pl.ds(rowsrc_ref[pos + MOE_AHEAD + q], 1)],
                                      ring.at[slot_g, pl.ds(row + q, 1)], ring_sems.at[slot_g]).start()
            for q in range(MOE_ISSUE):
                pltpu.make_async_copy(ybuf.at[slot_s, pl.ds(row + q, 1)],
                                      ysc_hbm.at[pl.ds(rowslot_ref[pos + q], 1)], scat_sems.at[slot_s]).start()

        def tiles(ts):
            for t in ts:
                issue(t)
            xs = [xbuf[t] for t in ts]
            gates = [jnp.minimum(jnp.dot(x, wgb[...], preferred_element_type=F32) + bg, SWIGLU_LIMIT) for x in xs]
            ups = [jnp.clip(jnp.dot(x, wub[...], preferred_element_type=F32) + bu, -SWIGLU_LIMIT, SWIGLU_LIMIT)
                   for x in xs]
            hids = [((u + 1.0) * g * _sigmoid(SWIGLU_ALPHA * g)).astype(BF16) for g, u in zip(gates, ups)]
            for t, h in zip(ts, hids):
                acc[t] += jnp.dot(h, wdb[...], preferred_element_type=F32)

        def pair(p, carry):
            tiles([2 * p, 2 * p + 1])
            return carry

        lax.fori_loop(0, nt // 2, pair, 0)

        @pl.when(nt % 2 == 1)
        def _():
            tiles([nt - 1])

        @pl.when(f == MOE_NF - 1)
        def _():
            def pack(t, carry):
                vt = tile0 + t + MOE_AHEAD_TILES

                @pl.when(vt >= MOE_RING)
                def _():
                    _scatter_tile_wait(ybuf, ysc_hbm, scat_sems, vt)

                bits = lax.bitcast_convert_type(acc[t].astype(BF16).astype(F32), U32)
                ybuf[vt % MOE_RING] = (bits[:, HALF:] & jnp.uint32(0xFFFF0000)) | (bits[:, :HALF] >> 16)
                return carry

            lax.fori_loop(0, nt, pack, 0)

    @pl.when((i == n_items - 1) & (f == MOE_NF - 1))
    def _():
        for s in range(MOE_AHEAD_TILES):
            _ring_tile_wait(x_hbm, ring, ring_sems, n_used + s)

        def flush(u, carry):
            _scatter_row_copy(ybuf, ysc_hbm, scat_sems, rowslot_ref[u], u).start()
            return carry

        lax.fori_loop(n_used * MOE_TM, n_used * MOE_TM + MOE_AHEAD, flush, 0)

        def retire(vt, carry):
            _scatter_tile_wait(ybuf, ysc_hbm, scat_sems, vt)
            return carry

        lax.fori_loop(jnp.maximum(n_used - MOE_AHEAD_TILES, 0), n_used + MOE_AHEAD_TILES, retire, 0)


def _moe_experts(item_e, item_t0, item_nt, n_used, row_src, row_slot, x_packed, w_gu, b_gu, w_down, b_down,
                 n_items, n_tok):
    nf = MOE_NF
    dump0 = TOP_K * n_tok

    def fcol(i, f, int_):
        return jnp.where(int_[i] > 0, f, nf - 1)

    in_specs = [
        pl.BlockSpec(memory_space=pl.ANY),
        pl.BlockSpec((1, D_MODEL, MOE_TF), lambda i, f, ie, it0, int_, *_: (ie[i], 0, fcol(i, f, int_))),
        pl.BlockSpec((1, D_MODEL, MOE_TF), lambda i, f, ie, it0, int_, *_: (ie[i], 0, nf + fcol(i, f, int_))),
        pl.BlockSpec((1, MOE_TF, D_MODEL), lambda i, f, ie, it0, int_, *_: (ie[i], fcol(i, f, int_), 0)),
        pl.BlockSpec((1, 1, MOE_TF), lambda i, f, ie, it0, int_, *_: (ie[i], 0, fcol(i, f, int_))),
        pl.BlockSpec((1, 1, MOE_TF), lambda i, f, ie, it0, int_, *_: (ie[i], 0, nf + fcol(i, f, int_))),
        pl.BlockSpec((1, 1, D_MODEL), lambda i, f, ie, it0, int_, *_: (ie[i], 0, 0)),
    ]
    return pl.pallas_call(
        functools.partial(_moe_kernel, dump0=dump0),
        out_shape=jax.ShapeDtypeStruct((dump0 + MOE_DUMP, HALF), U32),
        grid_spec=pltpu.PrefetchScalarGridSpec(
            num_scalar_prefetch=6,
            grid=(n_items, nf),
            in_specs=in_specs,
            out_specs=pl.BlockSpec(memory_space=pl.ANY),
            scratch_shapes=[pltpu.VMEM((MOE_RING, MOE_TM, HALF), U32),
                            pltpu.VMEM((MOE_RT, MOE_TM, D_MODEL), BF16),
                            pltpu.VMEM((MOE_RT, MOE_TM, D_MODEL), F32),
                            pltpu.VMEM((MOE_RING, MOE_TM, HALF), U32),
                            pltpu.VMEM((D_MODEL, MOE_TF), BF16),
                            pltpu.VMEM((D_MODEL, MOE_TF), BF16),
                            pltpu.VMEM((MOE_TF, D_MODEL), BF16),
                            pltpu.SemaphoreType.DMA((MOE_RING,)),
                            pltpu.SemaphoreType.DMA((MOE_RING,)),
                            pltpu.SemaphoreType.DMA(())]),
        compiler_params=_cparams(("arbitrary", "arbitrary")),
        name="moe_experts",
    )(item_e, item_t0, item_nt, n_used, row_src, row_slot, x_packed, w_gu, w_gu, w_down, b_gu, b_gu, b_down)


def _ple_kernel(h_ref, gate_ref, y0_ref, y1_ref, y2_ref, y3_ref, p_ref, np_ref, wg_ref, wp_ref, nfin_ref, o_ref):
    gates = gate_ref[...]
    lo = jnp.zeros((h_ref.shape[0], HALF), F32)
    hi = jnp.zeros((h_ref.shape[0], HALF), F32)
    for kk, y_ref in enumerate((y0_ref, y1_ref, y2_ref, y3_ref)):
        w = y_ref[...]
        g = gates[:, kk:kk + 1]
        lo = lo + g * lax.bitcast_convert_type(w << 16, F32)
        hi = hi + g * lax.bitcast_convert_type(w & jnp.uint32(0xFFFF0000), F32)
    h = h_ref[...] + jnp.concatenate([lo, hi], axis=1)
    gate = _sigmoid(_dot(_rmsnorm(h, np_ref[...]), wg_ref[...]))
    h = h + gate * _dot(p_ref[...], wp_ref[...])
    o_ref[...] = _rmsnorm(h, nfin_ref[...])


def _ple_final(h, gates, ysc, plane, tok0, p, n_ple, wg, wp, n_fin, tm):
    t = h.shape[0]
    row = lambda n: pl.BlockSpec((tm, n), lambda i: (i, 0))
    full = lambda a, b: pl.BlockSpec((a, b), lambda i: (0, 0))
    ysp = lambda kk: pl.BlockSpec((tm, HALF), lambda i: ((kk * plane + tok0) // tm + i, 0))
    return pl.pallas_call(
        _ple_kernel,
        out_shape=jax.ShapeDtypeStruct((t, D_MODEL), F32),
        grid=(t // tm,),
        in_specs=[row(D_MODEL), row(LANE), ysp(0), ysp(1), ysp(2), ysp(3), row(D_PLE), full(1, D_MODEL),
                  full(D_MODEL, D_MODEL), full(D_PLE, D_MODEL), full(1, D_MODEL)],
        out_specs=row(D_MODEL),
        compiler_params=_cparams(("parallel",)),
        name="ple_final",
    )(h, gates, ysc, ysc, ysc, ysc, p, n_ple, wg, wp, n_fin)


def _pad_shift_cols(a):
    def z(n):
        return jnp.zeros(a.shape[:-1] + (n,), a.dtype)
    c0 = 3 * D_RWKV
    c1 = c0 + LORA_W
    c2 = c1 + LORA_A
    return jnp.concatenate([a[..., :c0], a[..., c0:c1], z(LW_PAD - LORA_W), a[..., c1:c2], z(LA_PAD - LORA_A),
                            a[..., c2:], z(LG_PAD - LORA_G)], axis=-1)


def _unpad_shift_cols(a):
    return jnp.concatenate([a[..., :OFF_WD], a[..., OFF_WD:OFF_WD + LORA_W], a[..., OFF_AD:OFF_AD + LORA_A],
                            a[..., OFF_GD:OFF_GD + LORA_G]], axis=-1)


def _pad_rows(a, n):
    return jnp.concatenate([a, jnp.zeros((n - a.shape[0],) + a.shape[1:], a.dtype)], axis=0)


def _routing(idx, n_rows_cap, n_items_cap):
    t = idx.shape[0]
    na = t * TOP_K
    flat_e = idx.reshape(na)
    onehot = (flat_e[:, None] == jnp.arange(N_EXPERTS, dtype=I32)[None, :]).astype(I32)
    cs = jnp.cumsum(onehot, axis=0)
    rank = jnp.take_along_axis(cs, flat_e[:, None], axis=1)[:, 0] - 1
    counts = cs[-1]
    ptiles = (counts + MOE_TM - 1) // MOE_TM
    pend = jnp.cumsum(ptiles)
    pstart = pend - ptiles
    pos = pstart[flat_e] * MOE_TM + rank
    n_stream = n_rows_cap + 2 * MOE_AHEAD
    assign = jnp.full((n_stream,), -1, I32).at[pos].set(jnp.arange(na, dtype=I32))
    is_pad = assign < 0
    row_src = jnp.where(is_pad, t, assign // TOP_K)
    u = jnp.arange(n_stream, dtype=I32)
    shifted = jnp.concatenate([jnp.full((MOE_AHEAD,), -1, I32), assign[:n_stream - MOE_AHEAD]])
    row_slot = jnp.where(shifted < 0, TOP_K * t + u % MOE_DUMP, (shifted % TOP_K) * t + shifted // TOP_K)
    items_per_e = (ptiles + MOE_RT - 1) // MOE_RT
    iend = jnp.cumsum(items_per_e)
    istart = iend - items_per_e
    ii = jnp.arange(n_items_cap, dtype=I32)
    e_of = jnp.minimum(jnp.searchsorted(iend, ii, side="right"), N_EXPERTS - 1).astype(I32)
    jj = ii - istart[e_of]
    used = ii < iend[-1]
    item_nt = jnp.where(used, jnp.clip(ptiles[e_of] - jj * MOE_RT, 0, MOE_RT), 0).astype(I32)
    item_t0 = jnp.where(used, pstart[e_of] + jj * MOE_RT, 0).astype(I32)
    last_e = e_of[jnp.maximum(iend[-1] - 1, 0)]
    item_e = jnp.where(used, e_of, last_e).astype(I32)
    n_used = jnp.stack([pend[-1], iend[-1]]).astype(I32)
    return row_src, row_slot, n_used, item_e, item_t0, item_nt


def kernel(x_prompt, x_sample, p_prompt, p_sample, state_rwkv_shift, state_rwkv, state_hgrn, norm_mix, w_in,
           mu_shift, w0, w_up, a0, a_up, g_up, k_k, k_a, r_k, lnx_w, lnx_b, hgrn_lb, hgrn_norm, w_out,
           norm_ffn, w_router, b_router, w_gu, b_gu, w_down, b_down, norm_ple, w_ple_gate, w_ple_proj,
           norm_final):
    depth = w_in.shape[0]
    assert depth == 1
    li = 0
    bp, lp = x_prompt.shape[0], x_prompt.shape[1]
    bs, ls = x_sample.shape[0], x_sample.shape[1]
    ls_pad = SUBLANE
    tp = bp * lp

    w_in_p = _win_layout(w_in[li], 128)
    row = lambda a: a.reshape(1, -1).astype(F32)
    pp = {
        "mu": row(_pad_shift_cols(mu_shift[li])),
        "w0": row(w0[li]), "a0": row(a0[li]), "k_k": row(k_k[li]), "k_a": row(k_a[li]),
        "w_up": _pad_rows(w_up[li], LW_PAD).astype(BF16),
        "a_up": _pad_rows(a_up[li], LA_PAD).astype(BF16),
        "g_up": _pad_rows(g_up[li], LG_PAD).astype(BF16),
    }
    rk = row(r_k[li])
    lnw = row(lnx_w[li])
    lnb = row(lnx_b[li])
    lower = jax.nn.softmax(hgrn_lb.astype(F32), axis=0)
    lb = row(jnp.cumsum(lower, axis=0)[li])
    nw = row(hgrn_norm[li])
    wo = w_out[li].astype(BF16)
    nf = row(norm_ffn[li])
    wr = jnp.concatenate([w_router[li], jnp.zeros((D_MODEL, LANE - N_EXPERTS), F32)], axis=1)
    br = jnp.concatenate([b_router[li], jnp.zeros((LANE - N_EXPERTS,), F32)]).reshape(1, LANE)
    n_ple = row(norm_ple[li])
    wpg = w_ple_gate[li].astype(BF16)
    wpp = w_ple_proj[li].astype(BF16)
    n_fin = row(norm_final)
    g_mix = row(norm_mix[li])

    def mixer(x2d, shift_prev, s_rwkv, s_hgrn, bn, l_total, l_valid, tm_in, bb_prep, tt, bb_scan, ub_scan, c,
              tm_out):
        proj = _inproj(x2d, g_mix, w_in_p, tm_in, 768)
        streams = _rwkv_prep(proj, shift_prev, pp, bn, l_total, l_valid, bb_prep, tt)
        yr, s_rwkv_new = _rwkv_scan(streams, rk, lnw, lnb, s_rwkv, bn, l_total, bb_scan, ub_scan, c)
        yh, s_hgrn_new = _hgrn_scan(proj, lb, nw, s_hgrn, bn, l_total, l_valid, bb_scan, ub_scan, c)
        h1, xn2, idx, gates = _outproj_router(x2d, yr.reshape(bn * l_total, D_RWKV),
                                              yh.reshape(bn * l_total, D_HGRN), wo, nf, wr, br, tm_out)
        new_shift = _unpad_shift_cols(proj.reshape(bn, l_total, D_IN_PAD)[:, l_valid - 1, :D_SHIFT_PAD])
        return h1, xn2, idx, gates, new_shift, s_rwkv_new, s_hgrn_new

    xp2 = x_prompt.reshape(tp, D_MODEL)
    zero_shift = jnp.zeros((bp, 1, D_SHIFT_PAD), F32)
    h1p, xn2p, idxp, gatesp, shift_p, rwkv_p, hgrn_p = mixer(
        xp2, zero_shift, None, None, bp, lp, lp, 1024, 1, 256, bp, 4, 64, 256)

    xs_pad = jnp.concatenate([x_sample, jnp.zeros((bs, ls_pad - ls, D_MODEL), F32)], axis=1)
    xs2 = xs_pad.reshape(bs * ls_pad, D_MODEL)
    shift_s0 = _pad_shift_cols(state_rwkv_shift[li]).reshape(bs, 1, D_SHIFT_PAD)
    h1s, xn2s, idxs, gatess, shift_s, rwkv_s, hgrn_s = mixer(
        xs2, shift_s0, state_rwkv[li], state_hgrn[li], bs, ls_pad, ls, 1024, 16, ls_pad, 16, 2, ls_pad, 256)

    def compact(a):
        return a.reshape(bs, ls_pad, a.shape[-1])[:, :ls].reshape(bs * ls, a.shape[-1])

    h1s, xn2s, idxs, gatess = compact(h1s), compact(xn2s), compact(idxs), compact(gatess)
    ts = bs * ls

    t_all = tp + ts
    idx_all = jnp.concatenate([idxp[:, :TOP_K], idxs[:, :TOP_K]], axis=0)
    n_tiles_cap = -(-(t_all * TOP_K) // MOE_TM) + N_EXPERTS
    n_rows_cap = n_tiles_cap * MOE_TM
    n_items_cap = N_EXPERTS + n_tiles_cap // MOE_RT
    row_src, row_slot, n_used, item_e, item_t0, item_nt = _routing(idx_all, n_rows_cap, n_items_cap)
    x_packed = jnp.concatenate([xn2p, xn2s, jnp.zeros((SUBLANE, HALF), U32)], axis=0)
    ysc = _moe_experts(item_e, item_t0, item_nt, n_used, row_src, row_slot, x_packed, w_gu[li],
                       b_gu[li].reshape(N_EXPERTS, 1, -1), w_down[li], b_down[li].reshape(N_EXPERTS, 1, -1),
                       n_items_cap, t_all)

    tm_fin = 256
    assert tp % tm_fin == 0 and ts % tm_fin == 0
    y_p = _ple_final(h1p, gatesp, ysc, t_all, 0, p_prompt[li].reshape(tp, D_PLE), n_ple, wpg, wpp, n_fin, tm_fin)
    y_s = _ple_final(h1s, gatess, ysc, t_all, tp, p_sample[li].reshape(ts, D_PLE), n_ple, wpg, wpp, n_fin, tm_fin)

    return (y_p.reshape(bp, lp, D_MODEL), y_s.reshape(bs, ls, D_MODEL),
            shift_p[None], rwkv_p[None], hgrn_p[None],
            shift_s[None], rwkv_s[None], hgrn_s[None])
```

```python
import functools

import jax
import jax.numpy as jnp
from jax import lax
from jax.experimental import pallas as pl
from jax.experimental.pallas import tpu as pltpu

F32 = jnp.float32
BF16 = jnp.bfloat16
I32 = jnp.int32
U32 = jnp.uint32

D_MODEL = 2048
D_RWKV = 1024
D_HGRN = 1024
RWKV_HEAD = 64
N_RWKV_HEADS = 16
HGRN_HEAD = 128
N_HGRN_HEADS = 8
LORA_W = 64
LORA_A = 64
LORA_G = 160
D_SHIFT = 3 * D_RWKV + LORA_W + LORA_A + LORA_G
N_EXPERTS = 32
TOP_K = 4
D_EXPERT = 2048
SWIGLU_LIMIT = 7.0
SWIGLU_ALPHA = 1.702
D_PLE = 256
RMS_EPS = 1e-6
GN_EPS = 64e-5
HGRN_NORM_EPS = 1e-5

LANE = 128
SUBLANE = 8
N_UNITS = 8

LW_PAD = LANE
LA_PAD = LANE
LG_PAD = 2 * LANE
OFF_WD = 3 * D_RWKV
OFF_AD = OFF_WD + LW_PAD
OFF_GD = OFF_AD + LA_PAD
D_SHIFT_PAD = OFF_GD + LG_PAD
D_IN_PAD = D_SHIFT_PAD + 4 * D_HGRN

MOE_TM = 256
MOE_RT = 5
MOE_TF = 256
VMEM_LIMIT = 56 * 1024 * 1024


def _cparams(sem, vmem=VMEM_LIMIT):
    return pltpu.CompilerParams(dimension_semantics=sem, vmem_limit_bytes=vmem)


def _rmsnorm(x, g):
    return x * lax.rsqrt(jnp.mean(x * x, axis=-1, keepdims=True) + RMS_EPS) * g


def _dot(a, b):
    return jnp.dot(a.astype(BF16), b.astype(BF16), preferred_element_type=F32)


def _dot_nt(a, b):
    return lax.dot_general(a.astype(BF16), b.astype(BF16), (((1,), (1,)), ((), ())),
                           preferred_element_type=F32)


def _dot_tn(a, b):
    return lax.dot_general(a.astype(BF16), b.astype(BF16), (((0,), (0,)), ((), ())),
                           preferred_element_type=F32)


def _split3(x):
    h = x.astype(BF16)
    r = x - h.astype(F32)
    m = r.astype(BF16)
    l = (r - m.astype(F32)).astype(BF16)
    return h, m, l


def _dot_exact_rhs(a, b_bf16):
    h, m, l = _split3(a)
    d = functools.partial(jnp.dot, preferred_element_type=F32)
    return d(h, b_bf16) + d(m, b_bf16) + d(l, b_bf16)


def _dot_exact_lhs(a_bf16, b):
    h, m, l = _split3(b)
    d = functools.partial(jnp.dot, preferred_element_type=F32)
    return d(a_bf16, h) + d(a_bf16, m) + d(a_bf16, l)


def _iota2(shape, dim):
    return lax.broadcasted_iota(I32, shape, dim)


def _cumsum_time(x):
    c = x.shape[0]
    tri = (_iota2((c, c), 0) >= _iota2((c, c), 1)).astype(BF16)
    return _dot_exact_lhs(tri, x)


def _same_head_mask():
    return (_iota2((LANE, LANE), 0) >= RWKV_HEAD) == (_iota2((LANE, LANE), 1) >= RWKV_HEAD)


def _sigmoid(x):
    return 1.0 / (1.0 + jnp.exp(-x))


def _win_layout_kernel(w_ref, o_ref):
    w = w_ref[...]
    rows = w.shape[0]
    c0 = 3 * D_RWKV
    c1 = c0 + LORA_W
    c2 = c1 + LORA_A
    zeros = lambda n: jnp.zeros((rows, n), F32)
    o_ref[...] = jnp.concatenate(
        [w[:, :c0], w[:, c0:c1], zeros(LW_PAD - LORA_W), w[:, c1:c2], zeros(LA_PAD - LORA_A),
         w[:, c2:D_SHIFT], zeros(LG_PAD - LORA_G), w[:, D_SHIFT:]], axis=1).astype(BF16)


def _win_layout(w, tk):
    d, n = w.shape
    return pl.pallas_call(
        _win_layout_kernel,
        out_shape=jax.ShapeDtypeStruct((d, D_IN_PAD), BF16),
        grid=(d // tk,),
        in_specs=[pl.BlockSpec((tk, n), lambda i: (i, 0))],
        out_specs=pl.BlockSpec((tk, D_IN_PAD), lambda i: (i, 0)),
        compiler_params=_cparams(("parallel",)),
        name="win_layout",
    )(w)


def _inproj_kernel(x_ref, g_ref, w_ref, o_ref, xn_ref):
    @pl.when(pl.program_id(1) == 0)
    def _():
        xn_ref[...] = _rmsnorm(x_ref[...], g_ref[...]).astype(BF16)

    o_ref[...] = jnp.dot(xn_ref[...], w_ref[...], preferred_element_type=F32)


def _inproj(x, g, w, tm, tn):
    t, d = x.shape
    n = w.shape[1]
    return pl.pallas_call(
        _inproj_kernel,
        out_shape=jax.ShapeDtypeStruct((t, n), F32),
        grid=(t // tm, n // tn),
        in_specs=[pl.BlockSpec((tm, d), lambda i, j: (i, 0)),
                  pl.BlockSpec((1, d), lambda i, j: (0, 0)),
                  pl.BlockSpec((d, tn), lambda i, j: (0, j))],
        out_specs=pl.BlockSpec((tm, tn), lambda i, j: (i, j)),
        scratch_shapes=[pltpu.VMEM((tm, d), BF16)],
        compiler_params=_cparams(("parallel", "arbitrary")),
        name="inproj",
    )(x, g, w)


def _rwkv_prep_kernel(x_ref, p8_ref, sh_ref, mu_ref, w0_ref, wup_ref, a0_ref, aup_ref, gup_ref,
                      kk_ref, ka_ref,
                      r_o, k_o, v_o, lw_o, a_o, b_o, g_o, *, bb, tt, l_valid, l_total):
    ti = pl.program_id(1)
    w = D_SHIFT_PAD
    x3 = x_ref[...].reshape(bb, tt, w)
    rolled = pltpu.roll(x3, 1, axis=1)
    prev_tail = p8_ref[...].reshape(bb, SUBLANE, w)[:, SUBLANE - 1:SUBLANE, :]
    first = jnp.where(ti == 0, sh_ref[...], prev_tail)
    t_in = _iota2((bb, tt, w), 1)
    prev = jnp.where(t_in == 0, first, rolled)
    xs = (x3 + (prev - x3) * mu_ref[...]).reshape(bb * tt, w)

    r = xs[:, 0:D_RWKV]
    k = xs[:, D_RWKV:2 * D_RWKV]
    v = xs[:, 2 * D_RWKV:3 * D_RWKV]
    wd = xs[:, OFF_WD:OFF_WD + LW_PAD]
    ad = xs[:, OFF_AD:OFF_AD + LA_PAD]
    gd = xs[:, OFF_GD:OFF_GD + LG_PAD]

    z = -(w0_ref[...] + _dot(jnp.tanh(wd), wup_ref[...]))
    softplus = jnp.maximum(z, 0.0) + jnp.log(1.0 + jnp.exp(-jnp.abs(z)))
    lw = -jnp.exp(-softplus - 0.5)
    asig = _sigmoid(a0_ref[...] + _dot(ad, aup_ref[...]))
    g = _dot(_sigmoid(gd), gup_ref[...])

    kk = k * kk_ref[...]
    same_head = _same_head_mask().astype(BF16)
    sq = kk * kk
    ssq = jnp.concatenate(
        [_dot_exact_rhs(sq[:, u * LANE:(u + 1) * LANE], same_head) for u in range(N_UNITS)], axis=1)
    kkn = kk / jnp.maximum(jnp.sqrt(ssq), 1e-12)
    k2 = k * (1.0 + (asig - 1.0) * ka_ref[...])
    a_vec = -kkn
    b_vec = kkn * asig

    if l_valid < l_total:
        t_glob = (ti * tt + _iota2((bb, tt, D_RWKV), 1)).reshape(bb * tt, D_RWKV)
        ok = t_glob < l_valid
        zero = jnp.zeros_like(k2)
        lw, k2, v, a_vec, b_vec = (jnp.where(ok, t, zero) for t in (lw, k2, v, a_vec, b_vec))

    r_o[...] = r
    k_o[...] = k2
    v_o[...] = v
    lw_o[...] = lw
    a_o[...] = a_vec
    b_o[...] = b_vec
    g_o[...] = g


def _rwkv_prep(proj, shift_pad, pp, bn, l_total, l_valid, bb, tt):
    nt = l_total // tt
    rows = bb * tt
    w = D_SHIFT_PAD
    row_spec = pl.BlockSpec((rows, w), lambda bi, ti: (bi * nt + ti, 0))
    p8_spec = pl.BlockSpec((bb * SUBLANE, w),
                           lambda bi, ti: (jnp.maximum((bi * nt + ti) * (tt // SUBLANE) - 1, 0), 0))
    vec = lambda n: pl.BlockSpec((1, n), lambda bi, ti: (0, 0))
    mat = lambda a, b: pl.BlockSpec((a, b), lambda bi, ti: (0, 0))
    out_spec = pl.BlockSpec((rows, D_RWKV), lambda bi, ti: (bi * nt + ti, 0))
    out_sds = jax.ShapeDtypeStruct((bn * l_total, D_RWKV), F32)
    kern = functools.partial(_rwkv_prep_kernel, bb=bb, tt=tt, l_valid=l_valid, l_total=l_total)
    return pl.pallas_call(
        kern,
        out_shape=[out_sds] * 7,
        grid=(bn // bb, nt),
        in_specs=[row_spec, p8_spec,
                  pl.BlockSpec((bb, 1, w), lambda bi, ti: (bi, 0, 0)),
                  vec(w), vec(D_RWKV), mat(LW_PAD, D_RWKV), vec(D_RWKV), mat(LA_PAD, D_RWKV),
                  mat(LG_PAD, D_RWKV), vec(D_RWKV), vec(D_RWKV)],
        out_specs=[out_spec] * 7,
        compiler_params=_cparams(("parallel", "arbitrary")),
        name="rwkv_prep",
    )(proj, proj, shift_pad, pp["mu"], pp["w0"], pp["w_up"], pp["a0"], pp["a_up"], pp["g_up"],
      pp["k_k"], pp["k_a"])


def _rwkv_scan_kernel(*refs, bb, ub, c, has_s0):
    if has_s0:
        (r_ref, k_ref, v_ref, lw_ref, a_ref, b_ref, g_ref, rk_ref, lnw_ref, lnb_ref, s0_ref,
         y_ref, so_ref, s_scr) = refs
    else:
        (r_ref, k_ref, v_ref, lw_ref, a_ref, b_ref, g_ref, rk_ref, lnw_ref, lnb_ref,
         y_ref, so_ref, s_scr) = refs
        s0_ref = None
    ci = pl.program_id(2)
    n_chunks = pl.num_programs(2)

    lane = _iota2((1, LANE), 1)
    m0 = (lane < RWKV_HEAD).astype(F32)
    m1 = 1.0 - m0
    bd_mask = _same_head_mask().astype(F32)

    c2 = 2 * c
    ri = _iota2((c2, c2), 0)
    cj = _iota2((c2, c2), 1)
    same_blk = (ri >= c) == (cj >= c)
    mask_s = jnp.where(same_blk, (ri > cj).astype(F32), 0.0)
    mask_i = jnp.where(same_blk, (ri >= cj).astype(F32), 0.0)
    eye = (ri == cj).astype(F32)
    n_sq = max((c - 1).bit_length() - 1, 0)

    head_avg = bd_mask.astype(BF16)

    chains = [(j, w) for j in range(bb) for w in range(ub)]
    seqs = range(len(chains))

    @pl.when(ci == 0)
    def _():
        for n, (j, w) in enumerate(chains):
            if has_s0:
                zero = jnp.zeros((RWKV_HEAD, RWKV_HEAD), F32)
                top = jnp.concatenate([s0_ref[j, 2 * w], zero], axis=1)
                bottom = jnp.concatenate([zero, s0_ref[j, 2 * w + 1]], axis=1)
                s_scr[n] = jnp.concatenate([top, bottom], axis=0)
            else:
                s_scr[n] = jnp.zeros((LANE, LANE), F32)

    stack2 = lambda lo, hi: jnp.concatenate([lo, hi], axis=0)
    unit = lambda ref, j, w: ref[j, :, w * LANE:(w + 1) * LANE]
    vec = lambda ref, w: ref[:, w * LANE:(w + 1) * LANE]
    r = [unit(r_ref, j, w) for j, w in chains]
    k = [unit(k_ref, j, w) for j, w in chains]
    v = [unit(v_ref, j, w) for j, w in chains]
    lw = [unit(lw_ref, j, w) for j, w in chains]
    a = [unit(a_ref, j, w) for j, w in chains]
    b = [unit(b_ref, j, w) for j, w in chains]
    cum = [_cumsum_time(x) for x in lw]
    clast = [x[c - 1:c, :] for x in cum]
    p_inv = [jnp.exp(-x) for x in cum]
    a2 = [stack2(a[j] * jnp.exp(cum[j] - lw[j]) * m0, a[j] * jnp.exp(cum[j] - lw[j]) * m1) for j in seqs]
    r2 = [stack2(r[j] * jnp.exp(cum[j]) * m0, r[j] * jnp.exp(cum[j]) * m1) for j in seqs]
    b2 = [stack2(b[j] * p_inv[j], b[j] * p_inv[j]) for j in seqs]
    k2 = [stack2(k[j] * p_inv[j], k[j] * p_inv[j]) for j in seqs]
    v2 = [stack2(v[j] * m0, v[j] * m1) for j in seqs]
    lab = [mask_s * _dot_nt(a2[j], b2[j]) for j in seqs]
    lak = [mask_s * _dot_nt(a2[j], k2[j]) for j in seqs]
    rb = [mask_i * _dot_nt(r2[j], b2[j]) for j in seqs]
    rkm = [mask_i * _dot_nt(r2[j], k2[j]) for j in seqs]
    tinv = [eye + x for x in lab]
    xp = lab
    for _ in range(n_sq):
        xp = [_dot(x, x) for x in xp]
        tinv = [tinv[j] + _dot(tinv[j], xp[j]) for j in seqs]
    lakv = [_dot(lak[j], v2[j]) for j in seqs]
    rkv = [_dot(rkm[j], v2[j]) for j in seqs]
    s0 = [s_scr[n] for n in seqs]
    ar_s0 = [_dot_nt(stack2(a2[n], r2[n]), s0[n]) for n in seqs]
    u2 = [_dot(tinv[n], ar_s0[n][:c2] + lakv[n]) for n in seqs]
    y2 = [ar_s0[n][c2:] + _dot(rb[n], u2[n]) + rkv[n] for n in seqs]
    y = [x[:c] + x[c:] for x in y2]
    u = [x[:c] + x[c:] for x in u2]
    p_last = [jnp.exp(clast[n] - cum[n]) for n in seqs]
    for n in seqs:
        s_scr[n] = s0[n] * jnp.exp(clast[n]) + bd_mask * _dot_tn(
            stack2(u[n], v[n]), stack2(b[n] * p_last[n], k[n] * p_last[n]))

    mu = [_dot(x, head_avg) * (1.0 / RWKV_HEAD) for x in y]
    dlt = [y[n] - mu[n] for n in seqs]
    var = [_dot(x * x, head_avg) * (1.0 / RWKV_HEAD) for x in dlt]
    bonus = [_dot(r[n] * k[n] * vec(rk_ref, w), head_avg) * v[n] for n, (j, w) in enumerate(chains)]
    for n, (j, w) in enumerate(chains):
        yn = dlt[n] * lax.rsqrt(var[n] + GN_EPS) * vec(lnw_ref, w) + vec(lnb_ref, w)
        y_ref[j, :, w * LANE:(w + 1) * LANE] = (yn + bonus[n]) * unit(g_ref, j, w)

    @pl.when(ci == n_chunks - 1)
    def _():
        for n, (j, w) in enumerate(chains):
            s_fin = s_scr[n]
            so_ref[j, 2 * w] = s_fin[:RWKV_HEAD, :RWKV_HEAD]
            so_ref[j, 2 * w + 1] = pltpu.roll(s_fin, RWKV_HEAD, axis=1)[RWKV_HEAD:, :RWKV_HEAD]


def _rwkv_scan(streams, rk, lnw, lnb, s0, bn, l_total, bb, ub, c):
    nc = l_total // c
    has_s0 = s0 is not None
    blk = pl.BlockSpec((bb, c, ub * LANE), lambda bi, u, ci: (bi, ci, u))
    vec = pl.BlockSpec((1, ub * LANE), lambda bi, u, ci: (0, u))
    st = pl.BlockSpec((bb, 2 * ub, RWKV_HEAD, RWKV_HEAD), lambda bi, u, ci: (bi, u, 0, 0))
    in_specs = [blk] * 7 + [vec] * 3 + ([st] if has_s0 else [])
    args = [s.reshape(bn, l_total, D_RWKV) for s in streams] + [rk, lnw, lnb] + ([s0] if has_s0 else [])
    kern = functools.partial(_rwkv_scan_kernel, bb=bb, ub=ub, c=c, has_s0=has_s0)
    return pl.pallas_call(
        kern,
        out_shape=[jax.ShapeDtypeStruct((bn, l_total, D_RWKV), F32),
                   jax.ShapeDtypeStruct((bn, N_RWKV_HEADS, RWKV_HEAD, RWKV_HEAD), F32)],
        grid=(bn // bb, N_UNITS // ub, nc),
        in_specs=in_specs,
        out_specs=[blk, st],
        scratch_shapes=[pltpu.VMEM((bb * ub, LANE, LANE), F32)],
        compiler_params=_cparams(("parallel", "parallel", "arbitrary")),
        name="rwkv_scan",
    )(*args)


def _hgrn_kernel(*refs, bb, ub, c, has_s0, l_valid, l_total):
    if has_s0:
        q_ref, f_ref, i_ref, og_ref, lb_ref, nw_ref, s0_ref, y_ref, so_ref, s_scr = refs
    else:
        q_ref, f_ref, i_ref, og_ref, lb_ref, nw_ref, y_ref, so_ref, s_scr = refs
        s0_ref = None
    ci = pl.program_id(2)
    n_chunks = pl.num_programs(2)
    tri = (_iota2((c, c), 0) >= _iota2((c, c), 1)).astype(F32)
    mid = max(c // 2 - 1, 0)
    chains = [(j, w) for j in range(bb) for w in range(ub)]
    seqs = range(len(chains))
    unit = lambda ref, j, w: ref[j, :, w * LANE:(w + 1) * LANE]

    @pl.when(ci == 0)
    def _():
        for n, (j, w) in enumerate(chains):
            if has_s0:
                s_scr[n] = s0_ref[j, w]
            else:
                s_scr[n] = jnp.zeros((LANE, LANE), F32)

    def key_column(x):
        ones = jnp.ones((c, LANE), BF16)
        d = lambda p: lax.dot_general(p, ones, (((0,), (0,)), ((), ())), preferred_element_type=F32)
        h, m, l = _split3(x)
        return d(h) + d(m) + d(l)

    q = [unit(q_ref, j, w) * _sigmoid(unit(q_ref, j, w)) for j, w in chains]
    f = [lb_ref[:, w * LANE:(w + 1) * LANE] + (1.0 - lb_ref[:, w * LANE:(w + 1) * LANE])
         * _sigmoid(unit(f_ref, j, w)) for j, w in chains]
    logf = [jnp.log(x) for x in f]
    kf = [1.0 - x for x in f]
    v = [unit(i_ref, j, w) for j, w in chains]
    if l_valid < l_total:
        ok = (ci * c + _iota2((c, LANE), 0)) < l_valid
        logf = [jnp.where(ok, x, 0.0) for x in logf]
        kf = [jnp.where(ok, x, 0.0) for x in kf]
    cum = [_cumsum_time(x) for x in logf]
    cref = [x[mid:mid + 1, :] for x in cum]
    clast = [x[c - 1:c, :] for x in cum]
    amat = [tri * _dot_nt(q[j] * jnp.exp(cum[j] - cref[j]), kf[j] * jnp.exp(cref[j] - cum[j])) for j in seqs]
    st = [s_scr[j] for j in seqs]
    o = [_dot(q[j] * jnp.exp(cum[j]), st[j]) + _dot(amat[j], v[j]) for j in seqs]
    decay = [jnp.exp(key_column(x)) for x in logf]
    for j in seqs:
        s_scr[j] = st[j] * decay[j] + _dot_tn(kf[j] * jnp.exp(clast[j] - cum[j]), v[j])
    for n, (j, w) in enumerate(chains):
        on = o[n] * lax.rsqrt(jnp.mean(o[n] * o[n], axis=-1, keepdims=True) + HGRN_NORM_EPS) * nw_ref[...]
        og = unit(og_ref, j, w)
        y_ref[j, :, w * LANE:(w + 1) * LANE] = on * (og * _sigmoid(og))

    @pl.when(ci == n_chunks - 1)
    def _():
        for n, (j, w) in enumerate(chains):
            so_ref[j, w] = s_scr[n]


def _hgrn_scan(proj, lb, nw, s0, bn, l_total, l_valid, bb, ub, c):
    nc = l_total // c
    has_s0 = s0 is not None
    base = D_SHIFT_PAD // (ub * LANE)
    per = D_HGRN // (ub * LANE)
    proj3 = proj.reshape(bn, l_total, D_IN_PAD)

    def col(part):
        return pl.BlockSpec((bb, c, ub * LANE), lambda bi, u, ci: (bi, ci, base + part * per + u))

    st = pl.BlockSpec((bb, ub, LANE, LANE), lambda bi, u, ci: (bi, u, 0, 0))
    in_specs = [col(0), col(1), col(2), col(3),
                pl.BlockSpec((1, ub * LANE), lambda bi, u, ci: (0, u)),
                pl.BlockSpec((1, LANE), lambda bi, u, ci: (0, 0))] + ([st] if has_s0 else [])
    args = [proj3, proj3, proj3, proj3, lb, nw] + ([s0] if has_s0 else [])
    kern = functools.partial(_hgrn_kernel, bb=bb, ub=ub, c=c, has_s0=has_s0, l_valid=l_valid,
                             l_total=l_total)
    return pl.pallas_call(
        kern,
        out_shape=[jax.ShapeDtypeStruct((bn, l_total, D_HGRN), F32),
                   jax.ShapeDtypeStruct((bn, N_UNITS, LANE, LANE), F32)],
        grid=(bn // bb, N_UNITS // ub, nc),
        in_specs=in_specs,
        out_specs=[pl.BlockSpec((bb, c, ub * LANE), lambda bi, u, ci: (bi, ci, u)), st],
        scratch_shapes=[pltpu.VMEM((bb * ub, LANE, LANE), F32)],
        compiler_params=_cparams(("parallel", "parallel", "arbitrary")),
        name="hgrn_scan",
    )(*args)


def _outproj_kernel(x_ref, yr_ref, yh_ref, wo_ref, nf_ref, wr_ref, br_ref, cnt0_ref,
                    h_ref, xn_ref, idx_ref, gate_ref, rank_ref, cnt_ref, run_scr, *, l_valid, l_total):
    @pl.when(pl.program_id(0) == 0)
    def _():
        run_scr[...] = cnt0_ref[...]

    h = (x_ref[...] + _dot(yr_ref[...], wo_ref[0:D_RWKV, :]) + _dot(yh_ref[...], wo_ref[D_RWKV:, :]))
    h_ref[...] = h
    xn = _rmsnorm(h, nf_ref[...])
    xh = xn.astype(BF16)
    bits = lax.bitcast_convert_type(xh.astype(F32), U32)
    xn_ref[...] = (bits[:, D_MODEL // 2:] & jnp.uint32(0xFFFF0000)) | (bits[:, :D_MODEL // 2] >> 16)
    wr = wr_ref[...]
    wh = wr.astype(BF16)
    logits = (_dot(xh, wh) + _dot(xn - xh.astype(F32), wh) + _dot(xh, wr - wh.astype(F32))
              + br_ref[...])
    tm = logits.shape[0]
    lane = _iota2((tm, LANE), 1).astype(F32)
    neg = jnp.float32(-jnp.inf)
    work = jnp.where(lane < N_EXPERTS, logits, neg)
    idx_out = jnp.zeros((tm, LANE), I32)
    val_out = jnp.zeros((tm, LANE), F32)
    top0 = None
    picks = []
    for kk in range(TOP_K):
        m = jnp.max(work, axis=-1, keepdims=True)
        sel = jnp.min(jnp.where(work == m, lane, float(LANE)), axis=-1, keepdims=True)
        if kk == 0:
            top0 = m
        idx_out = jnp.where(lane == kk, sel.astype(I32), idx_out)
        val_out = jnp.where(lane == kk, jnp.exp(m - top0), val_out)
        picks.append(lane == sel)
        work = jnp.where(lane == sel, neg, work)
    idx_ref[...] = idx_out
    gate_ref[...] = val_out / jnp.sum(val_out, axis=-1, keepdims=True)

    chosen = sum(p.astype(F32) for p in picks)
    if l_valid < l_total:
        assert l_total & (l_total - 1) == 0 and tm % l_total == 0
        t_in_seq = _iota2((tm, LANE), 0) & (l_total - 1)
        chosen = jnp.where(t_in_seq < l_valid, chosen, 0.0)
    earlier = (_iota2((tm, tm), 0) > _iota2((tm, tm), 1)).astype(BF16)
    before = jnp.dot(earlier, chosen.astype(BF16), preferred_element_type=F32) + run_scr[...]
    rank_out = jnp.zeros((tm, LANE), F32)
    for kk in range(TOP_K):
        r_k = jnp.sum(jnp.where(picks[kk], before, 0.0), axis=-1, keepdims=True)
        rank_out = jnp.where(lane == kk, r_k, rank_out)
    rank_ref[...] = rank_out.astype(I32)
    run_scr[...] = run_scr[...] + jnp.sum(chosen, axis=0, keepdims=True)
    cnt_ref[...] = run_scr[...]


def _outproj_router(x, yr, yh, wo, nf, wr, br, cnt0, tm, l_valid, l_total):
    t = x.shape[0]
    row = lambda n: pl.BlockSpec((tm, n), lambda i: (i, 0))
    full = lambda a, b: pl.BlockSpec((a, b), lambda i: (0, 0))
    return pl.pallas_call(
        functools.partial(_outproj_kernel, l_valid=l_valid, l_total=l_total),
        out_shape=[jax.ShapeDtypeStruct((t, D_MODEL), F32), jax.ShapeDtypeStruct((t, D_MODEL // 2), U32),
                   jax.ShapeDtypeStruct((t, LANE), I32), jax.ShapeDtypeStruct((t, LANE), F32),
                   jax.ShapeDtypeStruct((t, LANE), I32), jax.ShapeDtypeStruct((1, LANE), F32)],
        grid=(t // tm,),
        in_specs=[row(D_MODEL), row(D_RWKV), row(D_HGRN), full(D_MODEL, D_MODEL), full(1, D_MODEL),
                  full(D_MODEL, LANE), full(1, LANE), full(1, LANE)],
        out_specs=[row(D_MODEL), row(D_MODEL // 2), row(LANE), row(LANE), row(LANE), full(1, LANE)],
        scratch_shapes=[pltpu.VMEM((1, LANE), F32)],
        compiler_params=_cparams(("arbitrary",)),
        name="outproj_router",
    )(x, yr, yh, wo, nf, wr, br, cnt0)


MOE_NF = D_EXPERT // MOE_TF
MOE_ISSUE = MOE_TM // MOE_NF
MOE_AHEAD_TILES = MOE_RT
MOE_AHEAD = MOE_AHEAD_TILES * MOE_TM
MOE_RING = 2 * MOE_RT
MOE_DUMP = 16 * MOE_TM
HALF = D_MODEL // 2


def _ring_row_copy(x_hbm, tok, ring, sems, u):
    slot = (u // MOE_TM) % MOE_RING
    return pltpu.make_async_copy(x_hbm.at[pl.ds(tok, 1)], ring.at[slot, pl.ds(u % MOE_TM, 1)], sems.at[slot])


def _ring_tile_wait(x_hbm, ring, sems, tile):
    slot = tile % MOE_RING
    pltpu.make_async_copy(x_hbm.at[pl.ds(0, MOE_TM)], ring.at[slot], sems.at[slot]).wait()


def _scatter_row_copy(ybuf, ysc_hbm, sems, dst, u):
    slot = (u // MOE_TM) % MOE_RING
    return pltpu.make_async_copy(ybuf.at[slot, pl.ds(u % MOE_TM, 1)], ysc_hbm.at[pl.ds(dst, 1)], sems.at[slot])


def _scatter_tile_wait(ybuf, ysc_hbm, sems, tile):
    slot = tile % MOE_RING
    pltpu.make_async_copy(ybuf.at[slot], ysc_hbm.at[pl.ds(0, MOE_TM)], sems.at[slot]).wait()


def _moe_kernel(ie_ref, it0_ref, int_ref, nused_ref, rowsrc_ref, rowslot_ref, x_hbm, wg_ref, wu_ref, wd_ref,
                bg_ref, bu_ref, bd_ref, ysc_hbm, ring, xbuf, acc, ybuf, wgb, wub, wdb, ring_sems, scat_sems,
                fill_sem, *, dump0):
    i = pl.program_id(0)
    f = pl.program_id(1)
    n_items = pl.num_programs(0)
    nt = int_ref[i]
    tile0 = it0_ref[i]
    n_used = nused_ref[0]

    @pl.when((i == 0) & (f == 0))
    def _():
        for t in range(MOE_RING):
            ybuf[t] = jnp.zeros((MOE_TM, HALF), U32)
        for s in range(MOE_DUMP // MOE_TM):
            cp = pltpu.make_async_copy(ybuf.at[0], ysc_hbm.at[pl.ds(dump0 + s * MOE_TM, MOE_TM)], fill_sem)
            cp.start()
            cp.wait()

        def head(r, carry):
            _ring_row_copy(x_hbm, rowsrc_ref[r], ring, ring_sems, r).start()
            return carry

        lax.fori_loop(0, MOE_AHEAD, head, 0)

    @pl.when(nt > 0)
    def _():
        @pl.when(f == 0)
        def _():
            def load(t, carry):
                _ring_tile_wait(x_hbm, ring, ring_sems, tile0 + t)
                w = ring[(tile0 + t) % MOE_RING]
                lo = lax.bitcast_convert_type(w << 16, F32).astype(BF16)
                hi = lax.bitcast_convert_type(w & jnp.uint32(0xFFFF0000), F32).astype(BF16)
                xbuf[t, :, 0:HALF] = lo
                xbuf[t, :, HALF:D_MODEL] = hi
                acc[t] = jnp.broadcast_to(bd_ref[0], (MOE_TM, D_MODEL))
                return carry

            lax.fori_loop(0, nt, load, 0)

        wgb[...] = wg_ref[0].astype(BF16)
        wub[...] = wu_ref[0].astype(BF16)
        wdb[...] = wd_ref[0].astype(BF16)
        bg = bg_ref[0]
        bu = bu_ref[0]

        def issue(t):
            step = f * nt + t
            pos = tile0 * MOE_TM + step * MOE_ISSUE
            row = (step * MOE_ISSUE) % MOE_TM
            tile_s = tile0 + step // MOE_NF
            slot_s = tile_s % MOE_RING
            slot_g = (tile_s + MOE_AHEAD_TILES) % MOE_RING
            for q in range(MOE_ISSUE):
                pltpu.make_async_copy(x_hbm.at[pl.ds(rowsrc_ref[pos + MOE_AHEAD + q], 1)],
                                      ring.at[slot_g, pl.ds(row + q, 1)], ring_sems.at[slot_g]).start()
            for q in range(MOE_ISSUE):
                pltpu.make_async_copy(ybuf.at[slot_s, pl.ds(row + q, 1)],
                                      ysc_hbm.at[pl.ds(rowslot_ref[pos + q], 1)], scat_sems.at[slot_s]).start()

        def tiles(ts):
            for t in ts:
                issue(t)
            xs = [xbuf[t] for t in ts]
            gates = [jnp.minimum(jnp.dot(x, wgb[...], preferred_element_type=F32) + bg, SWIGLU_LIMIT) for x in xs]
            ups = [jnp.clip(jnp.dot(x, wub[...], preferred_element_type=F32) + bu, -SWIGLU_LIMIT, SWIGLU_LIMIT)
                   for x in xs]
            hids = [((u + 1.0) * g * _sigmoid(SWIGLU_ALPHA * g)).astype(BF16) for g, u in zip(gates, ups)]
            for t, h in zip(ts, hids):
                acc[t] += jnp.dot(h, wdb[...], preferred_element_type=F32)

        def pair(p, carry):
            tiles([2 * p, 2 * p + 1])
            return carry

        lax.fori_loop(0, nt // 2, pair, 0)

        @pl.when(nt % 2 == 1)
        def _():
            tiles([nt - 1])

        @pl.when(f == MOE_NF - 1)
        def _():
            def pack(t, carry):
                vt = tile0 + t + MOE_AHEAD_TILES

                @pl.when(vt >= MOE_RING)
                def _():
                    _scatter_tile_wait(ybuf, ysc_hbm, scat_sems, vt)

                bits = lax.bitcast_convert_type(acc[t].astype(BF16).astype(F32), U32)
                ybuf[vt % MOE_RING] = (bits[:, HALF:] & jnp.uint32(0xFFFF0000)) | (bits[:, :HALF] >> 16)
                return carry

            lax.fori_loop(0, nt, pack, 0)

    @pl.when((i == n_items - 1) & (f == MOE_NF - 1))
    def _():
        for s in range(MOE_AHEAD_TILES):
            _ring_tile_wait(x_hbm, ring, ring_sems, n_used + s)

        def flush(u, carry):
            _scatter_row_copy(ybuf, ysc_hbm, scat_sems, rowslot_ref[u], u).start()
            return carry

        lax.fori_loop(n_used * MOE_TM, n_used * MOE_TM + MOE_AHEAD, flush, 0)

        def retire(vt, carry):
            _scatter_tile_wait(ybuf, ysc_hbm, scat_sems, vt)
            return carry

        lax.fori_loop(jnp.maximum(n_used - MOE_AHEAD_TILES, 0), n_used + MOE_AHEAD_TILES, retire, 0)


def _moe_experts(item_e, item_t0, item_nt, n_used, row_src, row_slot, x_packed, w_gu, b_gu, w_down, b_down,
                 n_items, n_tok):
    nf = MOE_NF
    dump0 = TOP_K * n_tok

    def fcol(i, f, int_):
        return jnp.where(int_[i] > 0, f, nf - 1)

    in_specs = [
        pl.BlockSpec(memory_space=pl.ANY),
        pl.BlockSpec((1, D_MODEL, MOE_TF), lambda i, f, ie, it0, int_, *_: (ie[i], 0, fcol(i, f, int_))),
        pl.BlockSpec((1, D_MODEL, MOE_TF), lambda i, f, ie, it0, int_, *_: (ie[i], 0, nf + fcol(i, f, int_))),
        pl.BlockSpec((1, MOE_TF, D_MODEL), lambda i, f, ie, it0, int_, *_: (ie[i], fcol(i, f, int_), 0)),
        pl.BlockSpec((1, 1, MOE_TF), lambda i, f, ie, it0, int_, *_: (ie[i], 0, fcol(i, f, int_))),
        pl.BlockSpec((1, 1, MOE_TF), lambda i, f, ie, it0, int_, *_: (ie[i], 0, nf + fcol(i, f, int_))),
        pl.BlockSpec((1, 1, D_MODEL), lambda i, f, ie, it0, int_, *_: (ie[i], 0, 0)),
    ]
    return pl.pallas_call(
        functools.partial(_moe_kernel, dump0=dump0),
        out_shape=jax.ShapeDtypeStruct((dump0 + MOE_DUMP, HALF), U32),
        grid_spec=pltpu.PrefetchScalarGridSpec(
            num_scalar_prefetch=6,
            grid=(n_items, nf),
            in_specs=in_specs,
            out_specs=pl.BlockSpec(memory_space=pl.ANY),
            scratch_shapes=[pltpu.VMEM((MOE_RING, MOE_TM, HALF), U32),
                            pltpu.VMEM((MOE_RT, MOE_TM, D_MODEL), BF16),
                            pltpu.VMEM((MOE_RT, MOE_TM, D_MODEL), F32),
                            pltpu.VMEM((MOE_RING, MOE_TM, HALF), U32),
                            pltpu.VMEM((D_MODEL, MOE_TF), BF16),
                            pltpu.VMEM((D_MODEL, MOE_TF), BF16),
                            pltpu.VMEM((MOE_TF, D_MODEL), BF16),
                            pltpu.SemaphoreType.DMA((MOE_RING,)),
                            pltpu.SemaphoreType.DMA((MOE_RING,)),
                            pltpu.SemaphoreType.DMA(())]),
        compiler_params=_cparams(("arbitrary", "arbitrary")),
        name="moe_experts",
    )(item_e, item_t0, item_nt, n_used, row_src, row_slot, x_packed, w_gu, w_gu, w_down, b_gu, b_gu, b_down)


def _ple_kernel(h_ref, gate_ref, y0_ref, y1_ref, y2_ref, y3_ref, p_ref, np_ref, wg_ref, wp_ref, nfin_ref, o_ref):
    gates = gate_ref[...]
    lo = jnp.zeros((h_ref.shape[0], HALF), F32)
    hi = jnp.zeros((h_ref.shape[0], HALF), F32)
    for kk, y_ref in enumerate((y0_ref, y1_ref, y2_ref, y3_ref)):
        w = y_ref[...]
        g = gates[:, kk:kk + 1]
        lo = lo + g * lax.bitcast_convert_type(w << 16, F32)
        hi = hi + g * lax.bitcast_convert_type(w & jnp.uint32(0xFFFF0000), F32)
    h = h_ref[...] + jnp.concatenate([lo, hi], axis=1)
    gate = _sigmoid(_dot(_rmsnorm(h, np_ref[...]), wg_ref[...]))
    h = h + gate * _dot(p_ref[...], wp_ref[...])
    o_ref[...] = _rmsnorm(h, nfin_ref[...])


def _ple_final(h, gates, ysc, plane, tok0, p, n_ple, wg, wp, n_fin, tm):
    t = h.shape[0]
    row = lambda n: pl.BlockSpec((tm, n), lambda i: (i, 0))
    full = lambda a, b: pl.BlockSpec((a, b), lambda i: (0, 0))
    ysp = lambda kk: pl.BlockSpec((tm, HALF), lambda i: ((kk * plane + tok0) // tm + i, 0))
    return pl.pallas_call(
        _ple_kernel,
        out_shape=jax.ShapeDtypeStruct((t, D_MODEL), F32),
        grid=(t // tm,),
        in_specs=[row(D_MODEL), row(LANE), ysp(0), ysp(1), ysp(2), ysp(3), row(D_PLE), full(1, D_MODEL),
                  full(D_MODEL, D_MODEL), full(D_PLE, D_MODEL), full(1, D_MODEL)],
        out_specs=row(D_MODEL),
        compiler_params=_cparams(("parallel",)),
        name="ple_final",
    )(h, gates, ysc, ysc, ysc, ysc, p, n_ple, wg, wp, n_fin)


def _pad_shift_cols(a):
    def z(n):
        return jnp.zeros(a.shape[:-1] + (n,), a.dtype)
    c0 = 3 * D_RWKV
    c1 = c0 + LORA_W
    c2 = c1 + LORA_A
    return jnp.concatenate([a[..., :c0], a[..., c0:c1], z(LW_PAD - LORA_W), a[..., c1:c2], z(LA_PAD - LORA_A),
                            a[..., c2:], z(LG_PAD - LORA_G)], axis=-1)


def _unpad_shift_cols(a):
    return jnp.concatenate([a[..., :OFF_WD], a[..., OFF_WD:OFF_WD + LORA_W], a[..., OFF_AD:OFF_AD + LORA_A],
                            a[..., OFF_GD:OFF_GD + LORA_G]], axis=-1)


def _pad_rows(a, n):
    return jnp.concatenate([a, jnp.zeros((n - a.shape[0],) + a.shape[1:], a.dtype)], axis=0)


def _routing(idx, rank, counts, n_rows_cap, n_items_cap):
    t = idx.shape[0]
    na = t * TOP_K
    flat_e = idx.reshape(na)
    rank = rank.reshape(na)
    ptiles = (counts + MOE_TM - 1) // MOE_TM
    pend = jnp.cumsum(ptiles)
    pstart = pend - ptiles
    pos = pstart[flat_e] * MOE_TM + rank
    n_stream = n_rows_cap + 2 * MOE_AHEAD
    assign = jnp.full((n_stream,), -1, I32).at[pos].set(jnp.arange(na, dtype=I32))
    is_pad = assign < 0
    row_src = jnp.where(is_pad, t, assign // TOP_K)
    u = jnp.arange(n_stream, dtype=I32)
    shifted = jnp.concatenate([jnp.full((MOE_AHEAD,), -1, I32), assign[:n_stream - MOE_AHEAD]])
    row_slot = jnp.where(shifted < 0, TOP_K * t + u % MOE_DUMP, (shifted % TOP_K) * t + shifted // TOP_K)
    items_per_e = (ptiles + MOE_RT - 1) // MOE_RT
    iend = jnp.cumsum(items_per_e)
    istart = iend - items_per_e
    ii = jnp.arange(n_items_cap, dtype=I32)
    e_of = jnp.minimum(jnp.searchsorted(iend, ii, side="right"), N_EXPERTS - 1).astype(I32)
    jj = ii - istart[e_of]
    used = ii < iend[-1]
    item_nt = jnp.where(used, jnp.clip(ptiles[e_of] - jj * MOE_RT, 0, MOE_RT), 0).astype(I32)
    item_t0 = jnp.where(used, pstart[e_of] + jj * MOE_RT, 0).astype(I32)
    last_e = e_of[jnp.maximum(iend[-1] - 1, 0)]
    item_e = jnp.where(used, e_of, last_e).astype(I32)
    n_used = jnp.stack([pend[-1], iend[-1]]).astype(I32)
    return row_src, row_slot, n_used, item_e, item_t0, item_nt


def kernel(x_prompt, x_sample, p_prompt, p_sample, state_rwkv_shift, state_rwkv, state_hgrn, norm_mix, w_in,
           mu_shift, w0, w_up, a0, a_up, g_up, k_k, k_a, r_k, lnx_w, lnx_b, hgrn_lb, hgrn_norm, w_out,
           norm_ffn, w_router, b_router, w_gu, b_gu, w_down, b_down, norm_ple, w_ple_gate, w_ple_proj,
           norm_final):
    depth = w_in.shape[0]
    assert depth == 1
    li = 0
    bp, lp = x_prompt.shape[0], x_prompt.shape[1]
    bs, ls = x_sample.shape[0], x_sample.shape[1]
    ls_pad = SUBLANE
    tp = bp * lp

    w_in_p = _win_layout(w_in[li], 128)
    row = lambda a: a.reshape(1, -1).astype(F32)
    pp = {
        "mu": row(_pad_shift_cols(mu_shift[li])),
        "w0": row(w0[li]), "a0": row(a0[li]), "k_k": row(k_k[li]), "k_a": row(k_a[li]),
        "w_up": _pad_rows(w_up[li], LW_PAD).astype(BF16),
        "a_up": _pad_rows(a_up[li], LA_PAD).astype(BF16),
        "g_up": _pad_rows(g_up[li], LG_PAD).astype(BF16),
    }
    rk = row(r_k[li])
    lnw = row(lnx_w[li])
    lnb = row(lnx_b[li])
    lower = jax.nn.softmax(hgrn_lb.astype(F32), axis=0)
    lb = row(jnp.cumsum(lower, axis=0)[li])
    nw = row(hgrn_norm[li])
    wo = w_out[li].astype(BF16)
    nf = row(norm_ffn[li])
    wr = jnp.concatenate([w_router[li], jnp.zeros((D_MODEL, LANE - N_EXPERTS), F32)], axis=1)
    br = jnp.concatenate([b_router[li], jnp.zeros((LANE - N_EXPERTS,), F32)]).reshape(1, LANE)
    n_ple = row(norm_ple[li])
    wpg = w_ple_gate[li].astype(BF16)
    wpp = w_ple_proj[li].astype(BF16)
    n_fin = row(norm_final)
    g_mix = row(norm_mix[li])

    def mixer(x2d, shift_prev, s_rwkv, s_hgrn, cnt0, bn, l_total, l_valid, tm_in, bb_prep, tt, bb_scan, ub_scan, c,
              tm_out):
        proj = _inproj(x2d, g_mix, w_in_p, tm_in, 768)
        streams = _rwkv_prep(proj, shift_prev, pp, bn, l_total, l_valid, bb_prep, tt)
        yr, s_rwkv_new = _rwkv_scan(streams, rk, lnw, lnb, s_rwkv, bn, l_total, bb_scan, ub_scan, c)
        yh, s_hgrn_new = _hgrn_scan(proj, lb, nw, s_hgrn, bn, l_total, l_valid, bb_scan, ub_scan, c)
        h1, xn2, idx, gates, rank, cnt = _outproj_router(
            x2d, yr.reshape(bn * l_total, D_RWKV), yh.reshape(bn * l_total, D_HGRN), wo, nf, wr, br, cnt0, tm_out,
            l_valid, l_total)
        new_shift = _unpad_shift_cols(proj.reshape(bn, l_total, D_IN_PAD)[:, l_valid - 1, :D_SHIFT_PAD])
        return h1, xn2, idx, gates, rank, cnt, new_shift, s_rwkv_new, s_hgrn_new

    xp2 = x_prompt.reshape(tp, D_MODEL)
    zero_shift = jnp.zeros((bp, 1, D_SHIFT_PAD), F32)
    h1p, xn2p, idxp, gatesp, rankp, cntp, shift_p, rwkv_p, hgrn_p = mixer(
        xp2, zero_shift, None, None, jnp.zeros((1, LANE), F32), bp, lp, lp, 1024, 1, 256, bp, 4, 64, 256)

    xs_pad = jnp.concatenate([x_sample, jnp.zeros((bs, ls_pad - ls, D_MODEL), F32)], axis=1)
    xs2 = xs_pad.reshape(bs * ls_pad, D_MODEL)
    shift_s0 = _pad_shift_cols(state_rwkv_shift[li]).reshape(bs, 1, D_SHIFT_PAD)
    h1s, xn2s, idxs, gatess, ranks, cnt_all, shift_s, rwkv_s, hgrn_s = mixer(
        xs2, shift_s0, state_rwkv[li], state_hgrn[li], cntp, bs, ls_pad, ls, 1024, 16, ls_pad, 16, 2, ls_pad, 256)

    def compact(a):
        return a.reshape(bs, ls_pad, a.shape[-1])[:, :ls].reshape(bs * ls, a.shape[-1])

    h1s, xn2s, idxs, gatess, ranks = compact(h1s), compact(xn2s), compact(idxs), compact(gatess), compact(ranks)
    ts = bs * ls

    t_all = tp + ts
    idx_all = jnp.concatenate([idxp[:, :TOP_K], idxs[:, :TOP_K]], axis=0)
    rank_all = jnp.concatenate([rankp[:, :TOP_K], ranks[:, :TOP_K]], axis=0)
    counts = cnt_all[0, :N_EXPERTS].astype(I32)
    n_tiles_cap = -(-(t_all * TOP_K) // MOE_TM) + N_EXPERTS
    n_rows_cap = n_tiles_cap * MOE_TM
    n_items_cap = N_EXPERTS + n_tiles_cap // MOE_RT
    row_src, row_slot, n_used, item_e, item_t0, item_nt = _routing(idx_all, rank_all, counts, n_rows_cap,
                                                                   n_items_cap)
    x_packed = jnp.concatenate([xn2p, xn2s, jnp.zeros((SUBLANE, HALF), U32)], axis=0)
    ysc = _moe_experts(item_e, item_t0, item_nt, n_used, row_src, row_slot, x_packed, w_gu[li],
                       b_gu[li].reshape(N_EXPERTS, 1, -1), w_down[li], b_down[li].reshape(N_EXPERTS, 1, -1),
                       n_items_cap, t_all)

    tm_fin = 256
    assert tp % tm_fin == 0 and ts % tm_fin == 0
    y_p = _ple_final(h1p, gatesp, ysc, t_all, 0, p_prompt[li].reshape(tp, D_PLE), n_ple, wpg, wpp, n_fin, tm_fin)
    y_s = _ple_final(h1s, gatess, ysc, t_all, tp, p_sample[li].reshape(ts, D_PLE), n_ple, wpg, wpp, n_fin, tm_fin)

    return (y_p.reshape(bp, lp, D_MODEL), y_s.reshape(bs, ls, D_MODEL),
            shift_p[None], rwkv_p[None], hgrn_p[None],
            shift_s[None], rwkv_s[None], hgrn_s[None])
```

```python
import functools

import jax
import jax.numpy as jnp
from jax import lax
from jax.experimental import pallas as pl
from jax.experimental.pallas import tpu as pltpu

F32 = jnp.float32
BF16 = jnp.bfloat16
I32 = jnp.int32
U32 = jnp.uint32

D_MODEL = 2048
D_RWKV = 1024
D_HGRN = 1024
RWKV_HEAD = 64
N_RWKV_HEADS = 16
HGRN_HEAD = 128
N_HGRN_HEADS = 8
LORA_W = 64
LORA_A = 64
LORA_G = 160
D_SHIFT = 3 * D_RWKV + LORA_W + LORA_A + LORA_G
N_EXPERTS = 32
TOP_K = 4
D_EXPERT = 2048
SWIGLU_LIMIT = 7.0
SWIGLU_ALPHA = 1.702
D_PLE = 256
RMS_EPS = 1e-6
GN_EPS = 64e-5
HGRN_NORM_EPS = 1e-5

LANE = 128
SUBLANE = 8
N_UNITS = 8

LW_PAD = LANE
LA_PAD = LANE
LG_PAD = 2 * LANE
OFF_WD = 3 * D_RWKV
OFF_AD = OFF_WD + LW_PAD
OFF_GD = OFF_AD + LA_PAD
D_SHIFT_PAD = OFF_GD + LG_PAD
D_IN_PAD = D_SHIFT_PAD + 4 * D_HGRN

MOE_TM = 256
MOE_RT = 5
MOE_TF = 256
VMEM_LIMIT = 56 * 1024 * 1024


def _cparams(sem, vmem=VMEM_LIMIT):
    return pltpu.CompilerParams(dimension_semantics=sem, vmem_limit_bytes=vmem)


def _rmsnorm(x, g):
    return x * lax.rsqrt(jnp.mean(x * x, axis=-1, keepdims=True) + RMS_EPS) * g


def _dot(a, b):
    return jnp.dot(a.astype(BF16), b.astype(BF16), preferred_element_type=F32)


def _dot_nt(a, b):
    return lax.dot_general(a.astype(BF16), b.astype(BF16), (((1,), (1,)), ((), ())),
                           preferred_element_type=F32)


def _dot_tn(a, b):
    return lax.dot_general(a.astype(BF16), b.astype(BF16), (((0,), (0,)), ((), ())),
                           preferred_element_type=F32)


def _split3(x):
    h = x.astype(BF16)
    r = x - h.astype(F32)
    m = r.astype(BF16)
    l = (r - m.astype(F32)).astype(BF16)
    return h, m, l


def _dot_exact_rhs(a, b_bf16):
    h, m, l = _split3(a)
    d = functools.partial(jnp.dot, preferred_element_type=F32)
    return d(h, b_bf16) + d(m, b_bf16) + d(l, b_bf16)


def _dot_exact_lhs(a_bf16, b):
    h, m, l = _split3(b)
    d = functools.partial(jnp.dot, preferred_element_type=F32)
    return d(a_bf16, h) + d(a_bf16, m) + d(a_bf16, l)


def _iota2(shape, dim):
    return lax.broadcasted_iota(I32, shape, dim)


def _cumsum_time(x):
    c = x.shape[0]
    tri = (_iota2((c, c), 0) >= _iota2((c, c), 1)).astype(BF16)
    return _dot_exact_lhs(tri, x)


def _same_head_mask():
    return (_iota2((LANE, LANE), 0) >= RWKV_HEAD) == (_iota2((LANE, LANE), 1) >= RWKV_HEAD)


def _sigmoid(x):
    return 1.0 / (1.0 + jnp.exp(-x))


def _inproj_kernel(x_ref, g_ref, w_ref, o_ref, xn_ref):
    @pl.when(pl.program_id(1) == 0)
    def _():
        xn_ref[...] = _rmsnorm(x_ref[...], g_ref[...]).astype(BF16)

    o_ref[...] = lax.dot_general(xn_ref[...], w_ref[...], (((1,), (1,)), ((), ())), preferred_element_type=F32)


def _inproj(x, g, w, tm, tn):
    t, d = x.shape
    n = w.shape[0]
    return pl.pallas_call(
        _inproj_kernel,
        out_shape=jax.ShapeDtypeStruct((t, n), F32),
        grid=(t // tm, n // tn),
        in_specs=[pl.BlockSpec((tm, d), lambda i, j: (i, 0)),
                  pl.BlockSpec((1, d), lambda i, j: (0, 0)),
                  pl.BlockSpec((tn, d), lambda i, j: (j, 0))],
        out_specs=pl.BlockSpec((tm, tn), lambda i, j: (i, j)),
        scratch_shapes=[pltpu.VMEM((tm, d), BF16)],
        compiler_params=_cparams(("parallel", "arbitrary")),
        name="inproj",
    )(x, g, w)


def _rwkv_prep_kernel(x_ref, p8_ref, sh_ref, mu_ref, w0_ref, wup_ref, a0_ref, aup_ref, gup_ref,
                      kk_ref, ka_ref,
                      r_o, k_o, v_o, lw_o, a_o, b_o, g_o, *, bb, tt, l_valid, l_total):
    ti = pl.program_id(1)
    w = D_SHIFT_PAD
    x3 = x_ref[...].reshape(bb, tt, w)
    rolled = pltpu.roll(x3, 1, axis=1)
    prev_tail = p8_ref[...].reshape(bb, SUBLANE, w)[:, SUBLANE - 1:SUBLANE, :]
    first = jnp.where(ti == 0, sh_ref[...], prev_tail)
    t_in = _iota2((bb, tt, w), 1)
    prev = jnp.where(t_in == 0, first, rolled)
    xs = (x3 + (prev - x3) * mu_ref[...]).reshape(bb * tt, w)

    r = xs[:, 0:D_RWKV]
    k = xs[:, D_RWKV:2 * D_RWKV]
    v = xs[:, 2 * D_RWKV:3 * D_RWKV]
    wd = xs[:, OFF_WD:OFF_WD + LW_PAD]
    ad = xs[:, OFF_AD:OFF_AD + LA_PAD]
    gd = xs[:, OFF_GD:OFF_GD + LG_PAD]

    z = -(w0_ref[...] + _dot(jnp.tanh(wd), wup_ref[...]))
    softplus = jnp.maximum(z, 0.0) + jnp.log(1.0 + jnp.exp(-jnp.abs(z)))
    lw = -jnp.exp(-softplus - 0.5)
    asig = _sigmoid(a0_ref[...] + _dot(ad, aup_ref[...]))
    g = _dot(_sigmoid(gd), gup_ref[...])

    kk = k * kk_ref[...]
    same_head = _same_head_mask().astype(BF16)
    sq = kk * kk
    ssq = jnp.concatenate(
        [_dot_exact_rhs(sq[:, u * LANE:(u + 1) * LANE], same_head) for u in range(N_UNITS)], axis=1)
    kkn = kk / jnp.maximum(jnp.sqrt(ssq), 1e-12)
    k2 = k * (1.0 + (asig - 1.0) * ka_ref[...])
    a_vec = -kkn
    b_vec = kkn * asig

    if l_valid < l_total:
        t_glob = (ti * tt + _iota2((bb, tt, D_RWKV), 1)).reshape(bb * tt, D_RWKV)
        ok = t_glob < l_valid
        zero = jnp.zeros_like(k2)
        lw, k2, v, a_vec, b_vec = (jnp.where(ok, t, zero) for t in (lw, k2, v, a_vec, b_vec))

    r_o[...] = r
    k_o[...] = k2
    v_o[...] = v
    lw_o[...] = lw
    a_o[...] = a_vec
    b_o[...] = b_vec
    g_o[...] = g


def _rwkv_prep(proj, shift_pad, pp, bn, l_total, l_valid, bb, tt):
    nt = l_total // tt
    rows = bb * tt
    w = D_SHIFT_PAD
    row_spec = pl.BlockSpec((rows, w), lambda bi, ti: (bi * nt + ti, 0))
    p8_spec = pl.BlockSpec((bb * SUBLANE, w),
                           lambda bi, ti: (jnp.maximum((bi * nt + ti) * (tt // SUBLANE) - 1, 0), 0))
    vec = lambda n: pl.BlockSpec((1, n), lambda bi, ti: (0, 0))
    mat = lambda a, b: pl.BlockSpec((a, b), lambda bi, ti: (0, 0))
    out_spec = pl.BlockSpec((rows, D_RWKV), lambda bi, ti: (bi * nt + ti, 0))
    out_sds = jax.ShapeDtypeStruct((bn * l_total, D_RWKV), F32)
    kern = functools.partial(_rwkv_prep_kernel, bb=bb, tt=tt, l_valid=l_valid, l_total=l_total)
    return pl.pallas_call(
        kern,
        out_shape=[out_sds] * 7,
        grid=(bn // bb, nt),
        in_specs=[row_spec, p8_spec,
                  pl.BlockSpec((bb, 1, w), lambda bi, ti: (bi, 0, 0)),
                  vec(w), vec(D_RWKV), mat(LW_PAD, D_RWKV), vec(D_RWKV), mat(LA_PAD, D_RWKV),
                  mat(LG_PAD, D_RWKV), vec(D_RWKV), vec(D_RWKV)],
        out_specs=[out_spec] * 7,
        compiler_params=_cparams(("parallel", "arbitrary")),
        name="rwkv_prep",
    )(proj, proj, shift_pad, pp["mu"], pp["w0"], pp["w_up"], pp["a0"], pp["a_up"], pp["g_up"],
      pp["k_k"], pp["k_a"])


def _rwkv_scan_kernel(*refs, bb, ub, c, has_s0):
    if has_s0:
        (r_ref, k_ref, v_ref, lw_ref, a_ref, b_ref, g_ref, rk_ref, lnw_ref, lnb_ref, s0_ref,
         y_ref, so_ref, s_scr) = refs
    else:
        (r_ref, k_ref, v_ref, lw_ref, a_ref, b_ref, g_ref, rk_ref, lnw_ref, lnb_ref,
         y_ref, so_ref, s_scr) = refs
        s0_ref = None
    ci = pl.program_id(2)
    n_chunks = pl.num_programs(2)

    lane = _iota2((1, LANE), 1)
    m0 = (lane < RWKV_HEAD).astype(F32)
    m1 = 1.0 - m0
    bd_mask = _same_head_mask().astype(F32)

    c2 = 2 * c
    ri = _iota2((c2, c2), 0)
    cj = _iota2((c2, c2), 1)
    same_blk = (ri >= c) == (cj >= c)
    mask_s = jnp.where(same_blk, (ri > cj).astype(F32), 0.0)
    mask_i = jnp.where(same_blk, (ri >= cj).astype(F32), 0.0)
    eye = (ri == cj).astype(F32)
    n_sq = max((c - 1).bit_length() - 1, 0)

    head_avg = bd_mask.astype(BF16)

    chains = [(j, w) for j in range(bb) for w in range(ub)]
    seqs = range(len(chains))

    @pl.when(ci == 0)
    def _():
        for n, (j, w) in enumerate(chains):
            if has_s0:
                zero = jnp.zeros((RWKV_HEAD, RWKV_HEAD), F32)
                top = jnp.concatenate([s0_ref[j, 2 * w], zero], axis=1)
                bottom = jnp.concatenate([zero, s0_ref[j, 2 * w + 1]], axis=1)
                s_scr[n] = jnp.concatenate([top, bottom], axis=0)
            else:
                s_scr[n] = jnp.zeros((LANE, LANE), F32)

    stack2 = lambda lo, hi: jnp.concatenate([lo, hi], axis=0)
    unit = lambda ref, j, w: ref[j, :, w * LANE:(w + 1) * LANE]
    vec = lambda ref, w: ref[:, w * LANE:(w + 1) * LANE]
    r = [unit(r_ref, j, w) for j, w in chains]
    k = [unit(k_ref, j, w) for j, w in chains]
    v = [unit(v_ref, j, w) for j, w in chains]
    lw = [unit(lw_ref, j, w) for j, w in chains]
    a = [unit(a_ref, j, w) for j, w in chains]
    b = [unit(b_ref, j, w) for j, w in chains]
    cum = [_cumsum_time(x) for x in lw]
    clast = [x[c - 1:c, :] for x in cum]
    p_inv = [jnp.exp(-x) for x in cum]
    a2 = [stack2(a[j] * jnp.exp(cum[j] - lw[j]) * m0, a[j] * jnp.exp(cum[j] - lw[j]) * m1) for j in seqs]
    r2 = [stack2(r[j] * jnp.exp(cum[j]) * m0, r[j] * jnp.exp(cum[j]) * m1) for j in seqs]
    b2 = [stack2(b[j] * p_inv[j], b[j] * p_inv[j]) for j in seqs]
    k2 = [stack2(k[j] * p_inv[j], k[j] * p_inv[j]) for j in seqs]
    v2 = [stack2(v[j] * m0, v[j] * m1) for j in seqs]
    lab = [mask_s * _dot_nt(a2[j], b2[j]) for j in seqs]
    lak = [mask_s * _dot_nt(a2[j], k2[j]) for j in seqs]
    rb = [mask_i * _dot_nt(r2[j], b2[j]) for j in seqs]
    rkm = [mask_i * _dot_nt(r2[j], k2[j]) for j in seqs]
    tinv = [eye + x for x in lab]
    xp = lab
    for _ in range(n_sq):
        xp = [_dot(x, x) for x in xp]
        tinv = [tinv[j] + _dot(tinv[j], xp[j]) for j in seqs]
    lakv = [_dot(lak[j], v2[j]) for j in seqs]
    rkv = [_dot(rkm[j], v2[j]) for j in seqs]
    s0 = [s_scr[n] for n in seqs]
    ar_s0 = [_dot_nt(stack2(a2[n], r2[n]), s0[n]) for n in seqs]
    u2 = [_dot(tinv[n], ar_s0[n][:c2] + lakv[n]) for n in seqs]
    y2 = [ar_s0[n][c2:] + _dot(rb[n], u2[n]) + rkv[n] for n in seqs]
    y = [x[:c] + x[c:] for x in y2]
    u = [x[:c] + x[c:] for x in u2]
    p_last = [jnp.exp(clast[n] - cum[n]) for n in seqs]
    for n in seqs:
        s_scr[n] = s0[n] * jnp.exp(clast[n]) + bd_mask * _dot_tn(
            stack2(u[n], v[n]), stack2(b[n] * p_last[n], k[n] * p_last[n]))

    mu = [_dot(x, head_avg) * (1.0 / RWKV_HEAD) for x in y]
    dlt = [y[n] - mu[n] for n in seqs]
    var = [_dot(x * x, head_avg) * (1.0 / RWKV_HEAD) for x in dlt]
    bonus = [_dot(r[n] * k[n] * vec(rk_ref, w), head_avg) * v[n] for n, (j, w) in enumerate(chains)]
    for n, (j, w) in enumerate(chains):
        yn = dlt[n] * lax.rsqrt(var[n] + GN_EPS) * vec(lnw_ref, w) + vec(lnb_ref, w)
        y_ref[j, :, w * LANE:(w + 1) * LANE] = (yn + bonus[n]) * unit(g_ref, j, w)

    @pl.when(ci == n_chunks - 1)
    def _():
        for n, (j, w) in enumerate(chains):
            s_fin = s_scr[n]
            so_ref[j, 2 * w] = s_fin[:RWKV_HEAD, :RWKV_HEAD]
            so_ref[j, 2 * w + 1] = pltpu.roll(s_fin, RWKV_HEAD, axis=1)[RWKV_HEAD:, :RWKV_HEAD]


def _rwkv_scan(streams, rk, lnw, lnb, s0, bn, l_total, bb, ub, c):
    nc = l_total // c
    has_s0 = s0 is not None
    blk = pl.BlockSpec((bb, c, ub * LANE), lambda bi, u, ci: (bi, ci, u))
    vec = pl.BlockSpec((1, ub * LANE), lambda bi, u, ci: (0, u))
    st = pl.BlockSpec((bb, 2 * ub, RWKV_HEAD, RWKV_HEAD), lambda bi, u, ci: (bi, u, 0, 0))
    in_specs = [blk] * 7 + [vec] * 3 + ([st] if has_s0 else [])
    args = [s.reshape(bn, l_total, D_RWKV) for s in streams] + [rk, lnw, lnb] + ([s0] if has_s0 else [])
    kern = functools.partial(_rwkv_scan_kernel, bb=bb, ub=ub, c=c, has_s0=has_s0)
    return pl.pallas_call(
        kern,
        out_shape=[jax.ShapeDtypeStruct((bn, l_total, D_RWKV), F32),
                   jax.ShapeDtypeStruct((bn, N_RWKV_HEADS, RWKV_HEAD, RWKV_HEAD), F32)],
        grid=(bn // bb, N_UNITS // ub, nc),
        in_specs=in_specs,
        out_specs=[blk, st],
        scratch_shapes=[pltpu.VMEM((bb * ub, LANE, LANE), F32)],
        compiler_params=_cparams(("parallel", "parallel", "arbitrary")),
        name="rwkv_scan",
    )(*args)


def _hgrn_kernel(*refs, bb, ub, c, has_s0, l_valid, l_total):
    if has_s0:
        q_ref, f_ref, i_ref, og_ref, lb_ref, nw_ref, s0_ref, y_ref, so_ref, s_scr = refs
    else:
        q_ref, f_ref, i_ref, og_ref, lb_ref, nw_ref, y_ref, so_ref, s_scr = refs
        s0_ref = None
    ci = pl.program_id(2)
    n_chunks = pl.num_programs(2)
    tri = (_iota2((c, c), 0) >= _iota2((c, c), 1)).astype(F32)
    mid = max(c // 2 - 1, 0)
    chains = [(j, w) for j in range(bb) for w in range(ub)]
    seqs = range(len(chains))
    unit = lambda ref, j, w: ref[j, :, w * LANE:(w + 1) * LANE]

    @pl.when(ci == 0)
    def _():
        for n, (j, w) in enumerate(chains):
            if has_s0:
                s_scr[n] = s0_ref[j, w]
            else:
                s_scr[n] = jnp.zeros((LANE, LANE), F32)

    def key_column(x):
        ones = jnp.ones((c, LANE), BF16)
        d = lambda p: lax.dot_general(p, ones, (((0,), (0,)), ((), ())), preferred_element_type=F32)
        h, m, l = _split3(x)
        return d(h) + d(m) + d(l)

    q = [unit(q_ref, j, w) * _sigmoid(unit(q_ref, j, w)) for j, w in chains]
    f = [lb_ref[:, w * LANE:(w + 1) * LANE] + (1.0 - lb_ref[:, w * LANE:(w + 1) * LANE])
         * _sigmoid(unit(f_ref, j, w)) for j, w in chains]
    logf = [jnp.log(x) for x in f]
    kf = [1.0 - x for x in f]
    v = [unit(i_ref, j, w) for j, w in chains]
    if l_valid < l_total:
        ok = (ci * c + _iota2((c, LANE), 0)) < l_valid
        logf = [jnp.where(ok, x, 0.0) for x in logf]
        kf = [jnp.where(ok, x, 0.0) for x in kf]
    cum = [_cumsum_time(x) for x in logf]
    cref = [x[mid:mid + 1, :] for x in cum]
    clast = [x[c - 1:c, :] for x in cum]
    amat = [tri * _dot_nt(q[j] * jnp.exp(cum[j] - cref[j]), kf[j] * jnp.exp(cref[j] - cum[j])) for j in seqs]
    st = [s_scr[j] for j in seqs]
    o = [_dot(q[j] * jnp.exp(cum[j]), st[j]) + _dot(amat[j], v[j]) for j in seqs]
    decay = [jnp.exp(key_column(x)) for x in logf]
    for j in seqs:
        s_scr[j] = st[j] * decay[j] + _dot_tn(kf[j] * jnp.exp(clast[j] - cum[j]), v[j])
    for n, (j, w) in enumerate(chains):
        on = o[n] * lax.rsqrt(jnp.mean(o[n] * o[n], axis=-1, keepdims=True) + HGRN_NORM_EPS) * nw_ref[...]
        og = unit(og_ref, j, w)
        y_ref[j, :, w * LANE:(w + 1) * LANE] = on * (og * _sigmoid(og))

    @pl.when(ci == n_chunks - 1)
    def _():
        for n, (j, w) in enumerate(chains):
            so_ref[j, w] = s_scr[n]


def _hgrn_scan(proj, lb, nw, s0, bn, l_total, l_valid, bb, ub, c):
    nc = l_total // c
    has_s0 = s0 is not None
    base = D_SHIFT_PAD // (ub * LANE)
    per = D_HGRN // (ub * LANE)
    proj3 = proj.reshape(bn, l_total, D_IN_PAD)

    def col(part):
        return pl.BlockSpec((bb, c, ub * LANE), lambda bi, u, ci: (bi, ci, base + part * per + u))

    st = pl.BlockSpec((bb, ub, LANE, LANE), lambda bi, u, ci: (bi, u, 0, 0))
    in_specs = [col(0), col(1), col(2), col(3),
                pl.BlockSpec((1, ub * LANE), lambda bi, u, ci: (0, u)),
                pl.BlockSpec((1, LANE), lambda bi, u, ci: (0, 0))] + ([st] if has_s0 else [])
    args = [proj3, proj3, proj3, proj3, lb, nw] + ([s0] if has_s0 else [])
    kern = functools.partial(_hgrn_kernel, bb=bb, ub=ub, c=c, has_s0=has_s0, l_valid=l_valid,
                             l_total=l_total)
    return pl.pallas_call(
        kern,
        out_shape=[jax.ShapeDtypeStruct((bn, l_total, D_HGRN), F32),
                   jax.ShapeDtypeStruct((bn, N_UNITS, LANE, LANE), F32)],
        grid=(bn // bb, N_UNITS // ub, nc),
        in_specs=in_specs,
        out_specs=[pl.BlockSpec((bb, c, ub * LANE), lambda bi, u, ci: (bi, ci, u)), st],
        scratch_shapes=[pltpu.VMEM((bb * ub, LANE, LANE), F32)],
        compiler_params=_cparams(("parallel", "parallel", "arbitrary")),
        name="hgrn_scan",
    )(*args)


def _outproj_kernel(x_ref, yr_ref, yh_ref, wo_ref, nf_ref, wr_ref, br_ref, cnt0_ref,
                    h_ref, xn_ref, idx_ref, gate_ref, rank_ref, cnt_ref, run_scr, *, l_valid, l_total):
    @pl.when(pl.program_id(0) == 0)
    def _():
        run_scr[...] = cnt0_ref[...]

    h = (x_ref[...] + _dot(yr_ref[...], wo_ref[0:D_RWKV, :]) + _dot(yh_ref[...], wo_ref[D_RWKV:, :]))
    h_ref[...] = h
    xn = _rmsnorm(h, nf_ref[...])
    xh = xn.astype(BF16)
    bits = lax.bitcast_convert_type(xh.astype(F32), U32)
    xn_ref[...] = (bits[:, D_MODEL // 2:] & jnp.uint32(0xFFFF0000)) | (bits[:, :D_MODEL // 2] >> 16)
    wr = wr_ref[...]
    wh = wr.astype(BF16)
    logits = (_dot(xh, wh) + _dot(xn - xh.astype(F32), wh) + _dot(xh, wr - wh.astype(F32))
              + br_ref[...])
    tm = logits.shape[0]
    lane = _iota2((tm, LANE), 1).astype(F32)
    neg = jnp.float32(-jnp.inf)
    work = jnp.where(lane < N_EXPERTS, logits, neg)
    idx_out = jnp.zeros((tm, LANE), I32)
    val_out = jnp.zeros((tm, LANE), F32)
    top0 = None
    picks = []
    for kk in range(TOP_K):
        m = jnp.max(work, axis=-1, keepdims=True)
        sel = jnp.min(jnp.where(work == m, lane, float(LANE)), axis=-1, keepdims=True)
        if kk == 0:
            top0 = m
        idx_out = jnp.where(lane == kk, sel.astype(I32), idx_out)
        val_out = jnp.where(lane == kk, jnp.exp(m - top0), val_out)
        picks.append(lane == sel)
        work = jnp.where(lane == sel, neg, work)
    idx_ref[...] = idx_out
    gate_ref[...] = val_out / jnp.sum(val_out, axis=-1, keepdims=True)

    chosen = sum(p.astype(F32) for p in picks)
    if l_valid < l_total:
        assert l_total & (l_total - 1) == 0 and tm % l_total == 0
        t_in_seq = _iota2((tm, LANE), 0) & (l_total - 1)
        chosen = jnp.where(t_in_seq < l_valid, chosen, 0.0)
    earlier = (_iota2((tm, tm), 0) > _iota2((tm, tm), 1)).astype(BF16)
    before = jnp.dot(earlier, chosen.astype(BF16), preferred_element_type=F32) + run_scr[...]
    rank_out = jnp.zeros((tm, LANE), F32)
    for kk in range(TOP_K):
        r_k = jnp.sum(jnp.where(picks[kk], before, 0.0), axis=-1, keepdims=True)
        rank_out = jnp.where(lane == kk, r_k, rank_out)
    rank_ref[...] = rank_out.astype(I32)
    run_scr[...] = run_scr[...] + jnp.sum(chosen, axis=0, keepdims=True)
    cnt_ref[...] = run_scr[...]


def _outproj_router(x, yr, yh, wo, nf, wr, br, cnt0, tm, l_valid, l_total):
    t = x.shape[0]
    row = lambda n: pl.BlockSpec((tm, n), lambda i: (i, 0))
    full = lambda a, b: pl.BlockSpec((a, b), lambda i: (0, 0))
    return pl.pallas_call(
        functools.partial(_outproj_kernel, l_valid=l_valid, l_total=l_total),
        out_shape=[jax.ShapeDtypeStruct((t, D_MODEL), F32), jax.ShapeDtypeStruct((t, D_MODEL // 2), U32),
                   jax.ShapeDtypeStruct((t, LANE), I32), jax.ShapeDtypeStruct((t, LANE), F32),
                   jax.ShapeDtypeStruct((t, LANE), I32), jax.ShapeDtypeStruct((1, LANE), F32)],
        grid=(t // tm,),
        in_specs=[row(D_MODEL), row(D_RWKV), row(D_HGRN), full(D_MODEL, D_MODEL), full(1, D_MODEL),
                  full(D_MODEL, LANE), full(1, LANE), full(1, LANE)],
        out_specs=[row(D_MODEL), row(D_MODEL // 2), row(LANE), row(LANE), row(LANE), full(1, LANE)],
        scratch_shapes=[pltpu.VMEM((1, LANE), F32)],
        compiler_params=_cparams(("arbitrary",)),
        name="outproj_router",
    )(x, yr, yh, wo, nf, wr, br, cnt0)


MOE_NF = D_EXPERT // MOE_TF
MOE_ISSUE = MOE_TM // MOE_NF
MOE_AHEAD_TILES = MOE_RT
MOE_AHEAD = MOE_AHEAD_TILES * MOE_TM
MOE_RING = 2 * MOE_RT
MOE_DUMP = 16 * MOE_TM
HALF = D_MODEL // 2


def _ring_row_copy(x_hbm, tok, ring, sems, u):
    slot = (u // MOE_TM) % MOE_RING
    return pltpu.make_async_copy(x_hbm.at[pl.ds(tok, 1)], ring.at[slot, pl.ds(u % MOE_TM, 1)], sems.at[slot])


def _ring_tile_wait(x_hbm, ring, sems, tile):
    slot = tile % MOE_RING
    pltpu.make_async_copy(x_hbm.at[pl.ds(0, MOE_TM)], ring.at[slot], sems.at[slot]).wait()


def _scatter_row_copy(ybuf, ysc_hbm, sems, dst, u):
    slot = (u // MOE_TM) % MOE_RING
    return pltpu.make_async_copy(ybuf.at[slot, pl.ds(u % MOE_TM, 1)], ysc_hbm.at[pl.ds(dst, 1)], sems.at[slot])


def _scatter_tile_wait(ybuf, ysc_hbm, sems, tile):
    slot = tile % MOE_RING
    pltpu.make_async_copy(ybuf.at[slot], ysc_hbm.at[pl.ds(0, MOE_TM)], sems.at[slot]).wait()


def _moe_kernel(ie_ref, it0_ref, int_ref, nused_ref, rowsrc_ref, rowslot_ref, x_hbm, wg_ref, wu_ref, wd_ref,
                bg_ref, bu_ref, bd_ref, ysc_hbm, ring, xbuf, acc, ybuf, wgb, wub, wdb, ring_sems, scat_sems,
                fill_sem, *, dump0):
    i = pl.program_id(0)
    f = pl.program_id(1)
    n_items = pl.num_programs(0)
    nt = int_ref[i]
    tile0 = it0_ref[i]
    n_used = nused_ref[0]

    @pl.when((i == 0) & (f == 0))
    def _():
        for t in range(MOE_RING):
            ybuf[t] = jnp.zeros((MOE_TM, HALF), U32)
        for s in range(MOE_DUMP // MOE_TM):
            cp = pltpu.make_async_copy(ybuf.at[0], ysc_hbm.at[pl.ds(dump0 + s * MOE_TM, MOE_TM)], fill_sem)
            cp.start()
            cp.wait()

        def head(r, carry):
            _ring_row_copy(x_hbm, rowsrc_ref[r], ring, ring_sems, r).start()
            return carry

        lax.fori_loop(0, MOE_AHEAD, head, 0)

    @pl.when(nt > 0)
    def _():
        @pl.when(f == 0)
        def _():
            def load(t, carry):
                _ring_tile_wait(x_hbm, ring, ring_sems, tile0 + t)
                w = ring[(tile0 + t) % MOE_RING]
                lo = lax.bitcast_convert_type(w << 16, F32).astype(BF16)
                hi = lax.bitcast_convert_type(w & jnp.uint32(0xFFFF0000), F32).astype(BF16)
                xbuf[t, :, 0:HALF] = lo
                xbuf[t, :, HALF:D_MODEL] = hi
                acc[t] = jnp.broadcast_to(bd_ref[0], (MOE_TM, D_MODEL))
                return carry

            lax.fori_loop(0, nt, load, 0)

        wgb[...] = wg_ref[0].astype(BF16)
        wub[...] = wu_ref[0].astype(BF16)
        wdb[...] = wd_ref[0].astype(BF16)
        bg = bg_ref[0]
        bu = bu_ref[0]

        def issue(t):
            step = f * nt + t
            pos = tile0 * MOE_TM + step * MOE_ISSUE
            row = (step * MOE_ISSUE) % MOE_TM
            tile_s = tile0 + step // MOE_NF
            slot_s = tile_s % MOE_RING
            slot_g = (tile_s + MOE_AHEAD_TILES) % MOE_RING
            for q in range(MOE_ISSUE):
                pltpu.make_async_copy(x_hbm.at[pl.ds(rowsrc_ref[pos + MOE_AHEAD + q], 1)],
                                      ring.at[slot_g, pl.ds(row + q, 1)], ring_sems.at[slot_g]).start()
            for q in range(MOE_ISSUE):
                pltpu.make_async_copy(ybuf.at[slot_s, pl.ds(row + q, 1)],
                                      ysc_hbm.at[pl.ds(rowslot_ref[pos + q], 1)], scat_sems.at[slot_s]).start()

        def tiles(ts):
            for t in ts:
                issue(t)
            xs = [xbuf[t] for t in ts]
            gates = [jnp.minimum(jnp.dot(x, wgb[...], preferred_element_type=F32) + bg, SWIGLU_LIMIT) for x in xs]
            ups = [jnp.clip(jnp.dot(x, wub[...], preferred_element_type=F32) + bu, -SWIGLU_LIMIT, SWIGLU_LIMIT)
                   for x in xs]
            hids = [((u + 1.0) * g * _sigmoid(SWIGLU_ALPHA * g)).astype(BF16) for g, u in zip(gates, ups)]
            for t, h in zip(ts, hids):
                acc[t] += jnp.dot(h, wdb[...], preferred_element_type=F32)

        def pair(p, carry):
            tiles([2 * p, 2 * p + 1])
            return carry

        lax.fori_loop(0, nt // 2, pair, 0)

        @pl.when(nt % 2 == 1)
        def _():
            tiles([nt - 1])

        @pl.when(f == MOE_NF - 1)
        def _():
            def pack(t, carry):
                vt = tile0 + t + MOE_AHEAD_TILES

                @pl.when(vt >= MOE_RING)
                def _():
                    _scatter_tile_wait(ybuf, ysc_hbm, scat_sems, vt)

                bits = lax.bitcast_convert_type(acc[t].astype(BF16).astype(F32), U32)
                ybuf[vt % MOE_RING] = (bits[:, HALF:] & jnp.uint32(0xFFFF0000)) | (bits[:, :HALF] >> 16)
                return carry

            lax.fori_loop(0, nt, pack, 0)

    @pl.when((i == n_items - 1) & (f == MOE_NF - 1))
    def _():
        for s in range(MOE_AHEAD_TILES):
            _ring_tile_wait(x_hbm, ring, ring_sems, n_used + s)

        def flush(u, carry):
            _scatter_row_copy(ybuf, ysc_hbm, scat_sems, rowslot_ref[u], u).start()
            return carry

        lax.fori_loop(n_used * MOE_TM, n_used * MOE_TM + MOE_AHEAD, flush, 0)

        def retire(vt, carry):
            _scatter_tile_wait(ybuf, ysc_hbm, scat_sems, vt)
            return carry

        lax.fori_loop(jnp.maximum(n_used - MOE_AHEAD_TILES, 0), n_used + MOE_AHEAD_TILES, retire, 0)


def _moe_experts(item_e, item_t0, item_nt, n_used, row_src, row_slot, x_packed, w_gu, b_gu, w_down, b_down,
                 n_items, n_tok):
    nf = MOE_NF
    dump0 = TOP_K * n_tok

    def fcol(i, f, int_):
        return jnp.where(int_[i] > 0, f, nf - 1)

    in_specs = [
        pl.BlockSpec(memory_space=pl.ANY),
        pl.BlockSpec((1, D_MODEL, MOE_TF), lambda i, f, ie, it0, int_, *_: (ie[i], 0, fcol(i, f, int_))),
        pl.BlockSpec((1, D_MODEL, MOE_TF), lambda i, f, ie, it0, int_, *_: (ie[i], 0, nf + fcol(i, f, int_))),
        pl.BlockSpec((1, MOE_TF, D_MODEL), lambda i, f, ie, it0, int_, *_: (ie[i], fcol(i, f, int_), 0)),
        pl.BlockSpec((1, 1, MOE_TF), lambda i, f, ie, it0, int_, *_: (ie[i], 0, fcol(i, f, int_))),
        pl.BlockSpec((1, 1, MOE_TF), lambda i, f, ie, it0, int_, *_: (ie[i], 0, nf + fcol(i, f, int_))),
        pl.BlockSpec((1, 1, D_MODEL), lambda i, f, ie, it0, int_, *_: (ie[i], 0, 0)),
    ]
    return pl.pallas_call(
        functools.partial(_moe_kernel, dump0=dump0),
        out_shape=jax.ShapeDtypeStruct((dump0 + MOE_DUMP, HALF), U32),
        grid_spec=pltpu.PrefetchScalarGridSpec(
            num_scalar_prefetch=6,
            grid=(n_items, nf),
            in_specs=in_specs,
            out_specs=pl.BlockSpec(memory_space=pl.ANY),
            scratch_shapes=[pltpu.VMEM((MOE_RING, MOE_TM, HALF), U32),
                            pltpu.VMEM((MOE_RT, MOE_TM, D_MODEL), BF16),
                            pltpu.VMEM((MOE_RT, MOE_TM, D_MODEL), F32),
                            pltpu.VMEM((MOE_RING, MOE_TM, HALF), U32),
                            pltpu.VMEM((D_MODEL, MOE_TF), BF16),
                            pltpu.VMEM((D_MODEL, MOE_TF), BF16),
                            pltpu.VMEM((MOE_TF, D_MODEL), BF16),
                            pltpu.SemaphoreType.DMA((MOE_RING,)),
                            pltpu.SemaphoreType.DMA((MOE_RING,)),
                            pltpu.SemaphoreType.DMA(())]),
        compiler_params=_cparams(("arbitrary", "arbitrary")),
        name="moe_experts",
    )(item_e, item_t0, item_nt, n_used, row_src, row_slot, x_packed, w_gu, w_gu, w_down, b_gu, b_gu, b_down)


def _ple_kernel(h_ref, gate_ref, y0_ref, y1_ref, y2_ref, y3_ref, p_ref, np_ref, wg_ref, wp_ref, nfin_ref, o_ref):
    gates = gate_ref[...]
    lo = jnp.zeros((h_ref.shape[0], HALF), F32)
    hi = jnp.zeros((h_ref.shape[0], HALF), F32)
    for kk, y_ref in enumerate((y0_ref, y1_ref, y2_ref, y3_ref)):
        w = y_ref[...]
        g = gates[:, kk:kk + 1]
        lo = lo + g * lax.bitcast_convert_type(w << 16, F32)
        hi = hi + g * lax.bitcast_convert_type(w & jnp.uint32(0xFFFF0000), F32)
    h = h_ref[...] + jnp.concatenate([lo, hi], axis=1)
    gate = _sigmoid(_dot(_rmsnorm(h, np_ref[...]), wg_ref[...]))
    h = h + gate * _dot(p_ref[...], wp_ref[...])
    o_ref[...] = _rmsnorm(h, nfin_ref[...])


def _ple_final(h, gates, ysc, plane, tok0, p, n_ple, wg, wp, n_fin, tm):
    t = h.shape[0]
    row = lambda n: pl.BlockSpec((tm, n), lambda i: (i, 0))
    full = lambda a, b: pl.BlockSpec((a, b), lambda i: (0, 0))
    ysp = lambda kk: pl.BlockSpec((tm, HALF), lambda i: ((kk * plane + tok0) // tm + i, 0))
    return pl.pallas_call(
        _ple_kernel,
        out_shape=jax.ShapeDtypeStruct((t, D_MODEL), F32),
        grid=(t // tm,),
        in_specs=[row(D_MODEL), row(LANE), ysp(0), ysp(1), ysp(2), ysp(3), row(D_PLE), full(1, D_MODEL),
                  full(D_MODEL, D_MODEL), full(D_PLE, D_MODEL), full(1, D_MODEL)],
        out_specs=row(D_MODEL),
        compiler_params=_cparams(("parallel",)),
        name="ple_final",
    )(h, gates, ysc, ysc, ysc, ysc, p, n_ple, wg, wp, n_fin)


def _pad_shift_cols(a):
    def z(n):
        return jnp.zeros(a.shape[:-1] + (n,), a.dtype)
    c0 = 3 * D_RWKV
    c1 = c0 + LORA_W
    c2 = c1 + LORA_A
    return jnp.concatenate([a[..., :c0], a[..., c0:c1], z(LW_PAD - LORA_W), a[..., c1:c2], z(LA_PAD - LORA_A),
                            a[..., c2:], z(LG_PAD - LORA_G)], axis=-1)


def _pad_shift_rows(a):
    z = lambda n: jnp.zeros((n, a.shape[1]), a.dtype)
    c0 = 3 * D_RWKV
    c1 = c0 + LORA_W
    c2 = c1 + LORA_A
    return jnp.concatenate([a[:c0], a[c0:c1], z(LW_PAD - LORA_W), a[c1:c2], z(LA_PAD - LORA_A), a[c2:],
                            z(LG_PAD - LORA_G)], axis=0)


def _unpad_shift_cols(a):
    return jnp.concatenate([a[..., :OFF_WD], a[..., OFF_WD:OFF_WD + LORA_W], a[..., OFF_AD:OFF_AD + LORA_A],
                            a[..., OFF_GD:OFF_GD + LORA_G]], axis=-1)


def _pad_rows(a, n):
    return jnp.concatenate([a, jnp.zeros((n - a.shape[0],) + a.shape[1:], a.dtype)], axis=0)


def _routing(idx, rank, counts, n_rows_cap, n_items_cap):
    t = idx.shape[0]
    na = t * TOP_K
    flat_e = idx.reshape(na)
    rank = rank.reshape(na)
    ptiles = (counts + MOE_TM - 1) // MOE_TM
    pend = jnp.cumsum(ptiles)
    pstart = pend - ptiles
    pos = pstart[flat_e] * MOE_TM + rank
    n_stream = n_rows_cap + 2 * MOE_AHEAD
    assign = jnp.full((n_stream,), -1, I32).at[pos].set(jnp.arange(na, dtype=I32), unique_indices=True,
                                                        mode="promise_in_bounds")
    is_pad = assign < 0
    row_src = jnp.where(is_pad, t, assign // TOP_K)
    u = jnp.arange(n_stream, dtype=I32)
    shifted = jnp.concatenate([jnp.full((MOE_AHEAD,), -1, I32), assign[:n_stream - MOE_AHEAD]])
    row_slot = jnp.where(shifted < 0, TOP_K * t + u % MOE_DUMP, (shifted % TOP_K) * t + shifted // TOP_K)
    items_per_e = (ptiles + MOE_RT - 1) // MOE_RT
    iend = jnp.cumsum(items_per_e)
    istart = iend - items_per_e
    ii = jnp.arange(n_items_cap, dtype=I32)
    e_of = jnp.minimum(jnp.searchsorted(iend, ii, side="right"), N_EXPERTS - 1).astype(I32)
    jj = ii - istart[e_of]
    used = ii < iend[-1]
    item_nt = jnp.where(used, jnp.clip(ptiles[e_of] - jj * MOE_RT, 0, MOE_RT), 0).astype(I32)
    item_t0 = jnp.where(used, pstart[e_of] + jj * MOE_RT, 0).astype(I32)
    last_e = e_of[jnp.maximum(iend[-1] - 1, 0)]
    item_e = jnp.where(used, e_of, last_e).astype(I32)
    n_used = jnp.stack([pend[-1], iend[-1]]).astype(I32)
    return row_src, row_slot, n_used, item_e, item_t0, item_nt


def kernel(x_prompt, x_sample, p_prompt, p_sample, state_rwkv_shift, state_rwkv, state_hgrn, norm_mix, w_in,
           mu_shift, w0, w_up, a0, a_up, g_up, k_k, k_a, r_k, lnx_w, lnx_b, hgrn_lb, hgrn_norm, w_out,
           norm_ffn, w_router, b_router, w_gu, b_gu, w_down, b_down, norm_ple, w_ple_gate, w_ple_proj,
           norm_final):
    depth = w_in.shape[0]
    assert depth == 1
    li = 0
    bp, lp = x_prompt.shape[0], x_prompt.shape[1]
    bs, ls = x_sample.shape[0], x_sample.shape[1]
    ls_pad = SUBLANE
    tp = bp * lp

    w_t = jnp.transpose(w_in[li])
    w_in_p = jnp.concatenate([_pad_shift_rows(w_t[:D_SHIFT]), w_t[D_SHIFT:]], axis=0).astype(BF16)
    row = lambda a: a.reshape(1, -1).astype(F32)
    pp = {
        "mu": row(_pad_shift_cols(mu_shift[li])),
        "w0": row(w0[li]), "a0": row(a0[li]), "k_k": row(k_k[li]), "k_a": row(k_a[li]),
        "w_up": _pad_rows(w_up[li], LW_PAD).astype(BF16),
        "a_up": _pad_rows(a_up[li], LA_PAD).astype(BF16),
        "g_up": _pad_rows(g_up[li], LG_PAD).astype(BF16),
    }
    rk = row(r_k[li])
    lnw = row(lnx_w[li])
    lnb = row(lnx_b[li])
    lower = jax.nn.softmax(hgrn_lb.astype(F32), axis=0)
    lb = row(jnp.cumsum(lower, axis=0)[li])
    nw = row(hgrn_norm[li])
    wo = w_out[li].astype(BF16)
    nf = row(norm_ffn[li])
    wr = jnp.concatenate([w_router[li], jnp.zeros((D_MODEL, LANE - N_EXPERTS), F32)], axis=1)
    br = jnp.concatenate([b_router[li], jnp.zeros((LANE - N_EXPERTS,), F32)]).reshape(1, LANE)
    n_ple = row(norm_ple[li])
    wpg = w_ple_gate[li].astype(BF16)
    wpp = w_ple_proj[li].astype(BF16)
    n_fin = row(norm_final)
    g_mix = row(norm_mix[li])

    def mixer(x2d, shift_prev, s_rwkv, s_hgrn, cnt0, bn, l_total, l_valid, tm_in, bb_prep, tt, bb_scan, ub_scan, c,
              tm_out):
        proj = _inproj(x2d, g_mix, w_in_p, tm_in, 768)
        streams = _rwkv_prep(proj, shift_prev, pp, bn, l_total, l_valid, bb_prep, tt)
        yr, s_rwkv_new = _rwkv_scan(streams, rk, lnw, lnb, s_rwkv, bn, l_total, bb_scan, ub_scan, c)
        yh, s_hgrn_new = _hgrn_scan(proj, lb, nw, s_hgrn, bn, l_total, l_valid, bb_scan, ub_scan, c)
        h1, xn2, idx, gates, rank, cnt = _outproj_router(
            x2d, yr.reshape(bn * l_total, D_RWKV), yh.reshape(bn * l_total, D_HGRN), wo, nf, wr, br, cnt0, tm_out,
            l_valid, l_total)
        new_shift = _unpad_shift_cols(proj.reshape(bn, l_total, D_IN_PAD)[:, l_valid - 1, :D_SHIFT_PAD])
        return h1, xn2, idx, gates, rank, cnt, new_shift, s_rwkv_new, s_hgrn_new

    xp2 = x_prompt.reshape(tp, D_MODEL)
    zero_shift = jnp.zeros((bp, 1, D_SHIFT_PAD), F32)
    h1p, xn2p, idxp, gatesp, rankp, cntp, shift_p, rwkv_p, hgrn_p = mixer(
        xp2, zero_shift, None, None, jnp.zeros((1, LANE), F32), bp, lp, lp, 1024, 1, 256, bp, 4, 64, 256)

    xs_pad = jnp.concatenate([x_sample, jnp.zeros((bs, ls_pad - ls, D_MODEL), F32)], axis=1)
    xs2 = xs_pad.reshape(bs * ls_pad, D_MODEL)
    shift_s0 = _pad_shift_cols(state_rwkv_shift[li]).reshape(bs, 1, D_SHIFT_PAD)
    h1s, xn2s, idxs, gatess, ranks, cnt_all, shift_s, rwkv_s, hgrn_s = mixer(
        xs2, shift_s0, state_rwkv[li], state_hgrn[li], cntp, bs, ls_pad, ls, 1024, 16, ls_pad, 16, 2, ls_pad, 256)

    def compact(a):
        return a.reshape(bs, ls_pad, a.shape[-1])[:, :ls].reshape(bs * ls, a.shape[-1])

    h1s, xn2s, idxs, gatess, ranks = compact(h1s), compact(xn2s), compact(idxs), compact(gatess), compact(ranks)
    ts = bs * ls

    t_all = tp + ts
    idx_all = jnp.concatenate([idxp[:, :TOP_K], idxs[:, :TOP_K]], axis=0)
    rank_all = jnp.concatenate([rankp[:, :TOP_K], ranks[:, :TOP_K]], axis=0)
    counts = cnt_all[0, :N_EXPERTS].astype(I32)
    n_tiles_cap = -(-(t_all * TOP_K) // MOE_TM) + N_EXPERTS
    n_rows_cap = n_tiles_cap * MOE_TM
    n_items_cap = N_EXPERTS + n_tiles_cap // MOE_RT
    row_src, row_slot, n_used, item_e, item_t0, item_nt = _routing(idx_all, rank_all, counts, n_rows_cap,
                                                                   n_items_cap)
    x_packed = jnp.concatenate([xn2p, xn2s, jnp.zeros((SUBLANE, HALF), U32)], axis=0)
    ysc = _moe_experts(item_e, item_t0, item_nt, n_used, row_src, row_slot, x_packed, w_gu[li],
                       b_gu[li].reshape(N_EXPERTS, 1, -1), w_down[li], b_down[li].reshape(N_EXPERTS, 1, -1),
                       n_used[1], t_all)

    tm_fin = 256
    assert tp % tm_fin == 0 and ts % tm_fin == 0
    y_p = _ple_final(h1p, gatesp, ysc, t_all, 0, p_prompt[li].reshape(tp, D_PLE), n_ple, wpg, wpp, n_fin, tm_fin)
    y_s = _ple_final(h1s, gatess, ysc, t_all, tp, p_sample[li].reshape(ts, D_PLE), n_ple, wpg, wpp, n_fin, tm_fin)

    return (y_p.reshape(bp, lp, D_MODEL), y_s.reshape(bs, ls, D_MODEL),
            shift_p[None], rwkv_p[None], hgrn_p[None],
            shift_s[None], rwkv_s[None], hgrn_s[None])
```

```python
import functools

import jax
import jax.numpy as jnp
from jax import lax
from jax.experimental import pallas as pl
from jax.experimental.pallas import tpu as pltpu

F32 = jnp.float32
BF16 = jnp.bfloat16
I32 = jnp.int32
U32 = jnp.uint32

D_MODEL = 2048
D_RWKV = 1024
D_HGRN = 1024
RWKV_HEAD = 64
N_RWKV_HEADS = 16
HGRN_HEAD = 128
N_HGRN_HEADS = 8
LORA_W = 64
LORA_A = 64
LORA_G = 160
D_SHIFT = 3 * D_RWKV + LORA_W + LORA_A + LORA_G
N_EXPERTS = 32
TOP_K = 4
D_EXPERT = 2048
SWIGLU_LIMIT = 7.0
SWIGLU_ALPHA = 1.702
D_PLE = 256
RMS_EPS = 1e-6
GN_EPS = 64e-5
HGRN_NORM_EPS = 1e-5

LANE = 128
SUBLANE = 8
N_UNITS = 8

LW_PAD = LANE
LA_PAD = LANE
LG_PAD = 2 * LANE
OFF_WD = 3 * D_RWKV
OFF_AD = OFF_WD + LW_PAD
OFF_GD = OFF_AD + LA_PAD
D_SHIFT_PAD = OFF_GD + LG_PAD
D_IN_PAD = D_SHIFT_PAD + 4 * D_HGRN

MOE_TM = 128
MOE_RT = 10
MOE_TF = 256
VMEM_LIMIT = 56 * 1024 * 1024


def _cparams(sem, vmem=VMEM_LIMIT):
    return pltpu.CompilerParams(dimension_semantics=sem, vmem_limit_bytes=vmem)


def _rmsnorm(x, g):
    return x * lax.rsqrt(jnp.mean(x * x, axis=-1, keepdims=True) + RMS_EPS) * g


def _dot(a, b):
    return jnp.dot(a.astype(BF16), b.astype(BF16), preferred_element_type=F32)


def _dot_nt(a, b):
    return lax.dot_general(a.astype(BF16), b.astype(BF16), (((1,), (1,)), ((), ())),
                           preferred_element_type=F32)


def _dot_tn(a, b):
    return lax.dot_general(a.astype(BF16), b.astype(BF16), (((0,), (0,)), ((), ())),
                           preferred_element_type=F32)


def _split3(x):
    h = x.astype(BF16)
    r = x - h.astype(F32)
    m = r.astype(BF16)
    l = (r - m.astype(F32)).astype(BF16)
    return h, m, l


def _dot_exact_rhs(a, b_bf16):
    h, m, l = _split3(a)
    d = functools.partial(jnp.dot, preferred_element_type=F32)
    return d(h, b_bf16) + d(m, b_bf16) + d(l, b_bf16)


def _dot_exact_lhs(a_bf16, b):
    h, m, l = _split3(b)
    d = functools.partial(jnp.dot, preferred_element_type=F32)
    return d(a_bf16, h) + d(a_bf16, m) + d(a_bf16, l)


def _iota2(shape, dim):
    return lax.broadcasted_iota(I32, shape, dim)


def _cumsum_time(x):
    c = x.shape[0]
    tri = (_iota2((c, c), 0) >= _iota2((c, c), 1)).astype(BF16)
    return _dot_exact_lhs(tri, x)


def _same_head_mask():
    return (_iota2((LANE, LANE), 0) >= RWKV_HEAD) == (_iota2((LANE, LANE), 1) >= RWKV_HEAD)


def _sigmoid(x):
    return 1.0 / (1.0 + jnp.exp(-x))


def _inproj_kernel(x_ref, g_ref, w_ref, o_ref, xn_ref):
    @pl.when(pl.program_id(1) == 0)
    def _():
        xn_ref[...] = _rmsnorm(x_ref[...], g_ref[...]).astype(BF16)

    o_ref[...] = lax.dot_general(xn_ref[...], w_ref[...], (((1,), (1,)), ((), ())), preferred_element_type=F32)


def _inproj(x, g, w, tm, tn):
    t, d = x.shape
    n = w.shape[0]
    return pl.pallas_call(
        _inproj_kernel,
        out_shape=jax.ShapeDtypeStruct((t, n), F32),
        grid=(t // tm, n // tn),
        in_specs=[pl.BlockSpec((tm, d), lambda i, j: (i, 0)),
                  pl.BlockSpec((1, d), lambda i, j: (0, 0)),
                  pl.BlockSpec((tn, d), lambda i, j: (j, 0))],
        out_specs=pl.BlockSpec((tm, tn), lambda i, j: (i, j)),
        scratch_shapes=[pltpu.VMEM((tm, d), BF16)],
        compiler_params=_cparams(("parallel", "arbitrary")),
        name="inproj",
    )(x, g, w)


def _rwkv_prep_kernel(x_ref, p8_ref, sh_ref, mu_ref, w0_ref, wup_ref, a0_ref, aup_ref, gup_ref,
                      kk_ref, ka_ref,
                      r_o, k_o, v_o, lw_o, a_o, b_o, g_o, *, bb, tt, l_valid, l_total):
    ti = pl.program_id(1)
    w = D_SHIFT_PAD
    x3 = x_ref[...].reshape(bb, tt, w)
    rolled = pltpu.roll(x3, 1, axis=1)
    prev_tail = p8_ref[...].reshape(bb, SUBLANE, w)[:, SUBLANE - 1:SUBLANE, :]
    first = jnp.where(ti == 0, sh_ref[...], prev_tail)
    t_in = _iota2((bb, tt, w), 1)
    prev = jnp.where(t_in == 0, first, rolled)
    xs = (x3 + (prev - x3) * mu_ref[...]).reshape(bb * tt, w)

    r = xs[:, 0:D_RWKV]
    k = xs[:, D_RWKV:2 * D_RWKV]
    v = xs[:, 2 * D_RWKV:3 * D_RWKV]
    wd = xs[:, OFF_WD:OFF_WD + LW_PAD]
    ad = xs[:, OFF_AD:OFF_AD + LA_PAD]
    gd = xs[:, OFF_GD:OFF_GD + LG_PAD]

    z = -(w0_ref[...] + _dot(jnp.tanh(wd), wup_ref[...]))
    softplus = jnp.maximum(z, 0.0) + jnp.log(1.0 + jnp.exp(-jnp.abs(z)))
    lw = -jnp.exp(-softplus - 0.5)
    asig = _sigmoid(a0_ref[...] + _dot(ad, aup_ref[...]))
    g = _dot(_sigmoid(gd), gup_ref[...])

    kk = k * kk_ref[...]
    same_head = _same_head_mask().astype(BF16)
    sq = kk * kk
    ssq = jnp.concatenate(
        [_dot_exact_rhs(sq[:, u * LANE:(u + 1) * LANE], same_head) for u in range(N_UNITS)], axis=1)
    kkn = kk / jnp.maximum(jnp.sqrt(ssq), 1e-12)
    k2 = k * (1.0 + (asig - 1.0) * ka_ref[...])
    a_vec = -kkn
    b_vec = kkn * asig

    if l_valid < l_total:
        t_glob = (ti * tt + _iota2((bb, tt, D_RWKV), 1)).reshape(bb * tt, D_RWKV)
        ok = t_glob < l_valid
        zero = jnp.zeros_like(k2)
        lw, k2, v, a_vec, b_vec = (jnp.where(ok, t, zero) for t in (lw, k2, v, a_vec, b_vec))

    r_o[...] = r
    k_o[...] = k2
    v_o[...] = v
    lw_o[...] = lw
    a_o[...] = a_vec
    b_o[...] = b_vec
    g_o[...] = g


def _rwkv_prep(proj, shift_pad, pp, bn, l_total, l_valid, bb, tt):
    nt = l_total // tt
    rows = bb * tt
    w = D_SHIFT_PAD
    row_spec = pl.BlockSpec((rows, w), lambda bi, ti: (bi * nt + ti, 0))
    p8_spec = pl.BlockSpec((bb * SUBLANE, w),
                           lambda bi, ti: (jnp.maximum((bi * nt + ti) * (tt // SUBLANE) - 1, 0), 0))
    vec = lambda n: pl.BlockSpec((1, n), lambda bi, ti: (0, 0))
    mat = lambda a, b: pl.BlockSpec((a, b), lambda bi, ti: (0, 0))
    out_spec = pl.BlockSpec((rows, D_RWKV), lambda bi, ti: (bi * nt + ti, 0))
    out_sds = jax.ShapeDtypeStruct((bn * l_total, D_RWKV), F32)
    kern = functools.partial(_rwkv_prep_kernel, bb=bb, tt=tt, l_valid=l_valid, l_total=l_total)
    return pl.pallas_call(
        kern,
        out_shape=[out_sds] * 7,
        grid=(bn // bb, nt),
        in_specs=[row_spec, p8_spec,
                  pl.BlockSpec((bb, 1, w), lambda bi, ti: (bi, 0, 0)),
                  vec(w), vec(D_RWKV), mat(LW_PAD, D_RWKV), vec(D_RWKV), mat(LA_PAD, D_RWKV),
                  mat(LG_PAD, D_RWKV), vec(D_RWKV), vec(D_RWKV)],
        out_specs=[out_spec] * 7,
        compiler_params=_cparams(("parallel", "arbitrary")),
        name="rwkv_prep",
    )(proj, proj, shift_pad, pp["mu"], pp["w0"], pp["w_up"], pp["a0"], pp["a_up"], pp["g_up"],
      pp["k_k"], pp["k_a"])


def _rwkv_scan_kernel(*refs, bb, ub, c, has_s0):
    if has_s0:
        (r_ref, k_ref, v_ref, lw_ref, a_ref, b_ref, g_ref, rk_ref, lnw_ref, lnb_ref, s0_ref,
         y_ref, so_ref, s_scr) = refs
    else:
        (r_ref, k_ref, v_ref, lw_ref, a_ref, b_ref, g_ref, rk_ref, lnw_ref, lnb_ref,
         y_ref, so_ref, s_scr) = refs
        s0_ref = None
    ci = pl.program_id(2)
    n_chunks = pl.num_programs(2)

    lane = _iota2((1, LANE), 1)
    m0 = (lane < RWKV_HEAD).astype(F32)
    m1 = 1.0 - m0
    bd_mask = _same_head_mask().astype(F32)

    c2 = 2 * c
    ri = _iota2((c2, c2), 0)
    cj = _iota2((c2, c2), 1)
    same_blk = (ri >= c) == (cj >= c)
    mask_s = jnp.where(same_blk, (ri > cj).astype(F32), 0.0)
    mask_i = jnp.where(same_blk, (ri >= cj).astype(F32), 0.0)
    eye = (ri == cj).astype(F32)
    n_sq = max((c - 1).bit_length() - 1, 0)

    head_avg = bd_mask.astype(BF16)

    chains = [(j, w) for j in range(bb) for w in range(ub)]
    seqs = range(len(chains))

    @pl.when(ci == 0)
    def _():
        for n, (j, w) in enumerate(chains):
            if has_s0:
                zero = jnp.zeros((RWKV_HEAD, RWKV_HEAD), F32)
                top = jnp.concatenate([s0_ref[j, 2 * w], zero], axis=1)
                bottom = jnp.concatenate([zero, s0_ref[j, 2 * w + 1]], axis=1)
                s_scr[n] = jnp.concatenate([top, bottom], axis=0)
            else:
                s_scr[n] = jnp.zeros((LANE, LANE), F32)

    stack2 = lambda lo, hi: jnp.concatenate([lo, hi], axis=0)
    unit = lambda ref, j, w: ref[j, :, w * LANE:(w + 1) * LANE]
    vec = lambda ref, w: ref[:, w * LANE:(w + 1) * LANE]
    r = [unit(r_ref, j, w) for j, w in chains]
    k = [unit(k_ref, j, w) for j, w in chains]
    v = [unit(v_ref, j, w) for j, w in chains]
    lw = [unit(lw_ref, j, w) for j, w in chains]
    a = [unit(a_ref, j, w) for j, w in chains]
    b = [unit(b_ref, j, w) for j, w in chains]
    cum = [_cumsum_time(x) for x in lw]
    clast = [x[c - 1:c, :] for x in cum]
    p_inv = [jnp.exp(-x) for x in cum]
    a2 = [stack2(a[j] * jnp.exp(cum[j] - lw[j]) * m0, a[j] * jnp.exp(cum[j] - lw[j]) * m1) for j in seqs]
    r2 = [stack2(r[j] * jnp.exp(cum[j]) * m0, r[j] * jnp.exp(cum[j]) * m1) for j in seqs]
    b2 = [stack2(b[j] * p_inv[j], b[j] * p_inv[j]) for j in seqs]
    k2 = [stack2(k[j] * p_inv[j], k[j] * p_inv[j]) for j in seqs]
    v2 = [stack2(v[j] * m0, v[j] * m1) for j in seqs]
    lab = [mask_s * _dot_nt(a2[j], b2[j]) for j in seqs]
    lak = [mask_s * _dot_nt(a2[j], k2[j]) for j in seqs]
    rb = [mask_i * _dot_nt(r2[j], b2[j]) for j in seqs]
    rkm = [mask_i * _dot_nt(r2[j], k2[j]) for j in seqs]
    tinv = [eye + x for x in lab]
    xp = lab
    for _ in range(n_sq):
        xp = [_dot(x, x) for x in xp]
        tinv = [tinv[j] + _dot(tinv[j], xp[j]) for j in seqs]
    lakv = [_dot(lak[j], v2[j]) for j in seqs]
    rkv = [_dot(rkm[j], v2[j]) for j in seqs]
    s0 = [s_scr[n] for n in seqs]
    ar_s0 = [_dot_nt(stack2(a2[n], r2[n]), s0[n]) for n in seqs]
    u2 = [_dot(tinv[n], ar_s0[n][:c2] + lakv[n]) for n in seqs]
    y2 = [ar_s0[n][c2:] + _dot(rb[n], u2[n]) + rkv[n] for n in seqs]
    y = [x[:c] + x[c:] for x in y2]
    u = [x[:c] + x[c:] for x in u2]
    p_last = [jnp.exp(clast[n] - cum[n]) for n in seqs]
    for n in seqs:
        s_scr[n] = s0[n] * jnp.exp(clast[n]) + bd_mask * _dot_tn(
            stack2(u[n], v[n]), stack2(b[n] * p_last[n], k[n] * p_last[n]))

    mu = [_dot(x, head_avg) * (1.0 / RWKV_HEAD) for x in y]
    dlt = [y[n] - mu[n] for n in seqs]
    var = [_dot(x * x, head_avg) * (1.0 / RWKV_HEAD) for x in dlt]
    bonus = [_dot(r[n] * k[n] * vec(rk_ref, w), head_avg) * v[n] for n, (j, w) in enumerate(chains)]
    for n, (j, w) in enumerate(chains):
        yn = dlt[n] * lax.rsqrt(var[n] + GN_EPS) * vec(lnw_ref, w) + vec(lnb_ref, w)
        y_ref[j, :, w * LANE:(w + 1) * LANE] = (yn + bonus[n]) * unit(g_ref, j, w)

    @pl.when(ci == n_chunks - 1)
    def _():
        for n, (j, w) in enumerate(chains):
            s_fin = s_scr[n]
            so_ref[j, 2 * w] = s_fin[:RWKV_HEAD, :RWKV_HEAD]
            so_ref[j, 2 * w + 1] = pltpu.roll(s_fin, RWKV_HEAD, axis=1)[RWKV_HEAD:, :RWKV_HEAD]


def _rwkv_scan(streams, rk, lnw, lnb, s0, bn, l_total, bb, ub, c):
    nc = l_total // c
    has_s0 = s0 is not None
    blk = pl.BlockSpec((bb, c, ub * LANE), lambda bi, u, ci: (bi, ci, u))
    vec = pl.BlockSpec((1, ub * LANE), lambda bi, u, ci: (0, u))
    st = pl.BlockSpec((bb, 2 * ub, RWKV_HEAD, RWKV_HEAD), lambda bi, u, ci: (bi, u, 0, 0))
    in_specs = [blk] * 7 + [vec] * 3 + ([st] if has_s0 else [])
    args = [s.reshape(bn, l_total, D_RWKV) for s in streams] + [rk, lnw, lnb] + ([s0] if has_s0 else [])
    kern = functools.partial(_rwkv_scan_kernel, bb=bb, ub=ub, c=c, has_s0=has_s0)
    return pl.pallas_call(
        kern,
        out_shape=[jax.ShapeDtypeStruct((bn, l_total, D_RWKV), F32),
                   jax.ShapeDtypeStruct((bn, N_RWKV_HEADS, RWKV_HEAD, RWKV_HEAD), F32)],
        grid=(bn // bb, N_UNITS // ub, nc),
        in_specs=in_specs,
        out_specs=[blk, st],
        scratch_shapes=[pltpu.VMEM((bb * ub, LANE, LANE), F32)],
        compiler_params=_cparams(("parallel", "parallel", "arbitrary")),
        name="rwkv_scan",
    )(*args)


def _hgrn_kernel(*refs, bb, ub, c, has_s0, l_valid, l_total):
    if has_s0:
        q_ref, f_ref, i_ref, og_ref, lb_ref, nw_ref, s0_ref, y_ref, so_ref, s_scr = refs
    else:
        q_ref, f_ref, i_ref, og_ref, lb_ref, nw_ref, y_ref, so_ref, s_scr = refs
        s0_ref = None
    ci = pl.program_id(2)
    n_chunks = pl.num_programs(2)
    tri = (_iota2((c, c), 0) >= _iota2((c, c), 1)).astype(F32)
    mid = max(c // 2 - 1, 0)
    chains = [(j, w) for j in range(bb) for w in range(ub)]
    seqs = range(len(chains))
    unit = lambda ref, j, w: ref[j, :, w * LANE:(w + 1) * LANE]

    @pl.when(ci == 0)
    def _():
        for n, (j, w) in enumerate(chains):
            if has_s0:
                s_scr[n] = s0_ref[j, w]
            else:
                s_scr[n] = jnp.zeros((LANE, LANE), F32)

    def key_column(x):
        ones = jnp.ones((c, LANE), BF16)
        d = lambda p: lax.dot_general(p, ones, (((0,), (0,)), ((), ())), preferred_element_type=F32)
        h, m, l = _split3(x)
        return d(h) + d(m) + d(l)

    q = [unit(q_ref, j, w) * _sigmoid(unit(q_ref, j, w)) for j, w in chains]
    f = [lb_ref[:, w * LANE:(w + 1) * LANE] + (1.0 - lb_ref[:, w * LANE:(w + 1) * LANE])
         * _sigmoid(unit(f_ref, j, w)) for j, w in chains]
    logf = [jnp.log(x) for x in f]
    kf = [1.0 - x for x in f]
    v = [unit(i_ref, j, w) for j, w in chains]
    if l_valid < l_total:
        ok = (ci * c + _iota2((c, LANE), 0)) < l_valid
        logf = [jnp.where(ok, x, 0.0) for x in logf]
        kf = [jnp.where(ok, x, 0.0) for x in kf]
    cum = [_cumsum_time(x) for x in logf]
    cref = [x[mid:mid + 1, :] for x in cum]
    clast = [x[c - 1:c, :] for x in cum]
    amat = [tri * _dot_nt(q[j] * jnp.exp(cum[j] - cref[j]), kf[j] * jnp.exp(cref[j] - cum[j])) for j in seqs]
    st = [s_scr[j] for j in seqs]
    o = [_dot(q[j] * jnp.exp(cum[j]), st[j]) + _dot(amat[j], v[j]) for j in seqs]
    decay = [jnp.exp(key_column(x)) for x in logf]
    for j in seqs:
        s_scr[j] = st[j] * decay[j] + _dot_tn(kf[j] * jnp.exp(clast[j] - cum[j]), v[j])
    for n, (j, w) in enumerate(chains):
        on = o[n] * lax.rsqrt(jnp.mean(o[n] * o[n], axis=-1, keepdims=True) + HGRN_NORM_EPS) * nw_ref[...]
        og = unit(og_ref, j, w)
        y_ref[j, :, w * LANE:(w + 1) * LANE] = on * (og * _sigmoid(og))

    @pl.when(ci == n_chunks - 1)
    def _():
        for n, (j, w) in enumerate(chains):
            so_ref[j, w] = s_scr[n]


def _hgrn_scan(proj, lb, nw, s0, bn, l_total, l_valid, bb, ub, c):
    nc = l_total // c
    has_s0 = s0 is not None
    base = D_SHIFT_PAD // (ub * LANE)
    per = D_HGRN // (ub * LANE)
    proj3 = proj.reshape(bn, l_total, D_IN_PAD)

    def col(part):
        return pl.BlockSpec((bb, c, ub * LANE), lambda bi, u, ci: (bi, ci, base + part * per + u))

    st = pl.BlockSpec((bb, ub, LANE, LANE), lambda bi, u, ci: (bi, u, 0, 0))
    in_specs = [col(0), col(1), col(2), col(3),
                pl.BlockSpec((1, ub * LANE), lambda bi, u, ci: (0, u)),
                pl.BlockSpec((1, LANE), lambda bi, u, ci: (0, 0))] + ([st] if has_s0 else [])
    args = [proj3, proj3, proj3, proj3, lb, nw] + ([s0] if has_s0 else [])
    kern = functools.partial(_hgrn_kernel, bb=bb, ub=ub, c=c, has_s0=has_s0, l_valid=l_valid,
                             l_total=l_total)
    return pl.pallas_call(
        kern,
        out_shape=[jax.ShapeDtypeStruct((bn, l_total, D_HGRN), F32),
                   jax.ShapeDtypeStruct((bn, N_UNITS, LANE, LANE), F32)],
        grid=(bn // bb, N_UNITS // ub, nc),
        in_specs=in_specs,
        out_specs=[pl.BlockSpec((bb, c, ub * LANE), lambda bi, u, ci: (bi, ci, u)), st],
        scratch_shapes=[pltpu.VMEM((bb * ub, LANE, LANE), F32)],
        compiler_params=_cparams(("parallel", "parallel", "arbitrary")),
        name="hgrn_scan",
    )(*args)


def _outproj_kernel(x_ref, yr_ref, yh_ref, wo_ref, nf_ref, wr_ref, br_ref, cnt0_ref,
                    h_ref, xn_ref, idx_ref, gate_ref, rank_ref, cnt_ref, run_scr, *, l_valid, l_total):
    @pl.when(pl.program_id(0) == 0)
    def _():
        run_scr[...] = cnt0_ref[...]

    h = (x_ref[...] + _dot(yr_ref[...], wo_ref[0:D_RWKV, :]) + _dot(yh_ref[...], wo_ref[D_RWKV:, :]))
    h_ref[...] = h
    xn = _rmsnorm(h, nf_ref[...])
    xh = xn.astype(BF16)
    bits = lax.bitcast_convert_type(xh.astype(F32), U32)
    xn_ref[...] = (bits[:, D_MODEL // 2:] & jnp.uint32(0xFFFF0000)) | (bits[:, :D_MODEL // 2] >> 16)
    wr = wr_ref[...]
    wh = wr.astype(BF16)
    logits = (_dot(xh, wh) + _dot(xn - xh.astype(F32), wh) + _dot(xh, wr - wh.astype(F32))
              + br_ref[...])
    tm = logits.shape[0]
    lane = _iota2((tm, LANE), 1).astype(F32)
    neg = jnp.float32(-jnp.inf)
    work = jnp.where(lane < N_EXPERTS, logits, neg)
    idx_out = jnp.zeros((tm, LANE), I32)
    val_out = jnp.zeros((tm, LANE), F32)
    top0 = None
    picks = []
    for kk in range(TOP_K):
        m = jnp.max(work, axis=-1, keepdims=True)
        sel = jnp.min(jnp.where(work == m, lane, float(LANE)), axis=-1, keepdims=True)
        if kk == 0:
            top0 = m
        idx_out = jnp.where(lane == kk, sel.astype(I32), idx_out)
        val_out = jnp.where(lane == kk, jnp.exp(m - top0), val_out)
        picks.append(lane == sel)
        work = jnp.where(lane == sel, neg, work)
    idx_ref[...] = idx_out
    gate_ref[...] = val_out / jnp.sum(val_out, axis=-1, keepdims=True)

    chosen = sum(p.astype(F32) for p in picks)
    if l_valid < l_total:
        assert l_total & (l_total - 1) == 0 and tm % l_total == 0
        t_in_seq = _iota2((tm, LANE), 0) & (l_total - 1)
        chosen = jnp.where(t_in_seq < l_valid, chosen, 0.0)
    earlier = (_iota2((tm, tm), 0) > _iota2((tm, tm), 1)).astype(BF16)
    before = jnp.dot(earlier, chosen.astype(BF16), preferred_element_type=F32) + run_scr[...]
    rank_out = jnp.zeros((tm, LANE), F32)
    for kk in range(TOP_K):
        r_k = jnp.sum(jnp.where(picks[kk], before, 0.0), axis=-1, keepdims=True)
        rank_out = jnp.where(lane == kk, r_k, rank_out)
    rank_ref[...] = rank_out.astype(I32)
    run_scr[...] = run_scr[...] + jnp.sum(chosen, axis=0, keepdims=True)
    cnt_ref[...] = run_scr[...]


def _outproj_router(x, yr, yh, wo, nf, wr, br, cnt0, tm, l_valid, l_total):
    t = x.shape[0]
    row = lambda n: pl.BlockSpec((tm, n), lambda i: (i, 0))
    full = lambda a, b: pl.BlockSpec((a, b), lambda i: (0, 0))
    return pl.pallas_call(
        functools.partial(_outproj_kernel, l_valid=l_valid, l_total=l_total),
        out_shape=[jax.ShapeDtypeStruct((t, D_MODEL), F32), jax.ShapeDtypeStruct((t, D_MODEL // 2), U32),
                   jax.ShapeDtypeStruct((t, LANE), I32), jax.ShapeDtypeStruct((t, LANE), F32),
                   jax.ShapeDtypeStruct((t, LANE), I32), jax.ShapeDtypeStruct((1, LANE), F32)],
        grid=(t // tm,),
        in_specs=[row(D_MODEL), row(D_RWKV), row(D_HGRN), full(D_MODEL, D_MODEL), full(1, D_MODEL),
                  full(D_MODEL, LANE), full(1, LANE), full(1, LANE)],
        out_specs=[row(D_MODEL), row(D_MODEL // 2), row(LANE), row(LANE), row(LANE), full(1, LANE)],
        scratch_shapes=[pltpu.VMEM((1, LANE), F32)],
        compiler_params=_cparams(("arbitrary",)),
        name="outproj_router",
    )(x, yr, yh, wo, nf, wr, br, cnt0)


MOE_NF = D_EXPERT // MOE_TF
MOE_ISSUE = MOE_TM // MOE_NF
MOE_AHEAD_TILES = MOE_RT
MOE_AHEAD = MOE_AHEAD_TILES * MOE_TM
MOE_RING = 2 * MOE_RT
MOE_DUMP = (MOE_RING + MOE_RT + 2) * MOE_TM
HALF = D_MODEL // 2


def _ring_row_copy(x_hbm, tok, ring, sems, u):
    slot = (u // MOE_TM) % MOE_RING
    return pltpu.make_async_copy(x_hbm.at[pl.ds(tok, 1)], ring.at[slot, pl.ds(u % MOE_TM, 1)], sems.at[slot])


def _ring_tile_wait(x_hbm, ring, sems, tile):
    slot = tile % MOE_RING
    pltpu.make_async_copy(x_hbm.at[pl.ds(0, MOE_TM)], ring.at[slot], sems.at[slot]).wait()


def _scatter_row_copy(ybuf, ysc_hbm, sems, dst, u):
    slot = (u // MOE_TM) % MOE_RING
    return pltpu.make_async_copy(ybuf.at[slot, pl.ds(u % MOE_TM, 1)], ysc_hbm.at[pl.ds(dst, 1)], sems.at[slot])


def _scatter_tile_wait(ybuf, ysc_hbm, sems, tile):
    slot = tile % MOE_RING
    pltpu.make_async_copy(ybuf.at[slot], ysc_hbm.at[pl.ds(0, MOE_TM)], sems.at[slot]).wait()


def _moe_kernel(ie_ref, it0_ref, int_ref, nused_ref, rowsrc_ref, rowslot_ref, x_hbm, wg_ref, wu_ref, wd_ref,
                bg_ref, bu_ref, bd_ref, ysc_hbm, ring, xbuf, acc, ybuf, ring_sems, scat_sems,
                fill_sem, *, dump0):
    i = pl.program_id(0)
    f = pl.program_id(1)
    n_items = pl.num_programs(0)
    nt = int_ref[i]
    tile0 = it0_ref[i]
    n_used = nused_ref[0]

    @pl.when((i == 0) & (f == 0))
    def _():
        for t in range(MOE_RING):
            ybuf[t] = jnp.zeros((MOE_TM, HALF), U32)
        for s in range(MOE_DUMP // MOE_TM):
            cp = pltpu.make_async_copy(ybuf.at[0], ysc_hbm.at[pl.ds(dump0 + s * MOE_TM, MOE_TM)], fill_sem)
            cp.start()
            cp.wait()

        def head(r, carry):
            _ring_row_copy(x_hbm, rowsrc_ref[r], ring, ring_sems, r).start()
            return carry

        lax.fori_loop(0, MOE_AHEAD, head, 0)

    @pl.when(nt > 0)
    def _():
        @pl.when(f == 0)
        def _():
            def load(t, carry):
                _ring_tile_wait(x_hbm, ring, ring_sems, tile0 + t)
                w = ring[(tile0 + t) % MOE_RING]
                lo = lax.bitcast_convert_type(w << 16, F32).astype(BF16)
                hi = lax.bitcast_convert_type(w & jnp.uint32(0xFFFF0000), F32).astype(BF16)
                xbuf[t, :, 0:HALF] = lo
                xbuf[t, :, HALF:D_MODEL] = hi
                acc[t] = jnp.broadcast_to(bd_ref[0], (MOE_TM, D_MODEL))
                return carry

            lax.fori_loop(0, nt, load, 0)

        bg = bg_ref[0]
        bu = bu_ref[0]

        def issue(t):
            step = f * nt + t
            pos = tile0 * MOE_TM + step * MOE_ISSUE
            row = (step * MOE_ISSUE) % MOE_TM
            tile_s = tile0 + step // MOE_NF
            slot_s = tile_s % MOE_RING
            slot_g = (tile_s + MOE_AHEAD_TILES) % MOE_RING
            for q in range(MOE_ISSUE):
                pltpu.make_async_copy(x_hbm.at[pl.ds(rowsrc_ref[pos + MOE_AHEAD + q], 1)],
                                      ring.at[slot_g, pl.ds(row + q, 1)], ring_sems.at[slot_g]).start()
            for q in range(MOE_ISSUE):
                pltpu.make_async_copy(ybuf.at[slot_s, pl.ds(row + q, 1)],
                                      ysc_hbm.at[pl.ds(rowslot_ref[pos + q], 1)], scat_sems.at[slot_s]).start()

        def spans(todo):
            for t0, k in todo:
                for s in range(k):
                    issue(t0 + s)

            def hidden(t0, k):
                x = xbuf[pl.ds(t0, k)].reshape(k * MOE_TM, D_MODEL)
                g = jnp.minimum(jnp.dot(x, wg_ref[0].astype(BF16), preferred_element_type=F32) + bg, SWIGLU_LIMIT)
                u = jnp.clip(jnp.dot(x, wu_ref[0].astype(BF16), preferred_element_type=F32) + bu,
                             -SWIGLU_LIMIT, SWIGLU_LIMIT)
                return ((u + 1.0) * g * _sigmoid(SWIGLU_ALPHA * g)).astype(BF16)

            hid = hidden(*todo[0])
            for s, (t0, k) in enumerate(todo):
                nxt = hidden(*todo[s + 1]) if s + 1 < len(todo) else None
                down = jnp.dot(hid, wd_ref[0].astype(BF16), preferred_element_type=F32)
                acc[pl.ds(t0, k)] += down.reshape(k, MOE_TM, D_MODEL)
                hid = nxt

        def quad(p, carry):
            spans([(4 * p, 2), (4 * p + 2, 2)])
            return carry

        n_quads = nt // 4
        lax.fori_loop(0, n_quads, quad, 0)

        def pair(p, carry):
            spans([(4 * n_quads + 2 * p, 2)])
            return carry

        lax.fori_loop(0, (nt - 4 * n_quads) // 2, pair, 0)

        @pl.when(nt % 2 == 1)
        def _():
            spans([(nt - 1, 1)])

        @pl.when(f == MOE_NF - 1)
        def _():
            def pack(t, carry):
                vt = tile0 + t + MOE_AHEAD_TILES

                @pl.when(vt >= MOE_RING)
                def _():
                    _scatter_tile_wait(ybuf, ysc_hbm, scat_sems, vt)

                bits = lax.bitcast_convert_type(acc[t].astype(BF16).astype(F32), U32)
                ybuf[vt % MOE_RING] = (bits[:, HALF:] & jnp.uint32(0xFFFF0000)) | (bits[:, :HALF] >> 16)
                return carry

            lax.fori_loop(0, nt, pack, 0)

    @pl.when((i == n_items - 1) & (f == MOE_NF - 1))
    def _():
        for s in range(MOE_AHEAD_TILES):
            _ring_tile_wait(x_hbm, ring, ring_sems, n_used + s)

        def flush(u, carry):
            _scatter_row_copy(ybuf, ysc_hbm, scat_sems, rowslot_ref[u], u).start()
            return carry

        lax.fori_loop(n_used * MOE_TM, n_used * MOE_TM + MOE_AHEAD, flush, 0)

        def retire(vt, carry):
            _scatter_tile_wait(ybuf, ysc_hbm, scat_sems, vt)
            return carry

        lax.fori_loop(jnp.maximum(n_used - MOE_AHEAD_TILES, 0), n_used + MOE_AHEAD_TILES, retire, 0)


def _moe_experts(item_e, item_t0, item_nt, n_used, row_src, row_slot, x_packed, w_gu, b_gu, w_down, b_down,
                 n_items, n_tok):
    nf = MOE_NF
    dump0 = TOP_K * n_tok

    def fcol(i, f, int_):
        return jnp.where(int_[i] > 0, f, nf - 1)

    in_specs = [
        pl.BlockSpec(memory_space=pl.ANY),
        pl.BlockSpec((1, D_MODEL, MOE_TF), lambda i, f, ie, it0, int_, *_: (ie[i], 0, fcol(i, f, int_))),
        pl.BlockSpec((1, D_MODEL, MOE_TF), lambda i, f, ie, it0, int_, *_: (ie[i], 0, nf + fcol(i, f, int_))),
        pl.BlockSpec((1, MOE_TF, D_MODEL), lambda i, f, ie, it0, int_, *_: (ie[i], fcol(i, f, int_), 0)),
        pl.BlockSpec((1, 1, MOE_TF), lambda i, f, ie, it0, int_, *_: (ie[i], 0, fcol(i, f, int_))),
        pl.BlockSpec((1, 1, MOE_TF), lambda i, f, ie, it0, int_, *_: (ie[i], 0, nf + fcol(i, f, int_))),
        pl.BlockSpec((1, 1, D_MODEL), lambda i, f, ie, it0, int_, *_: (ie[i], 0, 0)),
    ]
    return pl.pallas_call(
        functools.partial(_moe_kernel, dump0=dump0),
        out_shape=jax.ShapeDtypeStruct((dump0 + MOE_DUMP, HALF), U32),
        grid_spec=pltpu.PrefetchScalarGridSpec(
            num_scalar_prefetch=6,
            grid=(n_items, nf),
            in_specs=in_specs,
            out_specs=pl.BlockSpec(memory_space=pl.ANY),
            scratch_shapes=[pltpu.VMEM((MOE_RING, MOE_TM, HALF), U32),
                            pltpu.VMEM((MOE_RT, MOE_TM, D_MODEL), BF16),
                            pltpu.VMEM((MOE_RT, MOE_TM, D_MODEL), F32),
                            pltpu.VMEM((MOE_RING, MOE_TM, HALF), U32),
                            pltpu.SemaphoreType.DMA((MOE_RING,)),
                            pltpu.SemaphoreType.DMA((MOE_RING,)),
                            pltpu.SemaphoreType.DMA(())]),
        compiler_params=_cparams(("arbitrary", "arbitrary")),
        name="moe_experts",
    )(item_e, item_t0, item_nt, n_used, row_src, row_slot, x_packed, w_gu, w_gu, w_down, b_gu, b_gu, b_down)


def _ple_kernel(h_ref, gate_ref, y0_ref, y1_ref, y2_ref, y3_ref, p_ref, np_ref, wg_ref, wp_ref, nfin_ref, o_ref):
    gates = gate_ref[...]
    lo = jnp.zeros((h_ref.shape[0], HALF), F32)
    hi = jnp.zeros((h_ref.shape[0], HALF), F32)
    for kk, y_ref in enumerate((y0_ref, y1_ref, y2_ref, y3_ref)):
        w = y_ref[...]
        g = gates[:, kk:kk + 1]
        lo = lo + g * lax.bitcast_convert_type(w << 16, F32)
        hi = hi + g * lax.bitcast_convert_type(w & jnp.uint32(0xFFFF0000), F32)
    h = h_ref[...] + jnp.concatenate([lo, hi], axis=1)
    gate = _sigmoid(_dot(_rmsnorm(h, np_ref[...]), wg_ref[...]))
    h = h + gate * _dot(p_ref[...], wp_ref[...])
    o_ref[...] = _rmsnorm(h, nfin_ref[...])


def _ple_final(h, gates, ysc, plane, tok0, p, n_ple, wg, wp, n_fin, tm):
    t = h.shape[0]
    row = lambda n: pl.BlockSpec((tm, n), lambda i: (i, 0))
    full = lambda a, b: pl.BlockSpec((a, b), lambda i: (0, 0))
    ysp = lambda kk: pl.BlockSpec((tm, HALF), lambda i: ((kk * plane + tok0) // tm + i, 0))
    return pl.pallas_call(
        _ple_kernel,
        out_shape=jax.ShapeDtypeStruct((t, D_MODEL), F32),
        grid=(t // tm,),
        in_specs=[row(D_MODEL), row(LANE), ysp(0), ysp(1), ysp(2), ysp(3), row(D_PLE), full(1, D_MODEL),
                  full(D_MODEL, D_MODEL), full(D_PLE, D_MODEL), full(1, D_MODEL)],
        out_specs=row(D_MODEL),
        compiler_params=_cparams(("parallel",)),
        name="ple_final",
    )(h, gates, ysc, ysc, ysc, ysc, p, n_ple, wg, wp, n_fin)


def _pad_shift_cols(a):
    def z(n):
        return jnp.zeros(a.shape[:-1] + (n,), a.dtype)
    c0 = 3 * D_RWKV
    c1 = c0 + LORA_W
    c2 = c1 + LORA_A
    return jnp.concatenate([a[..., :c0], a[..., c0:c1], z(LW_PAD - LORA_W), a[..., c1:c2], z(LA_PAD - LORA_A),
                            a[..., c2:], z(LG_PAD - LORA_G)], axis=-1)


def _pad_shift_rows(a):
    z = lambda n: jnp.zeros((n, a.shape[1]), a.dtype)
    c0 = 3 * D_RWKV
    c1 = c0 + LORA_W
    c2 = c1 + LORA_A
    return jnp.concatenate([a[:c0], a[c0:c1], z(LW_PAD - LORA_W), a[c1:c2], z(LA_PAD - LORA_A), a[c2:],
                            z(LG_PAD - LORA_G)], axis=0)


def _unpad_shift_cols(a):
    return jnp.concatenate([a[..., :OFF_WD], a[..., OFF_WD:OFF_WD + LORA_W], a[..., OFF_AD:OFF_AD + LORA_A],
                            a[..., OFF_GD:OFF_GD + LORA_G]], axis=-1)


def _pad_rows(a, n):
    return jnp.concatenate([a, jnp.zeros((n - a.shape[0],) + a.shape[1:], a.dtype)], axis=0)


def _routing(idx, rank, counts, n_rows_cap, n_items_cap):
    t = idx.shape[0]
    na = t * TOP_K
    flat_e = idx.reshape(na)
    rank = rank.reshape(na)
    ptiles = (counts + MOE_TM - 1) // MOE_TM
    pend = jnp.cumsum(ptiles)
    pstart = pend - ptiles
    pos = pstart[flat_e] * MOE_TM + rank
    n_stream = n_rows_cap + 2 * MOE_AHEAD
    assign = jnp.full((n_stream,), -1, I32).at[pos].set(jnp.arange(na, dtype=I32), unique_indices=True,
                                                        mode="promise_in_bounds")
    is_pad = assign < 0
    row_src = jnp.where(is_pad, t, assign // TOP_K)
    u = jnp.arange(n_stream, dtype=I32)
    shifted = jnp.concatenate([jnp.full((MOE_AHEAD,), -1, I32), assign[:n_stream - MOE_AHEAD]])
    row_slot = jnp.where(shifted < 0, TOP_K * t + u % MOE_DUMP, (shifted % TOP_K) * t + shifted // TOP_K)
    items_per_e = (ptiles + MOE_RT - 1) // MOE_RT
    iend = jnp.cumsum(items_per_e)
    istart = iend - items_per_e
    ii = jnp.arange(n_items_cap, dtype=I32)
    e_of = jnp.minimum(jnp.searchsorted(iend, ii, side="right"), N_EXPERTS - 1).astype(I32)
    jj = ii - istart[e_of]
    used = ii < iend[-1]
    item_nt = jnp.where(used, jnp.clip(ptiles[e_of] - jj * MOE_RT, 0, MOE_RT), 0).astype(I32)
    item_t0 = jnp.where(used, pstart[e_of] + jj * MOE_RT, 0).astype(I32)
    last_e = e_of[jnp.maximum(iend[-1] - 1, 0)]
    item_e = jnp.where(used, e_of, last_e).astype(I32)
    n_used = jnp.stack([pend[-1], iend[-1]]).astype(I32)
    return row_src, row_slot, n_used, item_e, item_t0, item_nt


def kernel(x_prompt, x_sample, p_prompt, p_sample, state_rwkv_shift, state_rwkv, state_hgrn, norm_mix, w_in,
           mu_shift, w0, w_up, a0, a_up, g_up, k_k, k_a, r_k, lnx_w, lnx_b, hgrn_lb, hgrn_norm, w_out,
           norm_ffn, w_router, b_router, w_gu, b_gu, w_down, b_down, norm_ple, w_ple_gate, w_ple_proj,
           norm_final):
    depth = w_in.shape[0]
    assert depth == 1
    li = 0
    bp, lp = x_prompt.shape[0], x_prompt.shape[1]
    bs, ls = x_sample.shape[0], x_sample.shape[1]
    ls_pad = SUBLANE
    tp = bp * lp

    w_t = jnp.transpose(w_in[li])
    w_in_p = jnp.concatenate([_pad_shift_rows(w_t[:D_SHIFT]), w_t[D_SHIFT:]], axis=0).astype(BF16)
    row = lambda a: a.reshape(1, -1).astype(F32)
    pp = {
        "mu": row(_pad_shift_cols(mu_shift[li])),
        "w0": row(w0[li]), "a0": row(a0[li]), "k_k": row(k_k[li]), "k_a": row(k_a[li]),
        "w_up": _pad_rows(w_up[li], LW_PAD).astype(BF16),
        "a_up": _pad_rows(a_up[li], LA_PAD).astype(BF16),
        "g_up": _pad_rows(g_up[li], LG_PAD).astype(BF16),
    }
    rk = row(r_k[li])
    lnw = row(lnx_w[li])
    lnb = row(lnx_b[li])
    lower = jax.nn.softmax(hgrn_lb.astype(F32), axis=0)
    lb = row(jnp.cumsum(lower, axis=0)[li])
    nw = row(hgrn_norm[li])
    wo = w_out[li].astype(BF16)
    nf = row(norm_ffn[li])
    wr = jnp.concatenate([w_router[li], jnp.zeros((D_MODEL, LANE - N_EXPERTS), F32)], axis=1)
    br = jnp.concatenate([b_router[li], jnp.zeros((LANE - N_EXPERTS,), F32)]).reshape(1, LANE)
    n_ple = row(norm_ple[li])
    wpg = w_ple_gate[li].astype(BF16)
    wpp = w_ple_proj[li].astype(BF16)
    n_fin = row(norm_final)
    g_mix = row(norm_mix[li])

    def mixer(x2d, shift_prev, s_rwkv, s_hgrn, cnt0, bn, l_total, l_valid, tm_in, bb_prep, tt, bb_scan, ub_scan, c,
              tm_out):
        proj = _inproj(x2d, g_mix, w_in_p, tm_in, 768)
        streams = _rwkv_prep(proj, shift_prev, pp, bn, l_total, l_valid, bb_prep, tt)
        yr, s_rwkv_new = _rwkv_scan(streams, rk, lnw, lnb, s_rwkv, bn, l_total, bb_scan, ub_scan, c)
        yh, s_hgrn_new = _hgrn_scan(proj, lb, nw, s_hgrn, bn, l_total, l_valid, bb_scan, ub_scan, c)
        h1, xn2, idx, gates, rank, cnt = _outproj_router(
            x2d, yr.reshape(bn * l_total, D_RWKV), yh.reshape(bn * l_total, D_HGRN), wo, nf, wr, br, cnt0, tm_out,
            l_valid, l_total)
        new_shift = _unpad_shift_cols(proj.reshape(bn, l_total, D_IN_PAD)[:, l_valid - 1, :D_SHIFT_PAD])
        return h1, xn2, idx, gates, rank, cnt, new_shift, s_rwkv_new, s_hgrn_new

    xp2 = x_prompt.reshape(tp, D_MODEL)
    zero_shift = jnp.zeros((bp, 1, D_SHIFT_PAD), F32)
    h1p, xn2p, idxp, gatesp, rankp, cntp, shift_p, rwkv_p, hgrn_p = mixer(
        xp2, zero_shift, None, None, jnp.zeros((1, LANE), F32), bp, lp, lp, 1024, 1, 256, bp, 4, 64, 256)

    xs_pad = jnp.concatenate([x_sample, jnp.zeros((bs, ls_pad - ls, D_MODEL), F32)], axis=1)
    xs2 = xs_pad.reshape(bs * ls_pad, D_MODEL)
    shift_s0 = _pad_shift_cols(state_rwkv_shift[li]).reshape(bs, 1, D_SHIFT_PAD)
    h1s, xn2s, idxs, gatess, ranks, cnt_all, shift_s, rwkv_s, hgrn_s = mixer(
        xs2, shift_s0, state_rwkv[li], state_hgrn[li], cntp, bs, ls_pad, ls, 1024, 16, ls_pad, 16, 2, ls_pad, 256)

    def compact(a):
        return a.reshape(bs, ls_pad, a.shape[-1])[:, :ls].reshape(bs * ls, a.shape[-1])

    h1s, xn2s, idxs, gatess, ranks = compact(h1s), compact(xn2s), compact(idxs), compact(gatess), compact(ranks)
    ts = bs * ls

    t_all = tp + ts
    idx_all = jnp.concatenate([idxp[:, :TOP_K], idxs[:, :TOP_K]], axis=0)
    rank_all = jnp.concatenate([rankp[:, :TOP_K], ranks[:, :TOP_K]], axis=0)
    counts = cnt_all[0, :N_EXPERTS].astype(I32)
    n_tiles_cap = -(-(t_all * TOP_K) // MOE_TM) + N_EXPERTS
    n_rows_cap = n_tiles_cap * MOE_TM
    n_items_cap = N_EXPERTS + n_tiles_cap // MOE_RT
    row_src, row_slot, n_used, item_e, item_t0, item_nt = _routing(idx_all, rank_all, counts, n_rows_cap,
                                                                   n_items_cap)
    x_packed = jnp.concatenate([xn2p, xn2s, jnp.zeros((SUBLANE, HALF), U32)], axis=0)
    ysc = _moe_experts(item_e, item_t0, item_nt, n_used, row_src, row_slot, x_packed, w_gu[li],
                       b_gu[li].reshape(N_EXPERTS, 1, -1), w_down[li], b_down[li].reshape(N_EXPERTS, 1, -1),
                       n_used[1], t_all)

    tm_fin = 256
    assert tp % tm_fin == 0 and ts % tm_fin == 0
    y_p = _ple_final(h1p, gatesp, ysc, t_all, 0, p_prompt[li].reshape(tp, D_PLE), n_ple, wpg, wpp, n_fin, tm_fin)
    y_s = _ple_final(h1s, gatess, ysc, t_all, tp, p_sample[li].reshape(ts, D_PLE), n_ple, wpg, wpp, n_fin, tm_fin)

    return (y_p.reshape(bp, lp, D_MODEL), y_s.reshape(bs, ls, D_MODEL),
            shift_p[None], rwkv_p[None], hgrn_p[None],
            shift_s[None], rwkv_s[None], hgrn_s[None])
```

```python
import functools

import jax
import jax.numpy as jnp
from jax import lax
from jax.experimental import pallas as pl
from jax.experimental.pallas import tpu as pltpu

F32 = jnp.float32
BF16 = jnp.bfloat16
I32 = jnp.int32
U32 = jnp.uint32

D_MODEL = 2048
D_RWKV = 1024
D_HGRN = 1024
RWKV_HEAD = 64
N_RWKV_HEADS = 16
HGRN_HEAD = 128
N_HGRN_HEADS = 8
LORA_W = 64
LORA_A = 64
LORA_G = 160
D_SHIFT = 3 * D_RWKV + LORA_W + LORA_A + LORA_G
N_EXPERTS = 32
TOP_K = 4
D_EXPERT = 2048
SWIGLU_LIMIT = 7.0
SWIGLU_ALPHA = 1.702
D_PLE = 256
RMS_EPS = 1e-6
GN_EPS = 64e-5
HGRN_NORM_EPS = 1e-5

LANE = 128
SUBLANE = 8
N_UNITS = 8

LW_PAD = LANE
LA_PAD = LANE
LG_PAD = 2 * LANE
OFF_WD = 3 * D_RWKV
OFF_AD = OFF_WD + LW_PAD
OFF_GD = OFF_AD + LA_PAD
D_SHIFT_PAD = OFF_GD + LG_PAD
D_IN_PAD = D_SHIFT_PAD + 4 * D_HGRN

MOE_TM = 128
MOE_RT = 10
MOE_TF = 256
VMEM_LIMIT = 56 * 1024 * 1024


def _cparams(sem, vmem=VMEM_LIMIT):
    return pltpu.CompilerParams(dimension_semantics=sem, vmem_limit_bytes=vmem)


def _rmsnorm(x, g):
    return x * lax.rsqrt(jnp.mean(x * x, axis=-1, keepdims=True) + RMS_EPS) * g


def _dot(a, b):
    return jnp.dot(a.astype(BF16), b.astype(BF16), preferred_element_type=F32)


def _dot_nt(a, b):
    return lax.dot_general(a.astype(BF16), b.astype(BF16), (((1,), (1,)), ((), ())),
                           preferred_element_type=F32)


def _dot_tn(a, b):
    return lax.dot_general(a.astype(BF16), b.astype(BF16), (((0,), (0,)), ((), ())),
                           preferred_element_type=F32)


def _split3(x):
    h = x.astype(BF16)
    r = x - h.astype(F32)
    m = r.astype(BF16)
    l = (r - m.astype(F32)).astype(BF16)
    return h, m, l


def _dot_exact_rhs(a, b_bf16):
    h, m, l = _split3(a)
    d = functools.partial(jnp.dot, preferred_element_type=F32)
    return d(h, b_bf16) + d(m, b_bf16) + d(l, b_bf16)


def _dot_exact_lhs(a_bf16, b):
    h, m, l = _split3(b)
    d = functools.partial(jnp.dot, preferred_element_type=F32)
    return d(a_bf16, h) + d(a_bf16, m) + d(a_bf16, l)


def _iota2(shape, dim):
    return lax.broadcasted_iota(I32, shape, dim)


def _cumsum_time(x):
    c = x.shape[0]
    tri = (_iota2((c, c), 0) >= _iota2((c, c), 1)).astype(BF16)
    return _dot_exact_lhs(tri, x)


def _same_head_mask():
    return (_iota2((LANE, LANE), 0) >= RWKV_HEAD) == (_iota2((LANE, LANE), 1) >= RWKV_HEAD)


def _sigmoid(x):
    return 1.0 / (1.0 + jnp.exp(-x))


def _inproj_kernel(x_ref, g_ref, w_ref, o_ref, xn_ref):
    @pl.when(pl.program_id(1) == 0)
    def _():
        xn_ref[...] = _rmsnorm(x_ref[...], g_ref[...]).astype(BF16)

    o_ref[...] = lax.dot_general(xn_ref[...], w_ref[...], (((1,), (1,)), ((), ())), preferred_element_type=F32)


def _inproj(x, g, w, tm, tn):
    t, d = x.shape
    n = w.shape[0]
    return pl.pallas_call(
        _inproj_kernel,
        out_shape=jax.ShapeDtypeStruct((t, n), F32),
        grid=(t // tm, n // tn),
        in_specs=[pl.BlockSpec((tm, d), lambda i, j: (i, 0)),
                  pl.BlockSpec((1, d), lambda i, j: (0, 0)),
                  pl.BlockSpec((tn, d), lambda i, j: (j, 0))],
        out_specs=pl.BlockSpec((tm, tn), lambda i, j: (i, j)),
        scratch_shapes=[pltpu.VMEM((tm, d), BF16)],
        compiler_params=_cparams(("parallel", "arbitrary")),
        name="inproj",
    )(x, g, w)


def _rwkv_prep_kernel(x_ref, p8_ref, sh_ref, mu_ref, w0_ref, wup_ref, a0_ref, aup_ref, gup_ref,
                      kk_ref, ka_ref,
                      r_o, k_o, v_o, lw_o, a_o, b_o, g_o, *, bb, tt, l_valid, l_total):
    ti = pl.program_id(1)
    w = D_SHIFT_PAD
    x3 = x_ref[...].reshape(bb, tt, w)
    rolled = pltpu.roll(x3, 1, axis=1)
    prev_tail = p8_ref[...].reshape(bb, SUBLANE, w)[:, SUBLANE - 1:SUBLANE, :]
    first = jnp.where(ti == 0, sh_ref[...], prev_tail)
    t_in = _iota2((bb, tt, w), 1)
    prev = jnp.where(t_in == 0, first, rolled)
    xs = (x3 + (prev - x3) * mu_ref[...]).reshape(bb * tt, w)

    r = xs[:, 0:D_RWKV]
    k = xs[:, D_RWKV:2 * D_RWKV]
    v = xs[:, 2 * D_RWKV:3 * D_RWKV]
    wd = xs[:, OFF_WD:OFF_WD + LW_PAD]
    ad = xs[:, OFF_AD:OFF_AD + LA_PAD]
    gd = xs[:, OFF_GD:OFF_GD + LG_PAD]

    z = -(w0_ref[...] + _dot(jnp.tanh(wd), wup_ref[...]))
    softplus = jnp.maximum(z, 0.0) + jnp.log(1.0 + jnp.exp(-jnp.abs(z)))
    lw = -jnp.exp(-softplus - 0.5)
    asig = _sigmoid(a0_ref[...] + _dot(ad, aup_ref[...]))
    g = _dot(_sigmoid(gd), gup_ref[...])

    kk = k * kk_ref[...]
    same_head = _same_head_mask().astype(BF16)
    sq = kk * kk
    ssq = jnp.concatenate(
        [_dot_exact_rhs(sq[:, u * LANE:(u + 1) * LANE], same_head) for u in range(N_UNITS)], axis=1)
    kkn = kk / jnp.maximum(jnp.sqrt(ssq), 1e-12)
    k2 = k * (1.0 + (asig - 1.0) * ka_ref[...])
    a_vec = -kkn
    b_vec = kkn * asig

    if l_valid < l_total:
        t_glob = (ti * tt + _iota2((bb, tt, D_RWKV), 1)).reshape(bb * tt, D_RWKV)
        ok = t_glob < l_valid
        zero = jnp.zeros_like(k2)
        lw, k2, v, a_vec, b_vec = (jnp.where(ok, t, zero) for t in (lw, k2, v, a_vec, b_vec))

    r_o[...] = r
    k_o[...] = k2
    v_o[...] = v
    lw_o[...] = lw
    a_o[...] = a_vec
    b_o[...] = b_vec
    g_o[...] = g


def _rwkv_prep(proj, shift_pad, pp, bn, l_total, l_valid, bb, tt):
    nt = l_total // tt
    rows = bb * tt
    w = D_SHIFT_PAD
    row_spec = pl.BlockSpec((rows, w), lambda bi, ti: (bi * nt + ti, 0))
    p8_spec = pl.BlockSpec((bb * SUBLANE, w),
                           lambda bi, ti: (jnp.maximum((bi * nt + ti) * (tt // SUBLANE) - 1, 0), 0))
    vec = lambda n: pl.BlockSpec((1, n), lambda bi, ti: (0, 0))
    mat = lambda a, b: pl.BlockSpec((a, b), lambda bi, ti: (0, 0))
    out_spec = pl.BlockSpec((rows, D_RWKV), lambda bi, ti: (bi * nt + ti, 0))
    out_sds = jax.ShapeDtypeStruct((bn * l_total, D_RWKV), F32)
    kern = functools.partial(_rwkv_prep_kernel, bb=bb, tt=tt, l_valid=l_valid, l_total=l_total)
    return pl.pallas_call(
        kern,
        out_shape=[out_sds] * 7,
        grid=(bn // bb, nt),
        in_specs=[row_spec, p8_spec,
                  pl.BlockSpec((bb, 1, w), lambda bi, ti: (bi, 0, 0)),
                  vec(w), vec(D_RWKV), mat(LW_PAD, D_RWKV), vec(D_RWKV), mat(LA_PAD, D_RWKV),
                  mat(LG_PAD, D_RWKV), vec(D_RWKV), vec(D_RWKV)],
        out_specs=[out_spec] * 7,
        compiler_params=_cparams(("parallel", "arbitrary")),
        name="rwkv_prep",
    )(proj, proj, shift_pad, pp["mu"], pp["w0"], pp["w_up"], pp["a0"], pp["a_up"], pp["g_up"],
      pp["k_k"], pp["k_a"])


def _rwkv_scan_kernel(*refs, bb, ub, c, has_s0):
    if has_s0:
        (r_ref, k_ref, v_ref, lw_ref, a_ref, b_ref, g_ref, rk_ref, lnw_ref, lnb_ref, s0_ref,
         y_ref, so_ref, s_scr) = refs
    else:
        (r_ref, k_ref, v_ref, lw_ref, a_ref, b_ref, g_ref, rk_ref, lnw_ref, lnb_ref,
         y_ref, so_ref, s_scr) = refs
        s0_ref = None
    ci = pl.program_id(2)
    n_chunks = pl.num_programs(2)

    lane = _iota2((1, LANE), 1)
    m0 = (lane < RWKV_HEAD).astype(F32)
    m1 = 1.0 - m0
    bd_mask = _same_head_mask().astype(F32)

    c2 = 2 * c
    ri = _iota2((c2, c2), 0)
    cj = _iota2((c2, c2), 1)
    same_blk = (ri >= c) == (cj >= c)
    mask_s = jnp.where(same_blk, (ri > cj).astype(F32), 0.0)
    mask_i = jnp.where(same_blk, (ri >= cj).astype(F32), 0.0)
    eye = (ri == cj).astype(F32)
    n_sq = max((c - 1).bit_length() - 1, 0)

    head_avg = bd_mask.astype(BF16)

    chains = [(j, w) for j in range(bb) for w in range(ub)]
    seqs = range(len(chains))

    @pl.when(ci == 0)
    def _():
        for n, (j, w) in enumerate(chains):
            if has_s0:
                zero = jnp.zeros((RWKV_HEAD, RWKV_HEAD), F32)
                top = jnp.concatenate([s0_ref[j, 2 * w], zero], axis=1)
                bottom = jnp.concatenate([zero, s0_ref[j, 2 * w + 1]], axis=1)
                s_scr[n] = jnp.concatenate([top, bottom], axis=0)
            else:
                s_scr[n] = jnp.zeros((LANE, LANE), F32)

    stack2 = lambda lo, hi: jnp.concatenate([lo, hi], axis=0)
    unit = lambda ref, j, w: ref[j, :, w * LANE:(w + 1) * LANE]
    vec = lambda ref, w: ref[:, w * LANE:(w + 1) * LANE]
    r = [unit(r_ref, j, w) for j, w in chains]
    k = [unit(k_ref, j, w) for j, w in chains]
    v = [unit(v_ref, j, w) for j, w in chains]
    lw = [unit(lw_ref, j, w) for j, w in chains]
    a = [unit(a_ref, j, w) for j, w in chains]
    b = [unit(b_ref, j, w) for j, w in chains]
    cum = [_cumsum_time(x) for x in lw]
    clast = [x[c - 1:c, :] for x in cum]
    p_inv = [jnp.exp(-x) for x in cum]
    a2 = [stack2(a[j] * jnp.exp(cum[j] - lw[j]) * m0, a[j] * jnp.exp(cum[j] - lw[j]) * m1) for j in seqs]
    r2 = [stack2(r[j] * jnp.exp(cum[j]) * m0, r[j] * jnp.exp(cum[j]) * m1) for j in seqs]
    b2 = [stack2(b[j] * p_inv[j], b[j] * p_inv[j]) for j in seqs]
    k2 = [stack2(k[j] * p_inv[j], k[j] * p_inv[j]) for j in seqs]
    v2 = [stack2(v[j] * m0, v[j] * m1) for j in seqs]
    lab = [mask_s * _dot_nt(a2[j], b2[j]) for j in seqs]
    lak = [mask_s * _dot_nt(a2[j], k2[j]) for j in seqs]
    rb = [mask_i * _dot_nt(r2[j], b2[j]) for j in seqs]
    rkm = [mask_i * _dot_nt(r2[j], k2[j]) for j in seqs]
    tinv = [eye + x for x in lab]
    xp = lab
    for _ in range(n_sq):
        xp = [_dot(x, x) for x in xp]
        tinv = [tinv[j] + _dot(tinv[j], xp[j]) for j in seqs]
    lakv = [_dot(lak[j], v2[j]) for j in seqs]
    rkv = [_dot(rkm[j], v2[j]) for j in seqs]
    s0 = [s_scr[n] for n in seqs]
    ar_s0 = [_dot_nt(stack2(a2[n], r2[n]), s0[n]) for n in seqs]
    u2 = [_dot(tinv[n], ar_s0[n][:c2] + lakv[n]) for n in seqs]
    y2 = [ar_s0[n][c2:] + _dot(rb[n], u2[n]) + rkv[n] for n in seqs]
    y = [x[:c] + x[c:] for x in y2]
    u = [x[:c] + x[c:] for x in u2]
    p_last = [jnp.exp(clast[n] - cum[n]) for n in seqs]
    for n in seqs:
        s_scr[n] = s0[n] * jnp.exp(clast[n]) + bd_mask * _dot_tn(
            stack2(u[n], v[n]), stack2(b[n] * p_last[n], k[n] * p_last[n]))

    mu = [_dot(x, head_avg) * (1.0 / RWKV_HEAD) for x in y]
    dlt = [y[n] - mu[n] for n in seqs]
    var = [_dot(x * x, head_avg) * (1.0 / RWKV_HEAD) for x in dlt]
    bonus = [_dot(r[n] * k[n] * vec(rk_ref, w), head_avg) * v[n] for n, (j, w) in enumerate(chains)]
    for n, (j, w) in enumerate(chains):
        yn = dlt[n] * lax.rsqrt(var[n] + GN_EPS) * vec(lnw_ref, w) + vec(lnb_ref, w)
        y_ref[j, :, w * LANE:(w + 1) * LANE] = (yn + bonus[n]) * unit(g_ref, j, w)

    @pl.when(ci == n_chunks - 1)
    def _():
        for n, (j, w) in enumerate(chains):
            s_fin = s_scr[n]
            so_ref[j, 2 * w] = s_fin[:RWKV_HEAD, :RWKV_HEAD]
            so_ref[j, 2 * w + 1] = pltpu.roll(s_fin, RWKV_HEAD, axis=1)[RWKV_HEAD:, :RWKV_HEAD]


def _rwkv_scan(streams, rk, lnw, lnb, s0, bn, l_total, bb, ub, c):
    nc = l_total // c
    has_s0 = s0 is not None
    blk = pl.BlockSpec((bb, c, ub * LANE), lambda bi, u, ci: (bi, ci, u))
    vec = pl.BlockSpec((1, ub * LANE), lambda bi, u, ci: (0, u))
    st = pl.BlockSpec((bb, 2 * ub, RWKV_HEAD, RWKV_HEAD), lambda bi, u, ci: (bi, u, 0, 0))
    in_specs = [blk] * 7 + [vec] * 3 + ([st] if has_s0 else [])
    args = [s.reshape(bn, l_total, D_RWKV) for s in streams] + [rk, lnw, lnb] + ([s0] if has_s0 else [])
    kern = functools.partial(_rwkv_scan_kernel, bb=bb, ub=ub, c=c, has_s0=has_s0)
    return pl.pallas_call(
        kern,
        out_shape=[jax.ShapeDtypeStruct((bn, l_total, D_RWKV), F32),
                   jax.ShapeDtypeStruct((bn, N_RWKV_HEADS, RWKV_HEAD, RWKV_HEAD), F32)],
        grid=(bn // bb, N_UNITS // ub, nc),
        in_specs=in_specs,
        out_specs=[blk, st],
        scratch_shapes=[pltpu.VMEM((bb * ub, LANE, LANE), F32)],
        compiler_params=_cparams(("parallel", "parallel", "arbitrary")),
        name="rwkv_scan",
    )(*args)


def _hgrn_kernel(*refs, bb, ub, c, has_s0, l_valid, l_total):
    if has_s0:
        q_ref, f_ref, i_ref, og_ref, lb_ref, nw_ref, s0_ref, y_ref, so_ref, s_scr = refs
    else:
        q_ref, f_ref, i_ref, og_ref, lb_ref, nw_ref, y_ref, so_ref, s_scr = refs
        s0_ref = None
    ci = pl.program_id(2)
    n_chunks = pl.num_programs(2)
    tri = (_iota2((c, c), 0) >= _iota2((c, c), 1)).astype(F32)
    mid = max(c // 2 - 1, 0)
    chains = [(j, w) for j in range(bb) for w in range(ub)]
    seqs = range(len(chains))
    unit = lambda ref, j, w: ref[j, :, w * LANE:(w + 1) * LANE]

    @pl.when(ci == 0)
    def _():
        for n, (j, w) in enumerate(chains):
            if has_s0:
                s_scr[n] = s0_ref[j, w]
            else:
                s_scr[n] = jnp.zeros((LANE, LANE), F32)

    def key_column(x):
        ones = jnp.ones((c, LANE), BF16)
        d = lambda p: lax.dot_general(p, ones, (((0,), (0,)), ((), ())), preferred_element_type=F32)
        h, m, l = _split3(x)
        return d(h) + d(m) + d(l)

    q = [unit(q_ref, j, w) * _sigmoid(unit(q_ref, j, w)) for j, w in chains]
    f = [lb_ref[:, w * LANE:(w + 1) * LANE] + (1.0 - lb_ref[:, w * LANE:(w + 1) * LANE])
         * _sigmoid(unit(f_ref, j, w)) for j, w in chains]
    logf = [jnp.log(x) for x in f]
    kf = [1.0 - x for x in f]
    v = [unit(i_ref, j, w) for j, w in chains]
    if l_valid < l_total:
        ok = (ci * c + _iota2((c, LANE), 0)) < l_valid
        logf = [jnp.where(ok, x, 0.0) for x in logf]
        kf = [jnp.where(ok, x, 0.0) for x in kf]
    cum = [_cumsum_time(x) for x in logf]
    cref = [x[mid:mid + 1, :] for x in cum]
    clast = [x[c - 1:c, :] for x in cum]
    amat = [tri * _dot_nt(q[j] * jnp.exp(cum[j] - cref[j]), kf[j] * jnp.exp(cref[j] - cum[j])) for j in seqs]
    st = [s_scr[j] for j in seqs]
    o = [_dot(q[j] * jnp.exp(cum[j]), st[j]) + _dot(amat[j], v[j]) for j in seqs]
    decay = [jnp.exp(key_column(x)) for x in logf]
    for j in seqs:
        s_scr[j] = st[j] * decay[j] + _dot_tn(kf[j] * jnp.exp(clast[j] - cum[j]), v[j])
    for n, (j, w) in enumerate(chains):
        on = o[n] * lax.rsqrt(jnp.mean(o[n] * o[n], axis=-1, keepdims=True) + HGRN_NORM_EPS) * nw_ref[...]
        og = unit(og_ref, j, w)
        y_ref[j, :, w * LANE:(w + 1) * LANE] = on * (og * _sigmoid(og))

    @pl.when(ci == n_chunks - 1)
    def _():
        for n, (j, w) in enumerate(chains):
            so_ref[j, w] = s_scr[n]


def _hgrn_scan(proj, lb, nw, s0, bn, l_total, l_valid, bb, ub, c):
    nc = l_total // c
    has_s0 = s0 is not None
    base = D_SHIFT_PAD // (ub * LANE)
    per = D_HGRN // (ub * LANE)
    proj3 = proj.reshape(bn, l_total, D_IN_PAD)

    def col(part):
        return pl.BlockSpec((bb, c, ub * LANE), lambda bi, u, ci: (bi, ci, base + part * per + u))

    st = pl.BlockSpec((bb, ub, LANE, LANE), lambda bi, u, ci: (bi, u, 0, 0))
    in_specs = [col(0), col(1), col(2), col(3),
                pl.BlockSpec((1, ub * LANE), lambda bi, u, ci: (0, u)),
                pl.BlockSpec((1, LANE), lambda bi, u, ci: (0, 0))] + ([st] if has_s0 else [])
    args = [proj3, proj3, proj3, proj3, lb, nw] + ([s0] if has_s0 else [])
    kern = functools.partial(_hgrn_kernel, bb=bb, ub=ub, c=c, has_s0=has_s0, l_valid=l_valid,
                             l_total=l_total)
    return pl.pallas_call(
        kern,
        out_shape=[jax.ShapeDtypeStruct((bn, l_total, D_HGRN), F32),
                   jax.ShapeDtypeStruct((bn, N_UNITS, LANE, LANE), F32)],
        grid=(bn // bb, N_UNITS // ub, nc),
        in_specs=in_specs,
        out_specs=[pl.BlockSpec((bb, c, ub * LANE), lambda bi, u, ci: (bi, ci, u)), st],
        scratch_shapes=[pltpu.VMEM((bb * ub, LANE, LANE), F32)],
        compiler_params=_cparams(("parallel", "parallel", "arbitrary")),
        name="hgrn_scan",
    )(*args)


def _outproj_kernel(x_ref, yr_ref, yh_ref, wo_ref, nf_ref, wr_ref, br_ref, cnt0_ref,
                    h_ref, xn_ref, idx_ref, gate_ref, rank_ref, cnt_ref, run_scr, *, l_valid, l_total):
    @pl.when(pl.program_id(0) == 0)
    def _():
        run_scr[...] = cnt0_ref[...]

    h = (x_ref[...] + _dot(yr_ref[...], wo_ref[0:D_RWKV, :]) + _dot(yh_ref[...], wo_ref[D_RWKV:, :]))
    h_ref[...] = h
    xn = _rmsnorm(h, nf_ref[...])
    xh = xn.astype(BF16)
    bits = lax.bitcast_convert_type(xh.astype(F32), U32)
    xn_ref[...] = (bits[:, D_MODEL // 2:] & jnp.uint32(0xFFFF0000)) | (bits[:, :D_MODEL // 2] >> 16)
    wr = wr_ref[...]
    wh = wr.astype(BF16)
    logits = (_dot(xh, wh) + _dot(xn - xh.astype(F32), wh) + _dot(xh, wr - wh.astype(F32))
              + br_ref[...])
    tm = logits.shape[0]
    lane = _iota2((tm, LANE), 1).astype(F32)
    neg = jnp.float32(-jnp.inf)
    work = jnp.where(lane < N_EXPERTS, logits, neg)
    idx_out = jnp.zeros((tm, LANE), I32)
    val_out = jnp.zeros((tm, LANE), F32)
    top0 = None
    picks = []
    for kk in range(TOP_K):
        m = jnp.max(work, axis=-1, keepdims=True)
        sel = jnp.min(jnp.where(work == m, lane, float(LANE)), axis=-1, keepdims=True)
        if kk == 0:
            top0 = m
        idx_out = jnp.where(lane == kk, sel.astype(I32), idx_out)
        val_out = jnp.where(lane == kk, jnp.exp(m - top0), val_out)
        picks.append(lane == sel)
        work = jnp.where(lane == sel, neg, work)
    idx_ref[...] = idx_out
    gate_ref[...] = val_out / jnp.sum(val_out, axis=-1, keepdims=True)

    chosen = sum(p.astype(F32) for p in picks)
    if l_valid < l_total:
        assert l_total & (l_total - 1) == 0 and tm % l_total == 0
        t_in_seq = _iota2((tm, LANE), 0) & (l_total - 1)
        chosen = jnp.where(t_in_seq < l_valid, chosen, 0.0)
    earlier = (_iota2((tm, tm), 0) > _iota2((tm, tm), 1)).astype(BF16)
    before = jnp.dot(earlier, chosen.astype(BF16), preferred_element_type=F32) + run_scr[...]
    rank_out = jnp.zeros((tm, LANE), F32)
    for kk in range(TOP_K):
        r_k = jnp.sum(jnp.where(picks[kk], before, 0.0), axis=-1, keepdims=True)
        rank_out = jnp.where(lane == kk, r_k, rank_out)
    rank_ref[...] = rank_out.astype(I32)
    run_scr[...] = run_scr[...] + jnp.sum(chosen, axis=0, keepdims=True)
    cnt_ref[...] = run_scr[...]


def _outproj_router(x, yr, yh, wo, nf, wr, br, cnt0, tm, l_valid, l_total):
    t = x.shape[0]
    row = lambda n: pl.BlockSpec((tm, n), lambda i: (i, 0))
    full = lambda a, b: pl.BlockSpec((a, b), lambda i: (0, 0))
    return pl.pallas_call(
        functools.partial(_outproj_kernel, l_valid=l_valid, l_total=l_total),
        out_shape=[jax.ShapeDtypeStruct((t, D_MODEL), F32), jax.ShapeDtypeStruct((t, D_MODEL // 2), U32),
                   jax.ShapeDtypeStruct((t, LANE), I32), jax.ShapeDtypeStruct((t, LANE), F32),
                   jax.ShapeDtypeStruct((t, LANE), I32), jax.ShapeDtypeStruct((1, LANE), F32)],
        grid=(t // tm,),
        in_specs=[row(D_MODEL), row(D_RWKV), row(D_HGRN), full(D_MODEL, D_MODEL), full(1, D_MODEL),
                  full(D_MODEL, LANE), full(1, LANE), full(1, LANE)],
        out_specs=[row(D_MODEL), row(D_MODEL // 2), row(LANE), row(LANE), row(LANE), full(1, LANE)],
        scratch_shapes=[pltpu.VMEM((1, LANE), F32)],
        compiler_params=_cparams(("arbitrary",)),
        name="outproj_router",
    )(x, yr, yh, wo, nf, wr, br, cnt0)


MOE_NF = D_EXPERT // MOE_TF
MOE_ISSUE = MOE_TM // MOE_NF
MOE_AHEAD_TILES = MOE_RT
MOE_AHEAD = MOE_AHEAD_TILES * MOE_TM
MOE_RING = 2 * MOE_RT
MOE_DUMP = (MOE_RING + MOE_RT + 2) * MOE_TM
HALF = D_MODEL // 2


def _ring_row_copy(x_hbm, tok, ring, sems, u):
    slot = (u // MOE_TM) % MOE_RING
    return pltpu.make_async_copy(x_hbm.at[pl.ds(tok, 1)], ring.at[slot, pl.ds(u % MOE_TM, 1)], sems.at[slot])


def _ring_tile_wait(x_hbm, ring, sems, tile):
    slot = tile % MOE_RING
    pltpu.make_async_copy(x_hbm.at[pl.ds(0, MOE_TM)], ring.at[slot], sems.at[slot]).wait()


def _scatter_row_copy(ybuf, ysc_hbm, sems, dst, u):
    slot = (u // MOE_TM) % MOE_RING
    return pltpu.make_async_copy(ybuf.at[slot, pl.ds(u % MOE_TM, 1)], ysc_hbm.at[pl.ds(dst, 1)], sems.at[slot])


def _scatter_tile_wait(ybuf, ysc_hbm, sems, tile):
    slot = tile % MOE_RING
    pltpu.make_async_copy(ybuf.at[slot], ysc_hbm.at[pl.ds(0, MOE_TM)], sems.at[slot]).wait()


def _moe_kernel(ie_ref, it0_ref, int_ref, nused_ref, rowsrc_ref, rowslot_ref, x_hbm, wg_ref, wu_ref, wd_ref,
                bg_ref, bu_ref, bd_ref, ysc_hbm, ring, xbuf, acc, ybuf, ring_sems, scat_sems,
                fill_sem, *, dump0):
    i = pl.program_id(0)
    f = pl.program_id(1)
    n_items = pl.num_programs(0)
    nt = int_ref[i]
    tile0 = it0_ref[i]
    n_used = nused_ref[0]

    @pl.when((i == 0) & (f == 0))
    def _():
        for t in range(MOE_RING):
            ybuf[t] = jnp.zeros((MOE_TM, HALF), U32)
        for s in range(MOE_DUMP // MOE_TM):
            cp = pltpu.make_async_copy(ybuf.at[0], ysc_hbm.at[pl.ds(dump0 + s * MOE_TM, MOE_TM)], fill_sem)
            cp.start()
            cp.wait()

        def head(r, carry):
            _ring_row_copy(x_hbm, rowsrc_ref[r], ring, ring_sems, r).start()
            return carry

        lax.fori_loop(0, MOE_AHEAD, head, 0)

    @pl.when(nt > 0)
    def _():
        @pl.when(f == 0)
        def _():
            def load(t, carry):
                _ring_tile_wait(x_hbm, ring, ring_sems, tile0 + t)
                w = ring[(tile0 + t) % MOE_RING]
                lo = lax.bitcast_convert_type(w << 16, F32).astype(BF16)
                hi = lax.bitcast_convert_type(w & jnp.uint32(0xFFFF0000), F32).astype(BF16)
                xbuf[t, :, 0:HALF] = lo
                xbuf[t, :, HALF:D_MODEL] = hi
                acc[t] = jnp.broadcast_to(bd_ref[0], (MOE_TM, D_MODEL))
                return carry

            lax.fori_loop(0, nt, load, 0)

        bg = bg_ref[0]
        bu = bu_ref[0]

        def issue(t):
            step = f * nt + t
            pos = tile0 * MOE_TM + step * MOE_ISSUE
            row = (step * MOE_ISSUE) % MOE_TM
            tile_s = tile0 + step // MOE_NF
            slot_s = tile_s % MOE_RING
            slot_g = (tile_s + MOE_AHEAD_TILES) % MOE_RING
            for q in range(MOE_ISSUE):
                pltpu.make_async_copy(x_hbm.at[pl.ds(rowsrc_ref[pos + MOE_AHEAD + q], 1)],
                                      ring.at[slot_g, pl.ds(row + q, 1)], ring_sems.at[slot_g]).start()
            for q in range(MOE_ISSUE):
                pltpu.make_async_copy(ybuf.at[slot_s, pl.ds(row + q, 1)],
                                      ysc_hbm.at[pl.ds(rowslot_ref[pos + q], 1)], scat_sems.at[slot_s]).start()

        def spans(todo):
            for t0, k in todo:
                for s in range(k):
                    issue(t0 + s)

            def hidden(t0, k):
                x = xbuf[pl.ds(t0, k)].reshape(k * MOE_TM, D_MODEL)
                g = jnp.minimum(jnp.dot(x, wg_ref[0].astype(BF16), preferred_element_type=F32) + bg, SWIGLU_LIMIT)
                u = jnp.clip(jnp.dot(x, wu_ref[0].astype(BF16), preferred_element_type=F32) + bu,
                             -SWIGLU_LIMIT, SWIGLU_LIMIT)
                return ((u + 1.0) * g * _sigmoid(SWIGLU_ALPHA * g)).astype(BF16)

            hid = hidden(*todo[0])
            for s, (t0, k) in enumerate(todo):
                nxt = hidden(*todo[s + 1]) if s + 1 < len(todo) else None
                down = jnp.dot(hid, wd_ref[0].astype(BF16), preferred_element_type=F32)
                acc[pl.ds(t0, k)] += down.reshape(k, MOE_TM, D_MODEL)
                hid = nxt

        def quad(p, carry):
            spans([(4 * p, 2), (4 * p + 2, 2)])
            return carry

        n_quads = nt // 4
        lax.fori_loop(0, n_quads, quad, 0)

        def pair(p, carry):
            spans([(4 * n_quads + 2 * p, 2)])
            return carry

        lax.fori_loop(0, (nt - 4 * n_quads) // 2, pair, 0)

        @pl.when(nt % 2 == 1)
        def _():
            spans([(nt - 1, 1)])

        @pl.when(f == MOE_NF - 1)
        def _():
            def pack(t, carry):
                vt = tile0 + t + MOE_AHEAD_TILES

                @pl.when(vt >= MOE_RING)
                def _():
                    _scatter_tile_wait(ybuf, ysc_hbm, scat_sems, vt)

                bits = lax.bitcast_convert_type(acc[t].astype(BF16).astype(F32), U32)
                ybuf[vt % MOE_RING] = (bits[:, HALF:] & jnp.uint32(0xFFFF0000)) | (bits[:, :HALF] >> 16)
                return carry

            lax.fori_loop(0, nt, pack, 0)

    @pl.when((i == n_items - 1) & (f == MOE_NF - 1))
    def _():
        for s in range(MOE_AHEAD_TILES):
            _ring_tile_wait(x_hbm, ring, ring_sems, n_used + s)

        def flush(u, carry):
            _scatter_row_copy(ybuf, ysc_hbm, scat_sems, rowslot_ref[u], u).start()
            return carry

        lax.fori_loop(n_used * MOE_TM, n_used * MOE_TM + MOE_AHEAD, flush, 0)

        def retire(vt, carry):
            _scatter_tile_wait(ybuf, ysc_hbm, scat_sems, vt)
            return carry

        lax.fori_loop(jnp.maximum(n_used - MOE_AHEAD_TILES, 0), n_used + MOE_AHEAD_TILES, retire, 0)


def _moe_experts(item_e, item_t0, item_nt, n_used, row_src, row_slot, x_packed, w_gu, b_gu, w_down, b_down,
                 n_items, n_tok):
    nf = MOE_NF
    dump0 = TOP_K * n_tok

    def fcol(i, f, int_):
        return jnp.where(int_[i] > 0, f, nf - 1)

    in_specs = [
        pl.BlockSpec(memory_space=pl.ANY),
        pl.BlockSpec((1, D_MODEL, MOE_TF), lambda i, f, ie, it0, int_, *_: (ie[i], 0, fcol(i, f, int_))),
        pl.BlockSpec((1, D_MODEL, MOE_TF), lambda i, f, ie, it0, int_, *_: (ie[i], 0, nf + fcol(i, f, int_))),
        pl.BlockSpec((1, MOE_TF, D_MODEL), lambda i, f, ie, it0, int_, *_: (ie[i], fcol(i, f, int_), 0)),
        pl.BlockSpec((1, 1, MOE_TF), lambda i, f, ie, it0, int_, *_: (ie[i], 0, fcol(i, f, int_))),
        pl.BlockSpec((1, 1, MOE_TF), lambda i, f, ie, it0, int_, *_: (ie[i], 0, nf + fcol(i, f, int_))),
        pl.BlockSpec((1, 1, D_MODEL), lambda i, f, ie, it0, int_, *_: (ie[i], 0, 0)),
    ]
    return pl.pallas_call(
        functools.partial(_moe_kernel, dump0=dump0),
        out_shape=jax.ShapeDtypeStruct((dump0 + MOE_DUMP, HALF), U32),
        grid_spec=pltpu.PrefetchScalarGridSpec(
            num_scalar_prefetch=6,
            grid=(n_items, nf),
            in_specs=in_specs,
            out_specs=pl.BlockSpec(memory_space=pl.ANY),
            scratch_shapes=[pltpu.VMEM((MOE_RING, MOE_TM, HALF), U32),
                            pltpu.VMEM((MOE_RT, MOE_TM, D_MODEL), BF16),
                            pltpu.VMEM((MOE_RT, MOE_TM, D_MODEL), F32),
                            pltpu.VMEM((MOE_RING, MOE_TM, HALF), U32),
                            pltpu.SemaphoreType.DMA((MOE_RING,)),
                            pltpu.SemaphoreType.DMA((MOE_RING,)),
                            pltpu.SemaphoreType.DMA(())]),
        compiler_params=_cparams(("arbitrary", "arbitrary")),
        name="moe_experts",
    )(item_e, item_t0, item_nt, n_used, row_src, row_slot, x_packed, w_gu, w_gu, w_down, b_gu, b_gu, b_down)


def _ple_kernel(h_ref, gate_ref, y0_ref, y1_ref, y2_ref, y3_ref, p_ref, np_ref, wg_ref, wp_ref, nfin_ref, o_ref):
    gates = gate_ref[...]
    lo = jnp.zeros((h_ref.shape[0], HALF), F32)
    hi = jnp.zeros((h_ref.shape[0], HALF), F32)
    for kk, y_ref in enumerate((y0_ref, y1_ref, y2_ref, y3_ref)):
        w = y_ref[...]
        g = gates[:, kk:kk + 1]
        lo = lo + g * lax.bitcast_convert_type(w << 16, F32)
        hi = hi + g * lax.bitcast_convert_type(w & jnp.uint32(0xFFFF0000), F32)
    h = h_ref[...] + jnp.concatenate([lo, hi], axis=1)
    gate = _sigmoid(_dot(_rmsnorm(h, np_ref[...]), wg_ref[...]))
    h = h + gate * _dot(p_ref[...], wp_ref[...])
    o_ref[...] = _rmsnorm(h, nfin_ref[...])


def _ple_final(h, gates, ysc, plane, tok0, p, n_ple, wg, wp, n_fin, tm):
    t = h.shape[0]
    row = lambda n: pl.BlockSpec((tm, n), lambda i: (i, 0))
    full = lambda a, b: pl.BlockSpec((a, b), lambda i: (0, 0))
    ysp = lambda kk: pl.BlockSpec((tm, HALF), lambda i: ((kk * plane + tok0) // tm + i, 0))
    return pl.pallas_call(
        _ple_kernel,
        out_shape=jax.ShapeDtypeStruct((t, D_MODEL), F32),
        grid=(t // tm,),
        in_specs=[row(D_MODEL), row(LANE), ysp(0), ysp(1), ysp(2), ysp(3), row(D_PLE), full(1, D_MODEL),
                  full(D_MODEL, D_MODEL), full(D_PLE, D_MODEL), full(1, D_MODEL)],
        out_specs=row(D_MODEL),
        compiler_params=_cparams(("parallel",)),
        name="ple_final",
    )(h, gates, ysc, ysc, ysc, ysc, p, n_ple, wg, wp, n_fin)


def _pad_shift_cols(a):
    def z(n):
        return jnp.zeros(a.shape[:-1] + (n,), a.dtype)
    c0 = 3 * D_RWKV
    c1 = c0 + LORA_W
    c2 = c1 + LORA_A
    return jnp.concatenate([a[..., :c0], a[..., c0:c1], z(LW_PAD - LORA_W), a[..., c1:c2], z(LA_PAD - LORA_A),
                            a[..., c2:], z(LG_PAD - LORA_G)], axis=-1)


def _pad_shift_rows(a):
    z = lambda n: jnp.zeros((n, a.shape[1]), a.dtype)
    c0 = 3 * D_RWKV
    c1 = c0 + LORA_W
    c2 = c1 + LORA_A
    return jnp.concatenate([a[:c0], a[c0:c1], z(LW_PAD - LORA_W), a[c1:c2], z(LA_PAD - LORA_A), a[c2:],
                            z(LG_PAD - LORA_G)], axis=0)


def _unpad_shift_cols(a):
    return jnp.concatenate([a[..., :OFF_WD], a[..., OFF_WD:OFF_WD + LORA_W], a[..., OFF_AD:OFF_AD + LORA_A],
                            a[..., OFF_GD:OFF_GD + LORA_G]], axis=-1)


def _pad_rows(a, n):
    return jnp.concatenate([a, jnp.zeros((n - a.shape[0],) + a.shape[1:], a.dtype)], axis=0)


def _route_tables_kernel(pos_ref, src0_hbm, slot0_hbm, src_ref, slot_ref, sems, *, n_tok):
    fill_src = pltpu.make_async_copy(src0_hbm, src_ref, sems.at[0])
    fill_slot = pltpu.make_async_copy(slot0_hbm, slot_ref, sems.at[1])
    fill_src.start()
    fill_slot.start()
    fill_src.wait()
    fill_slot.wait()

    def place(tok, carry):
        for kk in range(TOP_K):
            r = pos_ref[tok * TOP_K + kk]
            src_ref[r] = tok
            slot_ref[r + MOE_AHEAD] = tok + kk * n_tok
        return carry

    lax.fori_loop(0, n_tok, place, 0, unroll=2)


def _route_tables(pos, n_tok, n_stream):
    u = jnp.arange(n_stream, dtype=I32)
    src0 = jnp.full((n_stream,), n_tok, I32)
    slot0 = TOP_K * n_tok + u % MOE_DUMP
    smem = pl.BlockSpec(memory_space=pltpu.SMEM)
    hbm = pl.BlockSpec(memory_space=pl.ANY)
    return pl.pallas_call(
        functools.partial(_route_tables_kernel, n_tok=n_tok),
        out_shape=[jax.ShapeDtypeStruct((n_stream,), I32)] * 2,
        in_specs=[smem, hbm, hbm],
        out_specs=[smem, smem],
        scratch_shapes=[pltpu.SemaphoreType.DMA((2,))],
        name="route_tables",
    )(pos, src0, slot0)


def _routing(idx, rank, counts, n_rows_cap, n_items_cap):
    t = idx.shape[0]
    na = t * TOP_K
    flat_e = idx.reshape(na)
    rank = rank.reshape(na)
    ptiles = (counts + MOE_TM - 1) // MOE_TM
    pend = jnp.cumsum(ptiles)
    pstart = pend - ptiles
    pos = (pstart[flat_e] * MOE_TM + rank).astype(I32)
    row_src, row_slot = _route_tables(pos, t, n_rows_cap + 2 * MOE_AHEAD)
    items_per_e = (ptiles + MOE_RT - 1) // MOE_RT
    iend = jnp.cumsum(items_per_e)
    istart = iend - items_per_e
    ii = jnp.arange(n_items_cap, dtype=I32)
    e_of = jnp.minimum(jnp.searchsorted(iend, ii, side="right"), N_EXPERTS - 1).astype(I32)
    jj = ii - istart[e_of]
    used = ii < iend[-1]
    item_nt = jnp.where(used, jnp.clip(ptiles[e_of] - jj * MOE_RT, 0, MOE_RT), 0).astype(I32)
    item_t0 = jnp.where(used, pstart[e_of] + jj * MOE_RT, 0).astype(I32)
    last_e = e_of[jnp.maximum(iend[-1] - 1, 0)]
    item_e = jnp.where(used, e_of, last_e).astype(I32)
    n_used = jnp.stack([pend[-1], iend[-1]]).astype(I32)
    return row_src, row_slot, n_used, item_e, item_t0, item_nt


def kernel(x_prompt, x_sample, p_prompt, p_sample, state_rwkv_shift, state_rwkv, state_hgrn, norm_mix, w_in,
           mu_shift, w0, w_up, a0, a_up, g_up, k_k, k_a, r_k, lnx_w, lnx_b, hgrn_lb, hgrn_norm, w_out,
           norm_ffn, w_router, b_router, w_gu, b_gu, w_down, b_down, norm_ple, w_ple_gate, w_ple_proj,
           norm_final):
    depth = w_in.shape[0]
    assert depth == 1
    li = 0
    bp, lp = x_prompt.shape[0], x_prompt.shape[1]
    bs, ls = x_sample.shape[0], x_sample.shape[1]
    ls_pad = SUBLANE
    tp = bp * lp

    w_t = jnp.transpose(w_in[li])
    w_in_p = jnp.concatenate([_pad_shift_rows(w_t[:D_SHIFT]), w_t[D_SHIFT:]], axis=0).astype(BF16)
    row = lambda a: a.reshape(1, -1).astype(F32)
    pp = {
        "mu": row(_pad_shift_cols(mu_shift[li])),
        "w0": row(w0[li]), "a0": row(a0[li]), "k_k": row(k_k[li]), "k_a": row(k_a[li]),
        "w_up": _pad_rows(w_up[li], LW_PAD).astype(BF16),
        "a_up": _pad_rows(a_up[li], LA_PAD).astype(BF16),
        "g_up": _pad_rows(g_up[li], LG_PAD).astype(BF16),
    }
    rk = row(r_k[li])
    lnw = row(lnx_w[li])
    lnb = row(lnx_b[li])
    lower = jax.nn.softmax(hgrn_lb.astype(F32), axis=0)
    lb = row(jnp.cumsum(lower, axis=0)[li])
    nw = row(hgrn_norm[li])
    wo = w_out[li].astype(BF16)
    nf = row(norm_ffn[li])
    wr = jnp.concatenate([w_router[li], jnp.zeros((D_MODEL, LANE - N_EXPERTS), F32)], axis=1)
    br = jnp.concatenate([b_router[li], jnp.zeros((LANE - N_EXPERTS,), F32)]).reshape(1, LANE)
    n_ple = row(norm_ple[li])
    wpg = w_ple_gate[li].astype(BF16)
    wpp = w_ple_proj[li].astype(BF16)
    n_fin = row(norm_final)
    g_mix = row(norm_mix[li])

    def mixer(x2d, shift_prev, s_rwkv, s_hgrn, cnt0, bn, l_total, l_valid, tm_in, bb_prep, tt, bb_scan, ub_scan, c,
              tm_out):
        proj = _inproj(x2d, g_mix, w_in_p, tm_in, 768)
        streams = _rwkv_prep(proj, shift_prev, pp, bn, l_total, l_valid, bb_prep, tt)
        yr, s_rwkv_new = _rwkv_scan(streams, rk, lnw, lnb, s_rwkv, bn, l_total, bb_scan, ub_scan, c)
        yh, s_hgrn_new = _hgrn_scan(proj, lb, nw, s_hgrn, bn, l_total, l_valid, bb_scan, ub_scan, c)
        h1, xn2, idx, gates, rank, cnt = _outproj_router(
            x2d, yr.reshape(bn * l_total, D_RWKV), yh.reshape(bn * l_total, D_HGRN), wo, nf, wr, br, cnt0, tm_out,
            l_valid, l_total)
        new_shift = _unpad_shift_cols(proj.reshape(bn, l_total, D_IN_PAD)[:, l_valid - 1, :D_SHIFT_PAD])
        return h1, xn2, idx, gates, rank, cnt, new_shift, s_rwkv_new, s_hgrn_new

    xp2 = x_prompt.reshape(tp, D_MODEL)
    zero_shift = jnp.zeros((bp, 1, D_SHIFT_PAD), F32)
    h1p, xn2p, idxp, gatesp, rankp, cntp, shift_p, rwkv_p, hgrn_p = mixer(
        xp2, zero_shift, None, None, jnp.zeros((1, LANE), F32), bp, lp, lp, 1024, 1, 256, bp, 4, 64, 256)

    xs_pad = jnp.concatenate([x_sample, jnp.zeros((bs, ls_pad - ls, D_MODEL), F32)], axis=1)
    xs2 = xs_pad.reshape(bs * ls_pad, D_MODEL)
    shift_s0 = _pad_shift_cols(state_rwkv_shift[li]).reshape(bs, 1, D_SHIFT_PAD)
    h1s, xn2s, idxs, gatess, ranks, cnt_all, shift_s, rwkv_s, hgrn_s = mixer(
        xs2, shift_s0, state_rwkv[li], state_hgrn[li], cntp, bs, ls_pad, ls, 1024, 16, ls_pad, 16, 2, ls_pad, 256)

    def compact(a):
        return a.reshape(bs, ls_pad, a.shape[-1])[:, :ls].reshape(bs * ls, a.shape[-1])

    h1s, xn2s, idxs, gatess, ranks = compact(h1s), compact(xn2s), compact(idxs), compact(gatess), compact(ranks)
    ts = bs * ls

    t_all = tp + ts
    idx_all = jnp.concatenate([idxp[:, :TOP_K], idxs[:, :TOP_K]], axis=0)
    rank_all = jnp.concatenate([rankp[:, :TOP_K], ranks[:, :TOP_K]], axis=0)
    counts = cnt_all[0, :N_EXPERTS].astype(I32)
    n_tiles_cap = -(-(t_all * TOP_K) // MOE_TM) + N_EXPERTS
    n_rows_cap = n_tiles_cap * MOE_TM
    n_items_cap = N_EXPERTS + n_tiles_cap // MOE_RT
    row_src, row_slot, n_used, item_e, item_t0, item_nt = _routing(idx_all, rank_all, counts, n_rows_cap,
                                                                   n_items_cap)
    x_packed = jnp.concatenate([xn2p, xn2s, jnp.zeros((SUBLANE, HALF), U32)], axis=0)
    ysc = _moe_experts(item_e, item_t0, item_nt, n_used, row_src, row_slot, x_packed, w_gu[li],
                       b_gu[li].reshape(N_EXPERTS, 1, -1), w_down[li], b_down[li].reshape(N_EXPERTS, 1, -1),
                       n_used[1], t_all)

    tm_fin = 256
    assert tp % tm_fin == 0 and ts % tm_fin == 0
    y_p = _ple_final(h1p, gatesp, ysc, t_all, 0, p_prompt[li].reshape(tp, D_PLE), n_ple, wpg, wpp, n_fin, tm_fin)
    y_s = _ple_final(h1s, gatess, ysc, t_all, tp, p_sample[li].reshape(ts, D_PLE), n_ple, wpg, wpp, n_fin, tm_fin)

    return (y_p.reshape(bp, lp, D_MODEL), y_s.reshape(bs, ls, D_MODEL),
            shift_p[None], rwkv_p[None], hgrn_p[None],
            shift_s[None], rwkv_s[None], hgrn_s[None])
```

```python
import functools

import jax
import jax.numpy as jnp
from jax import lax
from jax.experimental import pallas as pl
from jax.experimental.pallas import tpu as pltpu

F32 = jnp.float32
BF16 = jnp.bfloat16
I32 = jnp.int32
U32 = jnp.uint32

D_MODEL = 2048
D_RWKV = 1024
D_HGRN = 1024
RWKV_HEAD = 64
N_RWKV_HEADS = 16
HGRN_HEAD = 128
N_HGRN_HEADS = 8
LORA_W = 64
LORA_A = 64
LORA_G = 160
D_SHIFT = 3 * D_RWKV + LORA_W + LORA_A + LORA_G
N_EXPERTS = 32
TOP_K = 4
D_EXPERT = 2048
SWIGLU_LIMIT = 7.0
SWIGLU_ALPHA = 1.702
D_PLE = 256
RMS_EPS = 1e-6
GN_EPS = 64e-5
HGRN_NORM_EPS = 1e-5

LANE = 128
SUBLANE = 8
N_UNITS = 8

LW_PAD = LANE
LA_PAD = LANE
LG_PAD = 2 * LANE
OFF_WD = 3 * D_RWKV
OFF_AD = OFF_WD + LW_PAD
OFF_GD = OFF_AD + LA_PAD
D_SHIFT_PAD = OFF_GD + LG_PAD
D_IN_PAD = D_SHIFT_PAD + 4 * D_HGRN

MOE_TM = 128
MOE_RT = 10
MOE_TF = 256
VMEM_LIMIT = 56 * 1024 * 1024


def _cparams(sem, vmem=VMEM_LIMIT):
    return pltpu.CompilerParams(dimension_semantics=sem, vmem_limit_bytes=vmem)


def _rmsnorm(x, g):
    return x * lax.rsqrt(jnp.mean(x * x, axis=-1, keepdims=True) + RMS_EPS) * g


def _dot(a, b):
    return jnp.dot(a.astype(BF16), b.astype(BF16), preferred_element_type=F32)


def _dot_nt(a, b):
    return lax.dot_general(a.astype(BF16), b.astype(BF16), (((1,), (1,)), ((), ())),
                           preferred_element_type=F32)


def _dot_tn(a, b):
    return lax.dot_general(a.astype(BF16), b.astype(BF16), (((0,), (0,)), ((), ())),
                           preferred_element_type=F32)


def _split3(x):
    h = x.astype(BF16)
    r = x - h.astype(F32)
    m = r.astype(BF16)
    l = (r - m.astype(F32)).astype(BF16)
    return h, m, l


def _dot_exact_rhs(a, b_bf16):
    h, m, l = _split3(a)
    d = functools.partial(jnp.dot, preferred_element_type=F32)
    return d(h, b_bf16) + d(m, b_bf16) + d(l, b_bf16)


def _dot_exact_lhs(a_bf16, b):
    h, m, l = _split3(b)
    d = functools.partial(jnp.dot, preferred_element_type=F32)
    return d(a_bf16, h) + d(a_bf16, m) + d(a_bf16, l)


def _iota2(shape, dim):
    return lax.broadcasted_iota(I32, shape, dim)


def _cumsum_time(x):
    c = x.shape[0]
    tri = (_iota2((c, c), 0) >= _iota2((c, c), 1)).astype(BF16)
    return _dot_exact_lhs(tri, x)


def _same_head_mask():
    return (_iota2((LANE, LANE), 0) >= RWKV_HEAD) == (_iota2((LANE, LANE), 1) >= RWKV_HEAD)


def _sigmoid(x):
    return 1.0 / (1.0 + jnp.exp(-x))


def _inproj_kernel(x_ref, g_ref, w_ref, o_ref, xn_ref):
    @pl.when(pl.program_id(1) == 0)
    def _():
        xn_ref[...] = _rmsnorm(x_ref[...], g_ref[...]).astype(BF16)

    o_ref[...] = lax.dot_general(xn_ref[...], w_ref[...], (((1,), (1,)), ((), ())), preferred_element_type=F32)


def _inproj(x, g, w, tm, tn):
    t, d = x.shape
    n = w.shape[0]
    return pl.pallas_call(
        _inproj_kernel,
        out_shape=jax.ShapeDtypeStruct((t, n), F32),
        grid=(t // tm, n // tn),
        in_specs=[pl.BlockSpec((tm, d), lambda i, j: (i, 0)),
                  pl.BlockSpec((1, d), lambda i, j: (0, 0)),
                  pl.BlockSpec((tn, d), lambda i, j: (j, 0))],
        out_specs=pl.BlockSpec((tm, tn), lambda i, j: (i, j)),
        scratch_shapes=[pltpu.VMEM((tm, d), BF16)],
        compiler_params=_cparams(("parallel", "arbitrary")),
        name="inproj",
    )(x, g, w)


def _rwkv_prep_kernel(x_ref, p8_ref, sh_ref, mu_ref, w0_ref, wup_ref, a0_ref, aup_ref, gup_ref,
                      kk_ref, ka_ref,
                      r_o, k_o, v_o, lw_o, a_o, b_o, g_o, *, bb, tt, l_valid, l_total):
    ti = pl.program_id(1)
    w = D_SHIFT_PAD
    x3 = x_ref[...].reshape(bb, tt, w)
    rolled = pltpu.roll(x3, 1, axis=1)
    prev_tail = p8_ref[...].reshape(bb, SUBLANE, w)[:, SUBLANE - 1:SUBLANE, :]
    first = jnp.where(ti == 0, sh_ref[...], prev_tail)
    t_in = _iota2((bb, tt, w), 1)
    prev = jnp.where(t_in == 0, first, rolled)
    xs = (x3 + (prev - x3) * mu_ref[...]).reshape(bb * tt, w)

    r = xs[:, 0:D_RWKV]
    k = xs[:, D_RWKV:2 * D_RWKV]
    v = xs[:, 2 * D_RWKV:3 * D_RWKV]
    wd = xs[:, OFF_WD:OFF_WD + LW_PAD]
    ad = xs[:, OFF_AD:OFF_AD + LA_PAD]
    gd = xs[:, OFF_GD:OFF_GD + LG_PAD]

    z = -(w0_ref[...] + _dot(jnp.tanh(wd), wup_ref[...]))
    softplus = jnp.maximum(z, 0.0) + jnp.log(1.0 + jnp.exp(-jnp.abs(z)))
    lw = -jnp.exp(-softplus - 0.5)
    asig = _sigmoid(a0_ref[...] + _dot(ad, aup_ref[...]))
    g = _dot(_sigmoid(gd), gup_ref[...])

    kk = k * kk_ref[...]
    same_head = _same_head_mask().astype(BF16)
    sq = kk * kk
    ssq = jnp.concatenate(
        [_dot_exact_rhs(sq[:, u * LANE:(u + 1) * LANE], same_head) for u in range(N_UNITS)], axis=1)
    kkn = kk / jnp.maximum(jnp.sqrt(ssq), 1e-12)
    k2 = k * (1.0 + (asig - 1.0) * ka_ref[...])
    a_vec = -kkn
    b_vec = kkn * asig

    if l_valid < l_total:
        t_glob = (ti * tt + _iota2((bb, tt, D_RWKV), 1)).reshape(bb * tt, D_RWKV)
        ok = t_glob < l_valid
        zero = jnp.zeros_like(k2)
        lw, k2, v, a_vec, b_vec = (jnp.where(ok, t, zero) for t in (lw, k2, v, a_vec, b_vec))

    r_o[...] = r
    k_o[...] = k2
    v_o[...] = v
    lw_o[...] = lw
    a_o[...] = a_vec
    b_o[...] = b_vec
    g_o[...] = g


def _rwkv_prep(proj, shift_pad, pp, bn, l_total, l_valid, bb, tt):
    nt = l_total // tt
    rows = bb * tt
    w = D_SHIFT_PAD
    row_spec = pl.BlockSpec((rows, w), lambda bi, ti: (bi * nt + ti, 0))
    p8_spec = pl.BlockSpec((bb * SUBLANE, w),
                           lambda bi, ti: (jnp.maximum((bi * nt + ti) * (tt // SUBLANE) - 1, 0), 0))
    vec = lambda n: pl.BlockSpec((1, n), lambda bi, ti: (0, 0))
    mat = lambda a, b: pl.BlockSpec((a, b), lambda bi, ti: (0, 0))
    out_spec = pl.BlockSpec((rows, D_RWKV), lambda bi, ti: (bi * nt + ti, 0))
    out_sds = jax.ShapeDtypeStruct((bn * l_total, D_RWKV), F32)
    kern = functools.partial(_rwkv_prep_kernel, bb=bb, tt=tt, l_valid=l_valid, l_total=l_total)
    return pl.pallas_call(
        kern,
        out_shape=[out_sds] * 7,
        grid=(bn // bb, nt),
        in_specs=[row_spec, p8_spec,
                  pl.BlockSpec((bb, 1, w), lambda bi, ti: (bi, 0, 0)),
                  vec(w), vec(D_RWKV), mat(LW_PAD, D_RWKV), vec(D_RWKV), mat(LA_PAD, D_RWKV),
                  mat(LG_PAD, D_RWKV), vec(D_RWKV), vec(D_RWKV)],
        out_specs=[out_spec] * 7,
        compiler_params=_cparams(("parallel", "arbitrary")),
        name="rwkv_prep",
    )(proj, proj, shift_pad, pp["mu"], pp["w0"], pp["w_up"], pp["a0"], pp["a_up"], pp["g_up"],
      pp["k_k"], pp["k_a"])


def _rwkv_scan_kernel(*refs, bb, ub, c, has_s0):
    if has_s0:
        (r_ref, k_ref, v_ref, lw_ref, a_ref, b_ref, g_ref, rk_ref, lnw_ref, lnb_ref, s0_ref,
         y_ref, so_ref, s_scr) = refs
    else:
        (r_ref, k_ref, v_ref, lw_ref, a_ref, b_ref, g_ref, rk_ref, lnw_ref, lnb_ref,
         y_ref, so_ref, s_scr) = refs
        s0_ref = None
    ci = pl.program_id(2)
    n_chunks = pl.num_programs(2)

    lane = _iota2((1, LANE), 1)
    m0 = (lane < RWKV_HEAD).astype(F32)
    m1 = 1.0 - m0
    bd_mask = _same_head_mask().astype(F32)

    c2 = 2 * c
    ri = _iota2((c2, c2), 0)
    cj = _iota2((c2, c2), 1)
    same_blk = (ri >= c) == (cj >= c)
    mask_s = jnp.where(same_blk, (ri > cj).astype(F32), 0.0)
    mask_i = jnp.where(same_blk, (ri >= cj).astype(F32), 0.0)
    eye = (ri == cj).astype(F32)
    n_sq = max((c - 1).bit_length() - 1, 0)

    head_avg = bd_mask.astype(BF16)

    chains = [(j, w) for j in range(bb) for w in range(ub)]
    seqs = range(len(chains))

    @pl.when(ci == 0)
    def _():
        for n, (j, w) in enumerate(chains):
            if has_s0:
                zero = jnp.zeros((RWKV_HEAD, RWKV_HEAD), F32)
                top = jnp.concatenate([s0_ref[j, 2 * w], zero], axis=1)
                bottom = jnp.concatenate([zero, s0_ref[j, 2 * w + 1]], axis=1)
                s_scr[n] = jnp.concatenate([top, bottom], axis=0)
            else:
                s_scr[n] = jnp.zeros((LANE, LANE), F32)

    stack2 = lambda lo, hi: jnp.concatenate([lo, hi], axis=0)
    unit = lambda ref, j, w: ref[j, :, w * LANE:(w + 1) * LANE]
    vec = lambda ref, w: ref[:, w * LANE:(w + 1) * LANE]
    r = [unit(r_ref, j, w) for j, w in chains]
    k = [unit(k_ref, j, w) for j, w in chains]
    v = [unit(v_ref, j, w) for j, w in chains]
    lw = [unit(lw_ref, j, w) for j, w in chains]
    a = [unit(a_ref, j, w) for j, w in chains]
    b = [unit(b_ref, j, w) for j, w in chains]
    cum = [_cumsum_time(x) for x in lw]
    clast = [x[c - 1:c, :] for x in cum]
    p_inv = [jnp.exp(-x) for x in cum]
    a2 = [stack2(a[j] * jnp.exp(cum[j] - lw[j]) * m0, a[j] * jnp.exp(cum[j] - lw[j]) * m1) for j in seqs]
    r2 = [stack2(r[j] * jnp.exp(cum[j]) * m0, r[j] * jnp.exp(cum[j]) * m1) for j in seqs]
    b2 = [stack2(b[j] * p_inv[j], b[j] * p_inv[j]) for j in seqs]
    k2 = [stack2(k[j] * p_inv[j], k[j] * p_inv[j]) for j in seqs]
    v2 = [stack2(v[j] * m0, v[j] * m1) for j in seqs]
    lab = [mask_s * _dot_nt(a2[j], b2[j]) for j in seqs]
    lak = [mask_s * _dot_nt(a2[j], k2[j]) for j in seqs]
    rb = [mask_i * _dot_nt(r2[j], b2[j]) for j in seqs]
    rkm = [mask_i * _dot_nt(r2[j], k2[j]) for j in seqs]
    tinv = [eye + x for x in lab]
    xp = lab
    for _ in range(n_sq):
        xp = [_dot(x, x) for x in xp]
        tinv = [tinv[j] + _dot(tinv[j], xp[j]) for j in seqs]
    lakv = [_dot(lak[j], v2[j]) for j in seqs]
    rkv = [_dot(rkm[j], v2[j]) for j in seqs]
    s0 = [s_scr[n] for n in seqs]
    ar_s0 = [_dot_nt(stack2(a2[n], r2[n]), s0[n]) for n in seqs]
    u2 = [_dot(tinv[n], ar_s0[n][:c2] + lakv[n]) for n in seqs]
    y2 = [ar_s0[n][c2:] + _dot(rb[n], u2[n]) + rkv[n] for n in seqs]
    y = [x[:c] + x[c:] for x in y2]
    u = [x[:c] + x[c:] for x in u2]
    p_last = [jnp.exp(clast[n] - cum[n]) for n in seqs]
    for n in seqs:
        s_scr[n] = s0[n] * jnp.exp(clast[n]) + bd_mask * _dot_tn(
            stack2(u[n], v[n]), stack2(b[n] * p_last[n], k[n] * p_last[n]))

    mu = [_dot(x, head_avg) * (1.0 / RWKV_HEAD) for x in y]
    dlt = [y[n] - mu[n] for n in seqs]
    var = [_dot(x * x, head_avg) * (1.0 / RWKV_HEAD) for x in dlt]
    bonus = [_dot(r[n] * k[n] * vec(rk_ref, w), head_avg) * v[n] for n, (j, w) in enumerate(chains)]
    for n, (j, w) in enumerate(chains):
        yn = dlt[n] * lax.rsqrt(var[n] + GN_EPS) * vec(lnw_ref, w) + vec(lnb_ref, w)
        y_ref[j, :, w * LANE:(w + 1) * LANE] = (yn + bonus[n]) * unit(g_ref, j, w)

    @pl.when(ci == n_chunks - 1)
    def _():
        for n, (j, w) in enumerate(chains):
            s_fin = s_scr[n]
            so_ref[j, 2 * w] = s_fin[:RWKV_HEAD, :RWKV_HEAD]
            so_ref[j, 2 * w + 1] = pltpu.roll(s_fin, RWKV_HEAD, axis=1)[RWKV_HEAD:, :RWKV_HEAD]


def _rwkv_scan(streams, rk, lnw, lnb, s0, bn, l_total, bb, ub, c):
    nc = l_total // c
    has_s0 = s0 is not None
    blk = pl.BlockSpec((bb, c, ub * LANE), lambda bi, u, ci: (bi, ci, u))
    vec = pl.BlockSpec((1, ub * LANE), lambda bi, u, ci: (0, u))
    st = pl.BlockSpec((bb, 2 * ub, RWKV_HEAD, RWKV_HEAD), lambda bi, u, ci: (bi, u, 0, 0))
    in_specs = [blk] * 7 + [vec] * 3 + ([st] if has_s0 else [])
    args = [s.reshape(bn, l_total, D_RWKV) for s in streams] + [rk, lnw, lnb] + ([s0] if has_s0 else [])
    kern = functools.partial(_rwkv_scan_kernel, bb=bb, ub=ub, c=c, has_s0=has_s0)
    return pl.pallas_call(
        kern,
        out_shape=[jax.ShapeDtypeStruct((bn, l_total, D_RWKV), F32),
                   jax.ShapeDtypeStruct((bn, N_RWKV_HEADS, RWKV_HEAD, RWKV_HEAD), F32)],
        grid=(bn // bb, N_UNITS // ub, nc),
        in_specs=in_specs,
        out_specs=[blk, st],
        scratch_shapes=[pltpu.VMEM((bb * ub, LANE, LANE), F32)],
        compiler_params=_cparams(("parallel", "parallel", "arbitrary")),
        name="rwkv_scan",
    )(*args)


def _hgrn_kernel(*refs, bb, ub, c, has_s0, l_valid, l_total):
    if has_s0:
        q_ref, f_ref, i_ref, og_ref, lb_ref, nw_ref, s0_ref, y_ref, so_ref, s_scr = refs
    else:
        q_ref, f_ref, i_ref, og_ref, lb_ref, nw_ref, y_ref, so_ref, s_scr = refs
        s0_ref = None
    ci = pl.program_id(2)
    n_chunks = pl.num_programs(2)
    tri = (_iota2((c, c), 0) >= _iota2((c, c), 1)).astype(F32)
    mid = max(c // 2 - 1, 0)
    chains = [(j, w) for j in range(bb) for w in range(ub)]
    seqs = range(len(chains))
    unit = lambda ref, j, w: ref[j, :, w * LANE:(w + 1) * LANE]

    @pl.when(ci == 0)
    def _():
        for n, (j, w) in enumerate(chains):
            if has_s0:
                s_scr[n] = s0_ref[j, w]
            else:
                s_scr[n] = jnp.zeros((LANE, LANE), F32)

    def key_column(x):
        ones = jnp.ones((c, LANE), BF16)
        d = lambda p: lax.dot_general(p, ones, (((0,), (0,)), ((), ())), preferred_element_type=F32)
        h, m, l = _split3(x)
        return d(h) + d(m) + d(l)

    q = [unit(q_ref, j, w) * _sigmoid(unit(q_ref, j, w)) for j, w in chains]
    f = [lb_ref[:, w * LANE:(w + 1) * LANE] + (1.0 - lb_ref[:, w * LANE:(w + 1) * LANE])
         * _sigmoid(unit(f_ref, j, w)) for j, w in chains]
    logf = [jnp.log(x) for x in f]
    kf = [1.0 - x for x in f]
    v = [unit(i_ref, j, w) for j, w in chains]
    if l_valid < l_total:
        ok = (ci * c + _iota2((c, LANE), 0)) < l_valid
        logf = [jnp.where(ok, x, 0.0) for x in logf]
        kf = [jnp.where(ok, x, 0.0) for x in kf]
    cum = [_cumsum_time(x) for x in logf]
    cref = [x[mid:mid + 1, :] for x in cum]
    clast = [x[c - 1:c, :] for x in cum]
    amat = [tri * _dot_nt(q[j] * jnp.exp(cum[j] - cref[j]), kf[j] * jnp.exp(cref[j] - cum[j])) for j in seqs]
    st = [s_scr[j] for j in seqs]
    o = [_dot(q[j] * jnp.exp(cum[j]), st[j]) + _dot(amat[j], v[j]) for j in seqs]
    decay = [jnp.exp(key_column(x)) for x in logf]
    for j in seqs:
        s_scr[j] = st[j] * decay[j] + _dot_tn(kf[j] * jnp.exp(clast[j] - cum[j]), v[j])
    for n, (j, w) in enumerate(chains):
        on = o[n] * lax.rsqrt(jnp.mean(o[n] * o[n], axis=-1, keepdims=True) + HGRN_NORM_EPS) * nw_ref[...]
        og = unit(og_ref, j, w)
        y_ref[j, :, w * LANE:(w + 1) * LANE] = on * (og * _sigmoid(og))

    @pl.when(ci == n_chunks - 1)
    def _():
        for n, (j, w) in enumerate(chains):
            so_ref[j, w] = s_scr[n]


def _hgrn_scan(proj, lb, nw, s0, bn, l_total, l_valid, bb, ub, c):
    nc = l_total // c
    has_s0 = s0 is not None
    base = D_SHIFT_PAD // (ub * LANE)
    per = D_HGRN // (ub * LANE)
    proj3 = proj.reshape(bn, l_total, D_IN_PAD)

    def col(part):
        return pl.BlockSpec((bb, c, ub * LANE), lambda bi, u, ci: (bi, ci, base + part * per + u))

    st = pl.BlockSpec((bb, ub, LANE, LANE), lambda bi, u, ci: (bi, u, 0, 0))
    in_specs = [col(0), col(1), col(2), col(3),
                pl.BlockSpec((1, ub * LANE), lambda bi, u, ci: (0, u)),
                pl.BlockSpec((1, LANE), lambda bi, u, ci: (0, 0))] + ([st] if has_s0 else [])
    args = [proj3, proj3, proj3, proj3, lb, nw] + ([s0] if has_s0 else [])
    kern = functools.partial(_hgrn_kernel, bb=bb, ub=ub, c=c, has_s0=has_s0, l_valid=l_valid,
                             l_total=l_total)
    return pl.pallas_call(
        kern,
        out_shape=[jax.ShapeDtypeStruct((bn, l_total, D_HGRN), F32),
                   jax.ShapeDtypeStruct((bn, N_UNITS, LANE, LANE), F32)],
        grid=(bn // bb, N_UNITS // ub, nc),
        in_specs=in_specs,
        out_specs=[pl.BlockSpec((bb, c, ub * LANE), lambda bi, u, ci: (bi, ci, u)), st],
        scratch_shapes=[pltpu.VMEM((bb * ub, LANE, LANE), F32)],
        compiler_params=_cparams(("parallel", "parallel", "arbitrary")),
        name="hgrn_scan",
    )(*args)


def _outproj_kernel(x_ref, yr_ref, yh_ref, wo_ref, nf_ref, wr_ref, br_ref, cnt0_ref,
                    h_ref, xn_ref, idx_ref, gate_ref, rank_ref, cnt_ref, run_scr, *, l_valid, l_total):
    @pl.when(pl.program_id(0) == 0)
    def _():
        run_scr[...] = cnt0_ref[...]

    h = (x_ref[...] + _dot(yr_ref[...], wo_ref[0:D_RWKV, :]) + _dot(yh_ref[...], wo_ref[D_RWKV:, :]))
    h_ref[...] = h
    xn = _rmsnorm(h, nf_ref[...])
    xh = xn.astype(BF16)
    bits = lax.bitcast_convert_type(xh.astype(F32), U32)
    xn_ref[...] = (bits[:, D_MODEL // 2:] & jnp.uint32(0xFFFF0000)) | (bits[:, :D_MODEL // 2] >> 16)
    wr = wr_ref[...]
    wh = wr.astype(BF16)
    logits = (_dot(xh, wh) + _dot(xn - xh.astype(F32), wh) + _dot(xh, wr - wh.astype(F32))
              + br_ref[...])
    tm = logits.shape[0]
    lane = _iota2((tm, LANE), 1).astype(F32)
    neg = jnp.float32(-jnp.inf)
    work = jnp.where(lane < N_EXPERTS, logits, neg)
    idx_out = jnp.zeros((tm, LANE), I32)
    val_out = jnp.zeros((tm, LANE), F32)
    top0 = None
    picks = []
    for kk in range(TOP_K):
        m = jnp.max(work, axis=-1, keepdims=True)
        sel = jnp.min(jnp.where(work == m, lane, float(LANE)), axis=-1, keepdims=True)
        if kk == 0:
            top0 = m
        idx_out = jnp.where(lane == kk, sel.astype(I32), idx_out)
        val_out = jnp.where(lane == kk, jnp.exp(m - top0), val_out)
        picks.append(lane == sel)
        work = jnp.where(lane == sel, neg, work)
    idx_ref[...] = idx_out
    gate_ref[...] = val_out / jnp.sum(val_out, axis=-1, keepdims=True)

    chosen = sum(p.astype(F32) for p in picks)
    if l_valid < l_total:
        assert l_total & (l_total - 1) == 0 and tm % l_total == 0
        t_in_seq = _iota2((tm, LANE), 0) & (l_total - 1)
        chosen = jnp.where(t_in_seq < l_valid, chosen, 0.0)
    earlier = (_iota2((tm, tm), 0) > _iota2((tm, tm), 1)).astype(BF16)
    before = jnp.dot(earlier, chosen.astype(BF16), preferred_element_type=F32) + run_scr[...]
    rank_out = jnp.zeros((tm, LANE), F32)
    for kk in range(TOP_K):
        r_k = jnp.sum(jnp.where(picks[kk], before, 0.0), axis=-1, keepdims=True)
        rank_out = jnp.where(lane == kk, r_k, rank_out)
    rank_ref[...] = rank_out.astype(I32)
    run_scr[...] = run_scr[...] + jnp.sum(chosen, axis=0, keepdims=True)
    cnt_ref[...] = run_scr[...]


def _outproj_router(x, yr, yh, wo, nf, wr, br, cnt0, tm, l_valid, l_total):
    t = x.shape[0]
    row = lambda n: pl.BlockSpec((tm, n), lambda i: (i, 0))
    full = lambda a, b: pl.BlockSpec((a, b), lambda i: (0, 0), pipeline_mode=pl.Buffered(1))
    return pl.pallas_call(
        functools.partial(_outproj_kernel, l_valid=l_valid, l_total=l_total),
        out_shape=[jax.ShapeDtypeStruct((t, D_MODEL), F32), jax.ShapeDtypeStruct((t, D_MODEL // 2), U32),
                   jax.ShapeDtypeStruct((t, LANE), I32), jax.ShapeDtypeStruct((t, LANE), F32),
                   jax.ShapeDtypeStruct((t, LANE), I32), jax.ShapeDtypeStruct((1, LANE), F32)],
        grid=(t // tm,),
        in_specs=[row(D_MODEL), row(D_RWKV), row(D_HGRN), full(D_MODEL, D_MODEL), full(1, D_MODEL),
                  full(D_MODEL, LANE), full(1, LANE), full(1, LANE)],
        out_specs=[row(D_MODEL), row(D_MODEL // 2), row(LANE), row(LANE), row(LANE), full(1, LANE)],
        scratch_shapes=[pltpu.VMEM((1, LANE), F32)],
        compiler_params=_cparams(("arbitrary",)),
        name="outproj_router",
    )(x, yr, yh, wo, nf, wr, br, cnt0)


MOE_NF = D_EXPERT // MOE_TF
MOE_ISSUE = MOE_TM // MOE_NF
MOE_AHEAD_TILES = MOE_RT
MOE_AHEAD = MOE_AHEAD_TILES * MOE_TM
MOE_RING = 2 * MOE_RT
MOE_DUMP = (MOE_RING + MOE_RT + 2) * MOE_TM
HALF = D_MODEL // 2


def _ring_row_copy(x_hbm, tok, ring, sems, u):
    slot = (u // MOE_TM) % MOE_RING
    return pltpu.make_async_copy(x_hbm.at[pl.ds(tok, 1)], ring.at[slot, pl.ds(u % MOE_TM, 1)], sems.at[slot])


def _ring_tile_wait(x_hbm, ring, sems, tile):
    slot = tile % MOE_RING
    pltpu.make_async_copy(x_hbm.at[pl.ds(0, MOE_TM)], ring.at[slot], sems.at[slot]).wait()


def _scatter_row_copy(ybuf, ysc_hbm, sems, dst, u):
    slot = (u // MOE_TM) % MOE_RING
    return pltpu.make_async_copy(ybuf.at[slot, pl.ds(u % MOE_TM, 1)], ysc_hbm.at[pl.ds(dst, 1)], sems.at[slot])


def _scatter_tile_wait(ybuf, ysc_hbm, sems, tile):
    slot = tile % MOE_RING
    pltpu.make_async_copy(ybuf.at[slot], ysc_hbm.at[pl.ds(0, MOE_TM)], sems.at[slot]).wait()


def _moe_kernel(ie_ref, it0_ref, int_ref, nused_ref, rowsrc_ref, rowslot_ref, x_hbm, wg_ref, wu_ref, wd_ref,
                bg_ref, bu_ref, bd_ref, ysc_hbm, ring, xbuf, acc, ybuf, ring_sems, scat_sems,
                fill_sem, *, dump0):
    i = pl.program_id(0)
    f = pl.program_id(1)
    n_items = pl.num_programs(0)
    nt = int_ref[i]
    tile0 = it0_ref[i]
    n_used = nused_ref[0]

    @pl.when((i == 0) & (f == 0))
    def _():
        for t in range(MOE_RING):
            ybuf[t] = jnp.zeros((MOE_TM, HALF), U32)
        for s in range(MOE_DUMP // MOE_TM):
            cp = pltpu.make_async_copy(ybuf.at[0], ysc_hbm.at[pl.ds(dump0 + s * MOE_TM, MOE_TM)], fill_sem)
            cp.start()
            cp.wait()

        def head(r, carry):
            _ring_row_copy(x_hbm, rowsrc_ref[r], ring, ring_sems, r).start()
            return carry

        lax.fori_loop(0, MOE_AHEAD, head, 0)

    @pl.when(nt > 0)
    def _():
        @pl.when(f == 0)
        def _():
            def load(t, carry):
                _ring_tile_wait(x_hbm, ring, ring_sems, tile0 + t)
                w = ring[(tile0 + t) % MOE_RING]
                lo = lax.bitcast_convert_type(w << 16, F32).astype(BF16)
                hi = lax.bitcast_convert_type(w & jnp.uint32(0xFFFF0000), F32).astype(BF16)
                xbuf[t, :, 0:HALF] = lo
                xbuf[t, :, HALF:D_MODEL] = hi
                acc[t] = jnp.broadcast_to(bd_ref[0], (MOE_TM, D_MODEL))
                return carry

            lax.fori_loop(0, nt, load, 0)

        bg = bg_ref[0]
        bu = bu_ref[0]

        def issue(t):
            step = f * nt + t
            pos = tile0 * MOE_TM + step * MOE_ISSUE
            row = (step * MOE_ISSUE) % MOE_TM
            tile_s = tile0 + step // MOE_NF
            slot_s = tile_s % MOE_RING
            slot_g = (tile_s + MOE_AHEAD_TILES) % MOE_RING
            for q in range(MOE_ISSUE):
                pltpu.make_async_copy(x_hbm.at[pl.ds(rowsrc_ref[pos + MOE_AHEAD + q], 1)],
                                      ring.at[slot_g, pl.ds(row + q, 1)], ring_sems.at[slot_g]).start()
            for q in range(MOE_ISSUE):
                pltpu.make_async_copy(ybuf.at[slot_s, pl.ds(row + q, 1)],
                                      ysc_hbm.at[pl.ds(rowslot_ref[pos + q], 1)], scat_sems.at[slot_s]).start()

        def spans(todo):
            for t0, k in todo:
                for s in range(k):
                    issue(t0 + s)

            def hidden(t0, k):
                x = xbuf[pl.ds(t0, k)].reshape(k * MOE_TM, D_MODEL)
                g = jnp.minimum(jnp.dot(x, wg_ref[0].astype(BF16), preferred_element_type=F32) + bg, SWIGLU_LIMIT)
                u = jnp.clip(jnp.dot(x, wu_ref[0].astype(BF16), preferred_element_type=F32) + bu,
                             -SWIGLU_LIMIT, SWIGLU_LIMIT)
                return ((u + 1.0) * g * _sigmoid(SWIGLU_ALPHA * g)).astype(BF16)

            hid = hidden(*todo[0])
            for s, (t0, k) in enumerate(todo):
                nxt = hidden(*todo[s + 1]) if s + 1 < len(todo) else None
                down = jnp.dot(hid, wd_ref[0].astype(BF16), preferred_element_type=F32)
                acc[pl.ds(t0, k)] += down.reshape(k, MOE_TM, D_MODEL)
                hid = nxt

        def quad(p, carry):
            spans([(4 * p, 2), (4 * p + 2, 2)])
            return carry

        n_quads = nt // 4
        lax.fori_loop(0, n_quads, quad, 0)

        def pair(p, carry):
            spans([(4 * n_quads + 2 * p, 2)])
            return carry

        lax.fori_loop(0, (nt - 4 * n_quads) // 2, pair, 0)

        @pl.when(nt % 2 == 1)
        def _():
            spans([(nt - 1, 1)])

        @pl.when(f == MOE_NF - 1)
        def _():
            def pack(t, carry):
                vt = tile0 + t + MOE_AHEAD_TILES

                @pl.when(vt >= MOE_RING)
                def _():
                    _scatter_tile_wait(ybuf, ysc_hbm, scat_sems, vt)

                bits = lax.bitcast_convert_type(acc[t].astype(BF16).astype(F32), U32)
                ybuf[vt % MOE_RING] = (bits[:, HALF:] & jnp.uint32(0xFFFF0000)) | (bits[:, :HALF] >> 16)
                return carry

            lax.fori_loop(0, nt, pack, 0)

    @pl.when((i == n_items - 1) & (f == MOE_NF - 1))
    def _():
        for s in range(MOE_AHEAD_TILES):
            _ring_tile_wait(x_hbm, ring, ring_sems, n_used + s)

        def flush(u, carry):
            _scatter_row_copy(ybuf, ysc_hbm, scat_sems, rowslot_ref[u], u).start()
            return carry

        lax.fori_loop(n_used * MOE_TM, n_used * MOE_TM + MOE_AHEAD, flush, 0)

        def retire(vt, carry):
            _scatter_tile_wait(ybuf, ysc_hbm, scat_sems, vt)
            return carry

        lax.fori_loop(jnp.maximum(n_used - MOE_AHEAD_TILES, 0), n_used + MOE_AHEAD_TILES, retire, 0)


def _moe_experts(item_e, item_t0, item_nt, n_used, row_src, row_slot, x_packed, w_gu, b_gu, w_down, b_down,
                 n_items, n_tok):
    nf = MOE_NF
    dump0 = TOP_K * n_tok

    def fcol(i, f, int_):
        return jnp.where(int_[i] > 0, f, nf - 1)

    in_specs = [
        pl.BlockSpec(memory_space=pl.ANY),
        pl.BlockSpec((1, D_MODEL, MOE_TF), lambda i, f, ie, it0, int_, *_: (ie[i], 0, fcol(i, f, int_))),
        pl.BlockSpec((1, D_MODEL, MOE_TF), lambda i, f, ie, it0, int_, *_: (ie[i], 0, nf + fcol(i, f, int_))),
        pl.BlockSpec((1, MOE_TF, D_MODEL), lambda i, f, ie, it0, int_, *_: (ie[i], fcol(i, f, int_), 0)),
        pl.BlockSpec((1, 1, MOE_TF), lambda i, f, ie, it0, int_, *_: (ie[i], 0, fcol(i, f, int_))),
        pl.BlockSpec((1, 1, MOE_TF), lambda i, f, ie, it0, int_, *_: (ie[i], 0, nf + fcol(i, f, int_))),
        pl.BlockSpec((1, 1, D_MODEL), lambda i, f, ie, it0, int_, *_: (ie[i], 0, 0)),
    ]
    return pl.pallas_call(
        functools.partial(_moe_kernel, dump0=dump0),
        out_shape=jax.ShapeDtypeStruct((dump0 + MOE_DUMP, HALF), U32),
        grid_spec=pltpu.PrefetchScalarGridSpec(
            num_scalar_prefetch=6,
            grid=(n_items, nf),
            in_specs=in_specs,
            out_specs=pl.BlockSpec(memory_space=pl.ANY),
            scratch_shapes=[pltpu.VMEM((MOE_RING, MOE_TM, HALF), U32),
                            pltpu.VMEM((MOE_RT, MOE_TM, D_MODEL), BF16),
                            pltpu.VMEM((MOE_RT, MOE_TM, D_MODEL), F32),
                            pltpu.VMEM((MOE_RING, MOE_TM, HALF), U32),
                            pltpu.SemaphoreType.DMA((MOE_RING,)),
                            pltpu.SemaphoreType.DMA((MOE_RING,)),
                            pltpu.SemaphoreType.DMA(())]),
        compiler_params=_cparams(("arbitrary", "arbitrary")),
        name="moe_experts",
    )(item_e, item_t0, item_nt, n_used, row_src, row_slot, x_packed, w_gu, w_gu, w_down, b_gu, b_gu, b_down)


def _ple_kernel(h_ref, gate_ref, y0_ref, y1_ref, y2_ref, y3_ref, p_ref, np_ref, wg_ref, wp_ref, nfin_ref, o_ref):
    gates = gate_ref[...]
    lo = jnp.zeros((h_ref.shape[0], HALF), F32)
    hi = jnp.zeros((h_ref.shape[0], HALF), F32)
    for kk, y_ref in enumerate((y0_ref, y1_ref, y2_ref, y3_ref)):
        w = y_ref[...]
        g = gates[:, kk:kk + 1]
        lo = lo + g * lax.bitcast_convert_type(w << 16, F32)
        hi = hi + g * lax.bitcast_convert_type(w & jnp.uint32(0xFFFF0000), F32)
    h = h_ref[...] + jnp.concatenate([lo, hi], axis=1)
    gate = _sigmoid(_dot(_rmsnorm(h, np_ref[...]), wg_ref[...]))
    h = h + gate * _dot(p_ref[...], wp_ref[...])
    o_ref[...] = _rmsnorm(h, nfin_ref[...])


def _ple_final(h, gates, ysc, plane, tok0, p, n_ple, wg, wp, n_fin, tm):
    t = h.shape[0]
    row = lambda n: pl.BlockSpec((tm, n), lambda i: (i, 0))
    full = lambda a, b: pl.BlockSpec((a, b), lambda i: (0, 0), pipeline_mode=pl.Buffered(1))
    ysp = lambda kk: pl.BlockSpec((tm, HALF), lambda i: ((kk * plane + tok0) // tm + i, 0))
    return pl.pallas_call(
        _ple_kernel,
        out_shape=jax.ShapeDtypeStruct((t, D_MODEL), F32),
        grid=(t // tm,),
        in_specs=[row(D_MODEL), row(LANE), ysp(0), ysp(1), ysp(2), ysp(3), row(D_PLE), full(1, D_MODEL),
                  full(D_MODEL, D_MODEL), full(D_PLE, D_MODEL), full(1, D_MODEL)],
        out_specs=row(D_MODEL),
        compiler_params=_cparams(("parallel",)),
        name="ple_final",
    )(h, gates, ysc, ysc, ysc, ysc, p, n_ple, wg, wp, n_fin)


def _pad_shift_cols(a):
    def z(n):
        return jnp.zeros(a.shape[:-1] + (n,), a.dtype)
    c0 = 3 * D_RWKV
    c1 = c0 + LORA_W
    c2 = c1 + LORA_A
    return jnp.concatenate([a[..., :c0], a[..., c0:c1], z(LW_PAD - LORA_W), a[..., c1:c2], z(LA_PAD - LORA_A),
                            a[..., c2:], z(LG_PAD - LORA_G)], axis=-1)


def _pad_shift_rows(a):
    z = lambda n: jnp.zeros((n, a.shape[1]), a.dtype)
    c0 = 3 * D_RWKV
    c1 = c0 + LORA_W
    c2 = c1 + LORA_A
    return jnp.concatenate([a[:c0], a[c0:c1], z(LW_PAD - LORA_W), a[c1:c2], z(LA_PAD - LORA_A), a[c2:],
                            z(LG_PAD - LORA_G)], axis=0)


def _unpad_shift_cols(a):
    return jnp.concatenate([a[..., :OFF_WD], a[..., OFF_WD:OFF_WD + LORA_W], a[..., OFF_AD:OFF_AD + LORA_A],
                            a[..., OFF_GD:OFF_GD + LORA_G]], axis=-1)


def _pad_rows(a, n):
    return jnp.concatenate([a, jnp.zeros((n - a.shape[0],) + a.shape[1:], a.dtype)], axis=0)


def _route_tables_kernel(pos_ref, src0_hbm, slot0_hbm, src_ref, slot_ref, sems, *, n_tok):
    fill_src = pltpu.make_async_copy(src0_hbm, src_ref, sems.at[0])
    fill_slot = pltpu.make_async_copy(slot0_hbm, slot_ref, sems.at[1])
    fill_src.start()
    fill_slot.start()
    fill_src.wait()
    fill_slot.wait()

    def place(tok, carry):
        for kk in range(TOP_K):
            r = pos_ref[tok * TOP_K + kk]
            src_ref[r] = tok
            slot_ref[r + MOE_AHEAD] = tok + kk * n_tok
        return carry

    lax.fori_loop(0, n_tok, place, 0, unroll=2)


def _route_tables(pos, n_tok, n_stream):
    u = jnp.arange(n_stream, dtype=I32)
    src0 = jnp.full((n_stream,), n_tok, I32)
    slot0 = TOP_K * n_tok + u % MOE_DUMP
    smem = pl.BlockSpec(memory_space=pltpu.SMEM)
    hbm = pl.BlockSpec(memory_space=pl.ANY)
    return pl.pallas_call(
        functools.partial(_route_tables_kernel, n_tok=n_tok),
        out_shape=[jax.ShapeDtypeStruct((n_stream,), I32)] * 2,
        in_specs=[smem, hbm, hbm],
        out_specs=[smem, smem],
        scratch_shapes=[pltpu.SemaphoreType.DMA((2,))],
        name="route_tables",
    )(pos, src0, slot0)


def _routing(idx, rank, counts, n_rows_cap, n_items_cap):
    t = idx.shape[0]
    na = t * TOP_K
    flat_e = idx.reshape(na)
    rank = rank.reshape(na)
    ptiles = (counts + MOE_TM - 1) // MOE_TM
    pend = jnp.cumsum(ptiles)
    pstart = pend - ptiles
    pos = (pstart[flat_e] * MOE_TM + rank).astype(I32)
    row_src, row_slot = _route_tables(pos, t, n_rows_cap + 2 * MOE_AHEAD)
    items_per_e = (ptiles + MOE_RT - 1) // MOE_RT
    iend = jnp.cumsum(items_per_e)
    istart = iend - items_per_e
    ii = jnp.arange(n_items_cap, dtype=I32)
    e_of = jnp.minimum(jnp.searchsorted(iend, ii, side="right"), N_EXPERTS - 1).astype(I32)
    jj = ii - istart[e_of]
    used = ii < iend[-1]
    item_nt = jnp.where(used, jnp.clip(ptiles[e_of] - jj * MOE_RT, 0, MOE_RT), 0).astype(I32)
    item_t0 = jnp.where(used, pstart[e_of] + jj * MOE_RT, 0).astype(I32)
    last_e = e_of[jnp.maximum(iend[-1] - 1, 0)]
    item_e = jnp.where(used, e_of, last_e).astype(I32)
    n_used = jnp.stack([pend[-1], iend[-1]]).astype(I32)
    return row_src, row_slot, n_used, item_e, item_t0, item_nt


def kernel(x_prompt, x_sample, p_prompt, p_sample, state_rwkv_shift, state_rwkv, state_hgrn, norm_mix, w_in,
           mu_shift, w0, w_up, a0, a_up, g_up, k_k, k_a, r_k, lnx_w, lnx_b, hgrn_lb, hgrn_norm, w_out,
           norm_ffn, w_router, b_router, w_gu, b_gu, w_down, b_down, norm_ple, w_ple_gate, w_ple_proj,
           norm_final):
    depth = w_in.shape[0]
    assert depth == 1
    li = 0
    bp, lp = x_prompt.shape[0], x_prompt.shape[1]
    bs, ls = x_sample.shape[0], x_sample.shape[1]
    ls_pad = SUBLANE
    tp = bp * lp

    w_t = jnp.transpose(w_in[li])
    w_in_p = jnp.concatenate([_pad_shift_rows(w_t[:D_SHIFT]), w_t[D_SHIFT:]], axis=0).astype(BF16)
    row = lambda a: a.reshape(1, -1).astype(F32)
    pp = {
        "mu": row(_pad_shift_cols(mu_shift[li])),
        "w0": row(w0[li]), "a0": row(a0[li]), "k_k": row(k_k[li]), "k_a": row(k_a[li]),
        "w_up": _pad_rows(w_up[li], LW_PAD).astype(BF16),
        "a_up": _pad_rows(a_up[li], LA_PAD).astype(BF16),
        "g_up": _pad_rows(g_up[li], LG_PAD).astype(BF16),
    }
    rk = row(r_k[li])
    lnw = row(lnx_w[li])
    lnb = row(lnx_b[li])
    lower = jax.nn.softmax(hgrn_lb.astype(F32), axis=0)
    lb = row(jnp.cumsum(lower, axis=0)[li])
    nw = row(hgrn_norm[li])
    wo = w_out[li].astype(BF16)
    nf = row(norm_ffn[li])
    wr = jnp.concatenate([w_router[li], jnp.zeros((D_MODEL, LANE - N_EXPERTS), F32)], axis=1)
    br = jnp.concatenate([b_router[li], jnp.zeros((LANE - N_EXPERTS,), F32)]).reshape(1, LANE)
    n_ple = row(norm_ple[li])
    wpg = w_ple_gate[li].astype(BF16)
    wpp = w_ple_proj[li].astype(BF16)
    n_fin = row(norm_final)
    g_mix = row(norm_mix[li])

    def mixer(x2d, shift_prev, s_rwkv, s_hgrn, cnt0, bn, l_total, l_valid, tm_in, bb_prep, tt, bb_scan, ub_scan, c,
              tm_out):
        proj = _inproj(x2d, g_mix, w_in_p, tm_in, 768)
        streams = _rwkv_prep(proj, shift_prev, pp, bn, l_total, l_valid, bb_prep, tt)
        yr, s_rwkv_new = _rwkv_scan(streams, rk, lnw, lnb, s_rwkv, bn, l_total, bb_scan, ub_scan, c)
        yh, s_hgrn_new = _hgrn_scan(proj, lb, nw, s_hgrn, bn, l_total, l_valid, bb_scan, ub_scan, c)
        h1, xn2, idx, gates, rank, cnt = _outproj_router(
            x2d, yr.reshape(bn * l_total, D_RWKV), yh.reshape(bn * l_total, D_HGRN), wo, nf, wr, br, cnt0, tm_out,
            l_valid, l_total)
        new_shift = _unpad_shift_cols(proj.reshape(bn, l_total, D_IN_PAD)[:, l_valid - 1, :D_SHIFT_PAD])
        return h1, xn2, idx, gates, rank, cnt, new_shift, s_rwkv_new, s_hgrn_new

    xp2 = x_prompt.reshape(tp, D_MODEL)
    zero_shift = jnp.zeros((bp, 1, D_SHIFT_PAD), F32)
    h1p, xn2p, idxp, gatesp, rankp, cntp, shift_p, rwkv_p, hgrn_p = mixer(
        xp2, zero_shift, None, None, jnp.zeros((1, LANE), F32), bp, lp, lp, 1024, 1, 256, bp, 4, 64, 512)

    xs_pad = jnp.concatenate([x_sample, jnp.zeros((bs, ls_pad - ls, D_MODEL), F32)], axis=1)
    xs2 = xs_pad.reshape(bs * ls_pad, D_MODEL)
    shift_s0 = _pad_shift_cols(state_rwkv_shift[li]).reshape(bs, 1, D_SHIFT_PAD)
    h1s, xn2s, idxs, gatess, ranks, cnt_all, shift_s, rwkv_s, hgrn_s = mixer(
        xs2, shift_s0, state_rwkv[li], state_hgrn[li], cntp, bs, ls_pad, ls, 1024, 16, ls_pad, 16, 4, ls_pad, 512)

    def compact(a):
        return a.reshape(bs, ls_pad, a.shape[-1])[:, :ls].reshape(bs * ls, a.shape[-1])

    h1s, xn2s, idxs, gatess, ranks = compact(h1s), compact(xn2s), compact(idxs), compact(gatess), compact(ranks)
    ts = bs * ls

    t_all = tp + ts
    idx_all = jnp.concatenate([idxp[:, :TOP_K], idxs[:, :TOP_K]], axis=0)
    rank_all = jnp.concatenate([rankp[:, :TOP_K], ranks[:, :TOP_K]], axis=0)
    counts = cnt_all[0, :N_EXPERTS].astype(I32)
    n_tiles_cap = -(-(t_all * TOP_K) // MOE_TM) + N_EXPERTS
    n_rows_cap = n_tiles_cap * MOE_TM
    n_items_cap = N_EXPERTS + n_tiles_cap // MOE_RT
    row_src, row_slot, n_used, item_e, item_t0, item_nt = _routing(idx_all, rank_all, counts, n_rows_cap,
                                                                   n_items_cap)
    x_packed = jnp.concatenate([xn2p, xn2s, jnp.zeros((SUBLANE, HALF), U32)], axis=0)
    ysc = _moe_experts(item_e, item_t0, item_nt, n_used, row_src, row_slot, x_packed, w_gu[li],
                       b_gu[li].reshape(N_EXPERTS, 1, -1), w_down[li], b_down[li].reshape(N_EXPERTS, 1, -1),
                       n_used[1], t_all)

    tm_fin = 512
    assert tp % tm_fin == 0 and ts % tm_fin == 0
    y_p = _ple_final(h1p, gatesp, ysc, t_all, 0, p_prompt[li].reshape(tp, D_PLE), n_ple, wpg, wpp, n_fin, tm_fin)
    y_s = _ple_final(h1s, gatess, ysc, t_all, tp, p_sample[li].reshape(ts, D_PLE), n_ple, wpg, wpp, n_fin, tm_fin)

    return (y_p.reshape(bp, lp, D_MODEL), y_s.reshape(bs, ls, D_MODEL),
            shift_p[None], rwkv_p[None], hgrn_p[None],
            shift_s[None], rwkv_s[None], hgrn_s[None])
```

```python
import functools

import jax
import jax.numpy as jnp
from jax import lax
from jax.experimental import pallas as pl
from jax.experimental.pallas import tpu as pltpu

F32 = jnp.float32
BF16 = jnp.bfloat16
I32 = jnp.int32
U32 = jnp.uint32

D_MODEL = 2048
D_RWKV = 1024
D_HGRN = 1024
RWKV_HEAD = 64
N_RWKV_HEADS = 16
HGRN_HEAD = 128
N_HGRN_HEADS = 8
LORA_W = 64
LORA_A = 64
LORA_G = 160
D_SHIFT = 3 * D_RWKV + LORA_W + LORA_A + LORA_G
N_EXPERTS = 32
TOP_K = 4
D_EXPERT = 2048
SWIGLU_LIMIT = 7.0
SWIGLU_ALPHA = 1.702
D_PLE = 256
RMS_EPS = 1e-6
GN_EPS = 64e-5
HGRN_NORM_EPS = 1e-5

LANE = 128
SUBLANE = 8
N_UNITS = 8

LW_PAD = LANE
LA_PAD = LANE
LG_PAD = 2 * LANE
OFF_WD = 3 * D_RWKV
OFF_AD = OFF_WD + LW_PAD
OFF_GD = OFF_AD + LA_PAD
D_SHIFT_PAD = OFF_GD + LG_PAD
D_IN_PAD = D_SHIFT_PAD + 4 * D_HGRN

MOE_TM = 128
MOE_RT = 10
MOE_TF = 256
VMEM_LIMIT = 56 * 1024 * 1024


def _cparams(sem, vmem=VMEM_LIMIT):
    return pltpu.CompilerParams(dimension_semantics=sem, vmem_limit_bytes=vmem)


def _rmsnorm(x, g):
    return x * lax.rsqrt(jnp.mean(x * x, axis=-1, keepdims=True) + RMS_EPS) * g


def _dot(a, b):
    return jnp.dot(a.astype(BF16), b.astype(BF16), preferred_element_type=F32)


def _dot_nt(a, b):
    return lax.dot_general(a.astype(BF16), b.astype(BF16), (((1,), (1,)), ((), ())),
                           preferred_element_type=F32)


def _dot_tn(a, b):
    return lax.dot_general(a.astype(BF16), b.astype(BF16), (((0,), (0,)), ((), ())),
                           preferred_element_type=F32)


def _split3(x):
    h = x.astype(BF16)
    r = x - h.astype(F32)
    m = r.astype(BF16)
    l = (r - m.astype(F32)).astype(BF16)
    return h, m, l


def _dot_exact_rhs(a, b_bf16):
    h, m, l = _split3(a)
    d = functools.partial(jnp.dot, preferred_element_type=F32)
    return d(h, b_bf16) + d(m, b_bf16) + d(l, b_bf16)


def _dot_exact_lhs(a_bf16, b):
    n = b.shape[1]
    parts = jnp.dot(a_bf16, jnp.concatenate(_split3(b), axis=1), preferred_element_type=F32)
    return parts[:, :n] + parts[:, n:2 * n] + parts[:, 2 * n:]


def _iota2(shape, dim):
    return lax.broadcasted_iota(I32, shape, dim)


def _cumsum_time(x):
    c = x.shape[0]
    tri = (_iota2((c, c), 0) >= _iota2((c, c), 1)).astype(BF16)
    return _dot_exact_lhs(tri, x)


def _same_head_mask():
    return (_iota2((LANE, LANE), 0) >= RWKV_HEAD) == (_iota2((LANE, LANE), 1) >= RWKV_HEAD)


def _sigmoid(x):
    return 1.0 / (1.0 + jnp.exp(-x))


def _inproj_kernel(x_ref, g_ref, w_ref, o_ref, xn_ref):
    @pl.when(pl.program_id(1) == 0)
    def _():
        xn_ref[...] = _rmsnorm(x_ref[...], g_ref[...]).astype(BF16)

    o_ref[...] = lax.dot_general(xn_ref[...], w_ref[...], (((1,), (1,)), ((), ())), preferred_element_type=F32)


def _inproj(x, g, w, tm, tn):
    t, d = x.shape
    n = w.shape[0]
    return pl.pallas_call(
        _inproj_kernel,
        out_shape=jax.ShapeDtypeStruct((t, n), F32),
        grid=(t // tm, n // tn),
        in_specs=[pl.BlockSpec((tm, d), lambda i, j: (i, 0)),
                  pl.BlockSpec((1, d), lambda i, j: (0, 0)),
                  pl.BlockSpec((tn, d), lambda i, j: (j, 0))],
        out_specs=pl.BlockSpec((tm, tn), lambda i, j: (i, j)),
        scratch_shapes=[pltpu.VMEM((tm, d), BF16)],
        compiler_params=_cparams(("parallel", "arbitrary")),
        name="inproj",
    )(x, g, w)


def _rwkv_prep_kernel(x_ref, p8_ref, sh_ref, mu_ref, w0_ref, wup_ref, a0_ref, aup_ref, gup_ref,
                      kk_ref, ka_ref,
                      r_o, k_o, v_o, lw_o, a_o, b_o, g_o, *, bb, tt, l_valid, l_total):
    ti = pl.program_id(1)
    w = D_SHIFT_PAD
    x3 = x_ref[...].reshape(bb, tt, w)
    rolled = pltpu.roll(x3, 1, axis=1)
    prev_tail = p8_ref[...].reshape(bb, SUBLANE, w)[:, SUBLANE - 1:SUBLANE, :]
    first = jnp.where(ti == 0, sh_ref[...], prev_tail)
    t_in = _iota2((bb, tt, w), 1)
    prev = jnp.where(t_in == 0, first, rolled)
    xs = (x3 + (prev - x3) * mu_ref[...]).reshape(bb * tt, w)

    r = xs[:, 0:D_RWKV]
    k = xs[:, D_RWKV:2 * D_RWKV]
    v = xs[:, 2 * D_RWKV:3 * D_RWKV]
    wd = xs[:, OFF_WD:OFF_WD + LW_PAD]
    ad = xs[:, OFF_AD:OFF_AD + LA_PAD]
    gd = xs[:, OFF_GD:OFF_GD + LG_PAD]

    z = -(w0_ref[...] + _dot(jnp.tanh(wd), wup_ref[...]))
    softplus = jnp.maximum(z, 0.0) + jnp.log(1.0 + jnp.exp(-jnp.abs(z)))
    lw = -jnp.exp(-softplus - 0.5)
    asig = _sigmoid(a0_ref[...] + _dot(ad, aup_ref[...]))
    g = _dot(_sigmoid(gd), gup_ref[...])

    kk = k * kk_ref[...]
    same_head = _same_head_mask().astype(BF16)
    sq = kk * kk
    ssq = jnp.concatenate(
        [_dot_exact_rhs(sq[:, u * LANE:(u + 1) * LANE], same_head) for u in range(N_UNITS)], axis=1)
    kkn = kk / jnp.maximum(jnp.sqrt(ssq), 1e-12)
    k2 = k * (1.0 + (asig - 1.0) * ka_ref[...])
    a_vec = -kkn
    b_vec = kkn * asig

    if l_valid < l_total:
        t_glob = (ti * tt + _iota2((bb, tt, D_RWKV), 1)).reshape(bb * tt, D_RWKV)
        ok = t_glob < l_valid
        zero = jnp.zeros_like(k2)
        lw, k2, v, a_vec, b_vec = (jnp.where(ok, t, zero) for t in (lw, k2, v, a_vec, b_vec))

    r_o[...] = r
    k_o[...] = k2
    v_o[...] = v
    lw_o[...] = lw
    a_o[...] = a_vec
    b_o[...] = b_vec
    g_o[...] = g


def _rwkv_prep(proj, shift_pad, pp, bn, l_total, l_valid, bb, tt):
    nt = l_total // tt
    rows = bb * tt
    w = D_SHIFT_PAD
    row_spec = pl.BlockSpec((rows, w), lambda bi, ti: (bi * nt + ti, 0))
    p8_spec = pl.BlockSpec((bb * SUBLANE, w),
                           lambda bi, ti: (jnp.maximum((bi * nt + ti) * (tt // SUBLANE) - 1, 0), 0))
    vec = lambda n: pl.BlockSpec((1, n), lambda bi, ti: (0, 0))
    mat = lambda a, b: pl.BlockSpec((a, b), lambda bi, ti: (0, 0))
    out_spec = pl.BlockSpec((rows, D_RWKV), lambda bi, ti: (bi * nt + ti, 0))
    out_sds = jax.ShapeDtypeStruct((bn * l_total, D_RWKV), F32)
    kern = functools.partial(_rwkv_prep_kernel, bb=bb, tt=tt, l_valid=l_valid, l_total=l_total)
    return pl.pallas_call(
        kern,
        out_shape=[out_sds] * 7,
        grid=(bn // bb, nt),
        in_specs=[row_spec, p8_spec,
                  pl.BlockSpec((bb, 1, w), lambda bi, ti: (bi, 0, 0)),
                  vec(w), vec(D_RWKV), mat(LW_PAD, D_RWKV), vec(D_RWKV), mat(LA_PAD, D_RWKV),
                  mat(LG_PAD, D_RWKV), vec(D_RWKV), vec(D_RWKV)],
        out_specs=[out_spec] * 7,
        compiler_params=_cparams(("parallel", "arbitrary")),
        name="rwkv_prep",
    )(proj, proj, shift_pad, pp["mu"], pp["w0"], pp["w_up"], pp["a0"], pp["a_up"], pp["g_up"],
      pp["k_k"], pp["k_a"])


def _rwkv_scan_kernel(*refs, bb, ub, c, has_s0):
    if has_s0:
        (r_ref, k_ref, v_ref, lw_ref, a_ref, b_ref, g_ref, rk_ref, lnw_ref, lnb_ref, s0_ref,
         y_ref, so_ref, s_scr) = refs
    else:
        (r_ref, k_ref, v_ref, lw_ref, a_ref, b_ref, g_ref, rk_ref, lnw_ref, lnb_ref,
         y_ref, so_ref, s_scr) = refs
        s0_ref = None
    ci = pl.program_id(2)
    n_chunks = pl.num_programs(2)

    lane = _iota2((1, LANE), 1)
    m0 = (lane < RWKV_HEAD).astype(F32)
    m1 = 1.0 - m0
    bd_mask = _same_head_mask().astype(F32)

    c2 = 2 * c
    ri = _iota2((c2, c2), 0)
    cj = _iota2((c2, c2), 1)
    same_blk = (ri >= c) == (cj >= c)
    mask_s = jnp.where(same_blk, (ri > cj).astype(F32), 0.0)
    mask_i = jnp.where(same_blk, (ri >= cj).astype(F32), 0.0)
    eye = (ri == cj).astype(F32)
    n_sq = max((c - 1).bit_length() - 1, 0)

    head_avg = bd_mask.astype(BF16)

    chains = [(j, w) for j in range(bb) for w in range(ub)]
    seqs = range(len(chains))

    @pl.when(ci == 0)
    def _():
        for n, (j, w) in enumerate(chains):
            if has_s0:
                zero = jnp.zeros((RWKV_HEAD, RWKV_HEAD), F32)
                top = jnp.concatenate([s0_ref[j, 2 * w], zero], axis=1)
                bottom = jnp.concatenate([zero, s0_ref[j, 2 * w + 1]], axis=1)
                s_scr[n] = jnp.concatenate([top, bottom], axis=0)
            else:
                s_scr[n] = jnp.zeros((LANE, LANE), F32)

    stack2 = lambda lo, hi: jnp.concatenate([lo, hi], axis=0)
    unit = lambda ref, j, w: ref[j, :, w * LANE:(w + 1) * LANE]
    vec = lambda ref, w: ref[:, w * LANE:(w + 1) * LANE]
    r = [unit(r_ref, j, w) for j, w in chains]
    k = [unit(k_ref, j, w) for j, w in chains]
    v = [unit(v_ref, j, w) for j, w in chains]
    lw = [unit(lw_ref, j, w) for j, w in chains]
    a = [unit(a_ref, j, w) for j, w in chains]
    b = [unit(b_ref, j, w) for j, w in chains]
    cum = [_cumsum_time(x) for x in lw]
    clast = [x[c - 1:c, :] for x in cum]
    p_inv = [jnp.exp(-x) for x in cum]
    a2 = [stack2(a[j] * jnp.exp(cum[j] - lw[j]) * m0, a[j] * jnp.exp(cum[j] - lw[j]) * m1) for j in seqs]
    r2 = [stack2(r[j] * jnp.exp(cum[j]) * m0, r[j] * jnp.exp(cum[j]) * m1) for j in seqs]
    b2 = [stack2(b[j] * p_inv[j], b[j] * p_inv[j]) for j in seqs]
    k2 = [stack2(k[j] * p_inv[j], k[j] * p_inv[j]) for j in seqs]
    v2 = [stack2(v[j] * m0, v[j] * m1) for j in seqs]
    lab = [mask_s * _dot_nt(a2[j], b2[j]) for j in seqs]
    lak = [mask_s * _dot_nt(a2[j], k2[j]) for j in seqs]
    rb = [mask_i * _dot_nt(r2[j], b2[j]) for j in seqs]
    rkm = [mask_i * _dot_nt(r2[j], k2[j]) for j in seqs]
    tinv = [eye + x for x in lab]
    xp = lab
    for _ in range(n_sq):
        xp = [_dot(x, x) for x in xp]
        tinv = [tinv[j] + _dot(tinv[j], xp[j]) for j in seqs]
    lakv = [_dot(lak[j], v2[j]) for j in seqs]
    rkv = [_dot(rkm[j], v2[j]) for j in seqs]
    s0 = [s_scr[n] for n in seqs]
    ar_s0 = [_dot_nt(stack2(a2[n], r2[n]), s0[n]) for n in seqs]
    u2 = [_dot(tinv[n], ar_s0[n][:c2] + lakv[n]) for n in seqs]
    y2 = [ar_s0[n][c2:] + _dot(rb[n], u2[n]) + rkv[n] for n in seqs]
    y = [x[:c] + x[c:] for x in y2]
    u = [x[:c] + x[c:] for x in u2]
    p_last = [jnp.exp(clast[n] - cum[n]) for n in seqs]
    for n in seqs:
        s_scr[n] = s0[n] * jnp.exp(clast[n]) + bd_mask * _dot_tn(
            stack2(u[n], v[n]), stack2(b[n] * p_last[n], k[n] * p_last[n]))

    sums = [_dot(stack2(y[n], r[n] * k[n] * vec(rk_ref, w)), head_avg) for n, (j, w) in enumerate(chains)]
    dlt = [y[n] - sums[n][:c] * (1.0 / RWKV_HEAD) for n in seqs]
    var = [_dot(x * x, head_avg) * (1.0 / RWKV_HEAD) for x in dlt]
    bonus = [sums[n][c:] * v[n] for n in seqs]
    for n, (j, w) in enumerate(chains):
        yn = dlt[n] * lax.rsqrt(var[n] + GN_EPS) * vec(lnw_ref, w) + vec(lnb_ref, w)
        y_ref[j, :, w * LANE:(w + 1) * LANE] = (yn + bonus[n]) * unit(g_ref, j, w)

    @pl.when(ci == n_chunks - 1)
    def _():
        for n, (j, w) in enumerate(chains):
            s_fin = s_scr[n]
            so_ref[j, 2 * w] = s_fin[:RWKV_HEAD, :RWKV_HEAD]
            so_ref[j, 2 * w + 1] = pltpu.roll(s_fin, RWKV_HEAD, axis=1)[RWKV_HEAD:, :RWKV_HEAD]


def _rwkv_scan(streams, rk, lnw, lnb, s0, bn, l_total, bb, ub, c):
    nc = l_total // c
    has_s0 = s0 is not None
    blk = pl.BlockSpec((bb, c, ub * LANE), lambda bi, u, ci: (bi, ci, u))
    vec = pl.BlockSpec((1, ub * LANE), lambda bi, u, ci: (0, u))
    st = pl.BlockSpec((bb, 2 * ub, RWKV_HEAD, RWKV_HEAD), lambda bi, u, ci: (bi, u, 0, 0))
    in_specs = [blk] * 7 + [vec] * 3 + ([st] if has_s0 else [])
    args = [s.reshape(bn, l_total, D_RWKV) for s in streams] + [rk, lnw, lnb] + ([s0] if has_s0 else [])
    kern = functools.partial(_rwkv_scan_kernel, bb=bb, ub=ub, c=c, has_s0=has_s0)
    return pl.pallas_call(
        kern,
        out_shape=[jax.ShapeDtypeStruct((bn, l_total, D_RWKV), F32),
                   jax.ShapeDtypeStruct((bn, N_RWKV_HEADS, RWKV_HEAD, RWKV_HEAD), F32)],
        grid=(bn // bb, N_UNITS // ub, nc),
        in_specs=in_specs,
        out_specs=[blk, st],
        scratch_shapes=[pltpu.VMEM((bb * ub, LANE, LANE), F32)],
        compiler_params=_cparams(("parallel", "parallel", "arbitrary")),
        name="rwkv_scan",
    )(*args)


def _hgrn_kernel(*refs, bb, ub, c, has_s0, l_valid, l_total):
    if has_s0:
        q_ref, f_ref, i_ref, og_ref, lb_ref, nw_ref, s0_ref, y_ref, so_ref, s_scr = refs
    else:
        q_ref, f_ref, i_ref, og_ref, lb_ref, nw_ref, y_ref, so_ref, s_scr = refs
        s0_ref = None
    ci = pl.program_id(2)
    n_chunks = pl.num_programs(2)
    tri = (_iota2((c, c), 0) >= _iota2((c, c), 1)).astype(F32)
    mid = max(c // 2 - 1, 0)
    chains = [(j, w) for j in range(bb) for w in range(ub)]
    seqs = range(len(chains))
    unit = lambda ref, j, w: ref[j, :, w * LANE:(w + 1) * LANE]

    @pl.when(ci == 0)
    def _():
        for n, (j, w) in enumerate(chains):
            if has_s0:
                s_scr[n] = s0_ref[j, w]
            else:
                s_scr[n] = jnp.zeros((LANE, LANE), F32)

    def key_column(x):
        ones = jnp.ones((c, LANE), BF16)
        d = lambda p: lax.dot_general(p, ones, (((0,), (0,)), ((), ())), preferred_element_type=F32)
        h, m, l = _split3(x)
        return d(h) + d(m) + d(l)

    q = [unit(q_ref, j, w) * _sigmoid(unit(q_ref, j, w)) for j, w in chains]
    f = [lb_ref[:, w * LANE:(w + 1) * LANE] + (1.0 - lb_ref[:, w * LANE:(w + 1) * LANE])
         * _sigmoid(unit(f_ref, j, w)) for j, w in chains]
    logf = [jnp.log(x) for x in f]
    kf = [1.0 - x for x in f]
    v = [unit(i_ref, j, w) for j, w in chains]
    if l_valid < l_total:
        ok = (ci * c + _iota2((c, LANE), 0)) < l_valid
        logf = [jnp.where(ok, x, 0.0) for x in logf]
        kf = [jnp.where(ok, x, 0.0) for x in kf]
    cum = [_cumsum_time(x) for x in logf]
    cref = [x[mid:mid + 1, :] for x in cum]
    clast = [x[c - 1:c, :] for x in cum]
    amat = [tri * _dot_nt(q[j] * jnp.exp(cum[j] - cref[j]), kf[j] * jnp.exp(cref[j] - cum[j])) for j in seqs]
    st = [s_scr[j] for j in seqs]
    o = [_dot(q[j] * jnp.exp(cum[j]), st[j]) + _dot(amat[j], v[j]) for j in seqs]
    decay = [jnp.exp(key_column(x)) for x in logf]
    for j in seqs:
        s_scr[j] = st[j] * decay[j] + _dot_tn(kf[j] * jnp.exp(clast[j] - cum[j]), v[j])
    for n, (j, w) in enumerate(chains):
        on = o[n] * lax.rsqrt(jnp.mean(o[n] * o[n], axis=-1, keepdims=True) + HGRN_NORM_EPS) * nw_ref[...]
        og = unit(og_ref, j, w)
        y_ref[j, :, w * LANE:(w + 1) * LANE] = on * (og * _sigmoid(og))

    @pl.when(ci == n_chunks - 1)
    def _():
        for n, (j, w) in enumerate(chains):
            so_ref[j, w] = s_scr[n]


def _hgrn_scan(proj, lb, nw, s0, bn, l_total, l_valid, bb, ub, c):
    nc = l_total // c
    has_s0 = s0 is not None
    base = D_SHIFT_PAD // (ub * LANE)
    per = D_HGRN // (ub * LANE)
    proj3 = proj.reshape(bn, l_total, D_IN_PAD)

    def col(part):
        return pl.BlockSpec((bb, c, ub * LANE), lambda bi, u, ci: (bi, ci, base + part * per + u))

    st = pl.BlockSpec((bb, ub, LANE, LANE), lambda bi, u, ci: (bi, u, 0, 0))
    in_specs = [col(0), col(1), col(2), col(3),
                pl.BlockSpec((1, ub * LANE), lambda bi, u, ci: (0, u)),
                pl.BlockSpec((1, LANE), lambda bi, u, ci: (0, 0))] + ([st] if has_s0 else [])
    args = [proj3, proj3, proj3, proj3, lb, nw] + ([s0] if has_s0 else [])
    kern = functools.partial(_hgrn_kernel, bb=bb, ub=ub, c=c, has_s0=has_s0, l_valid=l_valid,
                             l_total=l_total)
    return pl.pallas_call(
        kern,
        out_shape=[jax.ShapeDtypeStruct((bn, l_total, D_HGRN), F32),
                   jax.ShapeDtypeStruct((bn, N_UNITS, LANE, LANE), F32)],
        grid=(bn // bb, N_UNITS // ub, nc),
        in_specs=in_specs,
        out_specs=[pl.BlockSpec((bb, c, ub * LANE), lambda bi, u, ci: (bi, ci, u)), st],
        scratch_shapes=[pltpu.VMEM((bb * ub, LANE, LANE), F32)],
        compiler_params=_cparams(("parallel", "parallel", "arbitrary")),
        name="hgrn_scan",
    )(*args)


def _outproj_kernel(x_ref, yr_ref, yh_ref, wo_ref, nf_ref, wr_ref, br_ref, cnt0_ref,
                    h_ref, xn_ref, idx_ref, gate_ref, rank_ref, cnt_ref, run_scr, *, l_valid, l_total):
    @pl.when(pl.program_id(0) == 0)
    def _():
        run_scr[...] = cnt0_ref[...]

    h = (x_ref[...] + _dot(yr_ref[...], wo_ref[0:D_RWKV, :]) + _dot(yh_ref[...], wo_ref[D_RWKV:, :]))
    h_ref[...] = h
    xn = _rmsnorm(h, nf_ref[...])
    xh = xn.astype(BF16)
    bits = lax.bitcast_convert_type(xh.astype(F32), U32)
    xn_ref[...] = (bits[:, D_MODEL // 2:] & jnp.uint32(0xFFFF0000)) | (bits[:, :D_MODEL // 2] >> 16)
    wr = wr_ref[...]
    wh = wr.astype(BF16)
    logits = (_dot(xh, wh) + _dot(xn - xh.astype(F32), wh) + _dot(xh, wr - wh.astype(F32))
              + br_ref[...])
    tm = logits.shape[0]
    lane = _iota2((tm, LANE), 1).astype(F32)
    neg = jnp.float32(-jnp.inf)
    work = jnp.where(lane < N_EXPERTS, logits, neg)
    idx_out = jnp.zeros((tm, LANE), I32)
    val_out = jnp.zeros((tm, LANE), F32)
    top0 = None
    picks = []
    for kk in range(TOP_K):
        m = jnp.max(work, axis=-1, keepdims=True)
        sel = jnp.min(jnp.where(work == m, lane, float(LANE)), axis=-1, keepdims=True)
        if kk == 0:
            top0 = m
        idx_out = jnp.where(lane == kk, sel.astype(I32), idx_out)
        val_out = jnp.where(lane == kk, jnp.exp(m - top0), val_out)
        picks.append(lane == sel)
        work = jnp.where(lane == sel, neg, work)
    idx_ref[...] = idx_out
    gate_ref[...] = val_out / jnp.sum(val_out, axis=-1, keepdims=True)

    chosen = sum(p.astype(F32) for p in picks)
    if l_valid < l_total:
        assert l_total & (l_total - 1) == 0 and tm % l_total == 0
        t_in_seq = _iota2((tm, LANE), 0) & (l_total - 1)
        chosen = jnp.where(t_in_seq < l_valid, chosen, 0.0)
    earlier = (_iota2((tm, tm), 0) > _iota2((tm, tm), 1)).astype(BF16)
    before = jnp.dot(earlier, chosen.astype(BF16), preferred_element_type=F32) + run_scr[...]
    rank_out = jnp.zeros((tm, LANE), F32)
    for kk in range(TOP_K):
        r_k = jnp.sum(jnp.where(picks[kk], before, 0.0), axis=-1, keepdims=True)
        rank_out = jnp.where(lane == kk, r_k, rank_out)
    rank_ref[...] = rank_out.astype(I32)
    run_scr[...] = run_scr[...] + jnp.sum(chosen, axis=0, keepdims=True)
    cnt_ref[...] = run_scr[...]


def _outproj_router(x, yr, yh, wo, nf, wr, br, cnt0, tm, l_valid, l_total):
    t = x.shape[0]
    row = lambda n: pl.BlockSpec((tm, n), lambda i: (i, 0))
    full = lambda a, b: pl.BlockSpec((a, b), lambda i: (0, 0), pipeline_mode=pl.Buffered(1))
    return pl.pallas_call(
        functools.partial(_outproj_kernel, l_valid=l_valid, l_total=l_total),
        out_shape=[jax.ShapeDtypeStruct((t, D_MODEL), F32), jax.ShapeDtypeStruct((t, D_MODEL // 2), U32),
                   jax.ShapeDtypeStruct((t, LANE), I32), jax.ShapeDtypeStruct((t, LANE), F32),
                   jax.ShapeDtypeStruct((t, LANE), I32), jax.ShapeDtypeStruct((1, LANE), F32)],
        grid=(t // tm,),
        in_specs=[row(D_MODEL), row(D_RWKV), row(D_HGRN), full(D_MODEL, D_MODEL), full(1, D_MODEL),
                  full(D_MODEL, LANE), full(1, LANE), full(1, LANE)],
        out_specs=[row(D_MODEL), row(D_MODEL // 2), row(LANE), row(LANE), row(LANE), full(1, LANE)],
        scratch_shapes=[pltpu.VMEM((1, LANE), F32)],
        compiler_params=_cparams(("arbitrary",)),
        name="outproj_router",
    )(x, yr, yh, wo, nf, wr, br, cnt0)


MOE_NF = D_EXPERT // MOE_TF
MOE_ISSUE = MOE_TM // MOE_NF
MOE_AHEAD_TILES = MOE_RT
MOE_AHEAD = MOE_AHEAD_TILES * MOE_TM
MOE_RING = 2 * MOE_RT
MOE_DUMP = (MOE_RING + MOE_RT + 2) * MOE_TM
HALF = D_MODEL // 2


def _ring_row_copy(x_hbm, tok, ring, sems, u):
    slot = (u // MOE_TM) % MOE_RING
    return pltpu.make_async_copy(x_hbm.at[pl.ds(tok, 1)], ring.at[slot, pl.ds(u % MOE_TM, 1)], sems.at[slot])


def _ring_tile_wait(x_hbm, ring, sems, tile):
    slot = tile % MOE_RING
    pltpu.make_async_copy(x_hbm.at[pl.ds(0, MOE_TM)], ring.at[slot], sems.at[slot]).wait()


def _scatter_row_copy(ybuf, ysc_hbm, sems, dst, u):
    slot = (u // MOE_TM) % MOE_RING
    return pltpu.make_async_copy(ybuf.at[slot, pl.ds(u % MOE_TM, 1)], ysc_hbm.at[pl.ds(dst, 1)], sems.at[slot])


def _scatter_tile_wait(ybuf, ysc_hbm, sems, tile):
    slot = tile % MOE_RING
    pltpu.make_async_copy(ybuf.at[slot], ysc_hbm.at[pl.ds(0, MOE_TM)], sems.at[slot]).wait()


def _moe_kernel(ie_ref, it0_ref, int_ref, nused_ref, rowsrc_ref, rowslot_ref, x_hbm, wg_ref, wu_ref, wd_ref,
                bg_ref, bu_ref, bd_ref, ysc_hbm, ring, xbuf, acc, ybuf, ring_sems, scat_sems,
                fill_sem, *, dump0):
    i = pl.program_id(0)
    f = pl.program_id(1)
    n_items = pl.num_programs(0)
    nt = int_ref[i]
    tile0 = it0_ref[i]
    n_used = nused_ref[0]

    @pl.when((i == 0) & (f == 0))
    def _():
        for t in range(MOE_RING):
            ybuf[t] = jnp.zeros((MOE_TM, HALF), U32)
        for s in range(MOE_DUMP // MOE_TM):
            cp = pltpu.make_async_copy(ybuf.at[0], ysc_hbm.at[pl.ds(dump0 + s * MOE_TM, MOE_TM)], fill_sem)
            cp.start()
            cp.wait()

        def head(r, carry):
            _ring_row_copy(x_hbm, rowsrc_ref[r], ring, ring_sems, r).start()
            return carry

        lax.fori_loop(0, MOE_AHEAD, head, 0)

    @pl.when(nt > 0)
    def _():
        @pl.when(f == 0)
        def _():
            def load(t, carry):
                _ring_tile_wait(x_hbm, ring, ring_sems, tile0 + t)
                w = ring[(tile0 + t) % MOE_RING]
                lo = lax.bitcast_convert_type(w << 16, F32).astype(BF16)
                hi = lax.bitcast_convert_type(w & jnp.uint32(0xFFFF0000), F32).astype(BF16)
                xbuf[t, :, 0:HALF] = lo
                xbuf[t, :, HALF:D_MODEL] = hi
                acc[t] = jnp.broadcast_to(bd_ref[0], (MOE_TM, D_MODEL))
                return carry

            lax.fori_loop(0, nt, load, 0)

        bg = bg_ref[0]
        bu = bu_ref[0]

        def issue(t):
            step = f * nt + t
            pos = tile0 * MOE_TM + step * MOE_ISSUE
            row = (step * MOE_ISSUE) % MOE_TM
            tile_s = tile0 + step // MOE_NF
            slot_s = tile_s % MOE_RING
            slot_g = (tile_s + MOE_AHEAD_TILES) % MOE_RING
            for q in range(MOE_ISSUE):
                pltpu.make_async_copy(x_hbm.at[pl.ds(rowsrc_ref[pos + MOE_AHEAD + q], 1)],
                                      ring.at[slot_g, pl.ds(row + q, 1)], ring_sems.at[slot_g]).start()
            for q in range(MOE_ISSUE):
                pltpu.make_async_copy(ybuf.at[slot_s, pl.ds(row + q, 1)],
                                      ysc_hbm.at[pl.ds(rowslot_ref[pos + q], 1)], scat_sems.at[slot_s]).start()

        def spans(todo):
            for t0, k in todo:
                for s in range(k):
                    issue(t0 + s)

            def hidden(t0, k):
                x = xbuf[pl.ds(t0, k)].reshape(k * MOE_TM, D_MODEL)
                g = jnp.minimum(jnp.dot(x, wg_ref[0].astype(BF16), preferred_element_type=F32) + bg, SWIGLU_LIMIT)
                u = jnp.clip(jnp.dot(x, wu_ref[0].astype(BF16), preferred_element_type=F32) + bu,
                             -SWIGLU_LIMIT, SWIGLU_LIMIT)
                return ((u + 1.0) * g * _sigmoid(SWIGLU_ALPHA * g)).astype(BF16)

            hid = hidden(*todo[0])
            for s, (t0, k) in enumerate(todo):
                nxt = hidden(*todo[s + 1]) if s + 1 < len(todo) else None
                down = jnp.dot(hid, wd_ref[0].astype(BF16), preferred_element_type=F32)
                acc[pl.ds(t0, k)] += down.reshape(k, MOE_TM, D_MODEL)
                hid = nxt

        def quad(p, carry):
            spans([(4 * p, 2), (4 * p + 2, 2)])
            return carry

        n_quads = nt // 4
        lax.fori_loop(0, n_quads, quad, 0)

        def pair(p, carry):
            spans([(4 * n_quads + 2 * p, 2)])
            return carry

        lax.fori_loop(0, (nt - 4 * n_quads) // 2, pair, 0)

        @pl.when(nt % 2 == 1)
        def _():
            spans([(nt - 1, 1)])

        @pl.when(f == MOE_NF - 1)
        def _():
            def pack(t, carry):
                vt = tile0 + t + MOE_AHEAD_TILES

                @pl.when(vt >= MOE_RING)
                def _():
                    _scatter_tile_wait(ybuf, ysc_hbm, scat_sems, vt)

                bits = lax.bitcast_convert_type(acc[t].astype(BF16).astype(F32), U32)
                ybuf[vt % MOE_RING] = (bits[:, HALF:] & jnp.uint32(0xFFFF0000)) | (bits[:, :HALF] >> 16)
                return carry

            lax.fori_loop(0, nt, pack, 0)

    @pl.when((i == n_items - 1) & (f == MOE_NF - 1))
    def _():
        for s in range(MOE_AHEAD_TILES):
            _ring_tile_wait(x_hbm, ring, ring_sems, n_used + s)

        def flush(u, carry):
            _scatter_row_copy(ybuf, ysc_hbm, scat_sems, rowslot_ref[u], u).start()
            return carry

        lax.fori_loop(n_used * MOE_TM, n_used * MOE_TM + MOE_AHEAD, flush, 0)

        def retire(vt, carry):
            _scatter_tile_wait(ybuf, ysc_hbm, scat_sems, vt)
            return carry

        lax.fori_loop(jnp.maximum(n_used - MOE_AHEAD_TILES, 0), n_used + MOE_AHEAD_TILES, retire, 0)


def _moe_experts(item_e, item_t0, item_nt, n_used, row_src, row_slot, x_packed, w_gu, b_gu, w_down, b_down,
                 n_items, n_tok):
    nf = MOE_NF
    dump0 = TOP_K * n_tok

    def fcol(i, f, int_):
        return jnp.where(int_[i] > 0, f, nf - 1)

    in_specs = [
        pl.BlockSpec(memory_space=pl.ANY),
        pl.BlockSpec((1, D_MODEL, MOE_TF), lambda i, f, ie, it0, int_, *_: (ie[i], 0, fcol(i, f, int_))),
        pl.BlockSpec((1, D_MODEL, MOE_TF), lambda i, f, ie, it0, int_, *_: (ie[i], 0, nf + fcol(i, f, int_))),
        pl.BlockSpec((1, MOE_TF, D_MODEL), lambda i, f, ie, it0, int_, *_: (ie[i], fcol(i, f, int_), 0)),
        pl.BlockSpec((1, 1, MOE_TF), lambda i, f, ie, it0, int_, *_: (ie[i], 0, fcol(i, f, int_))),
        pl.BlockSpec((1, 1, MOE_TF), lambda i, f, ie, it0, int_, *_: (ie[i], 0, nf + fcol(i, f, int_))),
        pl.BlockSpec((1, 1, D_MODEL), lambda i, f, ie, it0, int_, *_: (ie[i], 0, 0)),
    ]
    return pl.pallas_call(
        functools.partial(_moe_kernel, dump0=dump0),
        out_shape=jax.ShapeDtypeStruct((dump0 + MOE_DUMP, HALF), U32),
        grid_spec=pltpu.PrefetchScalarGridSpec(
            num_scalar_prefetch=6,
            grid=(n_items, nf),
            in_specs=in_specs,
            out_specs=pl.BlockSpec(memory_space=pl.ANY),
            scratch_shapes=[pltpu.VMEM((MOE_RING, MOE_TM, HALF), U32),
                            pltpu.VMEM((MOE_RT, MOE_TM, D_MODEL), BF16),
                            pltpu.VMEM((MOE_RT, MOE_TM, D_MODEL), F32),
                            pltpu.VMEM((MOE_RING, MOE_TM, HALF), U32),
                            pltpu.SemaphoreType.DMA((MOE_RING,)),
                            pltpu.SemaphoreType.DMA((MOE_RING,)),
                            pltpu.SemaphoreType.DMA(())]),
        compiler_params=_cparams(("arbitrary", "arbitrary")),
        name="moe_experts",
    )(item_e, item_t0, item_nt, n_used, row_src, row_slot, x_packed, w_gu, w_gu, w_down, b_gu, b_gu, b_down)


def _ple_kernel(h_ref, gate_ref, y0_ref, y1_ref, y2_ref, y3_ref, p_ref, np_ref, wg_ref, wp_ref, nfin_ref, o_ref):
    gates = gate_ref[...]
    lo = jnp.zeros((h_ref.shape[0], HALF), F32)
    hi = jnp.zeros((h_ref.shape[0], HALF), F32)
    for kk, y_ref in enumerate((y0_ref, y1_ref, y2_ref, y3_ref)):
        w = y_ref[...]
        g = gates[:, kk:kk + 1]
        lo = lo + g * lax.bitcast_convert_type(w << 16, F32)
        hi = hi + g * lax.bitcast_convert_type(w & jnp.uint32(0xFFFF0000), F32)
    h = h_ref[...] + jnp.concatenate([lo, hi], axis=1)
    gate = _sigmoid(_dot(_rmsnorm(h, np_ref[...]), wg_ref[...]))
    h = h + gate * _dot(p_ref[...], wp_ref[...])
    o_ref[...] = _rmsnorm(h, nfin_ref[...])


def _ple_final(h, gates, ysc, plane, tok0, p, n_ple, wg, wp, n_fin, tm):
    t = h.shape[0]
    row = lambda n: pl.BlockSpec((tm, n), lambda i: (i, 0))
    full = lambda a, b: pl.BlockSpec((a, b), lambda i: (0, 0), pipeline_mode=pl.Buffered(1))
    ysp = lambda kk: pl.BlockSpec((tm, HALF), lambda i: ((kk * plane + tok0) // tm + i, 0))
    return pl.pallas_call(
        _ple_kernel,
        out_shape=jax.ShapeDtypeStruct((t, D_MODEL), F32),
        grid=(t // tm,),
        in_specs=[row(D_MODEL), row(LANE), ysp(0), ysp(1), ysp(2), ysp(3), row(D_PLE), full(1, D_MODEL),
                  full(D_MODEL, D_MODEL), full(D_PLE, D_MODEL), full(1, D_MODEL)],
        out_specs=row(D_MODEL),
        compiler_params=_cparams(("parallel",)),
        name="ple_final",
    )(h, gates, ysc, ysc, ysc, ysc, p, n_ple, wg, wp, n_fin)


def _pad_shift_cols(a):
    def z(n):
        return jnp.zeros(a.shape[:-1] + (n,), a.dtype)
    c0 = 3 * D_RWKV
    c1 = c0 + LORA_W
    c2 = c1 + LORA_A
    return jnp.concatenate([a[..., :c0], a[..., c0:c1], z(LW_PAD - LORA_W), a[..., c1:c2], z(LA_PAD - LORA_A),
                            a[..., c2:], z(LG_PAD - LORA_G)], axis=-1)


def _pad_shift_rows(a):
    z = lambda n: jnp.zeros((n, a.shape[1]), a.dtype)
    c0 = 3 * D_RWKV
    c1 = c0 + LORA_W
    c2 = c1 + LORA_A
    return jnp.concatenate([a[:c0], a[c0:c1], z(LW_PAD - LORA_W), a[c1:c2], z(LA_PAD - LORA_A), a[c2:],
                            z(LG_PAD - LORA_G)], axis=0)


def _unpad_shift_cols(a):
    return jnp.concatenate([a[..., :OFF_WD], a[..., OFF_WD:OFF_WD + LORA_W], a[..., OFF_AD:OFF_AD + LORA_A],
                            a[..., OFF_GD:OFF_GD + LORA_G]], axis=-1)


def _pad_rows(a, n):
    return jnp.concatenate([a, jnp.zeros((n - a.shape[0],) + a.shape[1:], a.dtype)], axis=0)


def _route_tables_kernel(pos_ref, src0_hbm, slot0_hbm, src_ref, slot_ref, sems, *, n_tok):
    fill_src = pltpu.make_async_copy(src0_hbm, src_ref, sems.at[0])
    fill_slot = pltpu.make_async_copy(slot0_hbm, slot_ref, sems.at[1])
    fill_src.start()
    fill_slot.start()
    fill_src.wait()
    fill_slot.wait()

    def place(tok, carry):
        for kk in range(TOP_K):
            r = pos_ref[tok * TOP_K + kk]
            src_ref[r] = tok
            slot_ref[r + MOE_AHEAD] = tok + kk * n_tok
        return carry

    lax.fori_loop(0, n_tok, place, 0, unroll=2)


def _route_tables(pos, n_tok, n_stream):
    u = jnp.arange(n_stream, dtype=I32)
    src0 = jnp.full((n_stream,), n_tok, I32)
    slot0 = TOP_K * n_tok + u % MOE_DUMP
    smem = pl.BlockSpec(memory_space=pltpu.SMEM)
    hbm = pl.BlockSpec(memory_space=pl.ANY)
    return pl.pallas_call(
        functools.partial(_route_tables_kernel, n_tok=n_tok),
        out_shape=[jax.ShapeDtypeStruct((n_stream,), I32)] * 2,
        in_specs=[smem, hbm, hbm],
        out_specs=[smem, smem],
        scratch_shapes=[pltpu.SemaphoreType.DMA((2,))],
        name="route_tables",
    )(pos, src0, slot0)


def _routing(idx, rank, counts, n_rows_cap, n_items_cap):
    t = idx.shape[0]
    na = t * TOP_K
    flat_e = idx.reshape(na)
    rank = rank.reshape(na)
    ptiles = (counts + MOE_TM - 1) // MOE_TM
    pend = jnp.cumsum(ptiles)
    pstart = pend - ptiles
    pos = (pstart[flat_e] * MOE_TM + rank).astype(I32)
    row_src, row_slot = _route_tables(pos, t, n_rows_cap + 2 * MOE_AHEAD)
    items_per_e = (ptiles + MOE_RT - 1) // MOE_RT
    iend = jnp.cumsum(items_per_e)
    istart = iend - items_per_e
    ii = jnp.arange(n_items_cap, dtype=I32)
    e_of = jnp.minimum(jnp.searchsorted(iend, ii, side="right"), N_EXPERTS - 1).astype(I32)
    jj = ii - istart[e_of]
    used = ii < iend[-1]
    item_nt = jnp.where(used, jnp.clip(ptiles[e_of] - jj * MOE_RT, 0, MOE_RT), 0).astype(I32)
    item_t0 = jnp.where(used, pstart[e_of] + jj * MOE_RT, 0).astype(I32)
    last_e = e_of[jnp.maximum(iend[-1] - 1, 0)]
    item_e = jnp.where(used, e_of, last_e).astype(I32)
    n_used = jnp.stack([pend[-1], iend[-1]]).astype(I32)
    return row_src, row_slot, n_used, item_e, item_t0, item_nt


def kernel(x_prompt, x_sample, p_prompt, p_sample, state_rwkv_shift, state_rwkv, state_hgrn, norm_mix, w_in,
           mu_shift, w0, w_up, a0, a_up, g_up, k_k, k_a, r_k, lnx_w, lnx_b, hgrn_lb, hgrn_norm, w_out,
           norm_ffn, w_router, b_router, w_gu, b_gu, w_down, b_down, norm_ple, w_ple_gate, w_ple_proj,
           norm_final):
    depth = w_in.shape[0]
    assert depth == 1
    li = 0
    bp, lp = x_prompt.shape[0], x_prompt.shape[1]
    bs, ls = x_sample.shape[0], x_sample.shape[1]
    ls_pad = SUBLANE
    tp = bp * lp

    w_t = jnp.transpose(w_in[li])
    w_in_p = jnp.concatenate([_pad_shift_rows(w_t[:D_SHIFT]), w_t[D_SHIFT:]], axis=0).astype(BF16)
    row = lambda a: a.reshape(1, -1).astype(F32)
    pp = {
        "mu": row(_pad_shift_cols(mu_shift[li])),
        "w0": row(w0[li]), "a0": row(a0[li]), "k_k": row(k_k[li]), "k_a": row(k_a[li]),
        "w_up": _pad_rows(w_up[li], LW_PAD).astype(BF16),
        "a_up": _pad_rows(a_up[li], LA_PAD).astype(BF16),
        "g_up": _pad_rows(g_up[li], LG_PAD).astype(BF16),
    }
    rk = row(r_k[li])
    lnw = row(lnx_w[li])
    lnb = row(lnx_b[li])
    lower = jax.nn.softmax(hgrn_lb.astype(F32), axis=0)
    lb = row(jnp.cumsum(lower, axis=0)[li])
    nw = row(hgrn_norm[li])
    wo = w_out[li].astype(BF16)
    nf = row(norm_ffn[li])
    wr = jnp.concatenate([w_router[li], jnp.zeros((D_MODEL, LANE - N_EXPERTS), F32)], axis=1)
    br = jnp.concatenate([b_router[li], jnp.zeros((LANE - N_EXPERTS,), F32)]).reshape(1, LANE)
    n_ple = row(norm_ple[li])
    wpg = w_ple_gate[li].astype(BF16)
    wpp = w_ple_proj[li].astype(BF16)
    n_fin = row(norm_final)
    g_mix = row(norm_mix[li])

    def mixer(x2d, shift_prev, s_rwkv, s_hgrn, cnt0, bn, l_total, l_valid, tm_in, bb_prep, tt, bb_scan, ub_scan, c,
              tm_out):
        proj = _inproj(x2d, g_mix, w_in_p, tm_in, 1536)
        streams = _rwkv_prep(proj, shift_prev, pp, bn, l_total, l_valid, bb_prep, tt)
        yr, s_rwkv_new = _rwkv_scan(streams, rk, lnw, lnb, s_rwkv, bn, l_total, bb_scan, ub_scan, c)
        yh, s_hgrn_new = _hgrn_scan(proj, lb, nw, s_hgrn, bn, l_total, l_valid, bb_scan, ub_scan, c)
        h1, xn2, idx, gates, rank, cnt = _outproj_router(
            x2d, yr.reshape(bn * l_total, D_RWKV), yh.reshape(bn * l_total, D_HGRN), wo, nf, wr, br, cnt0, tm_out,
            l_valid, l_total)
        new_shift = _unpad_shift_cols(proj.reshape(bn, l_total, D_IN_PAD)[:, l_valid - 1, :D_SHIFT_PAD])
        return h1, xn2, idx, gates, rank, cnt, new_shift, s_rwkv_new, s_hgrn_new

    xp2 = x_prompt.reshape(tp, D_MODEL)
    zero_shift = jnp.zeros((bp, 1, D_SHIFT_PAD), F32)
    h1p, xn2p, idxp, gatesp, rankp, cntp, shift_p, rwkv_p, hgrn_p = mixer(
        xp2, zero_shift, None, None, jnp.zeros((1, LANE), F32), bp, lp, lp, 1024, 1, 256, bp, 4, 64, 512)

    xs_pad = jnp.concatenate([x_sample, jnp.zeros((bs, ls_pad - ls, D_MODEL), F32)], axis=1)
    xs2 = xs_pad.reshape(bs * ls_pad, D_MODEL)
    shift_s0 = _pad_shift_cols(state_rwkv_shift[li]).reshape(bs, 1, D_SHIFT_PAD)
    h1s, xn2s, idxs, gatess, ranks, cnt_all, shift_s, rwkv_s, hgrn_s = mixer(
        xs2, shift_s0, state_rwkv[li], state_hgrn[li], cntp, bs, ls_pad, ls, 1024, 16, ls_pad, 16, 4, ls_pad, 512)

    def compact(a):
        return a.reshape(bs, ls_pad, a.shape[-1])[:, :ls].reshape(bs * ls, a.shape[-1])

    h1s, xn2s, idxs, gatess, ranks = compact(h1s), compact(xn2s), compact(idxs), compact(gatess), compact(ranks)
    ts = bs * ls

    t_all = tp + ts
    idx_all = jnp.concatenate([idxp[:, :TOP_K], idxs[:, :TOP_K]], axis=0)
    rank_all = jnp.concatenate([rankp[:, :TOP_K], ranks[:, :TOP_K]], axis=0)
    counts = cnt_all[0, :N_EXPERTS].astype(I32)
    n_tiles_cap = -(-(t_all * TOP_K) // MOE_TM) + N_EXPERTS
    n_rows_cap = n_tiles_cap * MOE_TM
    n_items_cap = N_EXPERTS + n_tiles_cap // MOE_RT
    row_src, row_slot, n_used, item_e, item_t0, item_nt = _routing(idx_all, rank_all, counts, n_rows_cap,
                                                                   n_items_cap)
    x_packed = jnp.concatenate([xn2p, xn2s, jnp.zeros((SUBLANE, HALF), U32)], axis=0)
    ysc = _moe_experts(item_e, item_t0, item_nt, n_used, row_src, row_slot, x_packed, w_gu[li],
                       b_gu[li].reshape(N_EXPERTS, 1, -1), w_down[li], b_down[li].reshape(N_EXPERTS, 1, -1),
                       n_used[1], t_all)

    tm_fin = 512
    assert tp % tm_fin == 0 and ts % tm_fin == 0
    y_p = _ple_final(h1p, gatesp, ysc, t_all, 0, p_prompt[li].reshape(tp, D_PLE), n_ple, wpg, wpp, n_fin, tm_fin)
    y_s = _ple_final(h1s, gatess, ysc, t_all, tp, p_sample[li].reshape(ts, D_PLE), n_ple, wpg, wpp, n_fin, tm_fin)

    return (y_p.reshape(bp, lp, D_MODEL), y_s.reshape(bs, ls, D_MODEL),
            shift_p[None], rwkv_p[None], hgrn_p[None],
            shift_s[None], rwkv_s[None], hgrn_s[None])
```

```python
import functools

import jax
import jax.numpy as jnp
from jax import lax
from jax.experimental import pallas as pl
from jax.experimental.pallas import tpu as pltpu

F32 = jnp.float32
BF16 = jnp.bfloat16
I32 = jnp.int32
U32 = jnp.uint32

D_MODEL = 2048
D_RWKV = 1024
D_HGRN = 1024
RWKV_HEAD = 64
N_RWKV_HEADS = 16
HGRN_HEAD = 128
N_HGRN_HEADS = 8
LORA_W = 64
LORA_A = 64
LORA_G = 160
D_SHIFT = 3 * D_RWKV + LORA_W + LORA_A + LORA_G
N_EXPERTS = 32
TOP_K = 4
D_EXPERT = 2048
SWIGLU_LIMIT = 7.0
SWIGLU_ALPHA = 1.702
D_PLE = 256
RMS_EPS = 1e-6
GN_EPS = 64e-5
HGRN_NORM_EPS = 1e-5

LANE = 128
SUBLANE = 8
N_UNITS = 8

LW_PAD = LANE
LA_PAD = LANE
LG_PAD = 2 * LANE
OFF_WD = 3 * D_RWKV
OFF_AD = OFF_WD + LW_PAD
OFF_GD = OFF_AD + LA_PAD
D_SHIFT_PAD = OFF_GD + LG_PAD
D_IN_PAD = D_SHIFT_PAD + 4 * D_HGRN

MOE_TM = 128
MOE_RT = 10
MOE_TF = 256
VMEM_LIMIT = 56 * 1024 * 1024


def _cparams(sem, vmem=VMEM_LIMIT):
    return pltpu.CompilerParams(dimension_semantics=sem, vmem_limit_bytes=vmem)


def _rmsnorm(x, g):
    return x * lax.rsqrt(jnp.mean(x * x, axis=-1, keepdims=True) + RMS_EPS) * g


def _dot(a, b):
    return jnp.dot(a.astype(BF16), b.astype(BF16), preferred_element_type=F32)


def _dot_nt(a, b):
    return lax.dot_general(a.astype(BF16), b.astype(BF16), (((1,), (1,)), ((), ())),
                           preferred_element_type=F32)


def _dot_tn(a, b):
    return lax.dot_general(a.astype(BF16), b.astype(BF16), (((0,), (0,)), ((), ())),
                           preferred_element_type=F32)


def _split3(x):
    h = x.astype(BF16)
    r = x - h.astype(F32)
    m = r.astype(BF16)
    l = (r - m.astype(F32)).astype(BF16)
    return h, m, l


def _dot_exact_rhs(a, b_bf16):
    h, m, l = _split3(a)
    d = functools.partial(jnp.dot, preferred_element_type=F32)
    return d(h, b_bf16) + d(m, b_bf16) + d(l, b_bf16)


def _dot_exact_lhs(a_bf16, b):
    n = b.shape[1]
    parts = jnp.dot(a_bf16, jnp.concatenate(_split3(b), axis=1), preferred_element_type=F32)
    return parts[:, :n] + parts[:, n:2 * n] + parts[:, 2 * n:]


def _iota2(shape, dim):
    return lax.broadcasted_iota(I32, shape, dim)


def _cumsum_time(x):
    c = x.shape[0]
    tri = (_iota2((c, c), 0) >= _iota2((c, c), 1)).astype(BF16)
    return _dot_exact_lhs(tri, x)


def _same_head_mask():
    return (_iota2((LANE, LANE), 0) >= RWKV_HEAD) == (_iota2((LANE, LANE), 1) >= RWKV_HEAD)


def _sigmoid(x):
    return 1.0 / (1.0 + jnp.exp(-x))


def _inproj_kernel(x_ref, g_ref, w_ref, o_ref, xn_ref):
    @pl.when(pl.program_id(1) == 0)
    def _():
        xn_ref[...] = _rmsnorm(x_ref[...], g_ref[...]).astype(BF16)

    o_ref[...] = lax.dot_general(xn_ref[...], w_ref[...], (((1,), (1,)), ((), ())), preferred_element_type=F32)


def _inproj(x, g, w, tm, tn):
    t, d = x.shape
    n = w.shape[0]
    return pl.pallas_call(
        _inproj_kernel,
        out_shape=jax.ShapeDtypeStruct((t, n), F32),
        grid=(t // tm, n // tn),
        in_specs=[pl.BlockSpec((tm, d), lambda i, j: (i, 0)),
                  pl.BlockSpec((1, d), lambda i, j: (0, 0)),
                  pl.BlockSpec((tn, d), lambda i, j: (j, 0))],
        out_specs=pl.BlockSpec((tm, tn), lambda i, j: (i, j)),
        scratch_shapes=[pltpu.VMEM((tm, d), BF16)],
        compiler_params=_cparams(("parallel", "arbitrary")),
        name="inproj",
    )(x, g, w)


def _rwkv_prep_kernel(x_ref, p8_ref, sh_ref, mu_ref, w0_ref, wup_ref, a0_ref, aup_ref, gup_ref,
                      kk_ref, ka_ref,
                      r_o, k_o, v_o, lw_o, a_o, b_o, g_o, *, bb, tt, l_valid, l_total):
    ti = pl.program_id(1)
    w = D_SHIFT_PAD
    x3 = x_ref[...].reshape(bb, tt, w)
    rolled = pltpu.roll(x3, 1, axis=1)
    prev_tail = p8_ref[...].reshape(bb, SUBLANE, w)[:, SUBLANE - 1:SUBLANE, :]
    first = jnp.where(ti == 0, sh_ref[...], prev_tail)
    t_in = _iota2((bb, tt, w), 1)
    prev = jnp.where(t_in == 0, first, rolled)
    xs = (x3 + (prev - x3) * mu_ref[...]).reshape(bb * tt, w)

    r = xs[:, 0:D_RWKV]
    k = xs[:, D_RWKV:2 * D_RWKV]
    v = xs[:, 2 * D_RWKV:3 * D_RWKV]
    wd = xs[:, OFF_WD:OFF_WD + LW_PAD]
    ad = xs[:, OFF_AD:OFF_AD + LA_PAD]
    gd = xs[:, OFF_GD:OFF_GD + LG_PAD]

    z = -(w0_ref[...] + _dot(jnp.tanh(wd), wup_ref[...]))
    softplus = jnp.maximum(z, 0.0) + jnp.log(1.0 + jnp.exp(-jnp.abs(z)))
    lw = -jnp.exp(-softplus - 0.5)
    asig = _sigmoid(a0_ref[...] + _dot(ad, aup_ref[...]))
    g = _dot(_sigmoid(gd), gup_ref[...])

    kk = k * kk_ref[...]
    same_head = _same_head_mask().astype(BF16)
    sq = kk * kk
    ssq = jnp.concatenate(
        [_dot_exact_rhs(sq[:, u * LANE:(u + 1) * LANE], same_head) for u in range(N_UNITS)], axis=1)
    kkn = kk / jnp.maximum(jnp.sqrt(ssq), 1e-12)
    k2 = k * (1.0 + (asig - 1.0) * ka_ref[...])
    a_vec = -kkn
    b_vec = kkn * asig

    if l_valid < l_total:
        t_glob = (ti * tt + _iota2((bb, tt, D_RWKV), 1)).reshape(bb * tt, D_RWKV)
        ok = t_glob < l_valid
        zero = jnp.zeros_like(k2)
        lw, k2, v, a_vec, b_vec = (jnp.where(ok, t, zero) for t in (lw, k2, v, a_vec, b_vec))

    r_o[...] = r
    k_o[...] = k2
    v_o[...] = v
    lw_o[...] = lw
    a_o[...] = a_vec
    b_o[...] = b_vec
    g_o[...] = g


def _rwkv_prep(proj, shift_pad, pp, bn, l_total, l_valid, bb, tt):
    nt = l_total // tt
    rows = bb * tt
    w = D_SHIFT_PAD
    row_spec = pl.BlockSpec((rows, w), lambda bi, ti: (bi * nt + ti, 0))
    p8_spec = pl.BlockSpec((bb * SUBLANE, w),
                           lambda bi, ti: (jnp.maximum((bi * nt + ti) * (tt // SUBLANE) - 1, 0), 0))
    vec = lambda n: pl.BlockSpec((1, n), lambda bi, ti: (0, 0))
    mat = lambda a, b: pl.BlockSpec((a, b), lambda bi, ti: (0, 0))
    out_spec = pl.BlockSpec((rows, D_RWKV), lambda bi, ti: (bi * nt + ti, 0))
    out_sds = jax.ShapeDtypeStruct((bn * l_total, D_RWKV), F32)
    kern = functools.partial(_rwkv_prep_kernel, bb=bb, tt=tt, l_valid=l_valid, l_total=l_total)
    return pl.pallas_call(
        kern,
        out_shape=[out_sds] * 7,
        grid=(bn // bb, nt),
        in_specs=[row_spec, p8_spec,
                  pl.BlockSpec((bb, 1, w), lambda bi, ti: (bi, 0, 0)),
                  vec(w), vec(D_RWKV), mat(LW_PAD, D_RWKV), vec(D_RWKV), mat(LA_PAD, D_RWKV),
                  mat(LG_PAD, D_RWKV), vec(D_RWKV), vec(D_RWKV)],
        out_specs=[out_spec] * 7,
        compiler_params=_cparams(("parallel", "arbitrary")),
        name="rwkv_prep",
    )(proj, proj, shift_pad, pp["mu"], pp["w0"], pp["w_up"], pp["a0"], pp["a_up"], pp["g_up"],
      pp["k_k"], pp["k_a"])


def _rwkv_scan_kernel(*refs, bb, ub, c, has_s0):
    if has_s0:
        (r_ref, k_ref, v_ref, lw_ref, a_ref, b_ref, g_ref, rk_ref, lnw_ref, lnb_ref, s0_ref,
         y_ref, so_ref, s_scr) = refs
    else:
        (r_ref, k_ref, v_ref, lw_ref, a_ref, b_ref, g_ref, rk_ref, lnw_ref, lnb_ref,
         y_ref, so_ref, s_scr) = refs
        s0_ref = None
    ci = pl.program_id(2)
    n_chunks = pl.num_programs(2)

    lane = _iota2((1, LANE), 1)
    m0 = (lane < RWKV_HEAD).astype(F32)
    m1 = 1.0 - m0
    bd_mask = _same_head_mask().astype(F32)

    c2 = 2 * c
    ri = _iota2((c2, c2), 0)
    cj = _iota2((c2, c2), 1)
    same_blk = (ri >= c) == (cj >= c)
    mask_s = jnp.where(same_blk, (ri > cj).astype(F32), 0.0)
    mask_i = jnp.where(same_blk, (ri >= cj).astype(F32), 0.0)
    eye = (ri == cj).astype(F32)
    n_sq = max((c - 1).bit_length() - 1, 0)

    head_avg = bd_mask.astype(BF16)

    chains = [(j, w) for j in range(bb) for w in range(ub)]
    seqs = range(len(chains))

    @pl.when(ci == 0)
    def _():
        for n, (j, w) in enumerate(chains):
            if has_s0:
                zero = jnp.zeros((RWKV_HEAD, RWKV_HEAD), F32)
                top = jnp.concatenate([s0_ref[j, 2 * w], zero], axis=1)
                bottom = jnp.concatenate([zero, s0_ref[j, 2 * w + 1]], axis=1)
                s_scr[n] = jnp.concatenate([top, bottom], axis=0)
            else:
                s_scr[n] = jnp.zeros((LANE, LANE), F32)

    stack2 = lambda lo, hi: jnp.concatenate([lo, hi], axis=0)
    unit = lambda ref, j, w: ref[j, :, w * LANE:(w + 1) * LANE]
    vec = lambda ref, w: ref[:, w * LANE:(w + 1) * LANE]
    r = [unit(r_ref, j, w) for j, w in chains]
    k = [unit(k_ref, j, w) for j, w in chains]
    v = [unit(v_ref, j, w) for j, w in chains]
    lw = [unit(lw_ref, j, w) for j, w in chains]
    a = [unit(a_ref, j, w) for j, w in chains]
    b = [unit(b_ref, j, w) for j, w in chains]
    cum = [_cumsum_time(x) for x in lw]
    clast = [x[c - 1:c, :] for x in cum]
    p_inv = [jnp.exp(-x) for x in cum]
    a2 = [stack2(a[j] * jnp.exp(cum[j] - lw[j]) * m0, a[j] * jnp.exp(cum[j] - lw[j]) * m1) for j in seqs]
    r2 = [stack2(r[j] * jnp.exp(cum[j]) * m0, r[j] * jnp.exp(cum[j]) * m1) for j in seqs]
    b2 = [stack2(b[j] * p_inv[j], b[j] * p_inv[j]) for j in seqs]
    k2 = [stack2(k[j] * p_inv[j], k[j] * p_inv[j]) for j in seqs]
    v2 = [stack2(v[j] * m0, v[j] * m1) for j in seqs]
    ar2 = [stack2(a2[j], r2[j]) for j in seqs]
    ar_b = [_dot_nt(ar2[j], b2[j]) for j in seqs]
    ar_k = [_dot_nt(ar2[j], k2[j]) for j in seqs]
    lab = [mask_s * x[:c2] for x in ar_b]
    lak = [mask_s * x[:c2] for x in ar_k]
    rb = [mask_i * x[c2:] for x in ar_b]
    rkm = [mask_i * x[c2:] for x in ar_k]
    tinv = [eye + x for x in lab]
    xp = lab
    for _ in range(n_sq):
        xp = [_dot(x, x) for x in xp]
        tinv = [tinv[j] + _dot(tinv[j], xp[j]) for j in seqs]
    lr_v = [_dot(stack2(lak[j], rkm[j]), v2[j]) for j in seqs]
    s0 = [s_scr[n] for n in seqs]
    ar_s0 = [_dot_nt(ar2[n], s0[n]) for n in seqs]
    u2 = [_dot(tinv[n], ar_s0[n][:c2] + lr_v[n][:c2]) for n in seqs]
    y2 = [ar_s0[n][c2:] + _dot(rb[n], u2[n]) + lr_v[n][c2:] for n in seqs]
    y = [x[:c] + x[c:] for x in y2]
    u = [x[:c] + x[c:] for x in u2]
    p_last = [jnp.exp(clast[n] - cum[n]) for n in seqs]
    for n in seqs:
        s_scr[n] = s0[n] * jnp.exp(clast[n]) + bd_mask * _dot_tn(
            stack2(u[n], v[n]), stack2(b[n] * p_last[n], k[n] * p_last[n]))

    sums = [_dot(stack2(y[n], r[n] * k[n] * vec(rk_ref, w)), head_avg) for n, (j, w) in enumerate(chains)]
    dlt = [y[n] - sums[n][:c] * (1.0 / RWKV_HEAD) for n in seqs]
    var = [_dot(x * x, head_avg) * (1.0 / RWKV_HEAD) for x in dlt]
    bonus = [sums[n][c:] * v[n] for n in seqs]
    for n, (j, w) in enumerate(chains):
        yn = dlt[n] * lax.rsqrt(var[n] + GN_EPS) * vec(lnw_ref, w) + vec(lnb_ref, w)
        y_ref[j, :, w * LANE:(w + 1) * LANE] = (yn + bonus[n]) * unit(g_ref, j, w)

    @pl.when(ci == n_chunks - 1)
    def _():
        for n, (j, w) in enumerate(chains):
            s_fin = s_scr[n]
            so_ref[j, 2 * w] = s_fin[:RWKV_HEAD, :RWKV_HEAD]
            so_ref[j, 2 * w + 1] = pltpu.roll(s_fin, RWKV_HEAD, axis=1)[RWKV_HEAD:, :RWKV_HEAD]


def _rwkv_scan(streams, rk, lnw, lnb, s0, bn, l_total, bb, ub, c):
    nc = l_total // c
    has_s0 = s0 is not None
    blk = pl.BlockSpec((bb, c, ub * LANE), lambda bi, u, ci: (bi, ci, u))
    vec = pl.BlockSpec((1, ub * LANE), lambda bi, u, ci: (0, u))
    st = pl.BlockSpec((bb, 2 * ub, RWKV_HEAD, RWKV_HEAD), lambda bi, u, ci: (bi, u, 0, 0))
    in_specs = [blk] * 7 + [vec] * 3 + ([st] if has_s0 else [])
    args = [s.reshape(bn, l_total, D_RWKV) for s in streams] + [rk, lnw, lnb] + ([s0] if has_s0 else [])
    kern = functools.partial(_rwkv_scan_kernel, bb=bb, ub=ub, c=c, has_s0=has_s0)
    return pl.pallas_call(
        kern,
        out_shape=[jax.ShapeDtypeStruct((bn, l_total, D_RWKV), F32),
                   jax.ShapeDtypeStruct((bn, N_RWKV_HEADS, RWKV_HEAD, RWKV_HEAD), F32)],
        grid=(bn // bb, N_UNITS // ub, nc),
        in_specs=in_specs,
        out_specs=[blk, st],
        scratch_shapes=[pltpu.VMEM((bb * ub, LANE, LANE), F32)],
        compiler_params=_cparams(("parallel", "parallel", "arbitrary")),
        name="rwkv_scan",
    )(*args)


def _hgrn_kernel(*refs, bb, ub, c, has_s0, l_valid, l_total):
    if has_s0:
        q_ref, f_ref, i_ref, og_ref, lb_ref, nw_ref, s0_ref, y_ref, so_ref, s_scr = refs
    else:
        q_ref, f_ref, i_ref, og_ref, lb_ref, nw_ref, y_ref, so_ref, s_scr = refs
        s0_ref = None
    ci = pl.program_id(2)
    n_chunks = pl.num_programs(2)
    tri = (_iota2((c, c), 0) >= _iota2((c, c), 1)).astype(F32)
    mid = max(c // 2 - 1, 0)
    chains = [(j, w) for j in range(bb) for w in range(ub)]
    seqs = range(len(chains))
    unit = lambda ref, j, w: ref[j, :, w * LANE:(w + 1) * LANE]

    @pl.when(ci == 0)
    def _():
        for n, (j, w) in enumerate(chains):
            if has_s0:
                s_scr[n] = s0_ref[j, w]
            else:
                s_scr[n] = jnp.zeros((LANE, LANE), F32)

    def key_column(x):
        ones = jnp.ones((3 * c, LANE), BF16)
        parts = jnp.concatenate([p.astype(F32) for p in _split3(x)], axis=0).astype(BF16)
        return lax.dot_general(parts, ones, (((0,), (0,)), ((), ())), preferred_element_type=F32)

    q = [unit(q_ref, j, w) * _sigmoid(unit(q_ref, j, w)) for j, w in chains]
    f = [lb_ref[:, w * LANE:(w + 1) * LANE] + (1.0 - lb_ref[:, w * LANE:(w + 1) * LANE])
         * _sigmoid(unit(f_ref, j, w)) for j, w in chains]
    logf = [jnp.log(x) for x in f]
    kf = [1.0 - x for x in f]
    v = [unit(i_ref, j, w) for j, w in chains]
    if l_valid < l_total:
        ok = (ci * c + _iota2((c, LANE), 0)) < l_valid
        logf = [jnp.where(ok, x, 0.0) for x in logf]
        kf = [jnp.where(ok, x, 0.0) for x in kf]
    cum = [_cumsum_time(x) for x in logf]
    cref = [x[mid:mid + 1, :] for x in cum]
    clast = [x[c - 1:c, :] for x in cum]
    amat = [tri * _dot_nt(q[j] * jnp.exp(cum[j] - cref[j]), kf[j] * jnp.exp(cref[j] - cum[j])) for j in seqs]
    st = [s_scr[j] for j in seqs]
    o = [_dot(q[j] * jnp.exp(cum[j]), st[j]) + _dot(amat[j], v[j]) for j in seqs]
    decay = [jnp.exp(key_column(x)) for x in logf]
    for j in seqs:
        s_scr[j] = st[j] * decay[j] + _dot_tn(kf[j] * jnp.exp(clast[j] - cum[j]), v[j])
    for n, (j, w) in enumerate(chains):
        on = o[n] * lax.rsqrt(jnp.mean(o[n] * o[n], axis=-1, keepdims=True) + HGRN_NORM_EPS) * nw_ref[...]
        og = unit(og_ref, j, w)
        y_ref[j, :, w * LANE:(w + 1) * LANE] = on * (og * _sigmoid(og))

    @pl.when(ci == n_chunks - 1)
    def _():
        for n, (j, w) in enumerate(chains):
            so_ref[j, w] = s_scr[n]


def _hgrn_scan(proj, lb, nw, s0, bn, l_total, l_valid, bb, ub, c):
    nc = l_total // c
    has_s0 = s0 is not None
    base = D_SHIFT_PAD // (ub * LANE)
    per = D_HGRN // (ub * LANE)
    proj3 = proj.reshape(bn, l_total, D_IN_PAD)

    def col(part):
        return pl.BlockSpec((bb, c, ub * LANE), lambda bi, u, ci: (bi, ci, base + part * per + u))

    st = pl.BlockSpec((bb, ub, LANE, LANE), lambda bi, u, ci: (bi, u, 0, 0))
    in_specs = [col(0), col(1), col(2), col(3),
                pl.BlockSpec((1, ub * LANE), lambda bi, u, ci: (0, u)),
                pl.BlockSpec((1, LANE), lambda bi, u, ci: (0, 0))] + ([st] if has_s0 else [])
    args = [proj3, proj3, proj3, proj3, lb, nw] + ([s0] if has_s0 else [])
    kern = functools.partial(_hgrn_kernel, bb=bb, ub=ub, c=c, has_s0=has_s0, l_valid=l_valid,
                             l_total=l_total)
    return pl.pallas_call(
        kern,
        out_shape=[jax.ShapeDtypeStruct((bn, l_total, D_HGRN), F32),
                   jax.ShapeDtypeStruct((bn, N_UNITS, LANE, LANE), F32)],
        grid=(bn // bb, N_UNITS // ub, nc),
        in_specs=in_specs,
        out_specs=[pl.BlockSpec((bb, c, ub * LANE), lambda bi, u, ci: (bi, ci, u)), st],
        scratch_shapes=[pltpu.VMEM((bb * ub, LANE, LANE), F32)],
        compiler_params=_cparams(("parallel", "parallel", "arbitrary")),
        name="hgrn_scan",
    )(*args)


def _outproj_kernel(x_ref, yr_ref, yh_ref, wo_ref, nf_ref, wr_ref, br_ref, cnt0_ref,
                    h_ref, xn_ref, idx_ref, gate_ref, rank_ref, cnt_ref, run_scr, *, l_valid, l_total):
    @pl.when(pl.program_id(0) == 0)
    def _():
        run_scr[...] = cnt0_ref[...]

    h = (x_ref[...] + _dot(yr_ref[...], wo_ref[0:D_RWKV, :]) + _dot(yh_ref[...], wo_ref[D_RWKV:, :]))
    h_ref[...] = h
    xn = _rmsnorm(h, nf_ref[...])
    xh = xn.astype(BF16)
    bits = lax.bitcast_convert_type(xh.astype(F32), U32)
    xn_ref[...] = (bits[:, D_MODEL // 2:] & jnp.uint32(0xFFFF0000)) | (bits[:, :D_MODEL // 2] >> 16)
    wr = wr_ref[...]
    wh = wr.astype(BF16)
    logits = (_dot(xh, wh) + _dot(xn - xh.astype(F32), wh) + _dot(xh, wr - wh.astype(F32))
              + br_ref[...])
    tm = logits.shape[0]
    lane = _iota2((tm, LANE), 1).astype(F32)
    neg = jnp.float32(-jnp.inf)
    work = jnp.where(lane < N_EXPERTS, logits, neg)
    idx_out = jnp.zeros((tm, LANE), I32)
    val_out = jnp.zeros((tm, LANE), F32)
    top0 = None
    picks = []
    for kk in range(TOP_K):
        m = jnp.max(work, axis=-1, keepdims=True)
        sel = jnp.min(jnp.where(work == m, lane, float(LANE)), axis=-1, keepdims=True)
        if kk == 0:
            top0 = m
        idx_out = jnp.where(lane == kk, sel.astype(I32), idx_out)
        val_out = jnp.where(lane == kk, jnp.exp(m - top0), val_out)
        picks.append(lane == sel)
        work = jnp.where(lane == sel, neg, work)
    idx_ref[...] = idx_out
    gate_ref[...] = val_out / jnp.sum(val_out, axis=-1, keepdims=True)

    chosen = sum(p.astype(F32) for p in picks)
    if l_valid < l_total:
        assert l_total & (l_total - 1) == 0 and tm % l_total == 0
        t_in_seq = _iota2((tm, LANE), 0) & (l_total - 1)
        chosen = jnp.where(t_in_seq < l_valid, chosen, 0.0)
    earlier = (_iota2((tm, tm), 0) > _iota2((tm, tm), 1)).astype(BF16)
    before = jnp.dot(earlier, chosen.astype(BF16), preferred_element_type=F32) + run_scr[...]
    rank_out = jnp.zeros((tm, LANE), F32)
    for kk in range(TOP_K):
        r_k = jnp.sum(jnp.where(picks[kk], before, 0.0), axis=-1, keepdims=True)
        rank_out = jnp.where(lane == kk, r_k, rank_out)
    rank_ref[...] = rank_out.astype(I32)
    run_scr[...] = run_scr[...] + jnp.sum(chosen, axis=0, keepdims=True)
    cnt_ref[...] = run_scr[...]


def _outproj_router(x, yr, yh, wo, nf, wr, br, cnt0, tm, l_valid, l_total):
    t = x.shape[0]
    row = lambda n: pl.BlockSpec((tm, n), lambda i: (i, 0))
    full = lambda a, b: pl.BlockSpec((a, b), lambda i: (0, 0), pipeline_mode=pl.Buffered(1))
    return pl.pallas_call(
        functools.partial(_outproj_kernel, l_valid=l_valid, l_total=l_total),
        out_shape=[jax.ShapeDtypeStruct((t, D_MODEL), F32), jax.ShapeDtypeStruct((t, D_MODEL // 2), U32),
                   jax.ShapeDtypeStruct((t, LANE), I32), jax.ShapeDtypeStruct((t, LANE), F32),
                   jax.ShapeDtypeStruct((t, LANE), I32), jax.ShapeDtypeStruct((1, LANE), F32)],
        grid=(t // tm,),
        in_specs=[row(D_MODEL), row(D_RWKV), row(D_HGRN), full(D_MODEL, D_MODEL), full(1, D_MODEL),
                  full(D_MODEL, LANE), full(1, LANE), full(1, LANE)],
        out_specs=[row(D_MODEL), row(D_MODEL // 2), row(LANE), row(LANE), row(LANE), full(1, LANE)],
        scratch_shapes=[pltpu.VMEM((1, LANE), F32)],
        compiler_params=_cparams(("arbitrary",)),
        name="outproj_router",
    )(x, yr, yh, wo, nf, wr, br, cnt0)


MOE_NF = D_EXPERT // MOE_TF
MOE_ISSUE = MOE_TM // MOE_NF
MOE_AHEAD_TILES = MOE_RT
MOE_AHEAD = MOE_AHEAD_TILES * MOE_TM
MOE_RING = 2 * MOE_RT
MOE_DUMP = (MOE_RING + MOE_RT + 2) * MOE_TM
HALF = D_MODEL // 2


def _ring_row_copy(x_hbm, tok, ring, sems, u):
    slot = (u // MOE_TM) % MOE_RING
    return pltpu.make_async_copy(x_hbm.at[pl.ds(tok, 1)], ring.at[slot, pl.ds(u % MOE_TM, 1)], sems.at[slot])


def _ring_tile_wait(x_hbm, ring, sems, tile):
    slot = tile % MOE_RING
    pltpu.make_async_copy(x_hbm.at[pl.ds(0, MOE_TM)], ring.at[slot], sems.at[slot]).wait()


def _scatter_row_copy(ybuf, ysc_hbm, sems, dst, u):
    slot = (u // MOE_TM) % MOE_RING
    return pltpu.make_async_copy(ybuf.at[slot, pl.ds(u % MOE_TM, 1)], ysc_hbm.at[pl.ds(dst, 1)], sems.at[slot])


def _scatter_tile_wait(ybuf, ysc_hbm, sems, tile):
    slot = tile % MOE_RING
    pltpu.make_async_copy(ybuf.at[slot], ysc_hbm.at[pl.ds(0, MOE_TM)], sems.at[slot]).wait()


def _moe_kernel(ie_ref, it0_ref, int_ref, nused_ref, rowsrc_ref, rowslot_ref, x_hbm, wg_ref, wu_ref, wd_ref,
                bg_ref, bu_ref, bd_ref, ysc_hbm, ring, xbuf, acc, ybuf, ring_sems, scat_sems,
                fill_sem, *, dump0):
    i = pl.program_id(0)
    f = pl.program_id(1)
    n_items = pl.num_programs(0)
    nt = int_ref[i]
    tile0 = it0_ref[i]
    n_used = nused_ref[0]

    @pl.when((i == 0) & (f == 0))
    def _():
        for t in range(MOE_RING):
            ybuf[t] = jnp.zeros((MOE_TM, HALF), U32)
        for s in range(MOE_DUMP // MOE_TM):
            cp = pltpu.make_async_copy(ybuf.at[0], ysc_hbm.at[pl.ds(dump0 + s * MOE_TM, MOE_TM)], fill_sem)
            cp.start()
            cp.wait()

        def head(r, carry):
            _ring_row_copy(x_hbm, rowsrc_ref[r], ring, ring_sems, r).start()
            return carry

        lax.fori_loop(0, MOE_AHEAD, head, 0)

    @pl.when(nt > 0)
    def _():
        @pl.when(f == 0)
        def _():
            def load(t, carry):
                _ring_tile_wait(x_hbm, ring, ring_sems, tile0 + t)
                w = ring[(tile0 + t) % MOE_RING]
                lo = lax.bitcast_convert_type(w << 16, F32).astype(BF16)
                hi = lax.bitcast_convert_type(w & jnp.uint32(0xFFFF0000), F32).astype(BF16)
                xbuf[t, :, 0:HALF] = lo
                xbuf[t, :, HALF:D_MODEL] = hi
                acc[t] = jnp.broadcast_to(bd_ref[0], (MOE_TM, D_MODEL))
                return carry

            lax.fori_loop(0, nt, load, 0)

        bg = bg_ref[0]
        bu = bu_ref[0]

        def issue(t):
            step = f * nt + t
            pos = tile0 * MOE_TM + step * MOE_ISSUE
            row = (step * MOE_ISSUE) % MOE_TM
            tile_s = tile0 + step // MOE_NF
            slot_s = tile_s % MOE_RING
            slot_g = (tile_s + MOE_AHEAD_TILES) % MOE_RING
            for q in range(MOE_ISSUE):
                pltpu.make_async_copy(x_hbm.at[pl.ds(rowsrc_ref[pos + MOE_AHEAD + q], 1)],
                                      ring.at[slot_g, pl.ds(row + q, 1)], ring_sems.at[slot_g]).start()
            for q in range(MOE_ISSUE):
                pltpu.make_async_copy(ybuf.at[slot_s, pl.ds(row + q, 1)],
                                      ysc_hbm.at[pl.ds(rowslot_ref[pos + q], 1)], scat_sems.at[slot_s]).start()

        def spans(todo):
            for t0, k in todo:
                for s in range(k):
                    issue(t0 + s)

            def hidden(t0, k):
                x = xbuf[pl.ds(t0, k)].reshape(k * MOE_TM, D_MODEL)
                g = jnp.minimum(jnp.dot(x, wg_ref[0].astype(BF16), preferred_element_type=F32) + bg, SWIGLU_LIMIT)
                u = jnp.clip(jnp.dot(x, wu_ref[0].astype(BF16), preferred_element_type=F32) + bu,
                             -SWIGLU_LIMIT, SWIGLU_LIMIT)
                return ((u + 1.0) * g * _sigmoid(SWIGLU_ALPHA * g)).astype(BF16)

            hid = hidden(*todo[0])
            for s, (t0, k) in enumerate(todo):
                nxt = hidden(*todo[s + 1]) if s + 1 < len(todo) else None
                down = jnp.dot(hid, wd_ref[0].astype(BF16), preferred_element_type=F32)
                acc[pl.ds(t0, k)] += down.reshape(k, MOE_TM, D_MODEL)
                hid = nxt

        def quad(p, carry):
            spans([(4 * p, 2), (4 * p + 2, 2)])
            return carry

        n_quads = nt // 4
        lax.fori_loop(0, n_quads, quad, 0)

        def pair(p, carry):
            spans([(4 * n_quads + 2 * p, 2)])
            return carry

        lax.fori_loop(0, (nt - 4 * n_quads) // 2, pair, 0)

        @pl.when(nt % 2 == 1)
        def _():
            spans([(nt - 1, 1)])

        @pl.when(f == MOE_NF - 1)
        def _():
            def pack(t, carry):
                vt = tile0 + t + MOE_AHEAD_TILES

                @pl.when(vt >= MOE_RING)
                def _():
                    _scatter_tile_wait(ybuf, ysc_hbm, scat_sems, vt)

                bits = lax.bitcast_convert_type(acc[t].astype(BF16).astype(F32), U32)
                ybuf[vt % MOE_RING] = (bits[:, HALF:] & jnp.uint32(0xFFFF0000)) | (bits[:, :HALF] >> 16)
                return carry

            lax.fori_loop(0, nt, pack, 0)

    @pl.when((i == n_items - 1) & (f == MOE_NF - 1))
    def _():
        for s in range(MOE_AHEAD_TILES):
            _ring_tile_wait(x_hbm, ring, ring_sems, n_used + s)

        def flush(u, carry):
            _scatter_row_copy(ybuf, ysc_hbm, scat_sems, rowslot_ref[u], u).start()
            return carry

        lax.fori_loop(n_used * MOE_TM, n_used * MOE_TM + MOE_AHEAD, flush, 0)

        def retire(vt, carry):
            _scatter_tile_wait(ybuf, ysc_hbm, scat_sems, vt)
            return carry

        lax.fori_loop(jnp.maximum(n_used - MOE_AHEAD_TILES, 0), n_used + MOE_AHEAD_TILES, retire, 0)


def _moe_experts(item_e, item_t0, item_nt, n_used, row_src, row_slot, x_packed, w_gu, b_gu, w_down, b_down,
                 n_items, n_tok):
    nf = MOE_NF
    dump0 = TOP_K * n_tok

    def fcol(i, f, int_):
        return jnp.where(int_[i] > 0, f, nf - 1)

    in_specs = [
        pl.BlockSpec(memory_space=pl.ANY),
        pl.BlockSpec((1, D_MODEL, MOE_TF), lambda i, f, ie, it0, int_, *_: (ie[i], 0, fcol(i, f, int_))),
        pl.BlockSpec((1, D_MODEL, MOE_TF), lambda i, f, ie, it0, int_, *_: (ie[i], 0, nf + fcol(i, f, int_))),
        pl.BlockSpec((1, MOE_TF, D_MODEL), lambda i, f, ie, it0, int_, *_: (ie[i], fcol(i, f, int_), 0)),
        pl.BlockSpec((1, 1, MOE_TF), lambda i, f, ie, it0, int_, *_: (ie[i], 0, fcol(i, f, int_))),
        pl.BlockSpec((1, 1, MOE_TF), lambda i, f, ie, it0, int_, *_: (ie[i], 0, nf + fcol(i, f, int_))),
        pl.BlockSpec((1, 1, D_MODEL), lambda i, f, ie, it0, int_, *_: (ie[i], 0, 0)),
    ]
    return pl.pallas_call(
        functools.partial(_moe_kernel, dump0=dump0),
        out_shape=jax.ShapeDtypeStruct((dump0 + MOE_DUMP, HALF), U32),
        grid_spec=pltpu.PrefetchScalarGridSpec(
            num_scalar_prefetch=6,
            grid=(n_items, nf),
            in_specs=in_specs,
            out_specs=pl.BlockSpec(memory_space=pl.ANY),
            scratch_shapes=[pltpu.VMEM((MOE_RING, MOE_TM, HALF), U32),
                            pltpu.VMEM((MOE_RT, MOE_TM, D_MODEL), BF16),
                            pltpu.VMEM((MOE_RT, MOE_TM, D_MODEL), F32),
                            pltpu.VMEM((MOE_RING, MOE_TM, HALF), U32),
                            pltpu.SemaphoreType.DMA((MOE_RING,)),
                            pltpu.SemaphoreType.DMA((MOE_RING,)),
                            pltpu.SemaphoreType.DMA(())]),
        compiler_params=_cparams(("arbitrary", "arbitrary")),
        name="moe_experts",
    )(item_e, item_t0, item_nt, n_used, row_src, row_slot, x_packed, w_gu, w_gu, w_down, b_gu, b_gu, b_down)


def _ple_kernel(h_ref, gate_ref, y0_ref, y1_ref, y2_ref, y3_ref, p_ref, np_ref, wg_ref, wp_ref, nfin_ref, o_ref):
    gates = gate_ref[...]
    lo = jnp.zeros((h_ref.shape[0], HALF), F32)
    hi = jnp.zeros((h_ref.shape[0], HALF), F32)
    for kk, y_ref in enumerate((y0_ref, y1_ref, y2_ref, y3_ref)):
        w = y_ref[...]
        g = gates[:, kk:kk + 1]
        lo = lo + g * lax.bitcast_convert_type(w << 16, F32)
        hi = hi + g * lax.bitcast_convert_type(w & jnp.uint32(0xFFFF0000), F32)
    h = h_ref[...] + jnp.concatenate([lo, hi], axis=1)
    gate = _sigmoid(_dot(_rmsnorm(h, np_ref[...]), wg_ref[...]))
    h = h + gate * _dot(p_ref[...], wp_ref[...])
    o_ref[...] = _rmsnorm(h, nfin_ref[...])


def _ple_final(h, gates, ysc, plane, tok0, p, n_ple, wg, wp, n_fin, tm):
    t = h.shape[0]
    row = lambda n: pl.BlockSpec((tm, n), lambda i: (i, 0))
    full = lambda a, b: pl.BlockSpec((a, b), lambda i: (0, 0), pipeline_mode=pl.Buffered(1))
    ysp = lambda kk: pl.BlockSpec((tm, HALF), lambda i: ((kk * plane + tok0) // tm + i, 0))
    return pl.pallas_call(
        _ple_kernel,
        out_shape=jax.ShapeDtypeStruct((t, D_MODEL), F32),
        grid=(t // tm,),
        in_specs=[row(D_MODEL), row(LANE), ysp(0), ysp(1), ysp(2), ysp(3), row(D_PLE), full(1, D_MODEL),
                  full(D_MODEL, D_MODEL), full(D_PLE, D_MODEL), full(1, D_MODEL)],
        out_specs=row(D_MODEL),
        compiler_params=_cparams(("parallel",)),
        name="ple_final",
    )(h, gates, ysc, ysc, ysc, ysc, p, n_ple, wg, wp, n_fin)


def _pad_shift_cols(a):
    def z(n):
        return jnp.zeros(a.shape[:-1] + (n,), a.dtype)
    c0 = 3 * D_RWKV
    c1 = c0 + LORA_W
    c2 = c1 + LORA_A
    return jnp.concatenate([a[..., :c0], a[..., c0:c1], z(LW_PAD - LORA_W), a[..., c1:c2], z(LA_PAD - LORA_A),
                            a[..., c2:], z(LG_PAD - LORA_G)], axis=-1)


def _pad_shift_rows(a):
    z = lambda n: jnp.zeros((n, a.shape[1]), a.dtype)
    c0 = 3 * D_RWKV
    c1 = c0 + LORA_W
    c2 = c1 + LORA_A
    return jnp.concatenate([a[:c0], a[c0:c1], z(LW_PAD - LORA_W), a[c1:c2], z(LA_PAD - LORA_A), a[c2:],
                            z(LG_PAD - LORA_G)], axis=0)


def _unpad_shift_cols(a):
    return jnp.concatenate([a[..., :OFF_WD], a[..., OFF_WD:OFF_WD + LORA_W], a[..., OFF_AD:OFF_AD + LORA_A],
                            a[..., OFF_GD:OFF_GD + LORA_G]], axis=-1)


def _pad_rows(a, n):
    return jnp.concatenate([a, jnp.zeros((n - a.shape[0],) + a.shape[1:], a.dtype)], axis=0)


def _route_tables_kernel(pos_ref, src0_hbm, slot0_hbm, src_ref, slot_ref, sems, *, n_tok):
    fill_src = pltpu.make_async_copy(src0_hbm, src_ref, sems.at[0])
    fill_slot = pltpu.make_async_copy(slot0_hbm, slot_ref, sems.at[1])
    fill_src.start()
    fill_slot.start()
    fill_src.wait()
    fill_slot.wait()

    def place(tok, carry):
        for kk in range(TOP_K):
            r = pos_ref[tok * TOP_K + kk]
            src_ref[r] = tok
            slot_ref[r + MOE_AHEAD] = tok + kk * n_tok
        return carry

    lax.fori_loop(0, n_tok, place, 0, unroll=4)


def _route_tables(pos, n_tok, n_stream):
    u = jnp.arange(n_stream, dtype=I32)
    src0 = jnp.full((n_stream,), n_tok, I32)
    slot0 = TOP_K * n_tok + u % MOE_DUMP
    smem = pl.BlockSpec(memory_space=pltpu.SMEM)
    hbm = pl.BlockSpec(memory_space=pl.ANY)
    return pl.pallas_call(
        functools.partial(_route_tables_kernel, n_tok=n_tok),
        out_shape=[jax.ShapeDtypeStruct((n_stream,), I32)] * 2,
        in_specs=[smem, hbm, hbm],
        out_specs=[smem, smem],
        scratch_shapes=[pltpu.SemaphoreType.DMA((2,))],
        name="route_tables",
    )(pos, src0, slot0)


def _routing(idx, rank, counts, n_rows_cap, n_items_cap):
    t = idx.shape[0]
    na = t * TOP_K
    flat_e = idx.reshape(na)
    rank = rank.reshape(na)
    ptiles = (counts + MOE_TM - 1) // MOE_TM
    pend = jnp.cumsum(ptiles)
    pstart = pend - ptiles
    pos = (pstart[flat_e] * MOE_TM + rank).astype(I32)
    row_src, row_slot = _route_tables(pos, t, n_rows_cap + 2 * MOE_AHEAD)
    items_per_e = (ptiles + MOE_RT - 1) // MOE_RT
    iend = jnp.cumsum(items_per_e)
    istart = iend - items_per_e
    ii = jnp.arange(n_items_cap, dtype=I32)
    e_of = jnp.minimum(jnp.searchsorted(iend, ii, side="right"), N_EXPERTS - 1).astype(I32)
    jj = ii - istart[e_of]
    used = ii < iend[-1]
    item_nt = jnp.where(used, jnp.clip(ptiles[e_of] - jj * MOE_RT, 0, MOE_RT), 0).astype(I32)
    item_t0 = jnp.where(used, pstart[e_of] + jj * MOE_RT, 0).astype(I32)
    last_e = e_of[jnp.maximum(iend[-1] - 1, 0)]
    item_e = jnp.where(used, e_of, last_e).astype(I32)
    n_used = jnp.stack([pend[-1], iend[-1]]).astype(I32)
    return row_src, row_slot, n_used, item_e, item_t0, item_nt


def kernel(x_prompt, x_sample, p_prompt, p_sample, state_rwkv_shift, state_rwkv, state_hgrn, norm_mix, w_in,
           mu_shift, w0, w_up, a0, a_up, g_up, k_k, k_a, r_k, lnx_w, lnx_b, hgrn_lb, hgrn_norm, w_out,
           norm_ffn, w_router, b_router, w_gu, b_gu, w_down, b_down, norm_ple, w_ple_gate, w_ple_proj,
           norm_final):
    depth = w_in.shape[0]
    assert depth == 1
    li = 0
    bp, lp = x_prompt.shape[0], x_prompt.shape[1]
    bs, ls = x_sample.shape[0], x_sample.shape[1]
    ls_pad = SUBLANE
    tp = bp * lp

    w_t = jnp.transpose(w_in[li])
    w_in_p = jnp.concatenate([_pad_shift_rows(w_t[:D_SHIFT]), w_t[D_SHIFT:]], axis=0).astype(BF16)
    row = lambda a: a.reshape(1, -1).astype(F32)
    pp = {
        "mu": row(_pad_shift_cols(mu_shift[li])),
        "w0": row(w0[li]), "a0": row(a0[li]), "k_k": row(k_k[li]), "k_a": row(k_a[li]),
        "w_up": _pad_rows(w_up[li], LW_PAD).astype(BF16),
        "a_up": _pad_rows(a_up[li], LA_PAD).astype(BF16),
        "g_up": _pad_rows(g_up[li], LG_PAD).astype(BF16),
    }
    rk = row(r_k[li])
    lnw = row(lnx_w[li])
    lnb = row(lnx_b[li])
    lower = jax.nn.softmax(hgrn_lb.astype(F32), axis=0)
    lb = row(jnp.cumsum(lower, axis=0)[li])
    nw = row(hgrn_norm[li])
    wo = w_out[li].astype(BF16)
    nf = row(norm_ffn[li])
    wr = jnp.concatenate([w_router[li], jnp.zeros((D_MODEL, LANE - N_EXPERTS), F32)], axis=1)
    br = jnp.concatenate([b_router[li], jnp.zeros((LANE - N_EXPERTS,), F32)]).reshape(1, LANE)
    n_ple = row(norm_ple[li])
    wpg = w_ple_gate[li].astype(BF16)
    wpp = w_ple_proj[li].astype(BF16)
    n_fin = row(norm_final)
    g_mix = row(norm_mix[li])

    def mixer(x2d, shift_prev, s_rwkv, s_hgrn, cnt0, bn, l_total, l_valid, tm_in, bb_prep, tt, bb_scan, ub_scan, c,
              tm_out):
        proj = _inproj(x2d, g_mix, w_in_p, tm_in, 1536)
        streams = _rwkv_prep(proj, shift_prev, pp, bn, l_total, l_valid, bb_prep, tt)
        yr, s_rwkv_new = _rwkv_scan(streams, rk, lnw, lnb, s_rwkv, bn, l_total, bb_scan, ub_scan, c)
        yh, s_hgrn_new = _hgrn_scan(proj, lb, nw, s_hgrn, bn, l_total, l_valid, bb_scan, ub_scan, c)
        h1, xn2, idx, gates, rank, cnt = _outproj_router(
            x2d, yr.reshape(bn * l_total, D_RWKV), yh.reshape(bn * l_total, D_HGRN), wo, nf, wr, br, cnt0, tm_out,
            l_valid, l_total)
        new_shift = _unpad_shift_cols(proj.reshape(bn, l_total, D_IN_PAD)[:, l_valid - 1, :D_SHIFT_PAD])
        return h1, xn2, idx, gates, rank, cnt, new_shift, s_rwkv_new, s_hgrn_new

    xp2 = x_prompt.reshape(tp, D_MODEL)
    zero_shift = jnp.zeros((bp, 1, D_SHIFT_PAD), F32)
    h1p, xn2p, idxp, gatesp, rankp, cntp, shift_p, rwkv_p, hgrn_p = mixer(
        xp2, zero_shift, None, None, jnp.zeros((1, LANE), F32), bp, lp, lp, 1024, 1, 256, bp, 4, 64, 512)

    xs_pad = jnp.concatenate([x_sample, jnp.zeros((bs, ls_pad - ls, D_MODEL), F32)], axis=1)
    xs2 = xs_pad.reshape(bs * ls_pad, D_MODEL)
    shift_s0 = _pad_shift_cols(state_rwkv_shift[li]).reshape(bs, 1, D_SHIFT_PAD)
    h1s, xn2s, idxs, gatess, ranks, cnt_all, shift_s, rwkv_s, hgrn_s = mixer(
        xs2, shift_s0, state_rwkv[li], state_hgrn[li], cntp, bs, ls_pad, ls, 1024, 16, ls_pad, 16, 4, ls_pad, 512)

    def compact(a):
        return a.reshape(bs, ls_pad, a.shape[-1])[:, :ls].reshape(bs * ls, a.shape[-1])

    h1s, xn2s, idxs, gatess, ranks = compact(h1s), compact(xn2s), compact(idxs), compact(gatess), compact(ranks)
    ts = bs * ls

    t_all = tp + ts
    idx_all = jnp.concatenate([idxp[:, :TOP_K], idxs[:, :TOP_K]], axis=0)
    rank_all = jnp.concatenate([rankp[:, :TOP_K], ranks[:, :TOP_K]], axis=0)
    counts = cnt_all[0, :N_EXPERTS].astype(I32)
    n_tiles_cap = -(-(t_all * TOP_K) // MOE_TM) + N_EXPERTS
    n_rows_cap = n_tiles_cap * MOE_TM
    n_items_cap = N_EXPERTS + n_tiles_cap // MOE_RT
    row_src, row_slot, n_used, item_e, item_t0, item_nt = _routing(idx_all, rank_all, counts, n_rows_cap,
                                                                   n_items_cap)
    x_packed = jnp.concatenate([xn2p, xn2s, jnp.zeros((SUBLANE, HALF), U32)], axis=0)
    ysc = _moe_experts(item_e, item_t0, item_nt, n_used, row_src, row_slot, x_packed, w_gu[li],
                       b_gu[li].reshape(N_EXPERTS, 1, -1), w_down[li], b_down[li].reshape(N_EXPERTS, 1, -1),
                       n_used[1], t_all)

    tm_fin = 512
    assert tp % tm_fin == 0 and ts % tm_fin == 0
    y_p = _ple_final(h1p, gatesp, ysc, t_all, 0, p_prompt[li].reshape(tp, D_PLE), n_ple, wpg, wpp, n_fin, tm_fin)
    y_s = _ple_final(h1s, gatess, ysc, t_all, tp, p_sample[li].reshape(ts, D_PLE), n_ple, wpg, wpp, n_fin, tm_fin)

    return (y_p.reshape(bp, lp, D_MODEL), y_s.reshape(bs, ls, D_MODEL),
            shift_p[None], rwkv_p[None], hgrn_p[None],
            shift_s[None], rwkv_s[None], hgrn_s[None])
```

```python
import functools

import jax
import jax.numpy as jnp
from jax import lax
from jax.experimental import pallas as pl
from jax.experimental.pallas import tpu as pltpu

F32 = jnp.float32
BF16 = jnp.bfloat16
I32 = jnp.int32
U32 = jnp.uint32

D_MODEL = 2048
D_RWKV = 1024
D_HGRN = 1024
RWKV_HEAD = 64
N_RWKV_HEADS = 16
HGRN_HEAD = 128
N_HGRN_HEADS = 8
LORA_W = 64
LORA_A = 64
LORA_G = 160
D_SHIFT = 3 * D_RWKV + LORA_W + LORA_A + LORA_G
N_EXPERTS = 32
TOP_K = 4
D_EXPERT = 2048
SWIGLU_LIMIT = 7.0
SWIGLU_ALPHA = 1.702
D_PLE = 256
RMS_EPS = 1e-6
GN_EPS = 64e-5
HGRN_NORM_EPS = 1e-5

LANE = 128
SUBLANE = 8
N_UNITS = 8

LW_PAD = LANE
LA_PAD = LANE
LG_PAD = 2 * LANE
OFF_WD = 3 * D_RWKV
OFF_AD = OFF_WD + LW_PAD
OFF_GD = OFF_AD + LA_PAD
D_SHIFT_PAD = OFF_GD + LG_PAD
D_IN_PAD = D_SHIFT_PAD + 4 * D_HGRN

MOE_TM = 128
MOE_RT = 10
MOE_TF = 256
VMEM_LIMIT = 56 * 1024 * 1024


def _cparams(sem, vmem=VMEM_LIMIT):
    return pltpu.CompilerParams(dimension_semantics=sem, vmem_limit_bytes=vmem)


def _rmsnorm(x, g):
    return x * lax.rsqrt(jnp.mean(x * x, axis=-1, keepdims=True) + RMS_EPS) * g


def _dot(a, b):
    return jnp.dot(a.astype(BF16), b.astype(BF16), preferred_element_type=F32)


def _dot_nt(a, b):
    return lax.dot_general(a.astype(BF16), b.astype(BF16), (((1,), (1,)), ((), ())),
                           preferred_element_type=F32)


def _dot_tn(a, b):
    return lax.dot_general(a.astype(BF16), b.astype(BF16), (((0,), (0,)), ((), ())),
                           preferred_element_type=F32)


def _split3(x):
    h = x.astype(BF16)
    r = x - h.astype(F32)
    m = r.astype(BF16)
    l = (r - m.astype(F32)).astype(BF16)
    return h, m, l


def _dot_exact_rhs(a, b_bf16):
    h, m, l = _split3(a)
    d = functools.partial(jnp.dot, preferred_element_type=F32)
    return d(h, b_bf16) + d(m, b_bf16) + d(l, b_bf16)


def _dot_exact_lhs(a_bf16, b):
    n = b.shape[1]
    parts = jnp.dot(a_bf16, jnp.concatenate(_split3(b), axis=1), preferred_element_type=F32)
    return parts[:, :n] + parts[:, n:2 * n] + parts[:, 2 * n:]


def _iota2(shape, dim):
    return lax.broadcasted_iota(I32, shape, dim)


def _cumsum_time(x):
    c = x.shape[0]
    tri = (_iota2((c, c), 0) >= _iota2((c, c), 1)).astype(BF16)
    return _dot_exact_lhs(tri, x)


def _same_head_mask():
    return (_iota2((LANE, LANE), 0) >= RWKV_HEAD) == (_iota2((LANE, LANE), 1) >= RWKV_HEAD)


def _sigmoid(x):
    return 1.0 / (1.0 + jnp.exp(-x))


def _inproj_kernel(x_ref, g_ref, w_ref, o_ref, xn_ref):
    @pl.when(pl.program_id(1) == 0)
    def _():
        xn_ref[...] = _rmsnorm(x_ref[...], g_ref[...]).astype(BF16)

    o_ref[...] = lax.dot_general(xn_ref[...], w_ref[...], (((1,), (1,)), ((), ())), preferred_element_type=F32)


def _inproj(x, g, w, tm, tn):
    t, d = x.shape
    n = w.shape[0]
    return pl.pallas_call(
        _inproj_kernel,
        out_shape=jax.ShapeDtypeStruct((t, n), F32),
        grid=(t // tm, n // tn),
        in_specs=[pl.BlockSpec((tm, d), lambda i, j: (i, 0)),
                  pl.BlockSpec((1, d), lambda i, j: (0, 0)),
                  pl.BlockSpec((tn, d), lambda i, j: (j, 0))],
        out_specs=pl.BlockSpec((tm, tn), lambda i, j: (i, j)),
        scratch_shapes=[pltpu.VMEM((tm, d), BF16)],
        compiler_params=_cparams(("parallel", "arbitrary")),
        name="inproj",
    )(x, g, w)


def _rwkv_prep_kernel(x_ref, p8_ref, sh_ref, mu_ref, w0_ref, wup_ref, a0_ref, aup_ref, gup_ref,
                      kk_ref, ka_ref,
                      r_o, k_o, v_o, lw_o, a_o, b_o, g_o, *, bb, tt, l_valid, l_total):
    ti = pl.program_id(1)
    w = D_SHIFT_PAD
    x3 = x_ref[...].reshape(bb, tt, w)
    rolled = pltpu.roll(x3, 1, axis=1)
    prev_tail = p8_ref[...].reshape(bb, SUBLANE, w)[:, SUBLANE - 1:SUBLANE, :]
    first = jnp.where(ti == 0, sh_ref[...], prev_tail)
    t_in = _iota2((bb, tt, w), 1)
    prev = jnp.where(t_in == 0, first, rolled)
    xs = (x3 + (prev - x3) * mu_ref[...]).reshape(bb * tt, w)

    r = xs[:, 0:D_RWKV]
    k = xs[:, D_RWKV:2 * D_RWKV]
    v = xs[:, 2 * D_RWKV:3 * D_RWKV]
    wd = xs[:, OFF_WD:OFF_WD + LW_PAD]
    ad = xs[:, OFF_AD:OFF_AD + LA_PAD]
    gd = xs[:, OFF_GD:OFF_GD + LG_PAD]

    z = -(w0_ref[...] + _dot(jnp.tanh(wd), wup_ref[...]))
    softplus = jnp.maximum(z, 0.0) + jnp.log(1.0 + jnp.exp(-jnp.abs(z)))
    lw = -jnp.exp(-softplus - 0.5)
    asig = _sigmoid(a0_ref[...] + _dot(ad, aup_ref[...]))
    g = _dot(_sigmoid(gd), gup_ref[...])

    kk = k * kk_ref[...]
    same_head = _same_head_mask().astype(BF16)
    sq = kk * kk
    ssq = jnp.concatenate(
        [_dot_exact_rhs(sq[:, u * LANE:(u + 1) * LANE], same_head) for u in range(N_UNITS)], axis=1)
    kkn = kk / jnp.maximum(jnp.sqrt(ssq), 1e-12)
    k2 = k * (1.0 + (asig - 1.0) * ka_ref[...])
    a_vec = -kkn
    b_vec = kkn * asig

    if l_valid < l_total:
        t_glob = (ti * tt + _iota2((bb, tt, D_RWKV), 1)).reshape(bb * tt, D_RWKV)
        ok = t_glob < l_valid
        zero = jnp.zeros_like(k2)
        lw, k2, v, a_vec, b_vec = (jnp.where(ok, t, zero) for t in (lw, k2, v, a_vec, b_vec))

    r_o[...] = r
    k_o[...] = k2
    v_o[...] = v
    lw_o[...] = lw
    a_o[...] = a_vec
    b_o[...] = b_vec
    g_o[...] = g


def _rwkv_prep(proj, shift_pad, pp, bn, l_total, l_valid, bb, tt):
    nt = l_total // tt
    rows = bb * tt
    w = D_SHIFT_PAD
    row_spec = pl.BlockSpec((rows, w), lambda bi, ti: (bi * nt + ti, 0))
    p8_spec = pl.BlockSpec((bb * SUBLANE, w),
                           lambda bi, ti: (jnp.maximum((bi * nt + ti) * (tt // SUBLANE) - 1, 0), 0))
    vec = lambda n: pl.BlockSpec((1, n), lambda bi, ti: (0, 0))
    mat = lambda a, b: pl.BlockSpec((a, b), lambda bi, ti: (0, 0))
    out_spec = pl.BlockSpec((rows, D_RWKV), lambda bi, ti: (bi * nt + ti, 0))
    out_sds = jax.ShapeDtypeStruct((bn * l_total, D_RWKV), F32)
    kern = functools.partial(_rwkv_prep_kernel, bb=bb, tt=tt, l_valid=l_valid, l_total=l_total)
    return pl.pallas_call(
        kern,
        out_shape=[out_sds] * 7,
        grid=(bn // bb, nt),
        in_specs=[row_spec, p8_spec,
                  pl.BlockSpec((bb, 1, w), lambda bi, ti: (bi, 0, 0)),
                  vec(w), vec(D_RWKV), mat(LW_PAD, D_RWKV), vec(D_RWKV), mat(LA_PAD, D_RWKV),
                  mat(LG_PAD, D_RWKV), vec(D_RWKV), vec(D_RWKV)],
        out_specs=[out_spec] * 7,
        compiler_params=_cparams(("parallel", "arbitrary")),
        name="rwkv_prep",
    )(proj, proj, shift_pad, pp["mu"], pp["w0"], pp["w_up"], pp["a0"], pp["a_up"], pp["g_up"],
      pp["k_k"], pp["k_a"])


def _rwkv_scan_kernel(*refs, bb, ub, c, has_s0):
    if has_s0:
        (r_ref, k_ref, v_ref, lw_ref, a_ref, b_ref, g_ref, rk_ref, lnw_ref, lnb_ref, s0_ref,
         y_ref, so_ref, s_scr) = refs
    else:
        (r_ref, k_ref, v_ref, lw_ref, a_ref, b_ref, g_ref, rk_ref, lnw_ref, lnb_ref,
         y_ref, so_ref, s_scr) = refs
        s0_ref = None
    ci = pl.program_id(2)
    n_chunks = pl.num_programs(2)

    lane = _iota2((1, LANE), 1)
    m0 = (lane < RWKV_HEAD).astype(F32)
    m1 = 1.0 - m0
    bd_mask = _same_head_mask().astype(F32)

    c2 = 2 * c
    ri = _iota2((c2, c2), 0)
    cj = _iota2((c2, c2), 1)
    same_blk = (ri >= c) == (cj >= c)
    mask_s = jnp.where(same_blk, (ri > cj).astype(F32), 0.0)
    mask_i = jnp.where(same_blk, (ri >= cj).astype(F32), 0.0)
    eye = (ri == cj).astype(F32)
    n_sq = max((c - 1).bit_length() - 1, 0)

    head_avg = bd_mask.astype(BF16)

    chains = [(j, w) for j in range(bb) for w in range(ub)]
    seqs = range(len(chains))

    @pl.when(ci == 0)
    def _():
        for n, (j, w) in enumerate(chains):
            if has_s0:
                zero = jnp.zeros((RWKV_HEAD, RWKV_HEAD), F32)
                top = jnp.concatenate([s0_ref[j, 2 * w], zero], axis=1)
                bottom = jnp.concatenate([zero, s0_ref[j, 2 * w + 1]], axis=1)
                s_scr[n] = jnp.concatenate([top, bottom], axis=0)
            else:
                s_scr[n] = jnp.zeros((LANE, LANE), F32)

    stack2 = lambda lo, hi: jnp.concatenate([lo, hi], axis=0)
    unit = lambda ref, j, w: ref[j, :, w * LANE:(w + 1) * LANE]
    vec = lambda ref, w: ref[:, w * LANE:(w + 1) * LANE]
    r = [unit(r_ref, j, w) for j, w in chains]
    k = [unit(k_ref, j, w) for j, w in chains]
    v = [unit(v_ref, j, w) for j, w in chains]
    lw = [unit(lw_ref, j, w) for j, w in chains]
    a = [unit(a_ref, j, w) for j, w in chains]
    b = [unit(b_ref, j, w) for j, w in chains]
    cum = [_cumsum_time(x) for x in lw]
    clast = [x[c - 1:c, :] for x in cum]
    p_inv = [jnp.exp(-x) for x in cum]
    a2 = [stack2(a[j] * jnp.exp(cum[j] - lw[j]) * m0, a[j] * jnp.exp(cum[j] - lw[j]) * m1) for j in seqs]
    r2 = [stack2(r[j] * jnp.exp(cum[j]) * m0, r[j] * jnp.exp(cum[j]) * m1) for j in seqs]
    b2 = [stack2(b[j] * p_inv[j], b[j] * p_inv[j]) for j in seqs]
    k2 = [stack2(k[j] * p_inv[j], k[j] * p_inv[j]) for j in seqs]
    v2 = [stack2(v[j] * m0, v[j] * m1) for j in seqs]
    ar2 = [stack2(a2[j], r2[j]) for j in seqs]
    ar_b = [_dot_nt(ar2[j], b2[j]) for j in seqs]
    ar_k = [_dot_nt(ar2[j], k2[j]) for j in seqs]
    lab = [mask_s * x[:c2] for x in ar_b]
    lak = [mask_s * x[:c2] for x in ar_k]
    rb = [mask_i * x[c2:] for x in ar_b]
    rkm = [mask_i * x[c2:] for x in ar_k]
    tinv = [eye + x for x in lab]
    xp = lab
    for _ in range(n_sq):
        xp = [_dot(x, x) for x in xp]
        tinv = [tinv[j] + _dot(tinv[j], xp[j]) for j in seqs]
    lr_v = [_dot(stack2(lak[j], rkm[j]), v2[j]) for j in seqs]
    s0 = [s_scr[n] for n in seqs]
    ar_s0 = [_dot_nt(ar2[n], s0[n]) for n in seqs]
    u2 = [_dot(tinv[n], ar_s0[n][:c2] + lr_v[n][:c2]) for n in seqs]
    y2 = [ar_s0[n][c2:] + _dot(rb[n], u2[n]) + lr_v[n][c2:] for n in seqs]
    y = [x[:c] + x[c:] for x in y2]
    u = [x[:c] + x[c:] for x in u2]
    p_last = [jnp.exp(clast[n] - cum[n]) for n in seqs]
    for n in seqs:
        s_scr[n] = s0[n] * jnp.exp(clast[n]) + bd_mask * _dot_tn(
            stack2(u[n], v[n]), stack2(b[n] * p_last[n], k[n] * p_last[n]))

    sums = [_dot(stack2(y[n], r[n] * k[n] * vec(rk_ref, w)), head_avg) for n, (j, w) in enumerate(chains)]
    dlt = [y[n] - sums[n][:c] * (1.0 / RWKV_HEAD) for n in seqs]
    var = [_dot(x * x, head_avg) * (1.0 / RWKV_HEAD) for x in dlt]
    bonus = [sums[n][c:] * v[n] for n in seqs]
    for n, (j, w) in enumerate(chains):
        yn = dlt[n] * lax.rsqrt(var[n] + GN_EPS) * vec(lnw_ref, w) + vec(lnb_ref, w)
        y_ref[j, :, w * LANE:(w + 1) * LANE] = (yn + bonus[n]) * unit(g_ref, j, w)

    @pl.when(ci == n_chunks - 1)
    def _():
        for n, (j, w) in enumerate(chains):
            s_fin = s_scr[n]
            so_ref[j, 2 * w] = s_fin[:RWKV_HEAD, :RWKV_HEAD]
            so_ref[j, 2 * w + 1] = pltpu.roll(s_fin, RWKV_HEAD, axis=1)[RWKV_HEAD:, :RWKV_HEAD]


def _rwkv_scan(streams, rk, lnw, lnb, s0, bn, l_total, bb, ub, c):
    nc = l_total // c
    has_s0 = s0 is not None
    blk = pl.BlockSpec((bb, c, ub * LANE), lambda bi, u, ci: (bi, ci, u))
    vec = pl.BlockSpec((1, ub * LANE), lambda bi, u, ci: (0, u))
    st = pl.BlockSpec((bb, 2 * ub, RWKV_HEAD, RWKV_HEAD), lambda bi, u, ci: (bi, u, 0, 0))
    in_specs = [blk] * 7 + [vec] * 3 + ([st] if has_s0 else [])
    args = [s.reshape(bn, l_total, D_RWKV) for s in streams] + [rk, lnw, lnb] + ([s0] if has_s0 else [])
    kern = functools.partial(_rwkv_scan_kernel, bb=bb, ub=ub, c=c, has_s0=has_s0)
    return pl.pallas_call(
        kern,
        out_shape=[jax.ShapeDtypeStruct((bn, l_total, D_RWKV), F32),
                   jax.ShapeDtypeStruct((bn, N_RWKV_HEADS, RWKV_HEAD, RWKV_HEAD), F32)],
        grid=(bn // bb, N_UNITS // ub, nc),
        in_specs=in_specs,
        out_specs=[blk, st],
        scratch_shapes=[pltpu.VMEM((bb * ub, LANE, LANE), F32)],
        compiler_params=_cparams(("parallel", "parallel", "arbitrary")),
        name="rwkv_scan",
    )(*args)


def _hgrn_kernel(*refs, bb, ub, c, has_s0, l_valid, l_total):
    if has_s0:
        q_ref, f_ref, i_ref, og_ref, lb_ref, nw_ref, s0_ref, y_ref, so_ref, s_scr = refs
    else:
        q_ref, f_ref, i_ref, og_ref, lb_ref, nw_ref, y_ref, so_ref, s_scr = refs
        s0_ref = None
    ci = pl.program_id(2)
    n_chunks = pl.num_programs(2)
    tri = (_iota2((c, c), 0) >= _iota2((c, c), 1)).astype(F32)
    mid = max(c // 2 - 1, 0)
    chains = [(j, w) for j in range(bb) for w in range(ub)]
    seqs = range(len(chains))
    unit = lambda ref, j, w: ref[j, :, w * LANE:(w + 1) * LANE]

    @pl.when(ci == 0)
    def _():
        for n, (j, w) in enumerate(chains):
            if has_s0:
                s_scr[n] = s0_ref[j, w]
            else:
                s_scr[n] = jnp.zeros((LANE, LANE), F32)

    def key_column(x):
        ones = jnp.ones((3 * c, LANE), BF16)
        parts = jnp.concatenate([p.astype(F32) for p in _split3(x)], axis=0).astype(BF16)
        return lax.dot_general(parts, ones, (((0,), (0,)), ((), ())), preferred_element_type=F32)

    q = [unit(q_ref, j, w) * _sigmoid(unit(q_ref, j, w)) for j, w in chains]
    f = [lb_ref[:, w * LANE:(w + 1) * LANE] + (1.0 - lb_ref[:, w * LANE:(w + 1) * LANE])
         * _sigmoid(unit(f_ref, j, w)) for j, w in chains]
    logf = [jnp.log(x) for x in f]
    kf = [1.0 - x for x in f]
    v = [unit(i_ref, j, w) for j, w in chains]
    if l_valid < l_total:
        ok = (ci * c + _iota2((c, LANE), 0)) < l_valid
        logf = [jnp.where(ok, x, 0.0) for x in logf]
        kf = [jnp.where(ok, x, 0.0) for x in kf]
    cum = [_cumsum_time(x) for x in logf]
    cref = [x[mid:mid + 1, :] for x in cum]
    clast = [x[c - 1:c, :] for x in cum]
    amat = [tri * _dot_nt(q[j] * jnp.exp(cum[j] - cref[j]), kf[j] * jnp.exp(cref[j] - cum[j])) for j in seqs]
    st = [s_scr[j] for j in seqs]
    o = [_dot(q[j] * jnp.exp(cum[j]), st[j]) + _dot(amat[j], v[j]) for j in seqs]
    decay = [jnp.exp(key_column(x)) for x in logf]
    for j in seqs:
        s_scr[j] = st[j] * decay[j] + _dot_tn(kf[j] * jnp.exp(clast[j] - cum[j]), v[j])
    for n, (j, w) in enumerate(chains):
        on = o[n] * lax.rsqrt(jnp.mean(o[n] * o[n], axis=-1, keepdims=True) + HGRN_NORM_EPS) * nw_ref[...]
        og = unit(og_ref, j, w)
        y_ref[j, :, w * LANE:(w + 1) * LANE] = on * (og * _sigmoid(og))

    @pl.when(ci == n_chunks - 1)
    def _():
        for n, (j, w) in enumerate(chains):
            so_ref[j, w] = s_scr[n]


def _hgrn_scan(proj, lb, nw, s0, bn, l_total, l_valid, bb, ub, c):
    nc = l_total // c
    has_s0 = s0 is not None
    base = D_SHIFT_PAD // (ub * LANE)
    per = D_HGRN // (ub * LANE)
    proj3 = proj.reshape(bn, l_total, D_IN_PAD)

    def col(part):
        return pl.BlockSpec((bb, c, ub * LANE), lambda bi, u, ci: (bi, ci, base + part * per + u))

    st = pl.BlockSpec((bb, ub, LANE, LANE), lambda bi, u, ci: (bi, u, 0, 0))
    in_specs = [col(0), col(1), col(2), col(3),
                pl.BlockSpec((1, ub * LANE), lambda bi, u, ci: (0, u)),
                pl.BlockSpec((1, LANE), lambda bi, u, ci: (0, 0))] + ([st] if has_s0 else [])
    args = [proj3, proj3, proj3, proj3, lb, nw] + ([s0] if has_s0 else [])
    kern = functools.partial(_hgrn_kernel, bb=bb, ub=ub, c=c, has_s0=has_s0, l_valid=l_valid,
                             l_total=l_total)
    return pl.pallas_call(
        kern,
        out_shape=[jax.ShapeDtypeStruct((bn, l_total, D_HGRN), F32),
                   jax.ShapeDtypeStruct((bn, N_UNITS, LANE, LANE), F32)],
        grid=(bn // bb, N_UNITS // ub, nc),
        in_specs=in_specs,
        out_specs=[pl.BlockSpec((bb, c, ub * LANE), lambda bi, u, ci: (bi, ci, u)), st],
        scratch_shapes=[pltpu.VMEM((bb * ub, LANE, LANE), F32)],
        compiler_params=_cparams(("parallel", "parallel", "arbitrary")),
        name="hgrn_scan",
    )(*args)


def _outproj_kernel(x_ref, yr_ref, yh_ref, wo_ref, nf_ref, wr_ref, br_ref, cnt0_ref,
                    h_ref, xn_ref, idx_ref, gate_ref, rank_ref, cnt_ref, run_scr, *, l_valid, l_total):
    @pl.when(pl.program_id(0) == 0)
    def _():
        run_scr[...] = cnt0_ref[...]

    h = (x_ref[...] + _dot(yr_ref[...], wo_ref[0:D_RWKV, :]) + _dot(yh_ref[...], wo_ref[D_RWKV:, :]))
    h_ref[...] = h
    xn = _rmsnorm(h, nf_ref[...])
    xh = xn.astype(BF16)
    bits = lax.bitcast_convert_type(xh.astype(F32), U32)
    xn_ref[...] = (bits[:, D_MODEL // 2:] & jnp.uint32(0xFFFF0000)) | (bits[:, :D_MODEL // 2] >> 16)
    wr = wr_ref[...]
    wh = wr.astype(BF16)
    logits = (_dot(xh, wh) + _dot(xn - xh.astype(F32), wh) + _dot(xh, wr - wh.astype(F32))
              + br_ref[...])
    tm = logits.shape[0]
    lane = _iota2((tm, LANE), 1).astype(F32)
    neg = jnp.float32(-jnp.inf)
    work = jnp.where(lane < N_EXPERTS, logits, neg)
    idx_out = jnp.zeros((tm, LANE), I32)
    val_out = jnp.zeros((tm, LANE), F32)
    top0 = None
    picks = []
    for kk in range(TOP_K):
        m = jnp.max(work, axis=-1, keepdims=True)
        sel = jnp.min(jnp.where(work == m, lane, float(LANE)), axis=-1, keepdims=True)
        if kk == 0:
            top0 = m
        idx_out = jnp.where(lane == kk, sel.astype(I32), idx_out)
        val_out = jnp.where(lane == kk, jnp.exp(m - top0), val_out)
        picks.append(lane == sel)
        work = jnp.where(lane == sel, neg, work)
    idx_ref[...] = idx_out
    gate_ref[...] = val_out / jnp.sum(val_out, axis=-1, keepdims=True)

    chosen = sum(p.astype(F32) for p in picks)
    if l_valid < l_total:
        assert l_total & (l_total - 1) == 0 and tm % l_total == 0
        t_in_seq = _iota2((tm, LANE), 0) & (l_total - 1)
        chosen = jnp.where(t_in_seq < l_valid, chosen, 0.0)
    earlier = (_iota2((tm, tm), 0) > _iota2((tm, tm), 1)).astype(BF16)
    before = jnp.dot(earlier, chosen.astype(BF16), preferred_element_type=F32) + run_scr[...]
    rank_out = jnp.zeros((tm, LANE), F32)
    for kk in range(TOP_K):
        r_k = jnp.sum(jnp.where(picks[kk], before, 0.0), axis=-1, keepdims=True)
        rank_out = jnp.where(lane == kk, r_k, rank_out)
    rank_ref[...] = rank_out.astype(I32)
    run_scr[...] = run_scr[...] + jnp.sum(chosen, axis=0, keepdims=True)
    cnt_ref[...] = run_scr[...]


def _outproj_router(x, yr, yh, wo, nf, wr, br, cnt0, tm, l_valid, l_total):
    t = x.shape[0]
    row = lambda n: pl.BlockSpec((tm, n), lambda i: (i, 0))
    full = lambda a, b: pl.BlockSpec((a, b), lambda i: (0, 0), pipeline_mode=pl.Buffered(1))
    return pl.pallas_call(
        functools.partial(_outproj_kernel, l_valid=l_valid, l_total=l_total),
        out_shape=[jax.ShapeDtypeStruct((t, D_MODEL), F32), jax.ShapeDtypeStruct((t, D_MODEL // 2), U32),
                   jax.ShapeDtypeStruct((t, LANE), I32), jax.ShapeDtypeStruct((t, LANE), F32),
                   jax.ShapeDtypeStruct((t, LANE), I32), jax.ShapeDtypeStruct((1, LANE), F32)],
        grid=(t // tm,),
        in_specs=[row(D_MODEL), row(D_RWKV), row(D_HGRN), full(D_MODEL, D_MODEL), full(1, D_MODEL),
                  full(D_MODEL, LANE), full(1, LANE), full(1, LANE)],
        out_specs=[row(D_MODEL), row(D_MODEL // 2), row(LANE), row(LANE), row(LANE), full(1, LANE)],
        scratch_shapes=[pltpu.VMEM((1, LANE), F32)],
        compiler_params=_cparams(("arbitrary",)),
        name="outproj_router",
    )(x, yr, yh, wo, nf, wr, br, cnt0)


MOE_NF = D_EXPERT // MOE_TF
MOE_ISSUE = MOE_TM // MOE_NF
MOE_AHEAD_TILES = MOE_RT
MOE_AHEAD = MOE_AHEAD_TILES * MOE_TM
MOE_RING = 2 * MOE_RT
MOE_DUMP = (MOE_RING + MOE_RT + 2) * MOE_TM
HALF = D_MODEL // 2


def _ring_row_copy(x_hbm, tok, ring, sems, u):
    slot = (u // MOE_TM) % MOE_RING
    return pltpu.make_async_copy(x_hbm.at[pl.ds(tok, 1)], ring.at[slot, pl.ds(u % MOE_TM, 1)], sems.at[slot])


def _ring_tile_wait(x_hbm, ring, sems, tile):
    slot = tile % MOE_RING
    pltpu.make_async_copy(x_hbm.at[pl.ds(0, MOE_TM)], ring.at[slot], sems.at[slot]).wait()


def _scatter_row_copy(ybuf, ysc_hbm, sems, dst, u):
    slot = (u // MOE_TM) % MOE_RING
    return pltpu.make_async_copy(ybuf.at[slot, pl.ds(u % MOE_TM, 1)], ysc_hbm.at[pl.ds(dst, 1)], sems.at[slot])


def _scatter_tile_wait(ybuf, ysc_hbm, sems, tile):
    slot = tile % MOE_RING
    pltpu.make_async_copy(ybuf.at[slot], ysc_hbm.at[pl.ds(0, MOE_TM)], sems.at[slot]).wait()


def _moe_kernel(ie_ref, it0_ref, int_ref, nused_ref, rowsrc_ref, rowslot_ref, x_hbm, wg_ref, wu_ref, wd_ref,
                bg_ref, bu_ref, bd_ref, ysc_hbm, ring, xbuf, acc, ybuf, ring_sems, scat_sems,
                fill_sem, *, dump0):
    i = pl.program_id(0)
    f = pl.program_id(1)
    n_items = pl.num_programs(0)
    nt = int_ref[i]
    tile0 = it0_ref[i]
    n_used = nused_ref[0]

    @pl.when((i == 0) & (f == 0))
    def _():
        for t in range(MOE_RING):
            ybuf[t] = jnp.zeros((MOE_TM, HALF), U32)
        for s in range(MOE_DUMP // MOE_TM):
            cp = pltpu.make_async_copy(ybuf.at[0], ysc_hbm.at[pl.ds(dump0 + s * MOE_TM, MOE_TM)], fill_sem)
            cp.start()
            cp.wait()

        def head(r, carry):
            _ring_row_copy(x_hbm, rowsrc_ref[r], ring, ring_sems, r).start()
            return carry

        lax.fori_loop(0, MOE_AHEAD, head, 0)

    @pl.when(nt > 0)
    def _():
        @pl.when(f == 0)
        def _():
            def load(t, carry):
                _ring_tile_wait(x_hbm, ring, ring_sems, tile0 + t)
                w = ring[(tile0 + t) % MOE_RING]
                lo = lax.bitcast_convert_type(w << 16, F32).astype(BF16)
                hi = lax.bitcast_convert_type(w & jnp.uint32(0xFFFF0000), F32).astype(BF16)
                xbuf[t, :, 0:HALF] = lo
                xbuf[t, :, HALF:D_MODEL] = hi
                acc[t] = jnp.broadcast_to(bd_ref[0], (MOE_TM, D_MODEL))
                return carry

            lax.fori_loop(0, nt, load, 0)

        bg = bg_ref[0]
        bu = bu_ref[0]

        def issue(t):
            step = f * nt + t
            pos = tile0 * MOE_TM + step * MOE_ISSUE
            row = (step * MOE_ISSUE) % MOE_TM
            tile_s = tile0 + step // MOE_NF
            slot_s = tile_s % MOE_RING
            slot_g = (tile_s + MOE_AHEAD_TILES) % MOE_RING
            for q in range(MOE_ISSUE):
                pltpu.make_async_copy(x_hbm.at[pl.ds(rowsrc_ref[pos + MOE_AHEAD + q], 1)],
                                      ring.at[slot_g, pl.ds(row + q, 1)], ring_sems.at[slot_g]).start()
            for q in range(MOE_ISSUE):
                pltpu.make_async_copy(ybuf.at[slot_s, pl.ds(row + q, 1)],
                                      ysc_hbm.at[pl.ds(rowslot_ref[pos + q], 1)],
                                      scat_sems.at[slot_s]).start(priority=1)

        def spans(todo):
            for t0, k in todo:
                for s in range(k):
                    issue(t0 + s)

            def hidden(t0, k):
                x = xbuf[pl.ds(t0, k)].reshape(k * MOE_TM, D_MODEL)
                g = jnp.minimum(jnp.dot(x, wg_ref[0].astype(BF16), preferred_element_type=F32) + bg, SWIGLU_LIMIT)
                u = jnp.clip(jnp.dot(x, wu_ref[0].astype(BF16), preferred_element_type=F32) + bu,
                             -SWIGLU_LIMIT, SWIGLU_LIMIT)
                return ((u + 1.0) * g * _sigmoid(SWIGLU_ALPHA * g)).astype(BF16)

            hid = hidden(*todo[0])
            for s, (t0, k) in enumerate(todo):
                nxt = hidden(*todo[s + 1]) if s + 1 < len(todo) else None
                down = jnp.dot(hid, wd_ref[0].astype(BF16), preferred_element_type=F32)
                acc[pl.ds(t0, k)] += down.reshape(k, MOE_TM, D_MODEL)
                hid = nxt

        def quad(p, carry):
            spans([(4 * p, 2), (4 * p + 2, 2)])
            return carry

        n_quads = nt // 4
        lax.fori_loop(0, n_quads, quad, 0)

        def pair(p, carry):
            spans([(4 * n_quads + 2 * p, 2)])
            return carry

        lax.fori_loop(0, (nt - 4 * n_quads) // 2, pair, 0)

        @pl.when(nt % 2 == 1)
        def _():
            spans([(nt - 1, 1)])

        @pl.when(f == MOE_NF - 1)
        def _():
            def pack(t, carry):
                vt = tile0 + t + MOE_AHEAD_TILES

                @pl.when(vt >= MOE_RING)
                def _():
                    _scatter_tile_wait(ybuf, ysc_hbm, scat_sems, vt)

                bits = lax.bitcast_convert_type(acc[t].astype(BF16).astype(F32), U32)
                ybuf[vt % MOE_RING] = (bits[:, HALF:] & jnp.uint32(0xFFFF0000)) | (bits[:, :HALF] >> 16)
                return carry

            lax.fori_loop(0, nt, pack, 0)

    @pl.when((i == n_items - 1) & (f == MOE_NF - 1))
    def _():
        for s in range(MOE_AHEAD_TILES):
            _ring_tile_wait(x_hbm, ring, ring_sems, n_used + s)

        def flush(u, carry):
            _scatter_row_copy(ybuf, ysc_hbm, scat_sems, rowslot_ref[u], u).start()
            return carry

        lax.fori_loop(n_used * MOE_TM, n_used * MOE_TM + MOE_AHEAD, flush, 0)

        def retire(vt, carry):
            _scatter_tile_wait(ybuf, ysc_hbm, scat_sems, vt)
            return carry

        lax.fori_loop(jnp.maximum(n_used - MOE_AHEAD_TILES, 0), n_used + MOE_AHEAD_TILES, retire, 0)


def _moe_experts(item_e, item_t0, item_nt, n_used, row_src, row_slot, x_packed, w_gu, b_gu, w_down, b_down,
                 n_items, n_tok):
    nf = MOE_NF
    dump0 = TOP_K * n_tok

    def fcol(i, f, int_):
        return jnp.where(int_[i] > 0, f, nf - 1)

    in_specs = [
        pl.BlockSpec(memory_space=pl.ANY),
        pl.BlockSpec((1, D_MODEL, MOE_TF), lambda i, f, ie, it0, int_, *_: (ie[i], 0, fcol(i, f, int_))),
        pl.BlockSpec((1, D_MODEL, MOE_TF), lambda i, f, ie, it0, int_, *_: (ie[i], 0, nf + fcol(i, f, int_))),
        pl.BlockSpec((1, MOE_TF, D_MODEL), lambda i, f, ie, it0, int_, *_: (ie[i], fcol(i, f, int_), 0)),
        pl.BlockSpec((1, 1, MOE_TF), lambda i, f, ie, it0, int_, *_: (ie[i], 0, fcol(i, f, int_))),
        pl.BlockSpec((1, 1, MOE_TF), lambda i, f, ie, it0, int_, *_: (ie[i], 0, nf + fcol(i, f, int_))),
        pl.BlockSpec((1, 1, D_MODEL), lambda i, f, ie, it0, int_, *_: (ie[i], 0, 0)),
    ]
    return pl.pallas_call(
        functools.partial(_moe_kernel, dump0=dump0),
        out_shape=jax.ShapeDtypeStruct((dump0 + MOE_DUMP, HALF), U32),
        grid_spec=pltpu.PrefetchScalarGridSpec(
            num_scalar_prefetch=6,
            grid=(n_items, nf),
            in_specs=in_specs,
            out_specs=pl.BlockSpec(memory_space=pl.ANY),
            scratch_shapes=[pltpu.VMEM((MOE_RING, MOE_TM, HALF), U32),
                            pltpu.VMEM((MOE_RT, MOE_TM, D_MODEL), BF16),
                            pltpu.VMEM((MOE_RT, MOE_TM, D_MODEL), F32),
                            pltpu.VMEM((MOE_RING, MOE_TM, HALF), U32),
                            pltpu.SemaphoreType.DMA((MOE_RING,)),
                            pltpu.SemaphoreType.DMA((MOE_RING,)),
                            pltpu.SemaphoreType.DMA(())]),
        compiler_params=_cparams(("arbitrary", "arbitrary")),
        name="moe_experts",
    )(item_e, item_t0, item_nt, n_used, row_src, row_slot, x_packed, w_gu, w_gu, w_down, b_gu, b_gu, b_down)


def _ple_kernel(h_ref, gate_ref, y0_ref, y1_ref, y2_ref, y3_ref, p_ref, np_ref, wg_ref, wp_ref, nfin_ref, o_ref):
    gates = gate_ref[...]
    lo = jnp.zeros((h_ref.shape[0], HALF), F32)
    hi = jnp.zeros((h_ref.shape[0], HALF), F32)
    for kk, y_ref in enumerate((y0_ref, y1_ref, y2_ref, y3_ref)):
        w = y_ref[...]
        g = gates[:, kk:kk + 1]
        lo = lo + g * lax.bitcast_convert_type(w << 16, F32)
        hi = hi + g * lax.bitcast_convert_type(w & jnp.uint32(0xFFFF0000), F32)
    h = h_ref[...] + jnp.concatenate([lo, hi], axis=1)
    gate = _sigmoid(_dot(_rmsnorm(h, np_ref[...]), wg_ref[...]))
    h = h + gate * _dot(p_ref[...], wp_ref[...])
    o_ref[...] = _rmsnorm(h, nfin_ref[...])


def _ple_final(h, gates, ysc, plane, tok0, p, n_ple, wg, wp, n_fin, tm):
    t = h.shape[0]
    row = lambda n: pl.BlockSpec((tm, n), lambda i: (i, 0))
    full = lambda a, b: pl.BlockSpec((a, b), lambda i: (0, 0), pipeline_mode=pl.Buffered(1))
    ysp = lambda kk: pl.BlockSpec((tm, HALF), lambda i: ((kk * plane + tok0) // tm + i, 0))
    return pl.pallas_call(
        _ple_kernel,
        out_shape=jax.ShapeDtypeStruct((t, D_MODEL), F32),
        grid=(t // tm,),
        in_specs=[row(D_MODEL), row(LANE), ysp(0), ysp(1), ysp(2), ysp(3), row(D_PLE), full(1, D_MODEL),
                  full(D_MODEL, D_MODEL), full(D_PLE, D_MODEL), full(1, D_MODEL)],
        out_specs=row(D_MODEL),
        compiler_params=_cparams(("parallel",)),
        name="ple_final",
    )(h, gates, ysc, ysc, ysc, ysc, p, n_ple, wg, wp, n_fin)


def _pad_shift_cols(a):
    def z(n):
        return jnp.zeros(a.shape[:-1] + (n,), a.dtype)
    c0 = 3 * D_RWKV
    c1 = c0 + LORA_W
    c2 = c1 + LORA_A
    return jnp.concatenate([a[..., :c0], a[..., c0:c1], z(LW_PAD - LORA_W), a[..., c1:c2], z(LA_PAD - LORA_A),
                            a[..., c2:], z(LG_PAD - LORA_G)], axis=-1)


def _pad_shift_rows(a):
    z = lambda n: jnp.zeros((n, a.shape[1]), a.dtype)
    c0 = 3 * D_RWKV
    c1 = c0 + LORA_W
    c2 = c1 + LORA_A
    return jnp.concatenate([a[:c0], a[c0:c1], z(LW_PAD - LORA_W), a[c1:c2], z(LA_PAD - LORA_A), a[c2:],
                            z(LG_PAD - LORA_G)], axis=0)


def _unpad_shift_cols(a):
    return jnp.concatenate([a[..., :OFF_WD], a[..., OFF_WD:OFF_WD + LORA_W], a[..., OFF_AD:OFF_AD + LORA_A],
                            a[..., OFF_GD:OFF_GD + LORA_G]], axis=-1)


def _pad_rows(a, n):
    return jnp.concatenate([a, jnp.zeros((n - a.shape[0],) + a.shape[1:], a.dtype)], axis=0)


def _route_tables_kernel(pos_ref, src0_hbm, slot0_hbm, src_ref, slot_ref, sems, *, n_tok):
    fill_src = pltpu.make_async_copy(src0_hbm, src_ref, sems.at[0])
    fill_slot = pltpu.make_async_copy(slot0_hbm, slot_ref, sems.at[1])
    fill_src.start()
    fill_slot.start()
    fill_src.wait()
    fill_slot.wait()

    def place(tok, carry):
        for kk in range(TOP_K):
            r = pos_ref[tok * TOP_K + kk]
            src_ref[r] = tok
            slot_ref[r + MOE_AHEAD] = tok + kk * n_tok
        return carry

    lax.fori_loop(0, n_tok, place, 0, unroll=4)


def _route_tables(pos, n_tok, n_stream):
    u = jnp.arange(n_stream, dtype=I32)
    src0 = jnp.full((n_stream,), n_tok, I32)
    slot0 = TOP_K * n_tok + u % MOE_DUMP
    smem = pl.BlockSpec(memory_space=pltpu.SMEM)
    hbm = pl.BlockSpec(memory_space=pl.ANY)
    return pl.pallas_call(
        functools.partial(_route_tables_kernel, n_tok=n_tok),
        out_shape=[jax.ShapeDtypeStruct((n_stream,), I32)] * 2,
        in_specs=[smem, hbm, hbm],
        out_specs=[smem, smem],
        scratch_shapes=[pltpu.SemaphoreType.DMA((2,))],
        name="route_tables",
    )(pos, src0, slot0)


def _routing(idx, rank, counts, n_rows_cap, n_items_cap):
    t = idx.shape[0]
    na = t * TOP_K
    flat_e = idx.reshape(na)
    rank = rank.reshape(na)
    ptiles = (counts + MOE_TM - 1) // MOE_TM
    pend = jnp.cumsum(ptiles)
    pstart = pend - ptiles
    pos = (pstart[flat_e] * MOE_TM + rank).astype(I32)
    row_src, row_slot = _route_tables(pos, t, n_rows_cap + 2 * MOE_AHEAD)
    items_per_e = (ptiles + MOE_RT - 1) // MOE_RT
    iend = jnp.cumsum(items_per_e)
    istart = iend - items_per_e
    ii = jnp.arange(n_items_cap, dtype=I32)
    e_of = jnp.minimum(jnp.searchsorted(iend, ii, side="right"), N_EXPERTS - 1).astype(I32)
    jj = ii - istart[e_of]
    used = ii < iend[-1]
    item_nt = jnp.where(used, jnp.clip(ptiles[e_of] - jj * MOE_RT, 0, MOE_RT), 0).astype(I32)
    item_t0 = jnp.where(used, pstart[e_of] + jj * MOE_RT, 0).astype(I32)
    last_e = e_of[jnp.maximum(iend[-1] - 1, 0)]
    item_e = jnp.where(used, e_of, last_e).astype(I32)
    n_used = jnp.stack([pend[-1], iend[-1]]).astype(I32)
    return row_src, row_slot, n_used, item_e, item_t0, item_nt


def kernel(x_prompt, x_sample, p_prompt, p_sample, state_rwkv_shift, state_rwkv, state_hgrn, norm_mix, w_in,
           mu_shift, w0, w_up, a0, a_up, g_up, k_k, k_a, r_k, lnx_w, lnx_b, hgrn_lb, hgrn_norm, w_out,
           norm_ffn, w_router, b_router, w_gu, b_gu, w_down, b_down, norm_ple, w_ple_gate, w_ple_proj,
           norm_final):
    depth = w_in.shape[0]
    assert depth == 1
    li = 0
    bp, lp = x_prompt.shape[0], x_prompt.shape[1]
    bs, ls = x_sample.shape[0], x_sample.shape[1]
    ls_pad = SUBLANE
    tp = bp * lp

    w_t = jnp.transpose(w_in[li])
    w_in_p = jnp.concatenate([_pad_shift_rows(w_t[:D_SHIFT]), w_t[D_SHIFT:]], axis=0).astype(BF16)
    row = lambda a: a.reshape(1, -1).astype(F32)
    pp = {
        "mu": row(_pad_shift_cols(mu_shift[li])),
        "w0": row(w0[li]), "a0": row(a0[li]), "k_k": row(k_k[li]), "k_a": row(k_a[li]),
        "w_up": _pad_rows(w_up[li], LW_PAD).astype(BF16),
        "a_up": _pad_rows(a_up[li], LA_PAD).astype(BF16),
        "g_up": _pad_rows(g_up[li], LG_PAD).astype(BF16),
    }
    rk = row(r_k[li])
    lnw = row(lnx_w[li])
    lnb = row(lnx_b[li])
    lower = jax.nn.softmax(hgrn_lb.astype(F32), axis=0)
    lb = row(jnp.cumsum(lower, axis=0)[li])
    nw = row(hgrn_norm[li])
    wo = w_out[li].astype(BF16)
    nf = row(norm_ffn[li])
    wr = jnp.concatenate([w_router[li], jnp.zeros((D_MODEL, LANE - N_EXPERTS), F32)], axis=1)
    br = jnp.concatenate([b_router[li], jnp.zeros((LANE - N_EXPERTS,), F32)]).reshape(1, LANE)
    n_ple = row(norm_ple[li])
    wpg = w_ple_gate[li].astype(BF16)
    wpp = w_ple_proj[li].astype(BF16)
    n_fin = row(norm_final)
    g_mix = row(norm_mix[li])

    def mixer(x2d, shift_prev, s_rwkv, s_hgrn, cnt0, bn, l_total, l_valid, tm_in, bb_prep, tt, bb_scan, ub_scan, c,
              tm_out):
        proj = _inproj(x2d, g_mix, w_in_p, tm_in, 1536)
        streams = _rwkv_prep(proj, shift_prev, pp, bn, l_total, l_valid, bb_prep, tt)
        yr, s_rwkv_new = _rwkv_scan(streams, rk, lnw, lnb, s_rwkv, bn, l_total, bb_scan, ub_scan, c)
        yh, s_hgrn_new = _hgrn_scan(proj, lb, nw, s_hgrn, bn, l_total, l_valid, bb_scan, ub_scan, c)
        h1, xn2, idx, gates, rank, cnt = _outproj_router(
            x2d, yr.reshape(bn * l_total, D_RWKV), yh.reshape(bn * l_total, D_HGRN), wo, nf, wr, br, cnt0, tm_out,
            l_valid, l_total)
        new_shift = _unpad_shift_cols(proj.reshape(bn, l_total, D_IN_PAD)[:, l_valid - 1, :D_SHIFT_PAD])
        return h1, xn2, idx, gates, rank, cnt, new_shift, s_rwkv_new, s_hgrn_new

    xp2 = x_prompt.reshape(tp, D_MODEL)
    zero_shift = jnp.zeros((bp, 1, D_SHIFT_PAD), F32)
    h1p, xn2p, idxp, gatesp, rankp, cntp, shift_p, rwkv_p, hgrn_p = mixer(
        xp2, zero_shift, None, None, jnp.zeros((1, LANE), F32), bp, lp, lp, 1024, 1, 256, bp, 4, 64, 512)

    xs_pad = jnp.concatenate([x_sample, jnp.zeros((bs, ls_pad - ls, D_MODEL), F32)], axis=1)
    xs2 = xs_pad.reshape(bs * ls_pad, D_MODEL)
    shift_s0 = _pad_shift_cols(state_rwkv_shift[li]).reshape(bs, 1, D_SHIFT_PAD)
    h1s, xn2s, idxs, gatess, ranks, cnt_all, shift_s, rwkv_s, hgrn_s = mixer(
        xs2, shift_s0, state_rwkv[li], state_hgrn[li], cntp, bs, ls_pad, ls, 1024, 16, ls_pad, 16, 4, ls_pad, 512)

    def compact(a):
        return a.reshape(bs, ls_pad, a.shape[-1])[:, :ls].reshape(bs * ls, a.shape[-1])

    h1s, xn2s, idxs, gatess, ranks = compact(h1s), compact(xn2s), compact(idxs), compact(gatess), compact(ranks)
    ts = bs * ls

    t_all = tp + ts
    idx_all = jnp.concatenate([idxp[:, :TOP_K], idxs[:, :TOP_K]], axis=0)
    rank_all = jnp.concatenate([rankp[:, :TOP_K], ranks[:, :TOP_K]], axis=0)
    counts = cnt_all[0, :N_EXPERTS].astype(I32)
    n_tiles_cap = -(-(t_all * TOP_K) // MOE_TM) + N_EXPERTS
    n_rows_cap = n_tiles_cap * MOE_TM
    n_items_cap = N_EXPERTS + n_tiles_cap // MOE_RT
    row_src, row_slot, n_used, item_e, item_t0, item_nt = _routing(idx_all, rank_all, counts, n_rows_cap,
                                                                   n_items_cap)
    x_packed = jnp.concatenate([xn2p, xn2s, jnp.zeros((SUBLANE, HALF), U32)], axis=0)
    ysc = _moe_experts(item_e, item_t0, item_nt, n_used, row_src, row_slot, x_packed, w_gu[li],
                       b_gu[li].reshape(N_EXPERTS, 1, -1), w_down[li], b_down[li].reshape(N_EXPERTS, 1, -1),
                       n_used[1], t_all)

    tm_fin = 512
    assert tp % tm_fin == 0 and ts % tm_fin == 0
    y_p = _ple_final(h1p, gatesp, ysc, t_all, 0, p_prompt[li].reshape(tp, D_PLE), n_ple, wpg, wpp, n_fin, tm_fin)
    y_s = _ple_final(h1s, gatess, ysc, t_all, tp, p_sample[li].reshape(ts, D_PLE), n_ple, wpg, wpp, n_fin, tm_fin)

    return (y_p.reshape(bp, lp, D_MODEL), y_s.reshape(bs, ls, D_MODEL),
            shift_p[None], rwkv_p[None], hgrn_p[None],
            shift_s[None], rwkv_s[None], hgrn_s[None])
```
